```python
import math
import jax, jax.numpy as jnp
from jax import lax
import numpy as np

D_MODEL = 4096
BATCH = 2
SEQ = 4096
DEPTH = 1

MEM_LEN = 256
RWKV_WIDTH = D_MODEL // 2
RWKV_HEAD = 64
RWKV_HEADS = RWKV_WIDTH // RWKV_HEAD
RWKV_DECAY_LORA = 96
RWKV_A_LORA = 96
RWKV_GATE_LORA = 256
RWKV_GN_EPS = 64e-5
RWKV_COLS = 3 * RWKV_WIDTH + RWKV_DECAY_LORA + RWKV_A_LORA + RWKV_GATE_LORA
GDN_WIDTH = D_MODEL // 2
GDN_HEAD = 128
GDN_HEADS = GDN_WIDTH // GDN_HEAD
GDN_CONV = 4
GDN_CHUNK = 64
GDN_COLS = 4 * GDN_WIDTH + 2 * GDN_HEADS
N_IN = RWKV_COLS + GDN_COLS + 2 * D_MODEL
XA_HEADS = 4
XA_HEAD = 128
XA_WIDTH = XA_HEADS * XA_HEAD
D_FF = 4 * D_MODEL
NORM_EPS = 1e-6
L2_EPS = 1e-6

kernel_name = 'hybrid_rwkv7_gdn_memxattn_layer'


def rms_norm(x, gain, eps=NORM_EPS):
    xf = x.astype(jnp.float32)
    y = xf * lax.rsqrt(jnp.mean(xf * xf, axis=-1, keepdims=True) + eps)
    return (y * gain.astype(jnp.float32)).astype(x.dtype)


def l2_normalize(x, eps=L2_EPS):
    xf = x.astype(jnp.float32)
    return xf * lax.rsqrt(jnp.sum(xf * xf, axis=-1, keepdims=True) + eps)


def causal_depthwise_conv(x, w):
    return lax.conv_general_dilated(
        x, w[:, None, :].astype(x.dtype), window_strides=(1,),
        padding=[(w.shape[0] - 1, 0)], dimension_numbers=('NWC', 'WIO', 'NWC'),
        feature_group_count=x.shape[-1])


def rwkv7_recurrence(r, log_w, k, v, a, b):
    def step(S, inp):
        r_t, lw_t, k_t, v_t, a_t, b_t = inp
        sa = jnp.einsum('bhvk,bhk->bhv', S, a_t)
        S = (S * jnp.exp(lw_t)[:, :, None, :] + sa[..., :, None] * b_t[..., None, :]
             + v_t[..., :, None] * k_t[..., None, :])
        return S, jnp.einsum('bhvk,bhk->bhv', S, r_t)
    B, T, H, N = r.shape
    xs = tuple(jnp.moveaxis(t, 1, 0) for t in (r, log_w, k, v, a, b))
    _, y = lax.scan(step, jnp.zeros((B, H, N, N), jnp.float32), xs)
    return jnp.moveaxis(y, 0, 1)


def rwkv7_time_mix(p_rw, shift_mix, w0, w_up, a0, a_up, g_up, k_k, k_a, r_k, gn_w, gn_b):
    B, T, _ = p_rw.shape
    dt = p_rw.dtype
    prev = jnp.pad(p_rw, ((0, 0), (1, 0), (0, 0)))[:, :-1]
    xm = p_rw + (prev - p_rw) * shift_mix
    o1 = RWKV_WIDTH
    o2 = o1 + RWKV_WIDTH
    o3 = o2 + RWKV_WIDTH
    o4 = o3 + RWKV_DECAY_LORA
    o5 = o4 + RWKV_A_LORA
    r, k, v, lw, la, lg = jnp.split(xm, [o1, o2, o3, o4, o5], axis=-1)
    w_raw = -jax.nn.softplus(-(w0 + jnp.tanh(lw) @ w_up)) - 0.5
    log_decay = -jnp.exp(w_raw.astype(jnp.float32))
    a = jax.nn.sigmoid(a0 + la @ a_up)
    gate = jax.nn.sigmoid(lg) @ g_up
    heads = lambda t: t.reshape(B, T, RWKV_HEADS, RWKV_HEAD)
    kk = l2_normalize(heads(k * k_k))
    k = k * (1.0 + (a - 1.0) * k_a)
    r_h, k_h, v_h, a_h = heads(r), heads(k), heads(v), heads(a).astype(jnp.float32)
    f32 = lambda t: t.astype(jnp.float32)
    y = rwkv7_recurrence(f32(r_h), heads(log_decay), f32(k_h), f32(v_h), -kk, kk * a_h)
    mean = jnp.mean(y, axis=-1, keepdims=True)
    var = jnp.mean(jnp.square(y - mean), axis=-1, keepdims=True)
    y = ((y - mean) * lax.rsqrt(var + RWKV_GN_EPS)).reshape(B, T, RWKV_WIDTH)
    y = y * gn_w.astype(jnp.float32) + gn_b.astype(jnp.float32)
    bonus = jnp.sum(r_h * k_h * r_k, axis=-1, keepdims=True) * v_h
    y = y + bonus.reshape(B, T, RWKV_WIDTH).astype(jnp.float32)
    return y.astype(dt) * gate


def gated_delta_chunked(q, k, v, g, beta):
    B, T, H, K = q.shape
    V = v.shape[-1]
    C = GDN_CHUNK
    n = T // C

    def chunks(t):
        return jnp.moveaxis(t.reshape((B, n, C, H) + t.shape[3:]), 3, 1)

    q = chunks(q) * (K ** -0.5)
    k, v, g, beta = chunks(k), chunks(v), chunks(g), chunks(beta)
    gc = jnp.cumsum(g, axis=-1)
    idx = jnp.arange(C)
    causal = idx[:, None] >= idx[None, :]
    strict = idx[:, None] > idx[None, :]
    decay = jnp.where(causal, jnp.exp(jnp.where(causal, gc[..., :, None] - gc[..., None, :], 0.0)), 0.0)
    kb = k * beta[..., None]
    L = jnp.where(strict, jnp.einsum('bhnik,bhnjk->bhnij', kb, k) * decay, 0.0)
    eye = jnp.eye(C, dtype=q.dtype)
    tinv = lax.linalg.triangular_solve(L + eye, jnp.broadcast_to(eye, L.shape),
                                       left_side=True, lower=True, unit_diagonal=True)
    u = jnp.einsum('bhnij,bhnjv->bhniv', tinv, v * beta[..., None])
    w = jnp.einsum('bhnij,bhnjk->bhnik', tinv, kb * jnp.exp(gc)[..., None])
    a_intra = jnp.where(causal, jnp.einsum('bhnik,bhnjk->bhnij', q, k) * decay, 0.0)
    q_dec = q * jnp.exp(gc)[..., None]
    k_dec = k * jnp.exp(gc[..., -1:] - gc)[..., None]
    g_last = jnp.exp(gc[..., -1])

    def step(S, inp):
        qd, kd, u_i, w_i, a_i, gl = inp
        v_new = u_i - jnp.einsum('bhck,bhkv->bhcv', w_i, S)
        o = jnp.einsum('bhck,bhkv->bhcv', qd, S) + jnp.einsum('bhcj,bhjv->bhcv', a_i, v_new)
        S = S * gl[..., None, None] + jnp.einsum('bhck,bhcv->bhkv', kd, v_new)
        return S, o

    xs = tuple(jnp.moveaxis(t, 2, 0) for t in (q_dec, k_dec, u, w, a_intra, g_last))
    _, o = lax.scan(step, jnp.zeros((B, H, K, V), jnp.float32), xs)
    o = jnp.moveaxis(o, 0, 2)
    return jnp.moveaxis(o, 1, 3).reshape(B, T, H, V)


def gated_deltanet_mix(p_qkv, p_z, p_beta, p_alpha, conv_w, a_log, dt_bias, norm_w):
    B, T, _ = p_qkv.shape
    dt = p_qkv.dtype
    qkv = jax.nn.silu(causal_depthwise_conv(p_qkv, conv_w))
    q, k, v = jnp.split(qkv, 3, axis=-1)
    heads = lambda t: t.reshape(B, T, GDN_HEADS, GDN_HEAD)
    q = l2_normalize(heads(q))
    k = l2_normalize(heads(k))
    v = heads(v).astype(jnp.float32)
    beta = jax.nn.sigmoid(p_beta.astype(jnp.float32))
    g = -jnp.exp(a_log.astype(jnp.float32)) * jax.nn.softplus(
        p_alpha.astype(jnp.float32) + dt_bias.astype(jnp.float32))
    o = gated_delta_chunked(q, k, v, g, beta)
    o = o * lax.rsqrt(jnp.mean(o * o, axis=-1, keepdims=True) + NORM_EPS) * norm_w.astype(jnp.float32)
    o = o * jax.nn.silu(heads(p_z).astype(jnp.float32))
    return o.reshape(B, T, GDN_WIDTH).astype(dt)


def hybrid_mixer(u, w_in, shift_mix, w0, w_up, a0, a_up, g_up, k_k, k_a, r_k, gn_w, gn_b,
                 conv_w, a_log, dt_bias, gdn_norm_w, w_br_rwkv, w_br_gdn, w_out):
    p = u @ w_in
    o1 = RWKV_COLS
    o2 = o1 + 3 * GDN_WIDTH
    o3 = o2 + GDN_WIDTH
    o4 = o3 + GDN_HEADS
    o5 = o4 + GDN_HEADS
    o6 = o5 + D_MODEL
    p_rw, p_qkv, p_z, p_beta, p_alpha, p_gate_rw, p_gate_gdn = jnp.split(
        p, [o1, o2, o3, o4, o5, o6], axis=-1)
    y_rw = rwkv7_time_mix(p_rw, shift_mix, w0, w_up, a0, a_up, g_up, k_k, k_a, r_k, gn_w, gn_b)
    y_gdn = gated_deltanet_mix(p_qkv, p_z, p_beta, p_alpha, conv_w, a_log, dt_bias, gdn_norm_w)
    merged = (jax.nn.sigmoid(p_gate_rw) * (y_rw @ w_br_rwkv)
              + jax.nn.sigmoid(p_gate_gdn) * (y_gdn @ w_br_gdn))
    return merged @ w_out


def memory_cross_attention(c, m, w_q, w_kv, w_o):
    B, T, _ = c.shape
    M = m.shape[1]
    q = (c @ w_q).reshape(B, T, XA_HEADS, XA_HEAD)
    k, v = jnp.split(m @ w_kv, 2, axis=-1)
    k = k.reshape(B, M, XA_HEADS, XA_HEAD)
    v = v.reshape(B, M, XA_HEADS, XA_HEAD)
    s = jnp.einsum('bthd,bmhd->bhtm', q, k).astype(jnp.float32) * (XA_HEAD ** -0.5)
    pr = jax.nn.softmax(s, axis=-1).astype(v.dtype)
    o = jnp.einsum('bhtm,bmhd->bthd', pr, v).reshape(B, T, XA_WIDTH)
    return o @ w_o


def squared_relu_mlp(f, w_up, w_down):
    return jnp.square(jax.nn.relu(f @ w_up)) @ w_down


def setup_inputs(seed: int = 0) -> dict:
    key = jax.random.key(seed)
    ks = iter(jax.random.split(key, 48))
    Ld = DEPTH

    def nrm(shape, scale):
        return scale * jax.random.normal(next(ks), shape, jnp.float32)

    def unif(shape, lo, hi):
        return jax.random.uniform(next(ks), shape, jnp.float32, lo, hi)

    def gain(width):
        return 1.0 + nrm((Ld, width), 0.02)

    dt = jnp.exp(unif((Ld, GDN_HEADS), math.log(1e-3), math.log(1e-1)))
    dt_bias = dt + jnp.log(-jnp.expm1(-dt))
    return {
        'x': nrm((BATCH, SEQ, D_MODEL), 1.0),
        'mem': nrm((BATCH, MEM_LEN, D_MODEL), 1.0),
        'mix_norm_pre': gain(D_MODEL),
        'mix_norm_post': gain(D_MODEL),
        'w_in': nrm((Ld, D_MODEL, N_IN), D_MODEL ** -0.5),
        'rwkv_shift_mix': unif((Ld, RWKV_COLS), 0.0, 1.0),
        'rwkv_w0': unif((Ld, RWKV_WIDTH), -6.0, -1.0),
        'rwkv_w_up': nrm((Ld, RWKV_DECAY_LORA, RWKV_WIDTH), 0.5 * RWKV_DECAY_LORA ** -0.5),
        'rwkv_a0': nrm((Ld, RWKV_WIDTH), 0.1),
        'rwkv_a_up': nrm((Ld, RWKV_A_LORA, RWKV_WIDTH), RWKV_A_LORA ** -0.5),
        'rwkv_g_up': nrm((Ld, RWKV_GATE_LORA, RWKV_WIDTH), RWKV_GATE_LORA ** -0.5),
        'rwkv_k_k': 0.85 + nrm((Ld, RWKV_WIDTH), 0.02),
        'rwkv_k_a': 1.0 + nrm((Ld, RWKV_WIDTH), 0.02),
        'rwkv_r_k': -0.04 + nrm((Ld, RWKV_HEADS, RWKV_HEAD), 0.05),
        'rwkv_gn_w': gain(RWKV_WIDTH),
        'rwkv_gn_b': nrm((Ld, RWKV_WIDTH), 0.02),
        'gdn_conv_w': nrm((Ld, GDN_CONV, 3 * GDN_WIDTH), GDN_CONV ** -0.5),
        'gdn_a_log': jnp.log(unif((Ld, GDN_HEADS), 1.0, 16.0)),
        'gdn_dt_bias': dt_bias,
        'gdn_norm_w': gain(GDN_HEAD),
        'w_branch_rwkv': nrm((Ld, RWKV_WIDTH, D_MODEL), RWKV_WIDTH ** -0.5),
        'w_branch_gdn': nrm((Ld, GDN_WIDTH, D_MODEL), GDN_WIDTH ** -0.5),
        'w_mix_out': nrm((Ld, D_MODEL, D_MODEL), D_MODEL ** -0.5),
        'xa_norm_pre': gain(D_MODEL),
        'xa_norm_mem': gain(D_MODEL),
        'xa_norm_post': gain(D_MODEL),
        'xa_w_q': nrm((Ld, D_MODEL, XA_WIDTH), D_MODEL ** -0.5),
        'xa_w_kv': nrm((Ld, D_MODEL, 2 * XA_WIDTH), D_MODEL ** -0.5),
        'xa_w_o': nrm((Ld, XA_WIDTH, D_MODEL), XA_WIDTH ** -0.5),
        'mlp_norm_pre': gain(D_MODEL),
        'mlp_norm_post': gain(D_MODEL),
        'mlp_w_up': nrm((Ld, D_MODEL, D_FF), D_MODEL ** -0.5),
        'mlp_w_down': nrm((Ld, D_FF, D_MODEL), D_FF ** -0.5),
    }


def reference(x, mem, mix_norm_pre, mix_norm_post, w_in,
              rwkv_shift_mix, rwkv_w0, rwkv_w_up, rwkv_a0, rwkv_a_up, rwkv_g_up,
              rwkv_k_k, rwkv_k_a, rwkv_r_k, rwkv_gn_w, rwkv_gn_b,
              gdn_conv_w, gdn_a_log, gdn_dt_bias, gdn_norm_w,
              w_branch_rwkv, w_branch_gdn, w_mix_out,
              xa_norm_pre, xa_norm_mem, xa_norm_post, xa_w_q, xa_w_kv, xa_w_o,
              mlp_norm_pre, mlp_norm_post, mlp_w_up, mlp_w_down):
    h = x
    for l in range(DEPTH):
        u = rms_norm(h, mix_norm_pre[l])
        y = hybrid_mixer(u, w_in[l], rwkv_shift_mix[l], rwkv_w0[l], rwkv_w_up[l], rwkv_a0[l],
                         rwkv_a_up[l], rwkv_g_up[l], rwkv_k_k[l], rwkv_k_a[l], rwkv_r_k[l],
                         rwkv_gn_w[l], rwkv_gn_b[l], gdn_conv_w[l], gdn_a_log[l], gdn_dt_bias[l],
                         gdn_norm_w[l], w_branch_rwkv[l], w_branch_gdn[l], w_mix_out[l])
        h = h + rms_norm(y, mix_norm_post[l])
        c = rms_norm(h, xa_norm_pre[l])
        m = rms_norm(mem, xa_norm_mem[l])
        h = h + rms_norm(memory_cross_attention(c, m, xa_w_q[l], xa_w_kv[l], xa_w_o[l]), xa_norm_post[l])
        f = rms_norm(h, mlp_norm_pre[l])
        h = h + rms_norm(squared_relu_mlp(f, mlp_w_up[l], mlp_w_down[l]), mlp_norm_post[l])
    return h
```

```python
import functools
import math

import jax
import jax.numpy as jnp
from jax import lax
from jax.experimental import pallas as pl
from jax.experimental.pallas import tpu as pltpu

F32 = jnp.float32
BF16 = jnp.bfloat16
HI = lax.Precision.HIGHEST

V7X_LANES = 128
V7X_SUBLANES = 8
V7X_VMEM_LIMIT_BYTES = 56 * 1024 * 1024

NORM_EPS = 1e-6
L2_EPS = 1e-6
RWKV_GN_EPS = 64e-5
RWKV_HEAD = 64
GDN_HEAD = 128
CHUNK = 64
XA_HEADS = 4
XA_HEAD = 128

RW = 2048
OFF_R, OFF_K, OFF_V = 0, RW, 2 * RW
OFF_MISC = 3 * RW
MISC_W = 512
OFF_GQ = OFF_MISC + MISC_W
OFF_GK, OFF_GV = OFF_GQ + RW, OFF_GQ + 2 * RW
OFF_Z = OFF_GQ + 3 * RW
OFF_GATE_RW = OFF_Z + RW
D_MODEL = 4096
OFF_GATE_GDN = OFF_GATE_RW + D_MODEL
NP = OFF_GATE_GDN + D_MODEL
LORA = 96
N_GDN_HEADS = RW // GDN_HEAD
BETA_LANE = LORA
ALPHA_LANE = LORA + N_GDN_HEADS


def _cparams(sem):
    return pltpu.CompilerParams(dimension_semantics=sem, vmem_limit_bytes=V7X_VMEM_LIMIT_BYTES)


def _dot(a, b, prec=None):
    return jnp.dot(a, b, preferred_element_type=F32, precision=prec)


def _dot_nt(a, b, prec=None):
    return lax.dot_general(a, b, (((1,), (1,)), ((), ())), preferred_element_type=F32, precision=prec)


def _dot_tn(a, b, prec=None):
    return lax.dot_general(a, b, (((0,), (0,)), ((), ())), preferred_element_type=F32, precision=prec)


def _sigmoid(x):
    return 1.0 / (1.0 + jnp.exp(-x))


def _softplus(x):
    return jnp.maximum(x, 0.0) + jnp.log1p(jnp.exp(-jnp.abs(x)))


def _iota(shape, dim):
    return lax.broadcasted_iota(jnp.int32, shape, dim)


def _rmsnorm_kernel(x_ref, g_ref, o_ref):
    x = x_ref[...]
    inv = lax.rsqrt(jnp.mean(x * x, axis=-1, keepdims=True) + NORM_EPS)
    o_ref[...] = (x * inv * g_ref[...]).astype(o_ref.dtype)


def _rmsnorm(x, gain, out_dtype, rows=256):
    n, d = x.shape
    return pl.pallas_call(
        _rmsnorm_kernel,
        grid=(n // rows,),
        in_specs=[pl.BlockSpec((rows, d), lambda i: (i, 0)), pl.BlockSpec((1, d), lambda i: (0, 0))],
        out_specs=pl.BlockSpec((rows, d), lambda i: (i, 0)),
        out_shape=jax.ShapeDtypeStruct((n, d), out_dtype),
        compiler_params=_cparams(("parallel",)),
        name="rmsnorm",
    )(x, gain.reshape(1, d))


def _resid_norm_kernel(h_ref, y_ref, g_ref, o_ref):
    y = y_ref[...]
    inv = lax.rsqrt(jnp.mean(y * y, axis=-1, keepdims=True) + NORM_EPS)
    o_ref[...] = h_ref[...] + y * inv * g_ref[...]


def _resid_norm(h, y, gain, rows=256):
    n, d = h.shape
    row = pl.BlockSpec((rows, d), lambda i: (i, 0))
    return pl.pallas_call(
        _resid_norm_kernel,
        grid=(n // rows,),
        in_specs=[row, row, pl.BlockSpec((1, d), lambda i: (0, 0))],
        out_specs=row,
        out_shape=jax.ShapeDtypeStruct((n, d), F32),
        compiler_params=_cparams(("parallel",)),
        name="resid_norm",
    )(h, y, gain.reshape(1, d))


def _mm_kernel(a_ref, b_ref, o_ref, *, relu2):
    acc = _dot(a_ref[...], b_ref[...])
    if relu2:
        acc = jnp.square(jnp.maximum(acc, 0.0))
    o_ref[...] = acc.astype(o_ref.dtype)


def _matmul(a, b, out_dtype, tm, tn, relu2=False, name="matmul"):
    m, k = a.shape
    _, n = b.shape
    return pl.pallas_call(
        functools.partial(_mm_kernel, relu2=relu2),
        grid=(m // tm, n // tn),
        in_specs=[pl.BlockSpec((tm, k), lambda i, j: (i, 0)), pl.BlockSpec((k, tn), lambda i, j: (0, j))],
        out_specs=pl.BlockSpec((tm, tn), lambda i, j: (i, j)),
        out_shape=jax.ShapeDtypeStruct((m, n), out_dtype),
        compiler_params=_cparams(("parallel", "arbitrary")),
        name=name,
    )(a, b)


def _mm_kacc_kernel(a_ref, b_ref, o_ref):
    @pl.when(pl.program_id(2) == 0)
    def _():
        o_ref[...] = jnp.zeros_like(o_ref)

    o_ref[...] += _dot(a_ref[...], b_ref[...])


def _matmul_kacc(a, b, tm, tn, tk, name="matmul_kacc"):
    m, k = a.shape
    _, n = b.shape
    return pl.pallas_call(
        _mm_kacc_kernel,
        grid=(m // tm, n // tn, k // tk),
        in_specs=[pl.BlockSpec((tm, tk), lambda i, j, l: (i, l)), pl.BlockSpec((tk, tn), lambda i, j, l: (l, j))],
        out_specs=pl.BlockSpec((tm, tn), lambda i, j, l: (i, j)),
        out_shape=jax.ShapeDtypeStruct((m, n), F32),
        compiler_params=_cparams(("parallel", "parallel", "arbitrary")),
        name=name,
    )(a, b)


def _branch_kernel(yr_ref, yg_ref, wr_ref, wg_ref, gr_ref, gg_ref, o_ref):
    br = _dot(yr_ref[...], wr_ref[...])
    bg = _dot(yg_ref[...], wg_ref[...])
    o_ref[...] = (_sigmoid(gr_ref[...]) * br + _sigmoid(gg_ref[...]) * bg).astype(o_ref.dtype)


def _branch_merge(y_rw, y_gdn, w_rw, w_gdn, p, tm=1024, tn=512):
    m, k = y_rw.shape
    n = w_rw.shape[1]
    jr, jg = OFF_GATE_RW // tn, OFF_GATE_GDN // tn
    return pl.pallas_call(
        _branch_kernel,
        grid=(m // tm, n // tn),
        in_specs=[
            pl.BlockSpec((tm, k), lambda i, j: (i, 0)),
            pl.BlockSpec((tm, k), lambda i, j: (i, 0)),
            pl.BlockSpec((k, tn), lambda i, j: (0, j)),
            pl.BlockSpec((k, tn), lambda i, j: (0, j)),
            pl.BlockSpec((tm, tn), lambda i, j: (i, jr + j)),
            pl.BlockSpec((tm, tn), lambda i, j: (i, jg + j)),
        ],
        out_specs=pl.BlockSpec((tm, tn), lambda i, j: (i, j)),
        out_shape=jax.ShapeDtypeStruct((m, n), BF16),
        compiler_params=_cparams(("parallel", "arbitrary")),
        name="branch_merge",
    )(y_rw, y_gdn, w_rw, w_gdn, p, p)


def _shift_rows(x, prev8, s):
    n = x.shape[0]
    xs = pltpu.roll(x, s, axis=0)
    ps = jnp.tile(pltpu.roll(prev8, s, axis=0), (n // V7X_SUBLANES, 1))
    return jnp.where(_iota(x.shape, 0) < s, ps, xs)


def _expand2(x):
    lane = _iota(x.shape, 1) % V7X_LANES
    return jnp.concatenate([jnp.where(lane < RWKV_HEAD, x, 0.0), jnp.where(lane >= RWKV_HEAD, x, 0.0)], axis=0)


def _rwkv_pair(r, k, v, pr, pk, pv, misc_m, cp, gp, w_up, a_up, g_up, s_ref, u):
    c = CHUNK
    w = V7X_LANES
    mix_r, mix_k, mix_v, w0, a0, k_k, k_a, r_k = (cp[i:i + 1, :] for i in range(8))
    gn_w, gn_b = gp[0:1, :], gp[1:2, :]

    row = _iota((c, w), 0)
    xr = r + (jnp.where(row == 0, pr, pltpu.roll(r, 1, axis=0)) - r) * mix_r
    xk = k + (jnp.where(row == 0, pk, pltpu.roll(k, 1, axis=0)) - k) * mix_k
    xv = v + (jnp.where(row == 0, pv, pltpu.roll(v, 1, axis=0)) - v) * mix_v

    wlin = w0 + _dot(jnp.tanh(misc_m[:, 0:128]), w_up, HI)
    logw = -jnp.exp(-_softplus(-wlin) - 0.5)
    a = _sigmoid(a0 + _dot(misc_m[:, 128:256], a_up, HI))
    gate = _dot(_sigmoid(misc_m[:, 256:512]), g_up, HI)

    li = _iota((w, w), 0)
    lj = _iota((w, w), 1)
    same_head = (li // RWKV_HEAD) == (lj // RWKV_HEAD)
    bones = jnp.where(same_head, 1.0, 0.0)

    kkr = xk * k_k
    kkn = kkr * lax.rsqrt(_dot(kkr * kkr, bones, HI) + L2_EPS)
    k2 = xk * (1.0 + (a - 1.0) * k_a)
    am = -kkn
    bm = kkn * a

    ti = _iota((c, c), 0)
    tj = _iota((c, c), 1)
    cum = _dot(jnp.where(ti >= tj, 1.0, 0.0), logw, HI)
    cum_ex = cum - logw
    c_last = cum[c - 1:c, :]
    c_mid = cum[c // 2 - 1:c // 2, :]
    e_in = jnp.exp(cum - c_mid)
    e_out = jnp.exp(c_mid - cum)
    e_last = jnp.exp(c_last - cum)
    rt = xr * e_in
    at = am * jnp.exp(cum_ex - c_mid)
    bt = bm * e_out
    kt = k2 * e_out
    a_abs = am * jnp.exp(cum_ex)
    r_abs = xr * jnp.exp(cum)
    bh = bm * e_last
    kh = k2 * e_last
    w_c = jnp.exp(c_last)

    aa = _dot_nt(jnp.concatenate([at, rt], axis=0), jnp.concatenate([_expand2(bt), _expand2(kt)], axis=0), HI)
    wi = _iota((c, 2 * c), 0)
    wj = _iota((c, 2 * c), 1) % c
    strict = wi > wj
    incl = wi >= wj
    a_ab = jnp.where(strict, aa[0:c, 0:2 * c], 0.0)
    a_ak = jnp.where(strict, aa[0:c, 2 * c:4 * c], 0.0)
    a_rb = jnp.where(incl, aa[c:2 * c, 0:2 * c], 0.0)
    a_rk = jnp.where(incl, aa[c:2 * c, 2 * c:4 * c], 0.0)

    x = jnp.concatenate([a_abs, _dot(a_ak, _expand2(xv), HI)], axis=1)
    pk_ = a_ab
    steps = int(math.log2(c))
    for it in range(steps):
        x = x + _dot(pk_, _expand2(x), HI)
        if it + 1 < steps:
            pk_ = _dot(pk_, _expand2(pk_), HI)
    wm = x[:, 0:w]
    u0 = x[:, w:2 * w]

    eye = jnp.where(li == lj, 1.0, 0.0)
    m_mat = jnp.where(same_head, _dot_tn(bh, wm, HI), 0.0) + eye * w_c
    n_mat = jnp.where(same_head, _dot_tn(jnp.concatenate([bh, kh], axis=0), jnp.concatenate([u0, xv], axis=0), HI), 0.0)
    p_mat = r_abs + _dot(a_rb, _expand2(wm), HI)
    q_mat = _dot(jnp.concatenate([a_rb, a_rk], axis=1), jnp.concatenate([_expand2(u0), _expand2(xv)], axis=0), HI)

    h0 = s_ref[u]
    y = _dot(p_mat, h0, HI) + q_mat
    s_ref[u] = _dot(m_mat, h0, HI) + n_mat

    inv_n = 1.0 / RWKV_HEAD
    mean = _dot(y, bones, HI) * inv_n
    d = y - mean
    var = _dot(d * d, bones, HI) * inv_n
    yn = d * lax.rsqrt(var + RWKV_GN_EPS) * gn_w + gn_b
    bonus = _dot(xr * k2 * r_k, bones, HI) * xv
    return (yn + bonus) * gate


def _rwkv_kernel(r_ref, k_ref, v_ref, pr_ref, pk_ref, pv_ref, misc_ref, pmisc_ref,
                 cp_ref, gp_ref, mixm_ref, wup_ref, aup_ref, gup_ref, o_ref, s_ref, *, groups):
    ci = pl.program_id(2)

    @pl.when(ci == 0)
    def _():
        s_ref[...] = jnp.zeros_like(s_ref)

    first = ci == 0
    misc = misc_ref[...]
    pm = jnp.where(first, 0.0, pmisc_ref[7:8, :])
    rowm = _iota(misc.shape, 0)
    misc_m = misc + (jnp.where(rowm == 0, pm, pltpu.roll(misc, 1, axis=0)) - misc) * mixm_ref[...]
    for u in range(groups):
        sl = slice(u * V7X_LANES, (u + 1) * V7X_LANES)
        out = _rwkv_pair(
            r_ref[:, sl], k_ref[:, sl], v_ref[:, sl],
            jnp.where(first, 0.0, pr_ref[7:8, sl]), jnp.where(first, 0.0, pk_ref[7:8, sl]),
            jnp.where(first, 0.0, pv_ref[7:8, sl]),
            misc_m, cp_ref[:, sl], gp_ref[:, sl], wup_ref[:, sl], aup_ref[:, sl], gup_ref[:, sl], s_ref, u)
        o_ref[:, sl] = out.astype(o_ref.dtype)


def _rwkv_mix(p, cp, gp, mixm, w_up_p, a_up_p, g_up, batch, seq, lw=256):
    c = CHUNK
    nc = seq // c
    nb = RW // lw
    groups = lw // V7X_LANES
    rows8 = seq // V7X_SUBLANES

    def cur(off):
        return pl.BlockSpec((c, lw), lambda b, h, t: (b * nc + t, off // lw + h))

    def prev(off):
        return pl.BlockSpec((V7X_SUBLANES, lw),
                            lambda b, h, t: (jnp.maximum(b * rows8 + t * (c // V7X_SUBLANES) - 1, 0), off // lw + h))

    def par(rows):
        return pl.BlockSpec((rows, lw), lambda b, h, t: (0, h))

    return pl.pallas_call(
        functools.partial(_rwkv_kernel, groups=groups),
        grid=(batch, nb, nc),
        in_specs=[
            cur(OFF_R), cur(OFF_K), cur(OFF_V), prev(OFF_R), prev(OFF_K), prev(OFF_V),
            pl.BlockSpec((c, MISC_W), lambda b, h, t: (b * nc + t, OFF_MISC // MISC_W)),
            pl.BlockSpec((V7X_SUBLANES, MISC_W),
                         lambda b, h, t: (jnp.maximum(b * rows8 + t * (c // V7X_SUBLANES) - 1, 0), OFF_MISC // MISC_W)),
            par(8), par(8),
            pl.BlockSpec((1, MISC_W), lambda b, h, t: (0, 0)),
            par(128), par(128), par(256),
        ],
        out_specs=pl.BlockSpec((c, lw), lambda b, h, t: (b * nc + t, h)),
        out_shape=jax.ShapeDtypeStruct((batch * seq, RW), BF16),
        scratch_shapes=[pltpu.VMEM((groups, V7X_LANES, V7X_LANES), F32)],
        compiler_params=_cparams(("parallel", "parallel", "arbitrary")),
        name="rwkv7_mix",
    )(p, p, p, p, p, p, p, p, cp, gp, mixm, w_up_p, a_up_p, g_up)


def _gdn_head(q_raw, k_raw, v_raw, pq, pk, pv, cwq, cwk, cwv, z, beta_full, gc_cols, norm_w, hg, s_ref, u):
    c = CHUNK
    w = V7X_LANES

    def conv_silu(x, prev8, cw):
        acc = x * cw[3:4, :]
        for s in (1, 2, 3):
            acc = acc + _shift_rows(x, prev8, s) * cw[3 - s:4 - s, :]
        return acc * _sigmoid(acc)

    q = conv_silu(q_raw, pq, cwq)
    k = conv_silu(k_raw, pk, cwk)
    v = conv_silu(v_raw, pv, cwv)
    q = q * lax.rsqrt(jnp.sum(q * q, axis=-1, keepdims=True) + L2_EPS) * (GDN_HEAD ** -0.5)
    k = k * lax.rsqrt(jnp.sum(k * k, axis=-1, keepdims=True) + L2_EPS)

    sel_b = jnp.where(_iota((w, w), 0) == BETA_LANE + hg, 1.0, 0.0)
    sel_g = jnp.where(_iota((w, w), 0) == ALPHA_LANE + hg, 1.0, 0.0)
    bcol = _dot(beta_full, sel_b, HI)
    gcol = _dot(gc_cols, sel_g, HI)
    grow = _dot_nt(jnp.where(_iota((c, w), 1) == ALPHA_LANE + hg, 1.0, 0.0), gc_cols, HI)

    ti = _iota((c, c), 0)
    tj = _iota((c, c), 1)
    causal = ti >= tj
    strict = ti > tj
    decay = jnp.where(causal, jnp.exp(jnp.where(causal, gcol[:, 0:c] - grow, 0.0)), 0.0)

    kb = k * bcol
    kk = _dot_nt(jnp.concatenate([kb, q], axis=0), k, HI)
    lmat = jnp.where(strict, kk[0:c] * decay, 0.0)
    a_intra = kk[c:2 * c] * decay

    egc = jnp.exp(gcol)
    g_last = gcol[c - 1:c, :]
    x = jnp.concatenate([v * bcol, kb * egc], axis=1)
    pk_ = -lmat
    steps = int(math.log2(c))
    for it in range(steps):
        x = x + _dot(pk_, x, HI)
        if it + 1 < steps:
            pk_ = _dot(pk_, pk_, HI)
    uu = x[:, 0:w]
    ww = x[:, w:2 * w]

    kd = k * jnp.exp(g_last - gcol)
    qd = q * egc
    eye = jnp.where(_iota((w, w), 0) == _iota((w, w), 1), 1.0, 0.0)
    m_mat = eye * jnp.exp(g_last) - _dot_tn(kd, ww, HI)
    n_mat = _dot_tn(kd, uu, HI)
    p_mat = qd - _dot(a_intra, ww, HI)
    q_mat = _dot(a_intra, uu, HI)

    s0 = s_ref[u]
    o = _dot(p_mat, s0, HI) + q_mat
    s_ref[u] = _dot(m_mat, s0, HI) + n_mat

    o = o * lax.rsqrt(jnp.mean(o * o, axis=-1, keepdims=True) + NORM_EPS) * norm_w
    return o * (z * _sigmoid(z))


def _gdn_kernel(q_ref, k_ref, v_ref, pq_ref, pk_ref, pv_ref, z_ref, misc_ref,
                cwq_ref, cwk_ref, cwv_ref, alog_ref, dtb_ref, nw_ref, o_ref, s_ref, *, heads):
    ci = pl.program_id(2)

    @pl.when(ci == 0)
    def _():
        s_ref[...] = jnp.zeros_like(s_ref)

    first = ci == 0
    c = CHUNK
    bm = misc_ref[:, 0:128]
    beta_full = _sigmoid(bm)
    g_full = -jnp.exp(alog_ref[...]) * _softplus(bm + dtb_ref[...])
    gc_cols = _dot(jnp.where(_iota((c, c), 0) >= _iota((c, c), 1), 1.0, 0.0), g_full, HI)
    for u in range(heads):
        sl = slice(u * V7X_LANES, (u + 1) * V7X_LANES)
        hg = pl.program_id(1) * heads + u
        out = _gdn_head(
            q_ref[:, sl], k_ref[:, sl], v_ref[:, sl],
            jnp.where(first, 0.0, pq_ref[:, sl]), jnp.where(first, 0.0, pk_ref[:, sl]),
            jnp.where(first, 0.0, pv_ref[:, sl]),
            cwq_ref[:, sl], cwk_ref[:, sl], cwv_ref[:, sl], z_ref[:, sl],
            beta_full, gc_cols, nw_ref[...], hg, s_ref, u)
        o_ref[:, sl] = out.astype(o_ref.dtype)


def _gdn_mix(p, conv_w, alog_l, dtb_l, norm_w, batch, seq, lw=256):
    c = CHUNK
    nc = seq // c
    nb = RW // lw
    heads = lw // V7X_LANES
    rows8 = seq // V7X_SUBLANES

    def cur(off):
        return pl.BlockSpec((c, lw), lambda b, h, t: (b * nc + t, off // lw + h))

    def prev(off):
        return pl.BlockSpec((V7X_SUBLANES, lw),
                            lambda b, h, t: (jnp.maximum(b * rows8 + t * (c // V7X_SUBLANES) - 1, 0), off // lw + h))

    def cw(off):
        return pl.BlockSpec((4, lw), lambda b, h, t: (0, off // lw + h))

    lane_vec = pl.BlockSpec((1, V7X_LANES), lambda b, h, t: (0, 0))
    return pl.pallas_call(
        functools.partial(_gdn_kernel, heads=heads),
        grid=(batch, nb, nc),
        in_specs=[
            cur(OFF_GQ), cur(OFF_GK), cur(OFF_GV), prev(OFF_GQ), prev(OFF_GK), prev(OFF_GV), cur(OFF_Z),
            pl.BlockSpec((c, MISC_W), lambda b, h, t: (b * nc + t, OFF_MISC // MISC_W)),
            cw(0), cw(RW), cw(2 * RW), lane_vec, lane_vec, lane_vec,
        ],
        out_specs=pl.BlockSpec((c, lw), lambda b, h, t: (b * nc + t, h)),
        out_shape=jax.ShapeDtypeStruct((batch * seq, RW), BF16),
        scratch_shapes=[pltpu.VMEM((heads, V7X_LANES, V7X_LANES), F32)],
        compiler_params=_cparams(("parallel", "parallel", "arbitrary")),
        name="gdn_mix",
    )(p, p, p, p, p, p, p, p, conv_w, conv_w, conv_w, alog_l, dtb_l, norm_w)


def _xattn_kernel(h_ref, k_ref, v_ref, wq_ref, wo_ref, gpre_ref, gpost_ref, gnext_ref, h_out_ref, f_out_ref):
    h = h_ref[...]
    cn = (h * lax.rsqrt(jnp.mean(h * h, axis=-1, keepdims=True) + NORM_EPS) * gpre_ref[...]).astype(BF16)
    q = _dot(cn, wq_ref[...]).astype(BF16)
    kk = k_ref[0]
    vv = v_ref[0]
    outs = []
    for hh in range(XA_HEADS):
        sl = slice(hh * XA_HEAD, (hh + 1) * XA_HEAD)
        s = _dot_nt(q[:, sl], kk[:, sl]) * (XA_HEAD ** -0.5)
        s = s - jnp.max(s, axis=-1, keepdims=True)
        e = jnp.exp(s)
        pr = e / jnp.sum(e, axis=-1, keepdims=True)
        outs.append(_dot(pr.astype(BF16), vv[:, sl]))
    o = jnp.concatenate(outs, axis=1).astype(BF16)
    y = _dot(o, wo_ref[...])
    h2 = h + y * lax.rsqrt(jnp.mean(y * y, axis=-1, keepdims=True) + NORM_EPS) * gpost_ref[...]
    h_out_ref[...] = h2
    f_out_ref[...] = (h2 * lax.rsqrt(jnp.mean(h2 * h2, axis=-1, keepdims=True) + NORM_EPS)
                      * gnext_ref[...]).astype(f_out_ref.dtype)


def _xattn_block(h, kmem, vmem, w_q, w_o, g_pre, g_post, g_next, batch, seq, tt=256):
    n, d = h.shape
    mlen, xw = kmem.shape[1], kmem.shape[2]
    per_b = seq // tt
    row = pl.BlockSpec((tt, d), lambda i: (i, 0))
    vec = pl.BlockSpec((1, d), lambda i: (0, 0))
    mem = pl.BlockSpec((1, mlen, xw), lambda i: (i // per_b, 0, 0))
    return pl.pallas_call(
        _xattn_kernel,
        grid=(n // tt,),
        in_specs=[row, mem, mem, pl.BlockSpec((d, xw), lambda i: (0, 0)), pl.BlockSpec((xw, d), lambda i: (0, 0)),
                  vec, vec, vec],
        out_specs=[row, row],
        out_shape=[jax.ShapeDtypeStruct((n, d), F32), jax.ShapeDtypeStruct((n, d), BF16)],
        compiler_params=_cparams(("parallel",)),
        name="xattn_block",
    )(h, kmem, vmem, w_q, w_o, g_pre.reshape(1, d), g_post.reshape(1, d), g_next.reshape(1, d))


def _relayout_w_in(w_in):
    d = w_in.shape[0]
    o = 3 * RW
    lw_c = w_in[:, o:o + LORA]
    la_c = w_in[:, o + LORA:o + 2 * LORA]
    lg_c = w_in[:, o + 2 * LORA:o + 2 * LORA + 256]
    g0 = o + 2 * LORA + 256
    qkv = w_in[:, g0:g0 + 3 * RW]
    z = w_in[:, g0 + 3 * RW:g0 + 4 * RW]
    beta = w_in[:, g0 + 4 * RW:g0 + 4 * RW + N_GDN_HEADS]
    alpha = w_in[:, g0 + 4 * RW + N_GDN_HEADS:g0 + 4 * RW + 2 * N_GDN_HEADS]
    gates = w_in[:, g0 + 4 * RW + 2 * N_GDN_HEADS:]
    pad = jnp.zeros((d, 256 - 128 - LORA), w_in.dtype)
    return jnp.concatenate([w_in[:, :o], lw_c, beta, alpha, la_c, pad, lg_c, qkv, z, gates], axis=1).astype(BF16)


def _layer(x, mem, mix_norm_pre, mix_norm_post, w_in, rwkv_shift_mix, rwkv_w0, rwkv_w_up, rwkv_a0, rwkv_a_up,
           rwkv_g_up, rwkv_k_k, rwkv_k_a, rwkv_r_k, rwkv_gn_w, rwkv_gn_b, gdn_conv_w, gdn_a_log, gdn_dt_bias,
           gdn_norm_w, w_branch_rwkv, w_branch_gdn, w_mix_out, xa_norm_pre, xa_norm_mem, xa_norm_post, xa_w_q,
           xa_w_kv, xa_w_o, mlp_norm_pre, mlp_norm_post, mlp_w_up, mlp_w_down):
    batch, seq, d = x.shape
    n = batch * seq
    h = x.reshape(n, d)

    w_in_b = _relayout_w_in(w_in)
    mix = rwkv_shift_mix
    o = 3 * RW
    cp = jnp.stack([mix[0:RW], mix[RW:2 * RW], mix[2 * RW:o], rwkv_w0, rwkv_a0, rwkv_k_k, rwkv_k_a,
                    rwkv_r_k.reshape(RW)], axis=0)
    gp = jnp.concatenate([rwkv_gn_w[None], rwkv_gn_b[None], jnp.zeros((6, RW), F32)], axis=0)
    zeros = lambda k: jnp.zeros((k,), F32)
    mixm = jnp.concatenate([mix[o:o + LORA], zeros(32), mix[o + LORA:o + 2 * LORA], zeros(32),
                            mix[o + 2 * LORA:o + 2 * LORA + 256]])[None]
    w_up_p = jnp.concatenate([rwkv_w_up, jnp.zeros((128 - LORA, RW), F32)], axis=0)
    a_up_p = jnp.concatenate([rwkv_a_up, jnp.zeros((128 - LORA, RW), F32)], axis=0)
    alog_l = jnp.concatenate([zeros(ALPHA_LANE), gdn_a_log])[None]
    dtb_l = jnp.concatenate([zeros(ALPHA_LANE), gdn_dt_bias])[None]

    u = _rmsnorm(h, mix_norm_pre, BF16)
    p = _matmul(u, w_in_b, F32, tm=1024, tn=512, name="in_proj")
    y_rw = _rwkv_mix(p, cp, gp, mixm, w_up_p, a_up_p, rwkv_g_up, batch, seq)
    y_gdn = _gdn_mix(p, gdn_conv_w, alog_l, dtb_l, gdn_norm_w[None], batch, seq)
    merged = _branch_merge(y_rw, y_gdn, w_branch_rwkv.astype(BF16), w_branch_gdn.astype(BF16), p)
    y_mix = _matmul(merged, w_mix_out.astype(BF16), F32, tm=1024, tn=512, name="mix_out")
    h = _resid_norm(h, y_mix, mix_norm_post)

    mlen = mem.shape[1]
    m = _rmsnorm(mem.reshape(batch * mlen, d), xa_norm_mem, BF16)
    kv = _matmul(m, xa_w_kv.astype(BF16), BF16, tm=batch * mlen, tn=512, name="xa_kv")
    xw = XA_HEADS * XA_HEAD
    kmem = kv[:, :xw].reshape(batch, mlen, xw)
    vmem = kv[:, xw:].reshape(batch, mlen, xw)
    h, f = _xattn_block(h, kmem, vmem, xa_w_q.astype(BF16), xa_w_o.astype(BF16), xa_norm_pre, xa_norm_post,
                        mlp_norm_pre, batch, seq)

    hmid = _matmul(f, mlp_w_up.astype(BF16), BF16, tm=1024, tn=512, relu2=True, name="mlp_up")
    y_mlp = _matmul_kacc(hmid, mlp_w_down.astype(BF16), tm=1024, tn=1024, tk=2048, name="mlp_down")
    h = _resid_norm(h, y_mlp, mlp_norm_post)
    return h.reshape(batch, seq, d)


def kernel(x, mem, mix_norm_pre, mix_norm_post, w_in, rwkv_shift_mix, rwkv_w0, rwkv_w_up, rwkv_a0, rwkv_a_up,
           rwkv_g_up, rwkv_k_k, rwkv_k_a, rwkv_r_k, rwkv_gn_w, rwkv_gn_b, gdn_conv_w, gdn_a_log, gdn_dt_bias,
           gdn_norm_w, w_branch_rwkv, w_branch_gdn, w_mix_out, xa_norm_pre, xa_norm_mem, xa_norm_post, xa_w_q,
           xa_w_kv, xa_w_o, mlp_norm_pre, mlp_norm_post, mlp_w_up, mlp_w_down):
    args = (mix_norm_pre, mix_norm_post, w_in, rwkv_shift_mix, rwkv_w0, rwkv_w_up, rwkv_a0, rwkv_a_up,
            rwkv_g_up, rwkv_k_k, rwkv_k_a, rwkv_r_k, rwkv_gn_w, rwkv_gn_b, gdn_conv_w, gdn_a_log, gdn_dt_bias,
            gdn_norm_w, w_branch_rwkv, w_branch_gdn, w_mix_out, xa_norm_pre, xa_norm_mem, xa_norm_post, xa_w_q,
            xa_w_kv, xa_w_o, mlp_norm_pre, mlp_norm_post, mlp_w_up, mlp_w_down)
    h = x
    for layer in range(mix_norm_pre.shape[0]):
        h = _layer(h, mem, *(a[layer] for a in args))
    return h
```

```python
import functools
import math

import jax
import jax.numpy as jnp
from jax import lax
from jax.experimental import pallas as pl
from jax.experimental.pallas import tpu as pltpu

F32 = jnp.float32
BF16 = jnp.bfloat16

V7X_LANES = 128
V7X_SUBLANES = 8
V7X_VMEM_LIMIT_BYTES = 56 * 1024 * 1024

NORM_EPS = 1e-6
L2_EPS = 1e-6
RWKV_GN_EPS = 64e-5
RWKV_HEAD = 64
GDN_HEAD = 128
CHUNK = 64
XA_HEADS = 4
XA_HEAD = 128

RW = 2048
OFF_R, OFF_K, OFF_V = 0, RW, 2 * RW
OFF_GQ, OFF_GK, OFF_GV = 3 * RW, 4 * RW, 5 * RW
OFF_Z = 6 * RW
OFF_MISC = 7 * RW
MISC_W = 512
OFF_GATE_RW = OFF_MISC + MISC_W
D_MODEL = 4096
OFF_GATE_GDN = OFF_GATE_RW + D_MODEL
NP = OFF_GATE_GDN + D_MODEL
LORA = 96
N_GDN_HEADS = RW // GDN_HEAD
BETA_LANE = LORA
ALPHA_LANE = LORA + N_GDN_HEADS


def _cparams(sem):
    return pltpu.CompilerParams(dimension_semantics=sem, vmem_limit_bytes=V7X_VMEM_LIMIT_BYTES)


def _dot(a, b):
    return jnp.dot(a, b, preferred_element_type=F32)


def _dot_nt(a, b):
    return lax.dot_general(a, b, (((1,), (1,)), ((), ())), preferred_element_type=F32)


def _dot_tn(a, b):
    return lax.dot_general(a, b, (((0,), (0,)), ((), ())), preferred_element_type=F32)


def _bf(x):
    return x.astype(BF16)


def _hi_lo(x):
    hi = x.astype(BF16)
    return hi, (x - hi.astype(F32)).astype(BF16)


def _mm(a, b):
    return _dot(_bf(a), _bf(b))


def _mm_nt(a, b):
    return _dot_nt(_bf(a), _bf(b))


def _mm_tn(a, b):
    return _dot_tn(_bf(a), _bf(b))


def _mm_xr(a, b01x2):
    hi, lo = _hi_lo(a)
    return _dot(jnp.concatenate([hi, lo], axis=1), b01x2)


def _mm_xl(a01x2, b):
    hi, lo = _hi_lo(b)
    return _dot(a01x2, jnp.concatenate([hi, lo], axis=0))


def _sigmoid(x):
    return 1.0 / (1.0 + jnp.exp(-x))


def _softplus(x):
    return jnp.maximum(x, 0.0) + jnp.log1p(jnp.exp(-jnp.abs(x)))


def _iota(shape, dim):
    return lax.broadcasted_iota(jnp.int32, shape, dim)


def _lane_groups(x):
    return [x[:, g * V7X_LANES:(g + 1) * V7X_LANES] for g in range(x.shape[1] // V7X_LANES)]


def _tril_ones_x2(c):
    return jnp.where(_iota((c, 2 * c), 0) >= _iota((c, 2 * c), 1) % c, 1.0, 0.0).astype(BF16)


def _rmsnorm_kernel(x_ref, g_ref, o_ref):
    x = x_ref[...]
    inv = lax.rsqrt(jnp.mean(x * x, axis=-1, keepdims=True) + NORM_EPS)
    o_ref[...] = (x * inv * g_ref[...]).astype(o_ref.dtype)


def _rmsnorm(x, gain, out_dtype, rows=256):
    n, d = x.shape
    return pl.pallas_call(
        _rmsnorm_kernel,
        grid=(n // rows,),
        in_specs=[pl.BlockSpec((rows, d), lambda i: (i, 0)), pl.BlockSpec((1, d), lambda i: (0, 0))],
        out_specs=pl.BlockSpec((rows, d), lambda i: (i, 0)),
        out_shape=jax.ShapeDtypeStruct((n, d), out_dtype),
        compiler_params=_cparams(("parallel",)),
        name="rmsnorm",
    )(x, gain.reshape(1, d))


def _resid_norm_kernel(h_ref, y_ref, g_ref, o_ref):
    y = y_ref[...]
    inv = lax.rsqrt(jnp.mean(y * y, axis=-1, keepdims=True) + NORM_EPS)
    o_ref[...] = h_ref[...] + y * inv * g_ref[...]


def _resid_norm(h, y, gain, rows=256):
    n, d = h.shape
    row = pl.BlockSpec((rows, d), lambda i: (i, 0))
    return pl.pallas_call(
        _resid_norm_kernel,
        grid=(n // rows,),
        in_specs=[row, row, pl.BlockSpec((1, d), lambda i: (0, 0))],
        out_specs=row,
        out_shape=jax.ShapeDtypeStruct((n, d), F32),
        compiler_params=_cparams(("parallel",)),
        name="resid_norm",
    )(h, y, gain.reshape(1, d))


def _mm_kernel(a_ref, b_ref, o_ref, *, relu2):
    acc = _dot(a_ref[...], b_ref[...])
    if relu2:
        acc = jnp.square(jnp.maximum(acc, 0.0))
    o_ref[...] = acc.astype(o_ref.dtype)


def _matmul(a, b, out_dtype, tm, tn, relu2=False, name="matmul"):
    m, k = a.shape
    _, n = b.shape
    return pl.pallas_call(
        functools.partial(_mm_kernel, relu2=relu2),
        grid=(m // tm, n // tn),
        in_specs=[pl.BlockSpec((tm, k), lambda i, j: (i, 0)), pl.BlockSpec((k, tn), lambda i, j: (0, j))],
        out_specs=pl.BlockSpec((tm, tn), lambda i, j: (i, j)),
        out_shape=jax.ShapeDtypeStruct((m, n), out_dtype),
        compiler_params=_cparams(("parallel", "arbitrary")),
        name=name,
    )(a, b)


def _mm_kacc_kernel(a_ref, b_ref, o_ref):
    @pl.when(pl.program_id(2) == 0)
    def _():
        o_ref[...] = jnp.zeros_like(o_ref)

    o_ref[...] += _dot(a_ref[...], b_ref[...])


def _matmul_kacc(a, b, tm, tn, tk, name="matmul_kacc"):
    m, k = a.shape
    _, n = b.shape
    return pl.pallas_call(
        _mm_kacc_kernel,
        grid=(m // tm, n // tn, k // tk),
        in_specs=[pl.BlockSpec((tm, tk), lambda i, j, l: (i, l)), pl.BlockSpec((tk, tn), lambda i, j, l: (l, j))],
        out_specs=pl.BlockSpec((tm, tn), lambda i, j, l: (i, j)),
        out_shape=jax.ShapeDtypeStruct((m, n), F32),
        compiler_params=_cparams(("parallel", "parallel", "arbitrary")),
        name=name,
    )(a, b)


def _branch_kernel(yr_ref, yg_ref, wr_ref, wg_ref, gr_ref, gg_ref, o_ref):
    br = _dot(yr_ref[...], wr_ref[...])
    bg = _dot(yg_ref[...], wg_ref[...])
    o_ref[...] = (_sigmoid(gr_ref[...]) * br + _sigmoid(gg_ref[...]) * bg).astype(o_ref.dtype)


def _branch_merge(y_rw, y_gdn, w_rw, w_gdn, p, tm=1024, tn=512):
    m, k = y_rw.shape
    n = w_rw.shape[1]
    jr, jg = OFF_GATE_RW // tn, OFF_GATE_GDN // tn
    return pl.pallas_call(
        _branch_kernel,
        grid=(m // tm, n // tn),
        in_specs=[
            pl.BlockSpec((tm, k), lambda i, j: (i, 0)),
            pl.BlockSpec((tm, k), lambda i, j: (i, 0)),
            pl.BlockSpec((k, tn), lambda i, j: (0, j)),
            pl.BlockSpec((k, tn), lambda i, j: (0, j)),
            pl.BlockSpec((tm, tn), lambda i, j: (i, jr + j)),
            pl.BlockSpec((tm, tn), lambda i, j: (i, jg + j)),
        ],
        out_specs=pl.BlockSpec((tm, tn), lambda i, j: (i, j)),
        out_shape=jax.ShapeDtypeStruct((m, n), BF16),
        compiler_params=_cparams(("parallel", "arbitrary")),
        name="branch_merge",
    )(y_rw, y_gdn, w_rw, w_gdn, p, p)


def _shift_rows(x, prev8, s):
    n = x.shape[0]
    xs = pltpu.roll(x, s, axis=0)
    ps = jnp.tile(pltpu.roll(prev8, s, axis=0), (n // V7X_SUBLANES, 1))
    return jnp.where(_iota(x.shape, 0) < s, ps, xs)


def _token_shift(x, prev_row, mix):
    prev = jnp.where(_iota(x.shape, 0) == 0, prev_row, pltpu.roll(x, 1, axis=0))
    return x + (prev - x) * mix


def _expand2(x, head0):
    return jnp.concatenate([jnp.where(head0, x, 0.0), jnp.where(head0, 0.0, x)], axis=0)


def _rwkv_kernel(r_ref, k_ref, v_ref, pr_ref, pk_ref, pv_ref, misc_ref, pmisc_ref,
                 cp_ref, gp_ref, mixm_ref, wup_ref, aup_ref, gup_ref, o_ref, s_ref):
    c = CHUNK
    w = V7X_LANES
    ci = pl.program_id(2)

    @pl.when(ci == 0)
    def _():
        s_ref[...] = jnp.zeros_like(s_ref)

    first = ci == 0
    prev_row = lambda ref: jnp.where(first, 0.0, ref[7:8, :])
    mix_r, mix_k, mix_v, w0, a0, k_k, k_a, r_k = (cp_ref[i:i + 1, :] for i in range(8))
    gn_w, gn_b = gp_ref[0:1, :], gp_ref[1:2, :]

    misc_m = _token_shift(misc_ref[...], prev_row(pmisc_ref), mixm_ref[...])
    xr = _token_shift(r_ref[...], prev_row(pr_ref), mix_r)
    xk = _token_shift(k_ref[...], prev_row(pk_ref), mix_k)
    xv = _token_shift(v_ref[...], prev_row(pv_ref), mix_v)

    wlin = w0 + _dot(_bf(jnp.tanh(misc_m[:, 0:128])), wup_ref[...])
    logw = -jnp.exp(-_softplus(-wlin) - 0.5)
    a = _sigmoid(a0 + _dot(_bf(misc_m[:, 128:256]), aup_ref[...]))
    gate = _dot(_bf(_sigmoid(misc_m[:, 256:512])), gup_ref[...])

    li = _iota((2 * w, w), 0) % w
    lj = _iota((2 * w, w), 1)
    bones2 = jnp.where((li // RWKV_HEAD) == (lj // RWKV_HEAD), 1.0, 0.0).astype(BF16)
    same_head = (_iota((w, w), 0) // RWKV_HEAD) == (_iota((w, w), 1) // RWKV_HEAD)
    eye = jnp.where(_iota((w, w), 0) == _iota((w, w), 1), 1.0, 0.0)

    def head_sums(x):
        return jnp.concatenate([_mm_xr(g, bones2) for g in _lane_groups(x)], axis=1)

    kkr = xk * k_k
    kkn = kkr * lax.rsqrt(head_sums(kkr * kkr) + L2_EPS)
    k2 = xk * (1.0 + (a - 1.0) * k_a)
    am = -kkn
    bm = kkn * a

    cum = _mm_xl(_tril_ones_x2(c), logw)
    cum_ex = cum - logw
    c_last = cum[c - 1:c, :]
    c_mid = cum[c // 2 - 1:c // 2, :]
    e_out = jnp.exp(c_mid - cum)
    e_last = jnp.exp(c_last - cum)
    rt = xr * jnp.exp(cum - c_mid)
    at = am * jnp.exp(cum_ex - c_mid)
    bt = bm * e_out
    kt = k2 * e_out
    a_abs = am * jnp.exp(cum_ex)
    r_abs = xr * jnp.exp(cum)
    bh = bm * e_last
    kh = k2 * e_last
    w_c = jnp.exp(c_last)

    head0 = _iota((c, w), 1) < RWKV_HEAD
    head0_2 = _iota((c, 2 * w), 1) % w < RWKV_HEAD
    wi = _iota((c, 2 * c), 0)
    wj = _iota((c, 2 * c), 1) % c
    strict = wi > wj
    incl = wi >= wj

    gs = range(r_ref.shape[1] // w)
    G = _lane_groups
    at_g, rt_g, bt_g, kt_g, xv_g = G(at), G(rt), G(bt), G(kt), G(xv)
    aa = [_mm_nt(jnp.concatenate([at_g[g], rt_g[g]], axis=0),
                 jnp.concatenate([_expand2(bt_g[g], head0), _expand2(kt_g[g], head0)], axis=0)) for g in gs]
    a_ab = [jnp.where(strict, aa[g][0:c, 0:2 * c], 0.0) for g in gs]
    a_ak = [jnp.where(strict, aa[g][0:c, 2 * c:4 * c], 0.0) for g in gs]
    a_rb = [jnp.where(incl, aa[g][c:2 * c, 0:2 * c], 0.0) for g in gs]
    a_rk = [jnp.where(incl, aa[g][c:2 * c, 2 * c:4 * c], 0.0) for g in gs]

    xv_e = [_expand2(xv_g[g], head0) for g in gs]
    akv = [_mm(a_ak[g], xv_e[g]) for g in gs]
    a_abs_g = G(a_abs)
    x = [jnp.concatenate([a_abs_g[g], akv[g]], axis=1) for g in gs]
    pw = a_ab
    steps = int(math.log2(c))
    for it in range(steps):
        x = [x[g] + _mm(pw[g], _expand2(x[g], head0_2)) for g in gs]
        if it + 1 < steps:
            pw = [_mm(pw[g], _expand2(pw[g], head0)) for g in gs]
    wm = [x[g][:, 0:w] for g in gs]
    u0 = [x[g][:, w:2 * w] for g in gs]

    bh_g, kh_g, wc_g, rabs_g = G(bh), G(kh), G(w_c), G(r_abs)
    m_mat = [jnp.where(same_head, _mm_tn(bh_g[g], wm[g]), 0.0) + eye * wc_g[g] for g in gs]
    n_mat = [jnp.where(same_head, _mm_tn(jnp.concatenate([bh_g[g], kh_g[g]], axis=0),
                                         jnp.concatenate([u0[g], xv_g[g]], axis=0)), 0.0) for g in gs]
    p_mat = [rabs_g[g] + _mm(a_rb[g], _expand2(wm[g], head0)) for g in gs]
    q_mat = [_mm(jnp.concatenate([a_rb[g], a_rk[g]], axis=1),
                 jnp.concatenate([_expand2(u0[g], head0), xv_e[g]], axis=0)) for g in gs]

    h0 = [_bf(s_ref[g]) for g in gs]
    y = jnp.concatenate([_dot(_bf(p_mat[g]), h0[g]) + q_mat[g] for g in gs], axis=1)
    for g in gs:
        s_ref[g] = _dot(_bf(m_mat[g]), h0[g]) + n_mat[g]

    inv_n = 1.0 / RWKV_HEAD
    d = y - head_sums(y) * inv_n
    var = head_sums(d * d) * inv_n
    yn = d * lax.rsqrt(var + RWKV_GN_EPS) * gn_w + gn_b
    bonus = head_sums(xr * k2 * r_k) * xv
    o_ref[...] = ((yn + bonus) * gate).astype(o_ref.dtype)


def _rwkv_mix(p, cp, gp, mixm, w_up_p, a_up_p, g_up, batch, seq, lw=1024):
    c = CHUNK
    nc = seq // c
    nb = RW // lw
    rows8 = seq // V7X_SUBLANES

    def cur(off):
        assert off % lw == 0
        return pl.BlockSpec((c, lw), lambda b, h, t: (b * nc + t, off // lw + h))

    def prev(off):
        return pl.BlockSpec((V7X_SUBLANES, lw),
                            lambda b, h, t: (jnp.maximum(b * rows8 + t * (c // V7X_SUBLANES) - 1, 0), off // lw + h))

    def par(rows):
        return pl.BlockSpec((rows, lw), lambda b, h, t: (0, h))

    return pl.pallas_call(
        _rwkv_kernel,
        grid=(batch, nb, nc),
        in_specs=[
            cur(OFF_R), cur(OFF_K), cur(OFF_V), prev(OFF_R), prev(OFF_K), prev(OFF_V),
            pl.BlockSpec((c, MISC_W), lambda b, h, t: (b * nc + t, OFF_MISC // MISC_W)),
            pl.BlockSpec((V7X_SUBLANES, MISC_W),
                         lambda b, h, t: (jnp.maximum(b * rows8 + t * (c // V7X_SUBLANES) - 1, 0), OFF_MISC // MISC_W)),
            par(8), par(8),
            pl.BlockSpec((1, MISC_W), lambda b, h, t: (0, 0)),
            par(128), par(128), par(256),
        ],
        out_specs=pl.BlockSpec((c, lw), lambda b, h, t: (b * nc + t, h)),
        out_shape=jax.ShapeDtypeStruct((batch * seq, RW), BF16),
        scratch_shapes=[pltpu.VMEM((lw // V7X_LANES, V7X_LANES, V7X_LANES), F32)],
        compiler_params=_cparams(("parallel", "parallel", "arbitrary")),
        name="rwkv7_mix",
    )(p, p, p, p, p, p, p, p, cp, gp, mixm, w_up_p, a_up_p, g_up)


def _gdn_kernel(q_ref, k_ref, v_ref, pq_ref, pk_ref, pv_ref, z_ref, misc_ref,
                cwq_ref, cwk_ref, cwv_ref, alog_ref, dtb_ref, nw_ref, o_ref, s_ref):
    c = CHUNK
    w = V7X_LANES
    ci = pl.program_id(2)
    heads = q_ref.shape[1] // w
    hs = range(heads)
    G = _lane_groups

    @pl.when(ci == 0)
    def _():
        s_ref[...] = jnp.zeros_like(s_ref)

    first = ci == 0

    def conv_silu(x_ref, prev_ref, cw_ref):
        x = x_ref[...]
        prev8 = jnp.where(first, 0.0, prev_ref[...])
        acc = x * cw_ref[3:4, :]
        for s in (1, 2, 3):
            acc = acc + _shift_rows(x, prev8, s) * cw_ref[3 - s:4 - s, :]
        return acc * _sigmoid(acc)

    def head_sums(x):
        return jnp.concatenate([jnp.broadcast_to(jnp.sum(g, axis=-1, keepdims=True), g.shape) for g in G(x)], axis=1)

    q = conv_silu(q_ref, pq_ref, cwq_ref)
    k = conv_silu(k_ref, pk_ref, cwk_ref)
    v = conv_silu(v_ref, pv_ref, cwv_ref)
    q = q * lax.rsqrt(head_sums(q * q) + L2_EPS) * (GDN_HEAD ** -0.5)
    k = k * lax.rsqrt(head_sums(k * k) + L2_EPS)

    bm = misc_ref[:, 0:128]
    beta_hl = jnp.concatenate(_hi_lo(_sigmoid(bm)), axis=1)
    g_full = -jnp.exp(alog_ref[...]) * _softplus(bm + dtb_ref[...])
    gc_hl = jnp.concatenate(_hi_lo(_mm_xl(_tril_ones_x2(c), g_full)), axis=1)

    hg0 = pl.program_id(1) * heads
    lane2 = _iota((2 * w, w), 0) % w
    lane2r = _iota((c, 2 * w), 1) % w
    bcol = [_dot(beta_hl, jnp.where(lane2 == BETA_LANE + hg0 + h, 1.0, 0.0).astype(BF16)) for h in hs]
    gcol = [_dot(gc_hl, jnp.where(lane2 == ALPHA_LANE + hg0 + h, 1.0, 0.0).astype(BF16)) for h in hs]
    grow = [_dot_nt(jnp.where(lane2r == ALPHA_LANE + hg0 + h, 1.0, 0.0).astype(BF16), gc_hl) for h in hs]

    ti = _iota((c, c), 0)
    tj = _iota((c, c), 1)
    causal = ti >= tj
    strict = ti > tj
    decay = [jnp.where(causal, jnp.exp(jnp.where(causal, gcol[h][:, 0:c] - grow[h], 0.0)), 0.0) for h in hs]

    q_g, k_g, v_g = G(q), G(k), G(v)
    kb = [k_g[h] * bcol[h] for h in hs]
    kk = [_mm_nt(jnp.concatenate([kb[h], q_g[h]], axis=0), k_g[h]) for h in hs]
    lneg = [jnp.where(strict, -kk[h][0:c] * decay[h], 0.0) for h in hs]
    a_intra = [kk[h][c:2 * c] * decay[h] for h in hs]

    egc = [jnp.exp(gcol[h]) for h in hs]
    g_last = [gcol[h][c - 1:c, :] for h in hs]
    x = [jnp.concatenate([v_g[h] * bcol[h], kb[h] * egc[h]], axis=1) for h in hs]
    pw = lneg
    steps = int(math.log2(c))
    for it in range(steps):
        x = [x[h] + _mm(pw[h], x[h]) for h in hs]
        if it + 1 < steps:
            pw = [_mm(pw[h], pw[h]) for h in hs]
    uu = [x[h][:, 0:w] for h in hs]
    ww = [x[h][:, w:2 * w] for h in hs]

    kd = [k_g[h] * jnp.exp(g_last[h] - gcol[h]) for h in hs]
    eye = jnp.where(_iota((w, w), 0) == _iota((w, w), 1), 1.0, 0.0)
    m_mat = [eye * jnp.exp(g_last[h]) - _mm_tn(kd[h], ww[h]) for h in hs]
    n_mat = [_mm_tn(kd[h], uu[h]) for h in hs]
    p_mat = [q_g[h] * egc[h] - _mm(a_intra[h], ww[h]) for h in hs]
    q_mat = [_mm(a_intra[h], uu[h]) for h in hs]

    s0 = [_bf(s_ref[h]) for h in hs]
    o = jnp.concatenate([_dot(_bf(p_mat[h]), s0[h]) + q_mat[h] for h in hs], axis=1)
    for h in hs:
        s_ref[h] = _dot(_bf(m_mat[h]), s0[h]) + n_mat[h]

    o = o * lax.rsqrt(head_sums(o * o) * (1.0 / GDN_HEAD) + NORM_EPS) * jnp.tile(nw_ref[...], (1, heads))
    z = z_ref[...]
    o_ref[...] = (o * (z * _sigmoid(z))).astype(o_ref.dtype)


def _gdn_mix(p, conv_w, alog_l, dtb_l, norm_w, batch, seq, lw=1024):
    c = CHUNK
    nc = seq // c
    nb = RW // lw
    rows8 = seq // V7X_SUBLANES

    def cur(off):
        assert off % lw == 0
        return pl.BlockSpec((c, lw), lambda b, h, t: (b * nc + t, off // lw + h))

    def prev(off):
        return pl.BlockSpec((V7X_SUBLANES, lw),
                            lambda b, h, t: (jnp.maximum(b * rows8 + t * (c // V7X_SUBLANES) - 1, 0), off // lw + h))

    def cw(off):
        return pl.BlockSpec((4, lw), lambda b, h, t: (0, off // lw + h))

    lane_vec = pl.BlockSpec((1, V7X_LANES), lambda b, h, t: (0, 0))
    return pl.pallas_call(
        _gdn_kernel,
        grid=(batch, nb, nc),
        in_specs=[
            cur(OFF_GQ), cur(OFF_GK), cur(OFF_GV), prev(OFF_GQ), prev(OFF_GK), prev(OFF_GV), cur(OFF_Z),
            pl.BlockSpec((c, MISC_W), lambda b, h, t: (b * nc + t, OFF_MISC // MISC_W)),
            cw(0), cw(RW), cw(2 * RW), lane_vec, lane_vec, lane_vec,
        ],
        out_specs=pl.BlockSpec((c, lw), lambda b, h, t: (b * nc + t, h)),
        out_shape=jax.ShapeDtypeStruct((batch * seq, RW), BF16),
        scratch_shapes=[pltpu.VMEM((lw // V7X_LANES, V7X_LANES, V7X_LANES), F32)],
        compiler_params=_cparams(("parallel", "parallel", "arbitrary")),
        name="gdn_mix",
    )(p, p, p, p, p, p, p, p, conv_w, conv_w, conv_w, alog_l, dtb_l, norm_w)


def _xattn_kernel(h_ref, k_ref, v_ref, wq_ref, wo_ref, gpre_ref, gpost_ref, gnext_ref, h_out_ref, f_out_ref):
    h = h_ref[...]
    cn = (h * lax.rsqrt(jnp.mean(h * h, axis=-1, keepdims=True) + NORM_EPS) * gpre_ref[...]).astype(BF16)
    q = _dot(cn, wq_ref[...]).astype(BF16)
    kk = k_ref[0]
    vv = v_ref[0]
    outs = []
    for hh in range(XA_HEADS):
        sl = slice(hh * XA_HEAD, (hh + 1) * XA_HEAD)
        s = _dot_nt(q[:, sl], kk[:, sl]) * (XA_HEAD ** -0.5)
        s = s - jnp.max(s, axis=-1, keepdims=True)
        e = jnp.exp(s)
        pr = e / jnp.sum(e, axis=-1, keepdims=True)
        outs.append(_dot(pr.astype(BF16), vv[:, sl]))
    o = jnp.concatenate(outs, axis=1).astype(BF16)
    y = _dot(o, wo_ref[...])
    h2 = h + y * lax.rsqrt(jnp.mean(y * y, axis=-1, keepdims=True) + NORM_EPS) * gpost_ref[...]
    h_out_ref[...] = h2
    f_out_ref[...] = (h2 * lax.rsqrt(jnp.mean(h2 * h2, axis=-1, keepdims=True) + NORM_EPS)
                      * gnext_ref[...]).astype(f_out_ref.dtype)


def _xattn_block(h, kmem, vmem, w_q, w_o, g_pre, g_post, g_next, batch, seq, tt=256):
    n, d = h.shape
    mlen, xw = kmem.shape[1], kmem.shape[2]
    per_b = seq // tt
    row = pl.BlockSpec((tt, d), lambda i: (i, 0))
    vec = pl.BlockSpec((1, d), lambda i: (0, 0))
    mem = pl.BlockSpec((1, mlen, xw), lambda i: (i // per_b, 0, 0))
    return pl.pallas_call(
        _xattn_kernel,
        grid=(n // tt,),
        in_specs=[row, mem, mem, pl.BlockSpec((d, xw), lambda i: (0, 0)), pl.BlockSpec((xw, d), lambda i: (0, 0)),
                  vec, vec, vec],
        out_specs=[row, row],
        out_shape=[jax.ShapeDtypeStruct((n, d), F32), jax.ShapeDtypeStruct((n, d), BF16)],
        compiler_params=_cparams(("parallel",)),
        name="xattn_block",
    )(h, kmem, vmem, w_q, w_o, g_pre.reshape(1, d), g_post.reshape(1, d), g_next.reshape(1, d))


def _relayout_w_in(w_in):
    d = w_in.shape[0]
    o = 3 * RW
    lw_c = w_in[:, o:o + LORA]
    la_c = w_in[:, o + LORA:o + 2 * LORA]
    lg_c = w_in[:, o + 2 * LORA:o + 2 * LORA + 256]
    g0 = o + 2 * LORA + 256
    qkv = w_in[:, g0:g0 + 3 * RW]
    z = w_in[:, g0 + 3 * RW:g0 + 4 * RW]
    beta = w_in[:, g0 + 4 * RW:g0 + 4 * RW + N_GDN_HEADS]
    alpha = w_in[:, g0 + 4 * RW + N_GDN_HEADS:g0 + 4 * RW + 2 * N_GDN_HEADS]
    gates = w_in[:, g0 + 4 * RW + 2 * N_GDN_HEADS:]
    pad = jnp.zeros((d, 256 - 128 - LORA), w_in.dtype)
    return jnp.concatenate([w_in[:, :o], qkv, z, lw_c, beta, alpha, la_c, pad, lg_c, gates], axis=1).astype(BF16)


def _layer(x, mem, mix_norm_pre, mix_norm_post, w_in, rwkv_shift_mix, rwkv_w0, rwkv_w_up, rwkv_a0, rwkv_a_up,
           rwkv_g_up, rwkv_k_k, rwkv_k_a, rwkv_r_k, rwkv_gn_w, rwkv_gn_b, gdn_conv_w, gdn_a_log, gdn_dt_bias,
           gdn_norm_w, w_branch_rwkv, w_branch_gdn, w_mix_out, xa_norm_pre, xa_norm_mem, xa_norm_post, xa_w_q,
           xa_w_kv, xa_w_o, mlp_norm_pre, mlp_norm_post, mlp_w_up, mlp_w_down):
    batch, seq, d = x.shape
    n = batch * seq
    h = x.reshape(n, d)

    w_in_b = _relayout_w_in(w_in)
    mix = rwkv_shift_mix
    o = 3 * RW
    cp = jnp.stack([mix[0:RW], mix[RW:2 * RW], mix[2 * RW:o], rwkv_w0, rwkv_a0, rwkv_k_k, rwkv_k_a,
                    rwkv_r_k.reshape(RW)], axis=0)
    gp = jnp.concatenate([rwkv_gn_w[None], rwkv_gn_b[None], jnp.zeros((6, RW), F32)], axis=0)
    zeros = lambda k: jnp.zeros((k,), F32)
    mixm = jnp.concatenate([mix[o:o + LORA], zeros(32), mix[o + LORA:o + 2 * LORA], zeros(32),
                            mix[o + 2 * LORA:o + 2 * LORA + 256]])[None]
    w_up_p = jnp.concatenate([rwkv_w_up, jnp.zeros((128 - LORA, RW), F32)], axis=0).astype(BF16)
    a_up_p = jnp.concatenate([rwkv_a_up, jnp.zeros((128 - LORA, RW), F32)], axis=0).astype(BF16)
    g_up_b = rwkv_g_up.astype(BF16)
    alog_l = jnp.concatenate([zeros(ALPHA_LANE), gdn_a_log])[None]
    dtb_l = jnp.concatenate([zeros(ALPHA_LANE), gdn_dt_bias])[None]

    u = _rmsnorm(h, mix_norm_pre, BF16)
    p = _matmul(u, w_in_b, F32, tm=1024, tn=512, name="in_proj")
    y_rw = _rwkv_mix(p, cp, gp, mixm, w_up_p, a_up_p, g_up_b, batch, seq)
    y_gdn = _gdn_mix(p, gdn_conv_w, alog_l, dtb_l, gdn_norm_w[None], batch, seq)
    merged = _branch_merge(y_rw, y_gdn, w_branch_rwkv.astype(BF16), w_branch_gdn.astype(BF16), p)
    y_mix = _matmul(merged, w_mix_out.astype(BF16), F32, tm=1024, tn=512, name="mix_out")
    h = _resid_norm(h, y_mix, mix_norm_post)

    mlen = mem.shape[1]
    m = _rmsnorm(mem.reshape(batch * mlen, d), xa_norm_mem, BF16)
    kv = _matmul(m, xa_w_kv.astype(BF16), BF16, tm=batch * mlen, tn=512, name="xa_kv")
    xw = XA_HEADS * XA_HEAD
    kmem = kv[:, :xw].reshape(batch, mlen, xw)
    vmem = kv[:, xw:].reshape(batch, mlen, xw)
    h, f = _xattn_block(h, kmem, vmem, xa_w_q.astype(BF16), xa_w_o.astype(BF16), xa_norm_pre, xa_norm_post,
                        mlp_norm_pre, batch, seq)

    hmid = _matmul(f, mlp_w_up.astype(BF16), BF16, tm=1024, tn=512, relu2=True, name="mlp_up")
    y_mlp = _matmul_kacc(hmid, mlp_w_down.astype(BF16), tm=1024, tn=1024, tk=2048, name="mlp_down")
    h = _resid_norm(h, y_mlp, mlp_norm_post)
    return h.reshape(batch, seq, d)


def kernel(x, mem, mix_norm_pre, mix_norm_post, w_in, rwkv_shift_mix, rwkv_w0, rwkv_w_up, rwkv_a0, rwkv_a_up,
           rwkv_g_up, rwkv_k_k, rwkv_k_a, rwkv_r_k, rwkv_gn_w, rwkv_gn_b, gdn_conv_w, gdn_a_log, gdn_dt_bias,
           gdn_norm_w, w_branch_rwkv, w_branch_gdn, w_mix_out, xa_norm_pre, xa_norm_mem, xa_norm_post, xa_w_q,
           xa_w_kv, xa_w_o, mlp_norm_pre, mlp_norm_post, mlp_w_up, mlp_w_down):
    args = (mix_norm_pre, mix_norm_post, w_in, rwkv_shift_mix, rwkv_w0, rwkv_w_up, rwkv_a0, rwkv_a_up,
            rwkv_g_up, rwkv_k_k, rwkv_k_a, rwkv_r_k, rwkv_gn_w, rwkv_gn_b, gdn_conv_w, gdn_a_log, gdn_dt_bias,
            gdn_norm_w, w_branch_rwkv, w_branch_gdn, w_mix_out, xa_norm_pre, xa_norm_mem, xa_norm_post, xa_w_q,
            xa_w_kv, xa_w_o, mlp_norm_pre, mlp_norm_post, mlp_w_up, mlp_w_down)
    h = x
    for layer in range(mix_norm_pre.shape[0]):
        h = _layer(h, mem, *(a[layer] for a in args))
    return h
```

```python
import functools
import math

import jax
import jax.numpy as jnp
from jax import lax
from jax.experimental import pallas as pl
from jax.experimental.pallas import tpu as pltpu

F32 = jnp.float32
BF16 = jnp.bfloat16

V7X_LANES = 128
V7X_SUBLANES = 8
V7X_VMEM_LIMIT_BYTES = 56 * 1024 * 1024

NORM_EPS = 1e-6
L2_EPS = 1e-6
RWKV_GN_EPS = 64e-5
RWKV_HEAD = 64
GDN_HEAD = 128
CHUNK = 64
XA_HEADS = 4
XA_HEAD = 128

RW = 2048
OFF_R, OFF_K, OFF_V = 0, RW, 2 * RW
OFF_GQ, OFF_GK, OFF_GV = 3 * RW, 4 * RW, 5 * RW
OFF_Z = 6 * RW
OFF_MISC = 7 * RW
MISC_W = 512
OFF_GATE_RW = OFF_MISC + MISC_W
D_MODEL = 4096
OFF_GATE_GDN = OFF_GATE_RW + D_MODEL
NP = OFF_GATE_GDN + D_MODEL
LORA = 96
N_GDN_HEADS = RW // GDN_HEAD
BETA_LANE = LORA
ALPHA_LANE = LORA + N_GDN_HEADS


def _cparams(sem):
    return pltpu.CompilerParams(dimension_semantics=sem, vmem_limit_bytes=V7X_VMEM_LIMIT_BYTES)


def _dot(a, b):
    return jnp.dot(a, b, preferred_element_type=F32)


def _dot_nt(a, b):
    return lax.dot_general(a, b, (((1,), (1,)), ((), ())), preferred_element_type=F32)


def _dot_tn(a, b):
    return lax.dot_general(a, b, (((0,), (0,)), ((), ())), preferred_element_type=F32)


def _bf(x):
    return x.astype(BF16)


def _hi_lo(x):
    hi = x.astype(BF16)
    return hi, (x - hi.astype(F32)).astype(BF16)


def _mm(a, b):
    return _dot(_bf(a), _bf(b))


def _mm_nt(a, b):
    return _dot_nt(_bf(a), _bf(b))


def _mm_tn(a, b):
    return _dot_tn(_bf(a), _bf(b))


def _mm_xr(a, b01x2):
    hi, lo = _hi_lo(a)
    return _dot(jnp.concatenate([hi, lo], axis=1), b01x2)


def _mm_xl(a01x2, b):
    hi, lo = _hi_lo(b)
    return _dot(a01x2, jnp.concatenate([hi, lo], axis=0))


def _sigmoid(x):
    return 1.0 / (1.0 + jnp.exp(-x))


def _softplus(x):
    return jnp.maximum(x, 0.0) + jnp.log1p(jnp.exp(-jnp.abs(x)))


def _iota(shape, dim):
    return lax.broadcasted_iota(jnp.int32, shape, dim)


def _lane_groups(x):
    return [x[:, g * V7X_LANES:(g + 1) * V7X_LANES] for g in range(x.shape[1] // V7X_LANES)]


def _tril_ones_x2(c):
    return jnp.where(_iota((c, 2 * c), 0) >= _iota((c, 2 * c), 1) % c, 1.0, 0.0).astype(BF16)


def _rmsnorm_kernel(x_ref, g_ref, o_ref):
    x = x_ref[...]
    inv = lax.rsqrt(jnp.mean(x * x, axis=-1, keepdims=True) + NORM_EPS)
    o_ref[...] = (x * inv * g_ref[...]).astype(o_ref.dtype)


def _rmsnorm(x, gain, out_dtype, rows=256):
    n, d = x.shape
    return pl.pallas_call(
        _rmsnorm_kernel,
        grid=(n // rows,),
        in_specs=[pl.BlockSpec((rows, d), lambda i: (i, 0)), pl.BlockSpec((1, d), lambda i: (0, 0))],
        out_specs=pl.BlockSpec((rows, d), lambda i: (i, 0)),
        out_shape=jax.ShapeDtypeStruct((n, d), out_dtype),
        compiler_params=_cparams(("parallel",)),
        name="rmsnorm",
    )(x, gain.reshape(1, d))


def _resid_norm_kernel(h_ref, y_ref, g_ref, o_ref):
    y = y_ref[...]
    inv = lax.rsqrt(jnp.mean(y * y, axis=-1, keepdims=True) + NORM_EPS)
    o_ref[...] = h_ref[...] + y * inv * g_ref[...]


def _resid_norm(h, y, gain, rows=256):
    n, d = h.shape
    row = pl.BlockSpec((rows, d), lambda i: (i, 0))
    return pl.pallas_call(
        _resid_norm_kernel,
        grid=(n // rows,),
        in_specs=[row, row, pl.BlockSpec((1, d), lambda i: (0, 0))],
        out_specs=row,
        out_shape=jax.ShapeDtypeStruct((n, d), F32),
        compiler_params=_cparams(("parallel",)),
        name="resid_norm",
    )(h, y, gain.reshape(1, d))


def _mm_kernel(a_ref, b_ref, o_ref, *, relu2):
    acc = _dot(a_ref[...], b_ref[...])
    if relu2:
        acc = jnp.square(jnp.maximum(acc, 0.0))
    o_ref[...] = acc.astype(o_ref.dtype)


def _matmul(a, b, out_dtype, tm, tn, relu2=False, name="matmul"):
    m, k = a.shape
    _, n = b.shape
    return pl.pallas_call(
        functools.partial(_mm_kernel, relu2=relu2),
        grid=(m // tm, n // tn),
        in_specs=[pl.BlockSpec((tm, k), lambda i, j: (i, 0)), pl.BlockSpec((k, tn), lambda i, j: (0, j))],
        out_specs=pl.BlockSpec((tm, tn), lambda i, j: (i, j)),
        out_shape=jax.ShapeDtypeStruct((m, n), out_dtype),
        compiler_params=_cparams(("parallel", "arbitrary")),
        name=name,
    )(a, b)


def _mm_kacc_kernel(a_ref, b_ref, o_ref):
    @pl.when(pl.program_id(2) == 0)
    def _():
        o_ref[...] = jnp.zeros_like(o_ref)

    o_ref[...] += _dot(a_ref[...], b_ref[...])


def _matmul_kacc(a, b, tm, tn, tk, name="matmul_kacc"):
    m, k = a.shape
    _, n = b.shape
    return pl.pallas_call(
        _mm_kacc_kernel,
        grid=(m // tm, n // tn, k // tk),
        in_specs=[pl.BlockSpec((tm, tk), lambda i, j, l: (i, l)), pl.BlockSpec((tk, tn), lambda i, j, l: (l, j))],
        out_specs=pl.BlockSpec((tm, tn), lambda i, j, l: (i, j)),
        out_shape=jax.ShapeDtypeStruct((m, n), F32),
        compiler_params=_cparams(("parallel", "parallel", "arbitrary")),
        name=name,
    )(a, b)


def _branch_kernel(yr_ref, yg_ref, wr_ref, wg_ref, gr_ref, gg_ref, o_ref):
    br = _dot(yr_ref[...], wr_ref[...])
    bg = _dot(yg_ref[...], wg_ref[...])
    o_ref[...] = (_sigmoid(gr_ref[...]) * br + _sigmoid(gg_ref[...]) * bg).astype(o_ref.dtype)


def _branch_merge(y_rw, y_gdn, w_rw, w_gdn, p, tm=1024, tn=512):
    m, k = y_rw.shape
    n = w_rw.shape[1]
    jr, jg = OFF_GATE_RW // tn, OFF_GATE_GDN // tn
    return pl.pallas_call(
        _branch_kernel,
        grid=(m // tm, n // tn),
        in_specs=[
            pl.BlockSpec((tm, k), lambda i, j: (i, 0)),
            pl.BlockSpec((tm, k), lambda i, j: (i, 0)),
            pl.BlockSpec((k, tn), lambda i, j: (0, j)),
            pl.BlockSpec((k, tn), lambda i, j: (0, j)),
            pl.BlockSpec((tm, tn), lambda i, j: (i, jr + j)),
            pl.BlockSpec((tm, tn), lambda i, j: (i, jg + j)),
        ],
        out_specs=pl.BlockSpec((tm, tn), lambda i, j: (i, j)),
        out_shape=jax.ShapeDtypeStruct((m, n), BF16),
        compiler_params=_cparams(("parallel", "arbitrary")),
        name="branch_merge",
    )(y_rw, y_gdn, w_rw, w_gdn, p, p)


def _shift_rows(x, prev8, s):
    n = x.shape[0]
    xs = pltpu.roll(x, s, axis=0)
    ps = jnp.tile(pltpu.roll(prev8, s, axis=0), (n // V7X_SUBLANES, 1))
    return jnp.where(_iota(x.shape, 0) < s, ps, xs)


def _token_shift(x, prev_row, mix):
    prev = jnp.where(_iota(x.shape, 0) == 0, prev_row, pltpu.roll(x, 1, axis=0))
    return x + (prev - x) * mix


def _expand2(x, head0):
    return jnp.concatenate([jnp.where(head0, x, 0.0), jnp.where(head0, 0.0, x)], axis=0)


def _rwkv_kernel(r_ref, k_ref, v_ref, pr_ref, pk_ref, pv_ref, misc_ref, pmisc_ref,
                 cp_ref, gp_ref, mixm_ref, wup_ref, aup_ref, gup_ref, o_ref, s_ref):
    c = CHUNK
    w = V7X_LANES
    ci = pl.program_id(2)

    @pl.when(ci == 0)
    def _():
        s_ref[...] = jnp.zeros_like(s_ref)

    first = ci == 0
    prev_row = lambda ref: jnp.where(first, 0.0, ref[7:8, :])
    mix_r, mix_k, mix_v, w0, a0, k_k, k_a, r_k = (cp_ref[i:i + 1, :] for i in range(8))
    gn_w, gn_b = gp_ref[0:1, :], gp_ref[1:2, :]

    misc_m = _token_shift(misc_ref[...], prev_row(pmisc_ref), mixm_ref[...])
    xr = _token_shift(r_ref[...], prev_row(pr_ref), mix_r)
    xk = _token_shift(k_ref[...], prev_row(pk_ref), mix_k)
    xv = _token_shift(v_ref[...], prev_row(pv_ref), mix_v)

    wlin = w0 + _dot(_bf(jnp.tanh(misc_m[:, 0:128])), wup_ref[...])
    logw = -jnp.exp(-_softplus(-wlin) - 0.5)
    a = _sigmoid(a0 + _dot(_bf(misc_m[:, 128:256]), aup_ref[...]))
    gate = _dot(_bf(_sigmoid(misc_m[:, 256:512])), gup_ref[...])

    li = _iota((2 * w, w), 0) % w
    lj = _iota((2 * w, w), 1)
    bones2 = jnp.where((li // RWKV_HEAD) == (lj // RWKV_HEAD), 1.0, 0.0).astype(BF16)
    same_head = (_iota((w, w), 0) // RWKV_HEAD) == (_iota((w, w), 1) // RWKV_HEAD)
    eye = jnp.where(_iota((w, w), 0) == _iota((w, w), 1), 1.0, 0.0)

    def head_sums(x):
        return jnp.concatenate([_mm_xr(g, bones2) for g in _lane_groups(x)], axis=1)

    kkr = xk * k_k
    kkn = kkr * lax.rsqrt(head_sums(kkr * kkr) + L2_EPS)
    k2 = xk * (1.0 + (a - 1.0) * k_a)
    am = -kkn
    bm = kkn * a

    cum = _mm_xl(_tril_ones_x2(c), logw)
    cum_ex = cum - logw
    c_last = cum[c - 1:c, :]
    c_mid = cum[c // 2 - 1:c // 2, :]
    e_out = jnp.exp(c_mid - cum)
    e_last = jnp.exp(c_last - cum)
    rt = xr * jnp.exp(cum - c_mid)
    at = am * jnp.exp(cum_ex - c_mid)
    bt = bm * e_out
    kt = k2 * e_out
    a_abs = am * jnp.exp(cum_ex)
    r_abs = xr * jnp.exp(cum)
    bh = bm * e_last
    kh = k2 * e_last
    w_c = jnp.exp(c_last)

    head0 = _iota((c, w), 1) < RWKV_HEAD
    head0_2 = _iota((c, 2 * w), 1) % w < RWKV_HEAD
    wi = _iota((c, 2 * c), 0)
    wj = _iota((c, 2 * c), 1) % c
    strict = wi > wj
    incl = wi >= wj

    gs = range(r_ref.shape[1] // w)
    G = _lane_groups
    at_g, rt_g, bt_g, kt_g, xv_g = G(at), G(rt), G(bt), G(kt), G(xv)
    aa = [_mm_nt(jnp.concatenate([at_g[g], rt_g[g]], axis=0),
                 jnp.concatenate([_expand2(bt_g[g], head0), _expand2(kt_g[g], head0)], axis=0)) for g in gs]
    a_ab = [jnp.where(strict, aa[g][0:c, 0:2 * c], 0.0) for g in gs]
    a_ak = [jnp.where(strict, aa[g][0:c, 2 * c:4 * c], 0.0) for g in gs]
    a_rb = [jnp.where(incl, aa[g][c:2 * c, 0:2 * c], 0.0) for g in gs]
    a_rk = [jnp.where(incl, aa[g][c:2 * c, 2 * c:4 * c], 0.0) for g in gs]

    xv_e = [_expand2(xv_g[g], head0) for g in gs]
    akv = [_mm(a_ak[g], xv_e[g]) for g in gs]
    a_abs_g = G(a_abs)
    x = [jnp.concatenate([a_abs_g[g], akv[g]], axis=1) for g in gs]
    pw = a_ab
    steps = int(math.log2(c))
    for it in range(steps):
        x = [x[g] + _mm(pw[g], _expand2(x[g], head0_2)) for g in gs]
        if it + 1 < steps:
            pw = [_mm(pw[g], _expand2(pw[g], head0)) for g in gs]
    wm = [x[g][:, 0:w] for g in gs]
    u0 = [x[g][:, w:2 * w] for g in gs]

    bh_g, kh_g, wc_g, rabs_g = G(bh), G(kh), G(w_c), G(r_abs)
    m_mat = [jnp.where(same_head, _mm_tn(bh_g[g], wm[g]), 0.0) + eye * wc_g[g] for g in gs]
    n_mat = [jnp.where(same_head, _mm_tn(jnp.concatenate([bh_g[g], kh_g[g]], axis=0),
                                         jnp.concatenate([u0[g], xv_g[g]], axis=0)), 0.0) for g in gs]
    p_mat = [rabs_g[g] + _mm(a_rb[g], _expand2(wm[g], head0)) for g in gs]
    q_mat = [_mm(jnp.concatenate([a_rb[g], a_rk[g]], axis=1),
                 jnp.concatenate([_expand2(u0[g], head0), xv_e[g]], axis=0)) for g in gs]

    h0 = [_bf(s_ref[g]) for g in gs]
    y = jnp.concatenate([_dot(_bf(p_mat[g]), h0[g]) + q_mat[g] for g in gs], axis=1)
    for g in gs:
        s_ref[g] = _dot(_bf(m_mat[g]), h0[g]) + n_mat[g]

    inv_n = 1.0 / RWKV_HEAD
    d = y - head_sums(y) * inv_n
    var = head_sums(d * d) * inv_n
    yn = d * lax.rsqrt(var + RWKV_GN_EPS) * gn_w + gn_b
    bonus = head_sums(xr * k2 * r_k) * xv
    o_ref[...] = ((yn + bonus) * gate).astype(o_ref.dtype)


def _rwkv_mix(p, cp, gp, mixm, w_up_p, a_up_p, g_up, batch, seq, lw=2048):
    c = CHUNK
    nc = seq // c
    nb = RW // lw
    rows8 = seq // V7X_SUBLANES

    def cur(off):
        assert off % lw == 0
        return pl.BlockSpec((c, lw), lambda b, h, t: (b * nc + t, off // lw + h))

    def prev(off):
        return pl.BlockSpec((V7X_SUBLANES, lw),
                            lambda b, h, t: (jnp.maximum(b * rows8 + t * (c // V7X_SUBLANES) - 1, 0), off // lw + h))

    def par(rows):
        return pl.BlockSpec((rows, lw), lambda b, h, t: (0, h))

    return pl.pallas_call(
        _rwkv_kernel,
        grid=(batch, nb, nc),
        in_specs=[
            cur(OFF_R), cur(OFF_K), cur(OFF_V), prev(OFF_R), prev(OFF_K), prev(OFF_V),
            pl.BlockSpec((c, MISC_W), lambda b, h, t: (b * nc + t, OFF_MISC // MISC_W)),
            pl.BlockSpec((V7X_SUBLANES, MISC_W),
                         lambda b, h, t: (jnp.maximum(b * rows8 + t * (c // V7X_SUBLANES) - 1, 0), OFF_MISC // MISC_W)),
            par(8), par(8),
            pl.BlockSpec((1, MISC_W), lambda b, h, t: (0, 0)),
            par(128), par(128), par(256),
        ],
        out_specs=pl.BlockSpec((c, lw), lambda b, h, t: (b * nc + t, h)),
        out_shape=jax.ShapeDtypeStruct((batch * seq, RW), BF16),
        scratch_shapes=[pltpu.VMEM((lw // V7X_LANES, V7X_LANES, V7X_LANES), F32)],
        compiler_params=_cparams(("parallel", "parallel", "arbitrary")),
        name="rwkv7_mix",
    )(p, p, p, p, p, p, p, p, cp, gp, mixm, w_up_p, a_up_p, g_up)


def _gdn_kernel(q_ref, k_ref, v_ref, pq_ref, pk_ref, pv_ref, z_ref, misc_ref,
                cwq_ref, cwk_ref, cwv_ref, alog_ref, dtb_ref, nw_ref, o_ref, s_ref):
    c = CHUNK
    w = V7X_LANES
    ci = pl.program_id(2)
    heads = q_ref.shape[1] // w
    hs = range(heads)
    G = _lane_groups

    @pl.when(ci == 0)
    def _():
        s_ref[...] = jnp.zeros_like(s_ref)

    first = ci == 0

    def conv_silu(x_ref, prev_ref, cw_ref):
        x = x_ref[...]
        prev8 = jnp.where(first, 0.0, prev_ref[...])
        acc = x * cw_ref[3:4, :]
        for s in (1, 2, 3):
            acc = acc + _shift_rows(x, prev8, s) * cw_ref[3 - s:4 - s, :]
        return acc * _sigmoid(acc)

    def head_sums(x):
        return jnp.concatenate([jnp.broadcast_to(jnp.sum(g, axis=-1, keepdims=True), g.shape) for g in G(x)], axis=1)

    q = conv_silu(q_ref, pq_ref, cwq_ref)
    k = conv_silu(k_ref, pk_ref, cwk_ref)
    v = conv_silu(v_ref, pv_ref, cwv_ref)
    q = q * lax.rsqrt(head_sums(q * q) + L2_EPS) * (GDN_HEAD ** -0.5)
    k = k * lax.rsqrt(head_sums(k * k) + L2_EPS)

    bm = misc_ref[:, 0:128]
    beta_hl = jnp.concatenate(_hi_lo(_sigmoid(bm)), axis=1)
    g_full = -jnp.exp(alog_ref[...]) * _softplus(bm + dtb_ref[...])
    gc_hl = jnp.concatenate(_hi_lo(_mm_xl(_tril_ones_x2(c), g_full)), axis=1)

    hg0 = pl.program_id(1) * heads
    lane2 = _iota((2 * w, w), 0) % w
    lane2r = _iota((c, 2 * w), 1) % w
    bcol = [_dot(beta_hl, jnp.where(lane2 == BETA_LANE + hg0 + h, 1.0, 0.0).astype(BF16)) for h in hs]
    gcol = [_dot(gc_hl, jnp.where(lane2 == ALPHA_LANE + hg0 + h, 1.0, 0.0).astype(BF16)) for h in hs]
    grow = [_dot_nt(jnp.where(lane2r == ALPHA_LANE + hg0 + h, 1.0, 0.0).astype(BF16), gc_hl) for h in hs]

    ti = _iota((c, c), 0)
    tj = _iota((c, c), 1)
    causal = ti >= tj
    strict = ti > tj
    decay = [jnp.where(causal, jnp.exp(jnp.where(causal, gcol[h][:, 0:c] - grow[h], 0.0)), 0.0) for h in hs]

    q_g, k_g, v_g = G(q), G(k), G(v)
    kb = [k_g[h] * bcol[h] for h in hs]
    kk = [_mm_nt(jnp.concatenate([kb[h], q_g[h]], axis=0), k_g[h]) for h in hs]
    lneg = [jnp.where(strict, -kk[h][0:c] * decay[h], 0.0) for h in hs]
    a_intra = [kk[h][c:2 * c] * decay[h] for h in hs]

    egc = [jnp.exp(gcol[h]) for h in hs]
    g_last = [gcol[h][c - 1:c, :] for h in hs]
    x = [jnp.concatenate([v_g[h] * bcol[h], kb[h] * egc[h]], axis=1) for h in hs]
    pw = lneg
    steps = int(math.log2(c))
    for it in range(steps):
        x = [x[h] + _mm(pw[h], x[h]) for h in hs]
        if it + 1 < steps:
            pw = [_mm(pw[h], pw[h]) for h in hs]
    uu = [x[h][:, 0:w] for h in hs]
    ww = [x[h][:, w:2 * w] for h in hs]

    kd = [k_g[h] * jnp.exp(g_last[h] - gcol[h]) for h in hs]
    eye = jnp.where(_iota((w, w), 0) == _iota((w, w), 1), 1.0, 0.0)
    m_mat = [eye * jnp.exp(g_last[h]) - _mm_tn(kd[h], ww[h]) for h in hs]
    n_mat = [_mm_tn(kd[h], uu[h]) for h in hs]
    p_mat = [q_g[h] * egc[h] - _mm(a_intra[h], ww[h]) for h in hs]
    q_mat = [_mm(a_intra[h], uu[h]) for h in hs]

    s0 = [_bf(s_ref[h]) for h in hs]
    o = jnp.concatenate([_dot(_bf(p_mat[h]), s0[h]) + q_mat[h] for h in hs], axis=1)
    for h in hs:
        s_ref[h] = _dot(_bf(m_mat[h]), s0[h]) + n_mat[h]

    o = o * lax.rsqrt(head_sums(o * o) * (1.0 / GDN_HEAD) + NORM_EPS) * jnp.tile(nw_ref[...], (1, heads))
    z = z_ref[...]
    o_ref[...] = (o * (z * _sigmoid(z))).astype(o_ref.dtype)


def _gdn_mix(p, conv_w, alog_l, dtb_l, norm_w, batch, seq, lw=2048):
    c = CHUNK
    nc = seq // c
    nb = RW // lw
    rows8 = seq // V7X_SUBLANES

    def cur(off):
        assert off % lw == 0
        return pl.BlockSpec((c, lw), lambda b, h, t: (b * nc + t, off // lw + h))

    def prev(off):
        return pl.BlockSpec((V7X_SUBLANES, lw),
                            lambda b, h, t: (jnp.maximum(b * rows8 + t * (c // V7X_SUBLANES) - 1, 0), off // lw + h))

    def cw(off):
        return pl.BlockSpec((4, lw), lambda b, h, t: (0, off // lw + h))

    lane_vec = pl.BlockSpec((1, V7X_LANES), lambda b, h, t: (0, 0))
    return pl.pallas_call(
        _gdn_kernel,
        grid=(batch, nb, nc),
        in_specs=[
            cur(OFF_GQ), cur(OFF_GK), cur(OFF_GV), prev(OFF_GQ), prev(OFF_GK), prev(OFF_GV), cur(OFF_Z),
            pl.BlockSpec((c, MISC_W), lambda b, h, t: (b * nc + t, OFF_MISC // MISC_W)),
            cw(0), cw(RW), cw(2 * RW), lane_vec, lane_vec, lane_vec,
        ],
        out_specs=pl.BlockSpec((c, lw), lambda b, h, t: (b * nc + t, h)),
        out_shape=jax.ShapeDtypeStruct((batch * seq, RW), BF16),
        scratch_shapes=[pltpu.VMEM((lw // V7X_LANES, V7X_LANES, V7X_LANES), F32)],
        compiler_params=_cparams(("parallel", "parallel", "arbitrary")),
        name="gdn_mix",
    )(p, p, p, p, p, p, p, p, conv_w, conv_w, conv_w, alog_l, dtb_l, norm_w)


def _xattn_kernel(h_ref, k_ref, v_ref, wq_ref, wo_ref, gpre_ref, gpost_ref, gnext_ref, h_out_ref, f_out_ref):
    h = h_ref[...]
    cn = (h * lax.rsqrt(jnp.mean(h * h, axis=-1, keepdims=True) + NORM_EPS) * gpre_ref[...]).astype(BF16)
    q = _dot(cn, wq_ref[...]).astype(BF16)
    kk = k_ref[0]
    vv = v_ref[0]
    outs = []
    for hh in range(XA_HEADS):
        sl = slice(hh * XA_HEAD, (hh + 1) * XA_HEAD)
        s = _dot_nt(q[:, sl], kk[:, sl]) * (XA_HEAD ** -0.5)
        s = s - jnp.max(s, axis=-1, keepdims=True)
        e = jnp.exp(s)
        pr = e / jnp.sum(e, axis=-1, keepdims=True)
        outs.append(_dot(pr.astype(BF16), vv[:, sl]))
    o = jnp.concatenate(outs, axis=1).astype(BF16)
    y = _dot(o, wo_ref[...])
    h2 = h + y * lax.rsqrt(jnp.mean(y * y, axis=-1, keepdims=True) + NORM_EPS) * gpost_ref[...]
    h_out_ref[...] = h2
    f_out_ref[...] = (h2 * lax.rsqrt(jnp.mean(h2 * h2, axis=-1, keepdims=True) + NORM_EPS)
                      * gnext_ref[...]).astype(f_out_ref.dtype)


def _xattn_block(h, kmem, vmem, w_q, w_o, g_pre, g_post, g_next, batch, seq, tt=256):
    n, d = h.shape
    mlen, xw = kmem.shape[1], kmem.shape[2]
    per_b = seq // tt
    row = pl.BlockSpec((tt, d), lambda i: (i, 0))
    vec = pl.BlockSpec((1, d), lambda i: (0, 0))
    mem = pl.BlockSpec((1, mlen, xw), lambda i: (i // per_b, 0, 0))
    return pl.pallas_call(
        _xattn_kernel,
        grid=(n // tt,),
        in_specs=[row, mem, mem, pl.BlockSpec((d, xw), lambda i: (0, 0)), pl.BlockSpec((xw, d), lambda i: (0, 0)),
                  vec, vec, vec],
        out_specs=[row, row],
        out_shape=[jax.ShapeDtypeStruct((n, d), F32), jax.ShapeDtypeStruct((n, d), BF16)],
        compiler_params=_cparams(("parallel",)),
        name="xattn_block",
    )(h, kmem, vmem, w_q, w_o, g_pre.reshape(1, d), g_post.reshape(1, d), g_next.reshape(1, d))


def _relayout_kernel(x_ref, o_ref):
    src_lora = 3 * RW
    src_gdn = src_lora + 2 * LORA + 256
    src_ba = src_gdn + 4 * RW
    src_gates = src_ba + 2 * N_GDN_HEADS
    col_chunk = 1024

    def copy(dst, src, width):
        for c0 in range(0, width, col_chunk):
            cw = min(col_chunk, width - c0)
            o_ref[:, dst + c0:dst + c0 + cw] = x_ref[:, src + c0:src + c0 + cw].astype(o_ref.dtype)

    copy(OFF_R, 0, 3 * RW)
    copy(OFF_GQ, src_gdn, 4 * RW)
    copy(OFF_MISC, src_lora, LORA)
    copy(OFF_MISC + BETA_LANE, src_ba, 2 * N_GDN_HEADS)
    copy(OFF_MISC + 128, src_lora + LORA, LORA)
    o_ref[:, OFF_MISC + 128 + LORA:OFF_MISC + 256] = jnp.zeros((o_ref.shape[0], 128 - LORA), o_ref.dtype)
    copy(OFF_MISC + 256, src_lora + 2 * LORA, 256)
    copy(OFF_GATE_RW, src_gates, 2 * D_MODEL)


def _relayout_w_in(w_in, rows=128):
    d, n_in = w_in.shape
    return pl.pallas_call(
        _relayout_kernel,
        grid=(d // rows,),
        in_specs=[pl.BlockSpec((rows, n_in), lambda i: (i, 0))],
        out_specs=pl.BlockSpec((rows, NP), lambda i: (i, 0)),
        out_shape=jax.ShapeDtypeStruct((d, NP), BF16),
        compiler_params=_cparams(("parallel",)),
        name="relayout_w_in",
    )(w_in)


def _layer(x, mem, mix_norm_pre, mix_norm_post, w_in, rwkv_shift_mix, rwkv_w0, rwkv_w_up, rwkv_a0, rwkv_a_up,
           rwkv_g_up, rwkv_k_k, rwkv_k_a, rwkv_r_k, rwkv_gn_w, rwkv_gn_b, gdn_conv_w, gdn_a_log, gdn_dt_bias,
           gdn_norm_w, w_branch_rwkv, w_branch_gdn, w_mix_out, xa_norm_pre, xa_norm_mem, xa_norm_post, xa_w_q,
           xa_w_kv, xa_w_o, mlp_norm_pre, mlp_norm_post, mlp_w_up, mlp_w_down):
    batch, seq, d = x.shape
    n = batch * seq
    h = x.reshape(n, d)

    w_in_b = _relayout_w_in(w_in)
    mix = rwkv_shift_mix
    o = 3 * RW
    cp = jnp.stack([mix[0:RW], mix[RW:2 * RW], mix[2 * RW:o], rwkv_w0, rwkv_a0, rwkv_k_k, rwkv_k_a,
                    rwkv_r_k.reshape(RW)], axis=0)
    gp = jnp.concatenate([rwkv_gn_w[None], rwkv_gn_b[None], jnp.zeros((6, RW), F32)], axis=0)
    zeros = lambda k: jnp.zeros((k,), F32)
    mixm = jnp.concatenate([mix[o:o + LORA], zeros(32), mix[o + LORA:o + 2 * LORA], zeros(32),
                            mix[o + 2 * LORA:o + 2 * LORA + 256]])[None]
    w_up_p = jnp.concatenate([rwkv_w_up, jnp.zeros((128 - LORA, RW), F32)], axis=0).astype(BF16)
    a_up_p = jnp.concatenate([rwkv_a_up, jnp.zeros((128 - LORA, RW), F32)], axis=0).astype(BF16)
    g_up_b = rwkv_g_up.astype(BF16)
    alog_l = jnp.concatenate([zeros(ALPHA_LANE), gdn_a_log])[None]
    dtb_l = jnp.concatenate([zeros(ALPHA_LANE), gdn_dt_bias])[None]

    u = _rmsnorm(h, mix_norm_pre, BF16)
    p = _matmul(u, w_in_b, F32, tm=1024, tn=512, name="in_proj")
    y_rw = _rwkv_mix(p, cp, gp, mixm, w_up_p, a_up_p, g_up_b, batch, seq)
    y_gdn = _gdn_mix(p, gdn_conv_w, alog_l, dtb_l, gdn_norm_w[None], batch, seq)
    merged = _branch_merge(y_rw, y_gdn, w_branch_rwkv.astype(BF16), w_branch_gdn.astype(BF16), p)
    y_mix = _matmul(merged, w_mix_out.astype(BF16), F32, tm=1024, tn=512, name="mix_out")
    h = _resid_norm(h, y_mix, mix_norm_post)

    mlen = mem.shape[1]
    m = _rmsnorm(mem.reshape(batch * mlen, d), xa_norm_mem, BF16)
    kv = _matmul(m, xa_w_kv.astype(BF16), BF16, tm=batch * mlen, tn=512, name="xa_kv")
    xw = XA_HEADS * XA_HEAD
    kmem = kv[:, :xw].reshape(batch, mlen, xw)
    vmem = kv[:, xw:].reshape(batch, mlen, xw)
    h, f = _xattn_block(h, kmem, vmem, xa_w_q.astype(BF16), xa_w_o.astype(BF16), xa_norm_pre, xa_norm_post,
                        mlp_norm_pre, batch, seq)

    hmid = _matmul(f, mlp_w_up.astype(BF16), BF16, tm=1024, tn=512, relu2=True, name="mlp_up")
    y_mlp = _matmul_kacc(hmid, mlp_w_down.astype(BF16), tm=1024, tn=1024, tk=2048, name="mlp_down")
    h = _resid_norm(h, y_mlp, mlp_norm_post)
    return h.reshape(batch, seq, d)


def kernel(x, mem, mix_norm_pre, mix_norm_post, w_in, rwkv_shift_mix, rwkv_w0, rwkv_w_up, rwkv_a0, rwkv_a_up,
           rwkv_g_up, rwkv_k_k, rwkv_k_a, rwkv_r_k, rwkv_gn_w, rwkv_gn_b, gdn_conv_w, gdn_a_log, gdn_dt_bias,
           gdn_norm_w, w_branch_rwkv, w_branch_gdn, w_mix_out, xa_norm_pre, xa_norm_mem, xa_norm_post, xa_w_q,
           xa_w_kv, xa_w_o, mlp_norm_pre, mlp_norm_post, mlp_w_up, mlp_w_down):
    args = (mix_norm_pre, mix_norm_post, w_in, rwkv_shift_mix, rwkv_w0, rwkv_w_up, rwkv_a0, rwkv_a_up,
            rwkv_g_up, rwkv_k_k, rwkv_k_a, rwkv_r_k, rwkv_gn_w, rwkv_gn_b, gdn_conv_w, gdn_a_log, gdn_dt_bias,
            gdn_norm_w, w_branch_rwkv, w_branch_gdn, w_mix_out, xa_norm_pre, xa_norm_mem, xa_norm_post, xa_w_q,
            xa_w_kv, xa_w_o, mlp_norm_pre, mlp_norm_post, mlp_w_up, mlp_w_down)
    h = x
    for layer in range(mix_norm_pre.shape[0]):
        h = _layer(h, mem, *(a[layer] for a in args))
    return h
```

```python
import functools
import math

import jax
import jax.numpy as jnp
from jax import lax
from jax.experimental import pallas as pl
from jax.experimental.pallas import tpu as pltpu

F32 = jnp.float32
BF16 = jnp.bfloat16

V7X_LANES = 128
V7X_SUBLANES = 8
V7X_VMEM_LIMIT_BYTES = 56 * 1024 * 1024

NORM_EPS = 1e-6
L2_EPS = 1e-6
RWKV_GN_EPS = 64e-5
RWKV_HEAD = 64
GDN_HEAD = 128
CHUNK = 64
XA_HEADS = 4
XA_HEAD = 128

RW = 2048
OFF_R, OFF_K, OFF_V = 0, RW, 2 * RW
OFF_GQ, OFF_GK, OFF_GV = 3 * RW, 4 * RW, 5 * RW
OFF_Z = 6 * RW
OFF_MISC = 7 * RW
MISC_W = 512
OFF_GATE_RW = OFF_MISC + MISC_W
D_MODEL = 4096
OFF_GATE_GDN = OFF_GATE_RW + D_MODEL
NP = OFF_GATE_GDN + D_MODEL
LORA = 96
N_GDN_HEADS = RW // GDN_HEAD
BETA_LANE = LORA
ALPHA_LANE = LORA + N_GDN_HEADS


def _cparams(sem):
    return pltpu.CompilerParams(dimension_semantics=sem, vmem_limit_bytes=V7X_VMEM_LIMIT_BYTES)


def _dot(a, b):
    return jnp.dot(a, b, preferred_element_type=F32)


def _dot_nt(a, b):
    return lax.dot_general(a, b, (((1,), (1,)), ((), ())), preferred_element_type=F32)


def _dot_tn(a, b):
    return lax.dot_general(a, b, (((0,), (0,)), ((), ())), preferred_element_type=F32)


def _bf(x):
    return x.astype(BF16)


def _hi_lo(x):
    hi = x.astype(BF16)
    return hi, (x - hi.astype(F32)).astype(BF16)


def _mm(a, b):
    return _dot(_bf(a), _bf(b))


def _mm_nt(a, b):
    return _dot_nt(_bf(a), _bf(b))


def _mm_tn(a, b):
    return _dot_tn(_bf(a), _bf(b))


def _mm_xr(a, b01x2):
    hi, lo = _hi_lo(a)
    return _dot(jnp.concatenate([hi, lo], axis=1), b01x2)


def _mm_xl(a01x2, b):
    hi, lo = _hi_lo(b)
    return _dot(a01x2, jnp.concatenate([hi, lo], axis=0))


def _sigmoid(x):
    return 1.0 / (1.0 + jnp.exp(-x))


def _softplus(x):
    return jnp.maximum(x, 0.0) + jnp.log1p(jnp.exp(-jnp.abs(x)))


def _iota(shape, dim):
    return lax.broadcasted_iota(jnp.int32, shape, dim)


def _lane_groups(x):
    return [x[:, g * V7X_LANES:(g + 1) * V7X_LANES] for g in range(x.shape[1] // V7X_LANES)]


def _tril_ones_x2(c):
    return jnp.where(_iota((c, 2 * c), 0) >= _iota((c, 2 * c), 1) % c, 1.0, 0.0).astype(BF16)


def _rmsnorm_kernel(x_ref, g_ref, o_ref):
    x = x_ref[...]
    inv = lax.rsqrt(jnp.mean(x * x, axis=-1, keepdims=True) + NORM_EPS)
    o_ref[...] = (x * inv * g_ref[...]).astype(o_ref.dtype)


def _rmsnorm(x, gain, out_dtype, rows=256):
    n, d = x.shape
    return pl.pallas_call(
        _rmsnorm_kernel,
        grid=(n // rows,),
        in_specs=[pl.BlockSpec((rows, d), lambda i: (i, 0)), pl.BlockSpec((1, d), lambda i: (0, 0))],
        out_specs=pl.BlockSpec((rows, d), lambda i: (i, 0)),
        out_shape=jax.ShapeDtypeStruct((n, d), out_dtype),
        compiler_params=_cparams(("parallel",)),
        name="rmsnorm",
    )(x, gain.reshape(1, d))


def _resid_norm_kernel(h_ref, y_ref, g_ref, o_ref):
    y = y_ref[...]
    inv = lax.rsqrt(jnp.mean(y * y, axis=-1, keepdims=True) + NORM_EPS)
    o_ref[...] = h_ref[...] + y * inv * g_ref[...]


def _resid_norm(h, y, gain, rows=256):
    n, d = h.shape
    row = pl.BlockSpec((rows, d), lambda i: (i, 0))
    return pl.pallas_call(
        _resid_norm_kernel,
        grid=(n // rows,),
        in_specs=[row, row, pl.BlockSpec((1, d), lambda i: (0, 0))],
        out_specs=row,
        out_shape=jax.ShapeDtypeStruct((n, d), F32),
        compiler_params=_cparams(("parallel",)),
        name="resid_norm",
    )(h, y, gain.reshape(1, d))


def _mm_kernel(a_ref, b_ref, o_ref, *, relu2, b_is_nk):
    acc = _dot_nt(a_ref[...], b_ref[...]) if b_is_nk else _dot(a_ref[...], b_ref[...])
    if relu2:
        acc = jnp.square(jnp.maximum(acc, 0.0))
    o_ref[...] = acc.astype(o_ref.dtype)


def _matmul(a, b, out_dtype, tm, tn, relu2=False, b_is_nk=False, name="matmul"):
    m, k = a.shape
    n = b.shape[0] if b_is_nk else b.shape[1]
    b_spec = pl.BlockSpec((tn, k), lambda i, j: (j, 0)) if b_is_nk else pl.BlockSpec((k, tn), lambda i, j: (0, j))
    return pl.pallas_call(
        functools.partial(_mm_kernel, relu2=relu2, b_is_nk=b_is_nk),
        grid=(m // tm, n // tn),
        in_specs=[pl.BlockSpec((tm, k), lambda i, j: (i, 0)), b_spec],
        out_specs=pl.BlockSpec((tm, tn), lambda i, j: (i, j)),
        out_shape=jax.ShapeDtypeStruct((m, n), out_dtype),
        compiler_params=_cparams(("parallel", "arbitrary")),
        name=name,
    )(a, b)


def _mm_kacc_kernel(a_ref, b_ref, o_ref):
    @pl.when(pl.program_id(2) == 0)
    def _():
        o_ref[...] = jnp.zeros_like(o_ref)

    o_ref[...] += _dot(a_ref[...], b_ref[...])


def _matmul_kacc(a, b, tm, tn, tk, name="matmul_kacc"):
    m, k = a.shape
    _, n = b.shape
    return pl.pallas_call(
        _mm_kacc_kernel,
        grid=(m // tm, n // tn, k // tk),
        in_specs=[pl.BlockSpec((tm, tk), lambda i, j, l: (i, l)), pl.BlockSpec((tk, tn), lambda i, j, l: (l, j))],
        out_specs=pl.BlockSpec((tm, tn), lambda i, j, l: (i, j)),
        out_shape=jax.ShapeDtypeStruct((m, n), F32),
        compiler_params=_cparams(("parallel", "parallel", "arbitrary")),
        name=name,
    )(a, b)


def _branch_kernel(yr_ref, yg_ref, wr_ref, wg_ref, gr_ref, gg_ref, o_ref):
    br = _dot(yr_ref[...], wr_ref[...])
    bg = _dot(yg_ref[...], wg_ref[...])
    o_ref[...] = (_sigmoid(gr_ref[...]) * br + _sigmoid(gg_ref[...]) * bg).astype(o_ref.dtype)


def _branch_merge(y_rw, y_gdn, w_rw, w_gdn, p, tm=1024, tn=512):
    m, k = y_rw.shape
    n = w_rw.shape[1]
    jr, jg = OFF_GATE_RW // tn, OFF_GATE_GDN // tn
    return pl.pallas_call(
        _branch_kernel,
        grid=(m // tm, n // tn),
        in_specs=[
            pl.BlockSpec((tm, k), lambda i, j: (i, 0)),
            pl.BlockSpec((tm, k), lambda i, j: (i, 0)),
            pl.BlockSpec((k, tn), lambda i, j: (0, j)),
            pl.BlockSpec((k, tn), lambda i, j: (0, j)),
            pl.BlockSpec((tm, tn), lambda i, j: (i, jr + j)),
            pl.BlockSpec((tm, tn), lambda i, j: (i, jg + j)),
        ],
        out_specs=pl.BlockSpec((tm, tn), lambda i, j: (i, j)),
        out_shape=jax.ShapeDtypeStruct((m, n), BF16),
        compiler_params=_cparams(("parallel", "arbitrary")),
        name="branch_merge",
    )(y_rw, y_gdn, w_rw, w_gdn, p, p)


def _shift_rows(x, prev8, s):
    n = x.shape[0]
    xs = pltpu.roll(x, s, axis=0)
    ps = jnp.tile(pltpu.roll(prev8, s, axis=0), (n // V7X_SUBLANES, 1))
    return jnp.where(_iota(x.shape, 0) < s, ps, xs)


def _token_shift(x, prev_row, mix):
    prev = jnp.where(_iota(x.shape, 0) == 0, prev_row, pltpu.roll(x, 1, axis=0))
    return x + (prev - x) * mix


def _expand2(x, head0):
    return jnp.concatenate([jnp.where(head0, x, 0.0), jnp.where(head0, 0.0, x)], axis=0)


def _rwkv_kernel(r_ref, k_ref, v_ref, pr_ref, pk_ref, pv_ref, misc_ref, pmisc_ref,
                 cp_ref, gp_ref, mixm_ref, wup_ref, aup_ref, gup_ref, o_ref, s_ref):
    c = CHUNK
    w = V7X_LANES
    ci = pl.program_id(2)

    @pl.when(ci == 0)
    def _():
        s_ref[...] = jnp.zeros_like(s_ref)

    first = ci == 0
    prev_row = lambda ref: jnp.where(first, 0.0, ref[7:8, :])
    mix_r, mix_k, mix_v, w0, a0, k_k, k_a, r_k = (cp_ref[i:i + 1, :] for i in range(8))
    gn_w, gn_b = gp_ref[0:1, :], gp_ref[1:2, :]

    misc_m = _token_shift(misc_ref[...], prev_row(pmisc_ref), mixm_ref[...])
    xr = _token_shift(r_ref[...], prev_row(pr_ref), mix_r)
    xk = _token_shift(k_ref[...], prev_row(pk_ref), mix_k)
    xv = _token_shift(v_ref[...], prev_row(pv_ref), mix_v)

    wlin = w0 + _dot(_bf(jnp.tanh(misc_m[:, 0:128])), wup_ref[...])
    logw = -jnp.exp(-_softplus(-wlin) - 0.5)
    a = _sigmoid(a0 + _dot(_bf(misc_m[:, 128:256]), aup_ref[...]))
    gate = _dot(_bf(_sigmoid(misc_m[:, 256:512])), gup_ref[...])

    li = _iota((2 * w, w), 0) % w
    lj = _iota((2 * w, w), 1)
    bones2 = jnp.where((li // RWKV_HEAD) == (lj // RWKV_HEAD), 1.0, 0.0).astype(BF16)
    same_head = (_iota((w, w), 0) // RWKV_HEAD) == (_iota((w, w), 1) // RWKV_HEAD)
    eye = jnp.where(_iota((w, w), 0) == _iota((w, w), 1), 1.0, 0.0)

    def head_sums(x):
        return jnp.concatenate([_mm_xr(g, bones2) for g in _lane_groups(x)], axis=1)

    kkr = xk * k_k
    kkn = kkr * lax.rsqrt(head_sums(kkr * kkr) + L2_EPS)
    k2 = xk * (1.0 + (a - 1.0) * k_a)
    am = -kkn
    bm = kkn * a

    cum = _mm_xl(_tril_ones_x2(c), logw)
    cum_ex = cum - logw
    c_last = cum[c - 1:c, :]
    c_mid = cum[c // 2 - 1:c // 2, :]
    e_out = jnp.exp(c_mid - cum)
    e_last = jnp.exp(c_last - cum)
    rt = xr * jnp.exp(cum - c_mid)
    at = am * jnp.exp(cum_ex - c_mid)
    bt = bm * e_out
    kt = k2 * e_out
    a_abs = am * jnp.exp(cum_ex)
    r_abs = xr * jnp.exp(cum)
    bh = bm * e_last
    kh = k2 * e_last
    w_c = jnp.exp(c_last)

    head0 = _iota((c, w), 1) < RWKV_HEAD
    head0_2 = _iota((c, 2 * w), 1) % w < RWKV_HEAD
    wi = _iota((c, 2 * c), 0)
    wj = _iota((c, 2 * c), 1) % c
    strict = wi > wj
    incl = wi >= wj

    gs = range(r_ref.shape[1] // w)
    G = _lane_groups
    at_g, rt_g, bt_g, kt_g, xv_g = G(at), G(rt), G(bt), G(kt), G(xv)
    aa = [_mm_nt(jnp.concatenate([at_g[g], rt_g[g]], axis=0),
                 jnp.concatenate([_expand2(bt_g[g], head0), _expand2(kt_g[g], head0)], axis=0)) for g in gs]
    a_ab = [jnp.where(strict, aa[g][0:c, 0:2 * c], 0.0) for g in gs]
    a_ak = [jnp.where(strict, aa[g][0:c, 2 * c:4 * c], 0.0) for g in gs]
    a_rb = [jnp.where(incl, aa[g][c:2 * c, 0:2 * c], 0.0) for g in gs]
    a_rk = [jnp.where(incl, aa[g][c:2 * c, 2 * c:4 * c], 0.0) for g in gs]

    xv_e = [_expand2(xv_g[g], head0) for g in gs]
    akv = [_mm(a_ak[g], xv_e[g]) for g in gs]
    a_abs_g = G(a_abs)
    x = [jnp.concatenate([a_abs_g[g], akv[g]], axis=1) for g in gs]
    pw = a_ab
    steps = int(math.log2(c))
    for it in range(steps):
        x = [x[g] + _mm(pw[g], _expand2(x[g], head0_2)) for g in gs]
        if it + 1 < steps:
            pw = [_mm(pw[g], _expand2(pw[g], head0)) for g in gs]
    wm = [x[g][:, 0:w] for g in gs]
    u0 = [x[g][:, w:2 * w] for g in gs]

    bh_g, kh_g, wc_g, rabs_g = G(bh), G(kh), G(w_c), G(r_abs)
    m_mat = [jnp.where(same_head, _mm_tn(bh_g[g], wm[g]), 0.0) + eye * wc_g[g] for g in gs]
    n_mat = [jnp.where(same_head, _mm_tn(jnp.concatenate([bh_g[g], kh_g[g]], axis=0),
                                         jnp.concatenate([u0[g], xv_g[g]], axis=0)), 0.0) for g in gs]
    p_mat = [rabs_g[g] + _mm(a_rb[g], _expand2(wm[g], head0)) for g in gs]
    q_mat = [_mm(jnp.concatenate([a_rb[g], a_rk[g]], axis=1),
                 jnp.concatenate([_expand2(u0[g], head0), xv_e[g]], axis=0)) for g in gs]

    h0 = [_bf(s_ref[g]) for g in gs]
    y = jnp.concatenate([_dot(_bf(p_mat[g]), h0[g]) + q_mat[g] for g in gs], axis=1)
    for g in gs:
        s_ref[g] = _dot(_bf(m_mat[g]), h0[g]) + n_mat[g]

    inv_n = 1.0 / RWKV_HEAD
    d = y - head_sums(y) * inv_n
    var = head_sums(d * d) * inv_n
    yn = d * lax.rsqrt(var + RWKV_GN_EPS) * gn_w + gn_b
    bonus = head_sums(xr * k2 * r_k) * xv
    o_ref[...] = ((yn + bonus) * gate).astype(o_ref.dtype)


def _rwkv_mix(p, cp, gp, mixm, w_up_p, a_up_p, g_up, batch, seq, lw=2048):
    c = CHUNK
    nc = seq // c
    nb = RW // lw
    rows8 = seq // V7X_SUBLANES

    def cur(off):
        assert off % lw == 0
        return pl.BlockSpec((c, lw), lambda b, h, t: (b * nc + t, off // lw + h))

    def prev(off):
        return pl.BlockSpec((V7X_SUBLANES, lw),
                            lambda b, h, t: (jnp.maximum(b * rows8 + t * (c // V7X_SUBLANES) - 1, 0), off // lw + h))

    def par(rows):
        return pl.BlockSpec((rows, lw), lambda b, h, t: (0, h))

    return pl.pallas_call(
        _rwkv_kernel,
        grid=(batch, nb, nc),
        in_specs=[
            cur(OFF_R), cur(OFF_K), cur(OFF_V), prev(OFF_R), prev(OFF_K), prev(OFF_V),
            pl.BlockSpec((c, MISC_W), lambda b, h, t: (b * nc + t, OFF_MISC // MISC_W)),
            pl.BlockSpec((V7X_SUBLANES, MISC_W),
                         lambda b, h, t: (jnp.maximum(b * rows8 + t * (c // V7X_SUBLANES) - 1, 0), OFF_MISC // MISC_W)),
            par(8), par(8),
            pl.BlockSpec((1, MISC_W), lambda b, h, t: (0, 0)),
            par(128), par(128), par(256),
        ],
        out_specs=pl.BlockSpec((c, lw), lambda b, h, t: (b * nc + t, h)),
        out_shape=jax.ShapeDtypeStruct((batch * seq, RW), BF16),
        scratch_shapes=[pltpu.VMEM((lw // V7X_LANES, V7X_LANES, V7X_LANES), F32)],
        compiler_params=_cparams(("parallel", "parallel", "arbitrary")),
        name="rwkv7_mix",
    )(p, p, p, p, p, p, p, p, cp, gp, mixm, w_up_p, a_up_p, g_up)


def _gdn_kernel(q_ref, k_ref, v_ref, pq_ref, pk_ref, pv_ref, z_ref, misc_ref,
                cwq_ref, cwk_ref, cwv_ref, alog_ref, dtb_ref, nw_ref, o_ref, s_ref):
    c = CHUNK
    w = V7X_LANES
    ci = pl.program_id(2)
    heads = q_ref.shape[1] // w
    hs = range(heads)
    G = _lane_groups

    @pl.when(ci == 0)
    def _():
        s_ref[...] = jnp.zeros_like(s_ref)

    first = ci == 0

    def conv_silu(x_ref, prev_ref, cw_ref):
        x = x_ref[...]
        prev8 = jnp.where(first, 0.0, prev_ref[...])
        acc = x * cw_ref[3:4, :]
        for s in (1, 2, 3):
            acc = acc + _shift_rows(x, prev8, s) * cw_ref[3 - s:4 - s, :]
        return acc * _sigmoid(acc)

    def head_sums(x):
        return jnp.concatenate([jnp.broadcast_to(jnp.sum(g, axis=-1, keepdims=True), g.shape) for g in G(x)], axis=1)

    q = conv_silu(q_ref, pq_ref, cwq_ref)
    k = conv_silu(k_ref, pk_ref, cwk_ref)
    v = conv_silu(v_ref, pv_ref, cwv_ref)
    q = q * lax.rsqrt(head_sums(q * q) + L2_EPS) * (GDN_HEAD ** -0.5)
    k = k * lax.rsqrt(head_sums(k * k) + L2_EPS)

    bm = misc_ref[:, 0:128]
    beta_hl = jnp.concatenate(_hi_lo(_sigmoid(bm)), axis=1)
    g_full = -jnp.exp(alog_ref[...]) * _softplus(bm + dtb_ref[...])
    gc_hl = jnp.concatenate(_hi_lo(_mm_xl(_tril_ones_x2(c), g_full)), axis=1)

    hg0 = pl.program_id(1) * heads
    lane2 = _iota((2 * w, w), 0) % w
    lane2r = _iota((c, 2 * w), 1) % w
    bcol = [_dot(beta_hl, jnp.where(lane2 == BETA_LANE + hg0 + h, 1.0, 0.0).astype(BF16)) for h in hs]
    gcol = [_dot(gc_hl, jnp.where(lane2 == ALPHA_LANE + hg0 + h, 1.0, 0.0).astype(BF16)) for h in hs]
    grow = [_dot_nt(jnp.where(lane2r == ALPHA_LANE + hg0 + h, 1.0, 0.0).astype(BF16), gc_hl) for h in hs]

    ti = _iota((c, c), 0)
    tj = _iota((c, c), 1)
    causal = ti >= tj
    strict = ti > tj
    decay = [jnp.where(causal, jnp.exp(jnp.where(causal, gcol[h][:, 0:c] - grow[h], 0.0)), 0.0) for h in hs]

    q_g, k_g, v_g = G(q), G(k), G(v)
    kb = [k_g[h] * bcol[h] for h in hs]
    kk = [_mm_nt(jnp.concatenate([kb[h], q_g[h]], axis=0), k_g[h]) for h in hs]
    lneg = [jnp.where(strict, -kk[h][0:c] * decay[h], 0.0) for h in hs]
    a_intra = [kk[h][c:2 * c] * decay[h] for h in hs]

    egc = [jnp.exp(gcol[h]) for h in hs]
    g_last = [gcol[h][c - 1:c, :] for h in hs]
    x = [jnp.concatenate([v_g[h] * bcol[h], kb[h] * egc[h]], axis=1) for h in hs]
    pw = lneg
    steps = int(math.log2(c))
    for it in range(steps):
        x = [x[h] + _mm(pw[h], x[h]) for h in hs]
        if it + 1 < steps:
            pw = [_mm(pw[h], pw[h]) for h in hs]
    uu = [x[h][:, 0:w] for h in hs]
    ww = [x[h][:, w:2 * w] for h in hs]

    kd = [k_g[h] * jnp.exp(g_last[h] - gcol[h]) for h in hs]
    eye = jnp.where(_iota((w, w), 0) == _iota((w, w), 1), 1.0, 0.0)
    m_mat = [eye * jnp.exp(g_last[h]) - _mm_tn(kd[h], ww[h]) for h in hs]
    n_mat = [_mm_tn(kd[h], uu[h]) for h in hs]
    p_mat = [q_g[h] * egc[h] - _mm(a_intra[h], ww[h]) for h in hs]
    q_mat = [_mm(a_intra[h], uu[h]) for h in hs]

    s0 = [_bf(s_ref[h]) for h in hs]
    o = jnp.concatenate([_dot(_bf(p_mat[h]), s0[h]) + q_mat[h] for h in hs], axis=1)
    for h in hs:
        s_ref[h] = _dot(_bf(m_mat[h]), s0[h]) + n_mat[h]

    o = o * lax.rsqrt(head_sums(o * o) * (1.0 / GDN_HEAD) + NORM_EPS) * jnp.tile(nw_ref[...], (1, heads))
    z = z_ref[...]
    o_ref[...] = (o * (z * _sigmoid(z))).astype(o_ref.dtype)


def _gdn_mix(p, conv_w, alog_l, dtb_l, norm_w, batch, seq, lw=2048):
    c = CHUNK
    nc = seq // c
    nb = RW // lw
    rows8 = seq // V7X_SUBLANES

    def cur(off):
        assert off % lw == 0
        return pl.BlockSpec((c, lw), lambda b, h, t: (b * nc + t, off // lw + h))

    def prev(off):
        return pl.BlockSpec((V7X_SUBLANES, lw),
                            lambda b, h, t: (jnp.maximum(b * rows8 + t * (c // V7X_SUBLANES) - 1, 0), off // lw + h))

    def cw(off):
        return pl.BlockSpec((4, lw), lambda b, h, t: (0, off // lw + h))

    lane_vec = pl.BlockSpec((1, V7X_LANES), lambda b, h, t: (0, 0))
    return pl.pallas_call(
        _gdn_kernel,
        grid=(batch, nb, nc),
        in_specs=[
            cur(OFF_GQ), cur(OFF_GK), cur(OFF_GV), prev(OFF_GQ), prev(OFF_GK), prev(OFF_GV), cur(OFF_Z),
            pl.BlockSpec((c, MISC_W), lambda b, h, t: (b * nc + t, OFF_MISC // MISC_W)),
            cw(0), cw(RW), cw(2 * RW), lane_vec, lane_vec, lane_vec,
        ],
        out_specs=pl.BlockSpec((c, lw), lambda b, h, t: (b * nc + t, h)),
        out_shape=jax.ShapeDtypeStruct((batch * seq, RW), BF16),
        scratch_shapes=[pltpu.VMEM((lw // V7X_LANES, V7X_LANES, V7X_LANES), F32)],
        compiler_params=_cparams(("parallel", "parallel", "arbitrary")),
        name="gdn_mix",
    )(p, p, p, p, p, p, p, p, conv_w, conv_w, conv_w, alog_l, dtb_l, norm_w)


def _xattn_kernel(h_ref, k_ref, v_ref, wq_ref, wo_ref, gpre_ref, gpost_ref, gnext_ref, h_out_ref, f_out_ref):
    h = h_ref[...]
    cn = (h * lax.rsqrt(jnp.mean(h * h, axis=-1, keepdims=True) + NORM_EPS) * gpre_ref[...]).astype(BF16)
    q = _dot(cn, wq_ref[...]).astype(BF16)
    kk = k_ref[0]
    vv = v_ref[0]
    outs = []
    for hh in range(XA_HEADS):
        sl = slice(hh * XA_HEAD, (hh + 1) * XA_HEAD)
        s = _dot_nt(q[:, sl], kk[:, sl]) * (XA_HEAD ** -0.5)
        s = s - jnp.max(s, axis=-1, keepdims=True)
        e = jnp.exp(s)
        pr = e / jnp.sum(e, axis=-1, keepdims=True)
        outs.append(_dot(pr.astype(BF16), vv[:, sl]))
    o = jnp.concatenate(outs, axis=1).astype(BF16)
    y = _dot(o, wo_ref[...])
    h2 = h + y * lax.rsqrt(jnp.mean(y * y, axis=-1, keepdims=True) + NORM_EPS) * gpost_ref[...]
    h_out_ref[...] = h2
    f_out_ref[...] = (h2 * lax.rsqrt(jnp.mean(h2 * h2, axis=-1, keepdims=True) + NORM_EPS)
                      * gnext_ref[...]).astype(f_out_ref.dtype)


def _xattn_block(h, kmem, vmem, w_q, w_o, g_pre, g_post, g_next, batch, seq, tt=256):
    n, d = h.shape
    mlen, xw = kmem.shape[1], kmem.shape[2]
    per_b = seq // tt
    row = pl.BlockSpec((tt, d), lambda i: (i, 0))
    vec = pl.BlockSpec((1, d), lambda i: (0, 0))
    mem = pl.BlockSpec((1, mlen, xw), lambda i: (i // per_b, 0, 0))
    return pl.pallas_call(
        _xattn_kernel,
        grid=(n // tt,),
        in_specs=[row, mem, mem, pl.BlockSpec((d, xw), lambda i: (0, 0)), pl.BlockSpec((xw, d), lambda i: (0, 0)),
                  vec, vec, vec],
        out_specs=[row, row],
        out_shape=[jax.ShapeDtypeStruct((n, d), F32), jax.ShapeDtypeStruct((n, d), BF16)],
        compiler_params=_cparams(("parallel",)),
        name="xattn_block",
    )(h, kmem, vmem, w_q, w_o, g_pre.reshape(1, d), g_post.reshape(1, d), g_next.reshape(1, d))


SRC_LORA = 3 * RW
SRC_GDN = SRC_LORA + 2 * LORA + 256
SRC_BA = SRC_GDN + 4 * RW
SRC_GATES = SRC_BA + 2 * N_GDN_HEADS
RELAYOUT_ROWS = 512


def _relayout_kernel(x_ref, misc_ref, o_ref):
    @pl.when(pl.program_id(0) != OFF_MISC // RELAYOUT_ROWS)
    def _():
        o_ref[...] = x_ref[...].astype(o_ref.dtype)

    @pl.when(pl.program_id(0) == OFF_MISC // RELAYOUT_ROWS)
    def _():
        o_ref[...] = misc_ref[...].astype(o_ref.dtype)


def _relayout_w_in_t(w_in_t):
    n_in, d = w_in_t.shape
    rb = RELAYOUT_ROWS
    assert OFF_GQ % rb == 0 and OFF_MISC % rb == 0 and OFF_GATE_RW % rb == 0 and MISC_W == rb
    zeros = jnp.zeros((128 - LORA, d), w_in_t.dtype)
    misc = jnp.concatenate([w_in_t[SRC_LORA:SRC_LORA + LORA], w_in_t[SRC_BA:SRC_BA + 2 * N_GDN_HEADS],
                            w_in_t[SRC_LORA + LORA:SRC_LORA + 2 * LORA], zeros,
                            w_in_t[SRC_LORA + 2 * LORA:SRC_GDN]], axis=0)

    def src_row(j):
        r = j * rb
        src = jnp.where(r < OFF_GQ, r, jnp.where(r < OFF_MISC, r + (SRC_GDN - OFF_GQ),
                                                 jnp.where(r < OFF_GATE_RW, 0, r + (SRC_GATES - OFF_GATE_RW))))
        return pl.multiple_of(src, math.gcd(rb, SRC_GDN, SRC_GATES))

    return pl.pallas_call(
        _relayout_kernel,
        grid=(NP // rb,),
        in_specs=[pl.BlockSpec((pl.Element(rb), pl.Element(d)), lambda j: (src_row(j), 0)),
                  pl.BlockSpec((rb, d), lambda j: (0, 0))],
        out_specs=pl.BlockSpec((rb, d), lambda j: (j, 0)),
        out_shape=jax.ShapeDtypeStruct((NP, d), BF16),
        compiler_params=_cparams(("parallel",)),
        name="relayout_w_in",
    )(w_in_t, misc)


def _layer(x, mem, mix_norm_pre, mix_norm_post, w_in, rwkv_shift_mix, rwkv_w0, rwkv_w_up, rwkv_a0, rwkv_a_up,
           rwkv_g_up, rwkv_k_k, rwkv_k_a, rwkv_r_k, rwkv_gn_w, rwkv_gn_b, gdn_conv_w, gdn_a_log, gdn_dt_bias,
           gdn_norm_w, w_branch_rwkv, w_branch_gdn, w_mix_out, xa_norm_pre, xa_norm_mem, xa_norm_post, xa_w_q,
           xa_w_kv, xa_w_o, mlp_norm_pre, mlp_norm_post, mlp_w_up, mlp_w_down):
    batch, seq, d = x.shape
    n = batch * seq
    h = x.reshape(n, d)

    w_in_b = _relayout_w_in_t(jnp.swapaxes(w_in, 0, 1))
    mix = rwkv_shift_mix
    o = 3 * RW
    cp = jnp.stack([mix[0:RW], mix[RW:2 * RW], mix[2 * RW:o], rwkv_w0, rwkv_a0, rwkv_k_k, rwkv_k_a,
                    rwkv_r_k.reshape(RW)], axis=0)
    gp = jnp.concatenate([rwkv_gn_w[None], rwkv_gn_b[None], jnp.zeros((6, RW), F32)], axis=0)
    zeros = lambda k: jnp.zeros((k,), F32)
    mixm = jnp.concatenate([mix[o:o + LORA], zeros(32), mix[o + LORA:o + 2 * LORA], zeros(32),
                            mix[o + 2 * LORA:o + 2 * LORA + 256]])[None]
    w_up_p = jnp.concatenate([rwkv_w_up, jnp.zeros((128 - LORA, RW), F32)], axis=0).astype(BF16)
    a_up_p = jnp.concatenate([rwkv_a_up, jnp.zeros((128 - LORA, RW), F32)], axis=0).astype(BF16)
    g_up_b = rwkv_g_up.astype(BF16)
    alog_l = jnp.concatenate([zeros(ALPHA_LANE), gdn_a_log])[None]
    dtb_l = jnp.concatenate([zeros(ALPHA_LANE), gdn_dt_bias])[None]

    u = _rmsnorm(h, mix_norm_pre, BF16)
    p = _matmul(u, w_in_b, F32, tm=1024, tn=512, b_is_nk=True, name="in_proj")
    y_rw = _rwkv_mix(p, cp, gp, mixm, w_up_p, a_up_p, g_up_b, batch, seq)
    y_gdn = _gdn_mix(p, gdn_conv_w, alog_l, dtb_l, gdn_norm_w[None], batch, seq)
    merged = _branch_merge(y_rw, y_gdn, w_branch_rwkv.astype(BF16), w_branch_gdn.astype(BF16), p)
    y_mix = _matmul(merged, w_mix_out.astype(BF16), F32, tm=1024, tn=512, name="mix_out")
    h = _resid_norm(h, y_mix, mix_norm_post)

    mlen = mem.shape[1]
    m = _rmsnorm(mem.reshape(batch * mlen, d), xa_norm_mem, BF16)
    kv = _matmul(m, xa_w_kv.astype(BF16), BF16, tm=batch * mlen, tn=512, name="xa_kv")
    xw = XA_HEADS * XA_HEAD
    kmem = kv[:, :xw].reshape(batch, mlen, xw)
    vmem = kv[:, xw:].reshape(batch, mlen, xw)
    h, f = _xattn_block(h, kmem, vmem, xa_w_q.astype(BF16), xa_w_o.astype(BF16), xa_norm_pre, xa_norm_post,
                        mlp_norm_pre, batch, seq)

    hmid = _matmul(f, mlp_w_up.astype(BF16), BF16, tm=1024, tn=512, relu2=True, name="mlp_up")
    y_mlp = _matmul_kacc(hmid, mlp_w_down.astype(BF16), tm=1024, tn=1024, tk=2048, name="mlp_down")
    h = _resid_norm(h, y_mlp, mlp_norm_post)
    return h.reshape(batch, seq, d)


def kernel(x, mem, mix_norm_pre, mix_norm_post, w_in, rwkv_shift_mix, rwkv_w0, rwkv_w_up, rwkv_a0, rwkv_a_up,
           rwkv_g_up, rwkv_k_k, rwkv_k_a, rwkv_r_k, rwkv_gn_w, rwkv_gn_b, gdn_conv_w, gdn_a_log, gdn_dt_bias,
           gdn_norm_w, w_branch_rwkv, w_branch_gdn, w_mix_out, xa_norm_pre, xa_norm_mem, xa_norm_post, xa_w_q,
           xa_w_kv, xa_w_o, mlp_norm_pre, mlp_norm_post, mlp_w_up, mlp_w_down):
    args = (mix_norm_pre, mix_norm_post, w_in, rwkv_shift_mix, rwkv_w0, rwkv_w_up, rwkv_a0, rwkv_a_up,
            rwkv_g_up, rwkv_k_k, rwkv_k_a, rwkv_r_k, rwkv_gn_w, rwkv_gn_b, gdn_conv_w, gdn_a_log, gdn_dt_bias,
            gdn_norm_w, w_branch_rwkv, w_branch_gdn, w_mix_out, xa_norm_pre, xa_norm_mem, xa_norm_post, xa_w_q,
            xa_w_kv, xa_w_o, mlp_norm_pre, mlp_norm_post, mlp_w_up, mlp_w_down)
    h = x
    for layer in range(mix_norm_pre.shape[0]):
        h = _layer(h, mem, *(a[layer] for a in args))
    return h
```

```python
import functools
import math

import jax
import jax.numpy as jnp
from jax import lax
from jax.experimental import pallas as pl
from jax.experimental.pallas import tpu as pltpu

F32 = jnp.float32
BF16 = jnp.bfloat16

V7X_LANES = 128
V7X_SUBLANES = 8
V7X_VMEM_LIMIT_BYTES = 56 * 1024 * 1024

NORM_EPS = 1e-6
L2_EPS = 1e-6
RWKV_GN_EPS = 64e-5
RWKV_HEAD = 64
GDN_HEAD = 128
CHUNK = 64
XA_HEADS = 4
XA_HEAD = 128

RW = 2048
OFF_R, OFF_K, OFF_V = 0, RW, 2 * RW
OFF_GQ, OFF_GK, OFF_GV = 3 * RW, 4 * RW, 5 * RW
OFF_Z = 6 * RW
OFF_MISC = 7 * RW
MISC_W = 512
OFF_GATE_RW = OFF_MISC + MISC_W
D_MODEL = 4096
OFF_GATE_GDN = OFF_GATE_RW + D_MODEL
NP = OFF_GATE_GDN + D_MODEL
LORA = 96
N_GDN_HEADS = RW // GDN_HEAD
BETA_LANE = LORA
ALPHA_LANE = LORA + N_GDN_HEADS


def _cparams(sem):
    return pltpu.CompilerParams(dimension_semantics=sem, vmem_limit_bytes=V7X_VMEM_LIMIT_BYTES)


def _dot(a, b):
    return jnp.dot(a, b, preferred_element_type=F32)


def _dot_nt(a, b):
    return lax.dot_general(a, b, (((1,), (1,)), ((), ())), preferred_element_type=F32)


def _dot_tn(a, b):
    return lax.dot_general(a, b, (((0,), (0,)), ((), ())), preferred_element_type=F32)


def _bf(x):
    return x.astype(BF16)


def _hi_lo(x):
    hi = x.astype(BF16)
    return hi, (x - hi.astype(F32)).astype(BF16)


def _mm(a, b):
    return _dot(_bf(a), _bf(b))


def _mm_nt(a, b):
    return _dot_nt(_bf(a), _bf(b))


def _mm_tn(a, b):
    return _dot_tn(_bf(a), _bf(b))


def _mm_xr(a, b01x2):
    hi, lo = _hi_lo(a)
    return _dot(jnp.concatenate([hi, lo], axis=1), b01x2)


def _mm_xl(a01x2, b):
    hi, lo = _hi_lo(b)
    return _dot(a01x2, jnp.concatenate([hi, lo], axis=0))


def _sigmoid(x):
    return 1.0 / (1.0 + jnp.exp(-x))


def _softplus(x):
    return jnp.maximum(x, 0.0) + jnp.log1p(jnp.exp(-jnp.abs(x)))


def _iota(shape, dim):
    return lax.broadcasted_iota(jnp.int32, shape, dim)


def _lane_groups(x):
    return [x[:, g * V7X_LANES:(g + 1) * V7X_LANES] for g in range(x.shape[1] // V7X_LANES)]


def _tril_ones_x2(c):
    return jnp.where(_iota((c, 2 * c), 0) >= _iota((c, 2 * c), 1) % c, 1.0, 0.0).astype(BF16)


def _rmsnorm_kernel(x_ref, g_ref, o_ref):
    x = x_ref[...]
    inv = lax.rsqrt(jnp.mean(x * x, axis=-1, keepdims=True) + NORM_EPS)
    o_ref[...] = (x * inv * g_ref[...]).astype(o_ref.dtype)


def _rmsnorm(x, gain, out_dtype, rows=256):
    n, d = x.shape
    return pl.pallas_call(
        _rmsnorm_kernel,
        grid=(n // rows,),
        in_specs=[pl.BlockSpec((rows, d), lambda i: (i, 0)), pl.BlockSpec((1, d), lambda i: (0, 0))],
        out_specs=pl.BlockSpec((rows, d), lambda i: (i, 0)),
        out_shape=jax.ShapeDtypeStruct((n, d), out_dtype),
        compiler_params=_cparams(("parallel",)),
        name="rmsnorm",
    )(x, gain.reshape(1, d))


def _resid_norm_kernel(h_ref, y_ref, g_ref, o_ref):
    y = y_ref[...]
    inv = lax.rsqrt(jnp.mean(y * y, axis=-1, keepdims=True) + NORM_EPS)
    o_ref[...] = h_ref[...] + y * inv * g_ref[...]


def _resid_norm(h, y, gain, rows=256):
    n, d = h.shape
    row = pl.BlockSpec((rows, d), lambda i: (i, 0))
    return pl.pallas_call(
        _resid_norm_kernel,
        grid=(n // rows,),
        in_specs=[row, row, pl.BlockSpec((1, d), lambda i: (0, 0))],
        out_specs=row,
        out_shape=jax.ShapeDtypeStruct((n, d), F32),
        compiler_params=_cparams(("parallel",)),
        name="resid_norm",
    )(h, y, gain.reshape(1, d))


def _mm_kernel(a_ref, b_ref, o_ref, *, relu2, b_is_nk):
    acc = _dot_nt(a_ref[...], b_ref[...]) if b_is_nk else _dot(a_ref[...], b_ref[...])
    if relu2:
        acc = jnp.square(jnp.maximum(acc, 0.0))
    o_ref[...] = acc.astype(o_ref.dtype)


def _matmul(a, b, out_dtype, tm, tn, relu2=False, b_is_nk=False, name="matmul"):
    m, k = a.shape
    n = b.shape[0] if b_is_nk else b.shape[1]
    b_spec = pl.BlockSpec((tn, k), lambda i, j: (j, 0)) if b_is_nk else pl.BlockSpec((k, tn), lambda i, j: (0, j))
    return pl.pallas_call(
        functools.partial(_mm_kernel, relu2=relu2, b_is_nk=b_is_nk),
        grid=(m // tm, n // tn),
        in_specs=[pl.BlockSpec((tm, k), lambda i, j: (i, 0)), b_spec],
        out_specs=pl.BlockSpec((tm, tn), lambda i, j: (i, j)),
        out_shape=jax.ShapeDtypeStruct((m, n), out_dtype),
        compiler_params=_cparams(("parallel", "arbitrary")),
        name=name,
    )(a, b)


def _mm_kacc_kernel(a_ref, b_ref, o_ref):
    @pl.when(pl.program_id(2) == 0)
    def _():
        o_ref[...] = jnp.zeros_like(o_ref)

    o_ref[...] += _dot(a_ref[...], b_ref[...])


def _matmul_kacc(a, b, tm, tn, tk, name="matmul_kacc"):
    m, k = a.shape
    _, n = b.shape
    return pl.pallas_call(
        _mm_kacc_kernel,
        grid=(m // tm, n // tn, k // tk),
        in_specs=[pl.BlockSpec((tm, tk), lambda i, j, l: (i, l)), pl.BlockSpec((tk, tn), lambda i, j, l: (l, j))],
        out_specs=pl.BlockSpec((tm, tn), lambda i, j, l: (i, j)),
        out_shape=jax.ShapeDtypeStruct((m, n), F32),
        compiler_params=_cparams(("parallel", "parallel", "arbitrary")),
        name=name,
    )(a, b)


def _branch_kernel(yr_ref, yg_ref, wr_ref, wg_ref, gr_ref, gg_ref, o_ref):
    br = _dot(yr_ref[...], wr_ref[...])
    bg = _dot(yg_ref[...], wg_ref[...])
    o_ref[...] = (_sigmoid(gr_ref[...]) * br + _sigmoid(gg_ref[...]) * bg).astype(o_ref.dtype)


def _branch_merge(y_rw, y_gdn, w_rw, w_gdn, p, tm=1024, tn=512):
    m, k = y_rw.shape
    n = w_rw.shape[1]
    jr, jg = OFF_GATE_RW // tn, OFF_GATE_GDN // tn
    return pl.pallas_call(
        _branch_kernel,
        grid=(m // tm, n // tn),
        in_specs=[
            pl.BlockSpec((tm, k), lambda i, j: (i, 0)),
            pl.BlockSpec((tm, k), lambda i, j: (i, 0)),
            pl.BlockSpec((k, tn), lambda i, j: (0, j)),
            pl.BlockSpec((k, tn), lambda i, j: (0, j)),
            pl.BlockSpec((tm, tn), lambda i, j: (i, jr + j)),
            pl.BlockSpec((tm, tn), lambda i, j: (i, jg + j)),
        ],
        out_specs=pl.BlockSpec((tm, tn), lambda i, j: (i, j)),
        out_shape=jax.ShapeDtypeStruct((m, n), BF16),
        compiler_params=_cparams(("parallel", "arbitrary")),
        name="branch_merge",
    )(y_rw, y_gdn, w_rw, w_gdn, p, p)


def _stage_rows(buf_ref, x_ref, prev_ref, first):
    buf_ref[0:V7X_SUBLANES, :] = jnp.where(first, 0.0, prev_ref[...])
    buf_ref[V7X_SUBLANES:, :] = x_ref[...]


def _shifted(buf_ref, s):
    return buf_ref[V7X_SUBLANES - s:buf_ref.shape[0] - s, :]


def _token_shift(buf_ref, x_ref, prev_ref, first, mix):
    _stage_rows(buf_ref, x_ref, prev_ref, first)
    x = x_ref[...]
    return x + (_shifted(buf_ref, 1) - x) * mix


def _expand2(x, head0):
    return jnp.concatenate([jnp.where(head0, x, 0.0), jnp.where(head0, 0.0, x)], axis=0)


def _rwkv_kernel(r_ref, k_ref, v_ref, pr_ref, pk_ref, pv_ref, misc_ref, pmisc_ref,
                 cp_ref, gp_ref, mixm_ref, wup_ref, aup_ref, gup_ref, o_ref, s_ref, rbuf, kbuf, vbuf, mbuf):
    c = CHUNK
    w = V7X_LANES
    ci = pl.program_id(2)

    @pl.when(ci == 0)
    def _():
        s_ref[...] = jnp.zeros_like(s_ref)

    first = ci == 0
    mix_r, mix_k, mix_v, w0, a0, k_k, k_a, r_k = (cp_ref[i:i + 1, :] for i in range(8))
    gn_w, gn_b = gp_ref[0:1, :], gp_ref[1:2, :]

    misc_m = _token_shift(mbuf, misc_ref, pmisc_ref, first, mixm_ref[...])
    xr = _token_shift(rbuf, r_ref, pr_ref, first, mix_r)
    xk = _token_shift(kbuf, k_ref, pk_ref, first, mix_k)
    xv = _token_shift(vbuf, v_ref, pv_ref, first, mix_v)

    wlin = w0 + _dot(_bf(jnp.tanh(misc_m[:, 0:128])), wup_ref[...])
    logw = -math.exp(-0.5) * _sigmoid(wlin)
    a = _sigmoid(a0 + _dot(_bf(misc_m[:, 128:256]), aup_ref[...]))
    gate = _dot(_bf(_sigmoid(misc_m[:, 256:512])), gup_ref[...])

    same_head = (_iota((w, w), 0) // RWKV_HEAD) == (_iota((w, w), 1) // RWKV_HEAD)
    bones = jnp.where(same_head, 1.0, 0.0).astype(BF16)
    eye = jnp.where(_iota((w, w), 0) == _iota((w, w), 1), 1.0, 0.0)

    def head_sums(x):
        n = x.shape[1] // w
        s = _dot(_bf(jnp.concatenate(_lane_groups(x), axis=0)), bones)
        return jnp.concatenate([s[g * c:(g + 1) * c] for g in range(n)], axis=1)

    kkr = xk * k_k
    kkn = kkr * lax.rsqrt(head_sums(kkr * kkr) + L2_EPS)
    k2 = xk * (1.0 + (a - 1.0) * k_a)
    am = -kkn
    bm = kkn * a

    cum = _mm_xl(_tril_ones_x2(c), logw)
    cum_ex = cum - logw
    c_last = cum[c - 1:c, :]
    c_mid = cum[c // 2 - 1:c // 2, :]
    e_out = jnp.exp(c_mid - cum)
    e_last = jnp.exp(c_last - cum)
    rt = xr * jnp.exp(cum - c_mid)
    at = am * jnp.exp(cum_ex - c_mid)
    bt = bm * e_out
    kt = k2 * e_out
    a_abs = am * jnp.exp(cum_ex)
    r_abs = xr * jnp.exp(cum)
    bh = bm * e_last
    kh = k2 * e_last
    w_c = jnp.exp(c_last)

    head0 = _iota((c, w), 1) < RWKV_HEAD
    head0_2 = _iota((c, 2 * w), 1) % w < RWKV_HEAD
    wi = _iota((c, 2 * c), 0)
    wj = _iota((c, 2 * c), 1) % c
    strict = wi > wj
    incl = wi >= wj

    gs = range(r_ref.shape[1] // w)
    G = _lane_groups
    at_g, rt_g, bt_g, kt_g, xv_g = G(at), G(rt), G(bt), G(kt), G(xv)
    aa = [_mm_nt(jnp.concatenate([at_g[g], rt_g[g]], axis=0),
                 jnp.concatenate([_expand2(bt_g[g], head0), _expand2(kt_g[g], head0)], axis=0)) for g in gs]
    a_ab = [jnp.where(strict, aa[g][0:c, 0:2 * c], 0.0) for g in gs]
    a_ak = [jnp.where(strict, aa[g][0:c, 2 * c:4 * c], 0.0) for g in gs]
    a_rb = [jnp.where(incl, aa[g][c:2 * c, 0:2 * c], 0.0) for g in gs]
    a_rk = [jnp.where(incl, aa[g][c:2 * c, 2 * c:4 * c], 0.0) for g in gs]

    xv_e = [_expand2(xv_g[g], head0) for g in gs]
    akv = [_mm(a_ak[g], xv_e[g]) for g in gs]
    a_abs_g = G(a_abs)
    x = [jnp.concatenate([a_abs_g[g], akv[g]], axis=1) for g in gs]
    pw = a_ab
    steps = int(math.log2(c))
    for it in range(steps):
        x = [x[g] + _mm(pw[g], _expand2(x[g], head0_2)) for g in gs]
        if it + 1 < steps:
            pw = [_mm(pw[g], _expand2(pw[g], head0)) for g in gs]
    wm = [x[g][:, 0:w] for g in gs]
    u0 = [x[g][:, w:2 * w] for g in gs]

    bh_g, kh_g, wc_g, rabs_g = G(bh), G(kh), G(w_c), G(r_abs)
    m_mat = [jnp.where(same_head, _mm_tn(bh_g[g], wm[g]), 0.0) + eye * wc_g[g] for g in gs]
    n_mat = [jnp.where(same_head, _mm_tn(jnp.concatenate([bh_g[g], kh_g[g]], axis=0),
                                         jnp.concatenate([u0[g], xv_g[g]], axis=0)), 0.0) for g in gs]
    p_mat = [rabs_g[g] + _mm(a_rb[g], _expand2(wm[g], head0)) for g in gs]
    q_mat = [_mm(jnp.concatenate([a_rb[g], a_rk[g]], axis=1),
                 jnp.concatenate([_expand2(u0[g], head0), xv_e[g]], axis=0)) for g in gs]

    h0 = [_bf(s_ref[g]) for g in gs]
    y = jnp.concatenate([_dot(_bf(p_mat[g]), h0[g]) + q_mat[g] for g in gs], axis=1)
    for g in gs:
        s_ref[g] = _dot(_bf(m_mat[g]), h0[g]) + n_mat[g]

    inv_n = 1.0 / RWKV_HEAD
    d = y - head_sums(y) * inv_n
    var = head_sums(d * d) * inv_n
    yn = d * lax.rsqrt(var + RWKV_GN_EPS) * gn_w + gn_b
    bonus = head_sums(xr * k2 * r_k) * xv
    o_ref[...] = ((yn + bonus) * gate).astype(o_ref.dtype)


def _rwkv_mix(p, cp, gp, mixm, w_up_p, a_up_p, g_up, batch, seq, lw=2048):
    c = CHUNK
    nc = seq // c
    nb = RW // lw
    rows8 = seq // V7X_SUBLANES

    def cur(off):
        assert off % lw == 0
        return pl.BlockSpec((c, lw), lambda b, h, t: (b * nc + t, off // lw + h))

    def prev(off):
        return pl.BlockSpec((V7X_SUBLANES, lw),
                            lambda b, h, t: (jnp.maximum(b * rows8 + t * (c // V7X_SUBLANES) - 1, 0), off // lw + h))

    def par(rows):
        return pl.BlockSpec((rows, lw), lambda b, h, t: (0, h))

    return pl.pallas_call(
        _rwkv_kernel,
        grid=(batch, nb, nc),
        in_specs=[
            cur(OFF_R), cur(OFF_K), cur(OFF_V), prev(OFF_R), prev(OFF_K), prev(OFF_V),
            pl.BlockSpec((c, MISC_W), lambda b, h, t: (b * nc + t, OFF_MISC // MISC_W)),
            pl.BlockSpec((V7X_SUBLANES, MISC_W),
                         lambda b, h, t: (jnp.maximum(b * rows8 + t * (c // V7X_SUBLANES) - 1, 0), OFF_MISC // MISC_W)),
            par(8), par(8),
            pl.BlockSpec((1, MISC_W), lambda b, h, t: (0, 0)),
            par(128), par(128), par(256),
        ],
        out_specs=pl.BlockSpec((c, lw), lambda b, h, t: (b * nc + t, h)),
        out_shape=jax.ShapeDtypeStruct((batch * seq, RW), BF16),
        scratch_shapes=[pltpu.VMEM((lw // V7X_LANES, V7X_LANES, V7X_LANES), F32)]
        + [pltpu.VMEM((V7X_SUBLANES + c, lw), F32)] * 3 + [pltpu.VMEM((V7X_SUBLANES + c, MISC_W), F32)],
        compiler_params=_cparams(("parallel", "parallel", "arbitrary")),
        name="rwkv7_mix",
    )(p, p, p, p, p, p, p, p, cp, gp, mixm, w_up_p, a_up_p, g_up)


def _gdn_kernel(q_ref, k_ref, v_ref, pq_ref, pk_ref, pv_ref, z_ref, misc_ref,
                cwq_ref, cwk_ref, cwv_ref, alog_ref, dtb_ref, nw_ref, o_ref, s_ref, qbuf, kbuf, vbuf):
    c = CHUNK
    w = V7X_LANES
    ci = pl.program_id(2)
    heads = q_ref.shape[1] // w
    hs = range(heads)
    G = _lane_groups

    @pl.when(ci == 0)
    def _():
        s_ref[...] = jnp.zeros_like(s_ref)

    first = ci == 0

    def conv_silu(x_ref, prev_ref, cw_ref, buf_ref):
        _stage_rows(buf_ref, x_ref, prev_ref, first)
        acc = x_ref[...] * cw_ref[3:4, :]
        for s in (1, 2, 3):
            acc = acc + _shifted(buf_ref, s) * cw_ref[3 - s:4 - s, :]
        return acc * _sigmoid(acc)

    def head_sums(x):
        return jnp.concatenate([jnp.broadcast_to(jnp.sum(g, axis=-1, keepdims=True), g.shape) for g in G(x)], axis=1)

    q = conv_silu(q_ref, pq_ref, cwq_ref, qbuf)
    k = conv_silu(k_ref, pk_ref, cwk_ref, kbuf)
    v = conv_silu(v_ref, pv_ref, cwv_ref, vbuf)
    q = q * lax.rsqrt(head_sums(q * q) + L2_EPS) * (GDN_HEAD ** -0.5)
    k = k * lax.rsqrt(head_sums(k * k) + L2_EPS)

    bm = misc_ref[:, 0:128]
    beta = _sigmoid(bm)
    g_full = -jnp.exp(alog_ref[...]) * _softplus(bm + dtb_ref[...])
    gc = _mm_xl(_tril_ones_x2(c), g_full)
    gc_t = jnp.concatenate([gc, gc], axis=0).T

    assert heads == N_GDN_HEADS
    bcol = [jnp.broadcast_to(beta[:, BETA_LANE + h:BETA_LANE + h + 1], (c, w)) for h in hs]
    gcol = [jnp.broadcast_to(gc[:, ALPHA_LANE + h:ALPHA_LANE + h + 1], (c, w)) for h in hs]
    grow = [jnp.broadcast_to(gc_t[ALPHA_LANE + h:ALPHA_LANE + h + 1, 0:c], (c, c)) for h in hs]

    ti = _iota((c, c), 0)
    tj = _iota((c, c), 1)
    causal = ti >= tj
    strict = ti > tj
    decay = [jnp.where(causal, jnp.exp(jnp.where(causal, gcol[h][:, 0:c] - grow[h], 0.0)), 0.0) for h in hs]

    q_g, k_g, v_g = G(q), G(k), G(v)
    kb = [k_g[h] * bcol[h] for h in hs]
    kk = [_mm_nt(jnp.concatenate([kb[h], q_g[h]], axis=0), k_g[h]) for h in hs]
    lneg = [jnp.where(strict, -kk[h][0:c] * decay[h], 0.0) for h in hs]
    a_intra = [kk[h][c:2 * c] * decay[h] for h in hs]

    egc = [jnp.exp(gcol[h]) for h in hs]
    g_last = [gcol[h][c - 1:c, :] for h in hs]
    x = [jnp.concatenate([v_g[h] * bcol[h], kb[h] * egc[h]], axis=1) for h in hs]
    pw = lneg
    steps = int(math.log2(c))
    for it in range(steps):
        x = [x[h] + _mm(pw[h], x[h]) for h in hs]
        if it + 1 < steps:
            pw = [_mm(pw[h], pw[h]) for h in hs]
    uu = [x[h][:, 0:w] for h in hs]
    ww = [x[h][:, w:2 * w] for h in hs]

    kd = [k_g[h] * jnp.exp(g_last[h] - gcol[h]) for h in hs]
    eye = jnp.where(_iota((w, w), 0) == _iota((w, w), 1), 1.0, 0.0)
    m_mat = [eye * jnp.exp(g_last[h]) - _mm_tn(kd[h], ww[h]) for h in hs]
    n_mat = [_mm_tn(kd[h], uu[h]) for h in hs]
    p_mat = [q_g[h] * egc[h] - _mm(a_intra[h], ww[h]) for h in hs]
    q_mat = [_mm(a_intra[h], uu[h]) for h in hs]

    s0 = [_bf(s_ref[h]) for h in hs]
    o = jnp.concatenate([_dot(_bf(p_mat[h]), s0[h]) + q_mat[h] for h in hs], axis=1)
    for h in hs:
        s_ref[h] = _dot(_bf(m_mat[h]), s0[h]) + n_mat[h]

    o = o * lax.rsqrt(head_sums(o * o) * (1.0 / GDN_HEAD) + NORM_EPS) * jnp.tile(nw_ref[...], (1, heads))
    z = z_ref[...]
    o_ref[...] = (o * (z * _sigmoid(z))).astype(o_ref.dtype)


def _gdn_mix(p, conv_w, alog_l, dtb_l, norm_w, batch, seq, lw=2048):
    c = CHUNK
    nc = seq // c
    nb = RW // lw
    rows8 = seq // V7X_SUBLANES

    def cur(off):
        assert off % lw == 0
        return pl.BlockSpec((c, lw), lambda b, h, t: (b * nc + t, off // lw + h))

    def prev(off):
        return pl.BlockSpec((V7X_SUBLANES, lw),
                            lambda b, h, t: (jnp.maximum(b * rows8 + t * (c // V7X_SUBLANES) - 1, 0), off // lw + h))

    def cw(off):
        return pl.BlockSpec((4, lw), lambda b, h, t: (0, off // lw + h))

    lane_vec = pl.BlockSpec((1, V7X_LANES), lambda b, h, t: (0, 0))
    return pl.pallas_call(
        _gdn_kernel,
        grid=(batch, nb, nc),
        in_specs=[
            cur(OFF_GQ), cur(OFF_GK), cur(OFF_GV), prev(OFF_GQ), prev(OFF_GK), prev(OFF_GV), cur(OFF_Z),
            pl.BlockSpec((c, MISC_W), lambda b, h, t: (b * nc + t, OFF_MISC // MISC_W)),
            cw(0), cw(RW), cw(2 * RW), lane_vec, lane_vec, lane_vec,
        ],
        out_specs=pl.BlockSpec((c, lw), lambda b, h, t: (b * nc + t, h)),
        out_shape=jax.ShapeDtypeStruct((batch * seq, RW), BF16),
        scratch_shapes=[pltpu.VMEM((lw // V7X_LANES, V7X_LANES, V7X_LANES), F32)]
        + [pltpu.VMEM((V7X_SUBLANES + c, lw), F32)] * 3,
        compiler_params=_cparams(("parallel", "parallel", "arbitrary")),
        name="gdn_mix",
    )(p, p, p, p, p, p, p, p, conv_w, conv_w, conv_w, alog_l, dtb_l, norm_w)


def _xattn_kernel(x_ref, ymix_ref, gmix_ref, k_ref, v_ref, wq_ref, wo_ref, gpre_ref, gpost_ref, gnext_ref,
                  h_out_ref, f_out_ref):
    ym = ymix_ref[...]
    h = x_ref[...] + ym * lax.rsqrt(jnp.mean(ym * ym, axis=-1, keepdims=True) + NORM_EPS) * gmix_ref[...]
    cn =(h * lax.rsqrt(jnp.mean(h * h, axis=-1, keepdims=True) + NORM_EPS) * gpre_ref[...]).astype(BF16)
    q = _dot(cn, wq_ref[...]).astype(BF16)
    kk = k_ref[0]
    vv = v_ref[0]
    outs = []
    for hh in range(XA_HEADS):
        sl = slice(hh * XA_HEAD, (hh + 1) * XA_HEAD)
        s = _dot_nt(q[:, sl], kk[:, sl]) * (XA_HEAD ** -0.5)
        s = s - jnp.max(s, axis=-1, keepdims=True)
        e = jnp.exp(s)
        pr = e / jnp.sum(e, axis=-1, keepdims=True)
        outs.append(_dot(pr.astype(BF16), vv[:, sl]))
    o = jnp.concatenate(outs, axis=1).astype(BF16)
    y = _dot(o, wo_ref[...])
    h2 = h + y * lax.rsqrt(jnp.mean(y * y, axis=-1, keepdims=True) + NORM_EPS) * gpost_ref[...]
    h_out_ref[...] = h2
    f_out_ref[...] = (h2 * lax.rsqrt(jnp.mean(h2 * h2, axis=-1, keepdims=True) + NORM_EPS)
                      * gnext_ref[...]).astype(f_out_ref.dtype)


def _xattn_block(x, y_mix, g_mix, kmem, vmem, w_q, w_o, g_pre, g_post, g_next, batch, seq, tt=256):
    n, d = x.shape
    mlen, xw = kmem.shape[1], kmem.shape[2]
    per_b = seq // tt
    row = pl.BlockSpec((tt, d), lambda i: (i, 0))
    vec = pl.BlockSpec((1, d), lambda i: (0, 0))
    mem = pl.BlockSpec((1, mlen, xw), lambda i: (i // per_b, 0, 0))
    return pl.pallas_call(
        _xattn_kernel,
        grid=(n // tt,),
        in_specs=[row, row, vec, mem, mem, pl.BlockSpec((d, xw), lambda i: (0, 0)),
                  pl.BlockSpec((xw, d), lambda i: (0, 0)), vec, vec, vec],
        out_specs=[row, row],
        out_shape=[jax.ShapeDtypeStruct((n, d), F32), jax.ShapeDtypeStruct((n, d), BF16)],
        compiler_params=_cparams(("parallel",)),
        name="xattn_block",
    )(x, y_mix, g_mix.reshape(1, d), kmem, vmem, w_q, w_o, g_pre.reshape(1, d), g_post.reshape(1, d),
      g_next.reshape(1, d))


SRC_LORA = 3 * RW
SRC_GDN = SRC_LORA + 2 * LORA + 256
SRC_BA = SRC_GDN + 4 * RW
SRC_GATES = SRC_BA + 2 * N_GDN_HEADS
RELAYOUT_ROWS = 512


def _relayout_kernel(x_ref, misc_ref, o_ref):
    @pl.when(pl.program_id(0) != OFF_MISC // RELAYOUT_ROWS)
    def _():
        o_ref[...] = x_ref[...].astype(o_ref.dtype)

    @pl.when(pl.program_id(0) == OFF_MISC // RELAYOUT_ROWS)
    def _():
        o_ref[...] = misc_ref[...].astype(o_ref.dtype)


def _relayout_w_in_t(w_in_t):
    n_in, d = w_in_t.shape
    rb = RELAYOUT_ROWS
    assert OFF_GQ % rb == 0 and OFF_MISC % rb == 0 and OFF_GATE_RW % rb == 0 and MISC_W == rb
    zeros = jnp.zeros((128 - LORA, d), w_in_t.dtype)
    misc = jnp.concatenate([w_in_t[SRC_LORA:SRC_LORA + LORA], w_in_t[SRC_BA:SRC_BA + 2 * N_GDN_HEADS],
                            w_in_t[SRC_LORA + LORA:SRC_LORA + 2 * LORA], zeros,
                            w_in_t[SRC_LORA + 2 * LORA:SRC_GDN]], axis=0)

    def src_row(j):
        r = j * rb
        src = jnp.where(r < OFF_GQ, r, jnp.where(r < OFF_MISC, r + (SRC_GDN - OFF_GQ),
                                                 jnp.where(r < OFF_GATE_RW, 0, r + (SRC_GATES - OFF_GATE_RW))))
        return pl.multiple_of(src, math.gcd(rb, SRC_GDN, SRC_GATES))

    return pl.pallas_call(
        _relayout_kernel,
        grid=(NP // rb,),
        in_specs=[pl.BlockSpec((pl.Element(rb), pl.Element(d)), lambda j: (src_row(j), 0)),
                  pl.BlockSpec((rb, d), lambda j: (0, 0))],
        out_specs=pl.BlockSpec((rb, d), lambda j: (j, 0)),
        out_shape=jax.ShapeDtypeStruct((NP, d), BF16),
        compiler_params=_cparams(("parallel",)),
        name="relayout_w_in",
    )(w_in_t, misc)


def _layer(x, mem, mix_norm_pre, mix_norm_post, w_in, rwkv_shift_mix, rwkv_w0, rwkv_w_up, rwkv_a0, rwkv_a_up,
           rwkv_g_up, rwkv_k_k, rwkv_k_a, rwkv_r_k, rwkv_gn_w, rwkv_gn_b, gdn_conv_w, gdn_a_log, gdn_dt_bias,
           gdn_norm_w, w_branch_rwkv, w_branch_gdn, w_mix_out, xa_norm_pre, xa_norm_mem, xa_norm_post, xa_w_q,
           xa_w_kv, xa_w_o, mlp_norm_pre, mlp_norm_post, mlp_w_up, mlp_w_down):
    batch, seq, d = x.shape
    n = batch * seq
    h = x.reshape(n, d)

    w_in_b = _relayout_w_in_t(jnp.swapaxes(w_in, 0, 1))
    mix = rwkv_shift_mix
    o = 3 * RW
    cp = jnp.stack([mix[0:RW], mix[RW:2 * RW], mix[2 * RW:o], rwkv_w0, rwkv_a0, rwkv_k_k, rwkv_k_a,
                    rwkv_r_k.reshape(RW)], axis=0)
    gp = jnp.concatenate([rwkv_gn_w[None], rwkv_gn_b[None], jnp.zeros((6, RW), F32)], axis=0)
    zeros = lambda k: jnp.zeros((k,), F32)
    mixm = jnp.concatenate([mix[o:o + LORA], zeros(32), mix[o + LORA:o + 2 * LORA], zeros(32),
                            mix[o + 2 * LORA:o + 2 * LORA + 256]])[None]
    w_up_p = jnp.concatenate([rwkv_w_up, jnp.zeros((128 - LORA, RW), F32)], axis=0).astype(BF16)
    a_up_p = jnp.concatenate([rwkv_a_up, jnp.zeros((128 - LORA, RW), F32)], axis=0).astype(BF16)
    g_up_b = rwkv_g_up.astype(BF16)
    alog_l = jnp.concatenate([zeros(ALPHA_LANE), gdn_a_log])[None]
    dtb_l = jnp.concatenate([zeros(ALPHA_LANE), gdn_dt_bias])[None]

    u = _rmsnorm(h, mix_norm_pre, BF16)
    p = _matmul(u, w_in_b, F32, tm=1024, tn=1152, b_is_nk=True, name="in_proj")
    y_rw = _rwkv_mix(p, cp, gp, mixm, w_up_p, a_up_p, g_up_b, batch, seq)
    y_gdn = _gdn_mix(p, gdn_conv_w, alog_l, dtb_l, gdn_norm_w[None], batch, seq)
    merged = _branch_merge(y_rw, y_gdn, w_branch_rwkv.astype(BF16), w_branch_gdn.astype(BF16), p)
    y_mix = _matmul(merged, w_mix_out.astype(BF16), F32, tm=1024, tn=1024, name="mix_out")

    mlen = mem.shape[1]
    m = _rmsnorm(mem.reshape(batch * mlen, d), xa_norm_mem, BF16)
    kv = _matmul(m, xa_w_kv.astype(BF16), BF16, tm=batch * mlen, tn=512, name="xa_kv")
    xw = XA_HEADS * XA_HEAD
    kmem = kv[:, :xw].reshape(batch, mlen, xw)
    vmem = kv[:, xw:].reshape(batch, mlen, xw)
    h, f = _xattn_block(h, y_mix, mix_norm_post, kmem, vmem, xa_w_q.astype(BF16), xa_w_o.astype(BF16),
                        xa_norm_pre, xa_norm_post, mlp_norm_pre, batch, seq)

    hmid = _matmul(f, mlp_w_up.astype(BF16), BF16, tm=1024, tn=1024, relu2=True, name="mlp_up")
    y_mlp = _matmul_kacc(hmid, mlp_w_down.astype(BF16), tm=1024, tn=1024, tk=4096, name="mlp_down")
    h = _resid_norm(h, y_mlp, mlp_norm_post)
    return h.reshape(batch, seq, d)


def kernel(x, mem, mix_norm_pre, mix_norm_post, w_in, rwkv_shift_mix, rwkv_w0, rwkv_w_up, rwkv_a0, rwkv_a_up,
           rwkv_g_up, rwkv_k_k, rwkv_k_a, rwkv_r_k, rwkv_gn_w, rwkv_gn_b, gdn_conv_w, gdn_a_log, gdn_dt_bias,
           gdn_norm_w, w_branch_rwkv, w_branch_gdn, w_mix_out, xa_norm_pre, xa_norm_mem, xa_norm_post, xa_w_q,
           xa_w_kv, xa_w_o, mlp_norm_pre, mlp_norm_post, mlp_w_up, mlp_w_down):
    args = (mix_norm_pre, mix_norm_post, w_in, rwkv_shift_mix, rwkv_w0, rwkv_w_up, rwkv_a0, rwkv_a_up,
            rwkv_g_up, rwkv_k_k, rwkv_k_a, rwkv_r_k, rwkv_gn_w, rwkv_gn_b, gdn_conv_w, gdn_a_log, gdn_dt_bias,
            gdn_norm_w, w_branch_rwkv, w_branch_gdn, w_mix_out, xa_norm_pre, xa_norm_mem, xa_norm_post, xa_w_q,
            xa_w_kv, xa_w_o, mlp_norm_pre, mlp_norm_post, mlp_w_up, mlp_w_down)
    h = x
    for layer in range(mix_norm_pre.shape[0]):
        h = _layer(h, mem, *(a[layer] for a in args))
    return h
```

```python
import functools
import math

import jax
import jax.numpy as jnp
from jax import lax
from jax.experimental import pallas as pl
from jax.experimental.pallas import tpu as pltpu

F32 = jnp.float32
BF16 = jnp.bfloat16

V7X_LANES = 128
V7X_SUBLANES = 8
V7X_VMEM_LIMIT_BYTES = 56 * 1024 * 1024

NORM_EPS = 1e-6
L2_EPS = 1e-6
RWKV_GN_EPS = 64e-5
RWKV_HEAD = 64
GDN_HEAD = 128
CHUNK = 64
XA_HEADS = 4
XA_HEAD = 128

RW = 2048
OFF_R, OFF_K, OFF_V = 0, RW, 2 * RW
OFF_GQ, OFF_GK, OFF_GV = 3 * RW, 4 * RW, 5 * RW
OFF_Z = 6 * RW
OFF_MISC = 7 * RW
MISC_W = 512
OFF_GATE_RW = OFF_MISC + MISC_W
D_MODEL = 4096
OFF_GATE_GDN = OFF_GATE_RW + D_MODEL
NP = OFF_GATE_GDN + D_MODEL
LORA = 96
N_GDN_HEADS = RW // GDN_HEAD
BETA_LANE = LORA
ALPHA_LANE = LORA + N_GDN_HEADS


def _cparams(sem):
    return pltpu.CompilerParams(dimension_semantics=sem, vmem_limit_bytes=V7X_VMEM_LIMIT_BYTES)


def _dot(a, b):
    return jnp.dot(a, b, preferred_element_type=F32)


def _dot_nt(a, b):
    return lax.dot_general(a, b, (((1,), (1,)), ((), ())), preferred_element_type=F32)


def _dot_tn(a, b):
    return lax.dot_general(a, b, (((0,), (0,)), ((), ())), preferred_element_type=F32)


def _bf(x):
    return x.astype(BF16)


def _hi_lo(x):
    hi = x.astype(BF16)
    return hi, (x - hi.astype(F32)).astype(BF16)


def _mm(a, b):
    return _dot(_bf(a), _bf(b))


def _mm_nt(a, b):
    return _dot_nt(_bf(a), _bf(b))


def _mm_tn(a, b):
    return _dot_tn(_bf(a), _bf(b))


def _mm_xr(a, b01x2):
    hi, lo = _hi_lo(a)
    return _dot(jnp.concatenate([hi, lo], axis=1), b01x2)


def _mm_xl(a01x2, b):
    hi, lo = _hi_lo(b)
    return _dot(a01x2, jnp.concatenate([hi, lo], axis=0))


def _sigmoid(x):
    return 1.0 / (1.0 + jnp.exp(-x))


def _softplus(x):
    return jnp.maximum(x, 0.0) + jnp.log1p(jnp.exp(-jnp.abs(x)))


def _iota(shape, dim):
    return lax.broadcasted_iota(jnp.int32, shape, dim)


def _lane_groups(x):
    return [x[:, g * V7X_LANES:(g + 1) * V7X_LANES] for g in range(x.shape[1] // V7X_LANES)]


def _tril_ones_x2(c):
    return jnp.where(_iota((c, 2 * c), 0) >= _iota((c, 2 * c), 1) % c, 1.0, 0.0).astype(BF16)


def _rmsnorm_kernel(x_ref, g_ref, o_ref):
    x = x_ref[...]
    inv = lax.rsqrt(jnp.mean(x * x, axis=-1, keepdims=True) + NORM_EPS)
    o_ref[...] = (x * inv * g_ref[...]).astype(o_ref.dtype)


def _rmsnorm(x, gain, out_dtype, rows=256):
    n, d = x.shape
    return pl.pallas_call(
        _rmsnorm_kernel,
        grid=(n // rows,),
        in_specs=[pl.BlockSpec((rows, d), lambda i: (i, 0)), pl.BlockSpec((1, d), lambda i: (0, 0))],
        out_specs=pl.BlockSpec((rows, d), lambda i: (i, 0)),
        out_shape=jax.ShapeDtypeStruct((n, d), out_dtype),
        compiler_params=_cparams(("parallel",)),
        name="rmsnorm",
    )(x, gain.reshape(1, d))


def _resid_norm_kernel(h_ref, y_ref, g_ref, o_ref):
    y = y_ref[...]
    inv = lax.rsqrt(jnp.mean(y * y, axis=-1, keepdims=True) + NORM_EPS)
    o_ref[...] = h_ref[...] + y * inv * g_ref[...]


def _resid_norm(h, y, gain, rows=256):
    n, d = h.shape
    row = pl.BlockSpec((rows, d), lambda i: (i, 0))
    return pl.pallas_call(
        _resid_norm_kernel,
        grid=(n // rows,),
        in_specs=[row, row, pl.BlockSpec((1, d), lambda i: (0, 0))],
        out_specs=row,
        out_shape=jax.ShapeDtypeStruct((n, d), F32),
        compiler_params=_cparams(("parallel",)),
        name="resid_norm",
    )(h, y, gain.reshape(1, d))


def _mm_kernel(a_ref, b_ref, o_ref, *, relu2, b_is_nk):
    acc = _dot_nt(a_ref[...], b_ref[...]) if b_is_nk else _dot(a_ref[...], b_ref[...])
    if relu2:
        acc = jnp.square(jnp.maximum(acc, 0.0))
    o_ref[...] = acc.astype(o_ref.dtype)


def _matmul(a, b, out_dtype, tm, tn, relu2=False, b_is_nk=False, name="matmul"):
    m, k = a.shape
    n = b.shape[0] if b_is_nk else b.shape[1]
    b_spec = pl.BlockSpec((tn, k), lambda i, j: (j, 0)) if b_is_nk else pl.BlockSpec((k, tn), lambda i, j: (0, j))
    return pl.pallas_call(
        functools.partial(_mm_kernel, relu2=relu2, b_is_nk=b_is_nk),
        grid=(m // tm, n // tn),
        in_specs=[pl.BlockSpec((tm, k), lambda i, j: (i, 0)), b_spec],
        out_specs=pl.BlockSpec((tm, tn), lambda i, j: (i, j)),
        out_shape=jax.ShapeDtypeStruct((m, n), out_dtype),
        compiler_params=_cparams(("parallel", "arbitrary")),
        name=name,
    )(a, b)


def _mm_wcast_kernel(a_ref, b_ref, o_ref, b_bf16, *, relu2):
    @pl.when(pl.program_id(1) == 0)
    def _():
        b_bf16[...] = b_ref[...].astype(BF16)

    acc = _dot(a_ref[...], b_bf16[...])
    if relu2:
        acc = jnp.square(jnp.maximum(acc, 0.0))
    o_ref[...] = acc.astype(o_ref.dtype)


def _matmul_wcast(a, b, out_dtype, tm, tn, relu2=False, name="matmul_wcast"):
    m, k = a.shape
    _, n = b.shape
    return pl.pallas_call(
        functools.partial(_mm_wcast_kernel, relu2=relu2),
        grid=(n // tn, m // tm),
        in_specs=[pl.BlockSpec((tm, k), lambda j, i: (i, 0)), pl.BlockSpec((k, tn), lambda j, i: (0, j))],
        out_specs=pl.BlockSpec((tm, tn), lambda j, i: (i, j)),
        out_shape=jax.ShapeDtypeStruct((m, n), out_dtype),
        scratch_shapes=[pltpu.VMEM((k, tn), BF16)],
        compiler_params=_cparams(("parallel", "arbitrary")),
        name=name,
    )(a, b)


def _mm_kacc_kernel(a_ref, b_ref, o_ref, *, k_chunk):
    @pl.when(pl.program_id(2) == 0)
    def _():
        o_ref[...] = jnp.zeros_like(o_ref)

    acc = None
    for k0 in range(0, a_ref.shape[1], k_chunk):
        part = _dot(a_ref[:, k0:k0 + k_chunk], b_ref[k0:k0 + k_chunk, :].astype(BF16))
        acc = part if acc is None else acc + part
    o_ref[...] += acc


def _matmul_kacc(a, b, tm, tn, tk, k_chunk=512, name="matmul_kacc"):
    m, k = a.shape
    _, n = b.shape
    return pl.pallas_call(
        functools.partial(_mm_kacc_kernel, k_chunk=k_chunk),
        grid=(m // tm, n // tn, k // tk),
        in_specs=[pl.BlockSpec((tm, tk), lambda i, j, l: (i, l)), pl.BlockSpec((tk, tn), lambda i, j, l: (l, j))],
        out_specs=pl.BlockSpec((tm, tn), lambda i, j, l: (i, j)),
        out_shape=jax.ShapeDtypeStruct((m, n), F32),
        compiler_params=_cparams(("parallel", "parallel", "arbitrary")),
        name=name,
    )(a, b)


def _branch_kernel(yr_ref, yg_ref, wr_ref, wg_ref, gr_ref, gg_ref, o_ref, wr_bf16, wg_bf16):
    @pl.when(pl.program_id(1) == 0)
    def _():
        wr_bf16[...] = wr_ref[...].astype(BF16)
        wg_bf16[...] = wg_ref[...].astype(BF16)

    br = _dot(yr_ref[...], wr_bf16[...])
    bg = _dot(yg_ref[...], wg_bf16[...])
    o_ref[...] = (_sigmoid(gr_ref[...]) * br + _sigmoid(gg_ref[...]) * bg).astype(o_ref.dtype)


def _branch_merge(y_rw, y_gdn, w_rw, w_gdn, p, tm=1024, tn=512):
    m, k = y_rw.shape
    n = w_rw.shape[1]
    jr, jg = OFF_GATE_RW // tn, OFF_GATE_GDN // tn
    return pl.pallas_call(
        _branch_kernel,
        grid=(n // tn, m // tm),
        in_specs=[
            pl.BlockSpec((tm, k), lambda j, i: (i, 0)),
            pl.BlockSpec((tm, k), lambda j, i: (i, 0)),
            pl.BlockSpec((k, tn), lambda j, i: (0, j)),
            pl.BlockSpec((k, tn), lambda j, i: (0, j)),
            pl.BlockSpec((tm, tn), lambda j, i: (i, jr + j)),
            pl.BlockSpec((tm, tn), lambda j, i: (i, jg + j)),
        ],
        out_specs=pl.BlockSpec((tm, tn), lambda j, i: (i, j)),
        out_shape=jax.ShapeDtypeStruct((m, n), BF16),
        scratch_shapes=[pltpu.VMEM((k, tn), BF16)] * 2,
        compiler_params=_cparams(("parallel", "arbitrary")),
        name="branch_merge",
    )(y_rw, y_gdn, w_rw, w_gdn, p, p)


def _stage_rows(buf_ref, x_ref, prev_ref, first):
    buf_ref[0:V7X_SUBLANES, :] = jnp.where(first, 0.0, prev_ref[...])
    buf_ref[V7X_SUBLANES:, :] = x_ref[...]


def _shifted(buf_ref, s):
    return buf_ref[V7X_SUBLANES - s:buf_ref.shape[0] - s, :]


def _token_shift(buf_ref, x_ref, prev_ref, first, mix):
    _stage_rows(buf_ref, x_ref, prev_ref, first)
    x = x_ref[...]
    return x + (_shifted(buf_ref, 1) - x) * mix


def _expand2(x, head0):
    return jnp.concatenate([jnp.where(head0, x, 0.0), jnp.where(head0, 0.0, x)], axis=0)


def _rwkv_kernel(r_ref, k_ref, v_ref, pr_ref, pk_ref, pv_ref, misc_ref, pmisc_ref,
                 cp_ref, gp_ref, mixm_ref, wup_ref, aup_ref, gup_ref, o_ref, s_ref, rbuf, kbuf, vbuf, mbuf):
    c = CHUNK
    w = V7X_LANES
    ci = pl.program_id(2)

    @pl.when(ci == 0)
    def _():
        s_ref[...] = jnp.zeros_like(s_ref)

    first = ci == 0
    mix_r, mix_k, mix_v, w0, a0, k_k, k_a, r_k = (cp_ref[i:i + 1, :] for i in range(8))
    gn_w, gn_b = gp_ref[0:1, :], gp_ref[1:2, :]

    misc_m = _token_shift(mbuf, misc_ref, pmisc_ref, first, mixm_ref[...])
    xr = _token_shift(rbuf, r_ref, pr_ref, first, mix_r)
    xk = _token_shift(kbuf, k_ref, pk_ref, first, mix_k)
    xv = _token_shift(vbuf, v_ref, pv_ref, first, mix_v)

    wlin = w0 + _dot(_bf(jnp.tanh(misc_m[:, 0:128])), wup_ref[...])
    logw = -math.exp(-0.5) * _sigmoid(wlin)
    a = _sigmoid(a0 + _dot(_bf(misc_m[:, 128:256]), aup_ref[...]))
    gate = _dot(_bf(_sigmoid(misc_m[:, 256:512])), gup_ref[...])

    same_head = (_iota((w, w), 0) // RWKV_HEAD) == (_iota((w, w), 1) // RWKV_HEAD)
    bones = jnp.where(same_head, 1.0, 0.0).astype(BF16)
    eye = jnp.where(_iota((w, w), 0) == _iota((w, w), 1), 1.0, 0.0)

    def head_sums(x):
        n = x.shape[1] // w
        s = _dot(_bf(jnp.concatenate(_lane_groups(x), axis=0)), bones)
        return jnp.concatenate([s[g * c:(g + 1) * c] for g in range(n)], axis=1)

    kkr = xk * k_k
    kkn = kkr * lax.rsqrt(head_sums(kkr * kkr) + L2_EPS)
    k2 = xk * (1.0 + (a - 1.0) * k_a)
    am = -kkn
    bm = kkn * a

    cum = _mm_xl(_tril_ones_x2(c), logw)
    cum_ex = cum - logw
    c_last = cum[c - 1:c, :]
    c_mid = cum[c // 2 - 1:c // 2, :]
    e_out = jnp.exp(c_mid - cum)
    e_last = jnp.exp(c_last - cum)
    rt = xr * jnp.exp(cum - c_mid)
    at = am * jnp.exp(cum_ex - c_mid)
    bt = bm * e_out
    kt = k2 * e_out
    a_abs = am * jnp.exp(cum_ex)
    r_abs = xr * jnp.exp(cum)
    bh = bm * e_last
    kh = k2 * e_last
    w_c = jnp.exp(c_last)

    head0 = _iota((c, w), 1) < RWKV_HEAD
    head0_2 = _iota((c, 2 * w), 1) % w < RWKV_HEAD
    wi = _iota((c, 2 * c), 0)
    wj = _iota((c, 2 * c), 1) % c
    strict = wi > wj
    incl = wi >= wj

    gs = range(r_ref.shape[1] // w)
    G = _lane_groups
    at_g, rt_g, bt_g, kt_g, xv_g = G(at), G(rt), G(bt), G(kt), G(xv)
    aa = [_mm_nt(jnp.concatenate([at_g[g], rt_g[g]], axis=0),
                 jnp.concatenate([_expand2(bt_g[g], head0), _expand2(kt_g[g], head0)], axis=0)) for g in gs]
    a_ab = [jnp.where(strict, aa[g][0:c, 0:2 * c], 0.0) for g in gs]
    a_ak = [jnp.where(strict, aa[g][0:c, 2 * c:4 * c], 0.0) for g in gs]
    a_rb = [jnp.where(incl, aa[g][c:2 * c, 0:2 * c], 0.0) for g in gs]
    a_rk = [jnp.where(incl, aa[g][c:2 * c, 2 * c:4 * c], 0.0) for g in gs]

    xv_e = [_expand2(xv_g[g], head0) for g in gs]
    akv = [_mm(a_ak[g], xv_e[g]) for g in gs]
    a_abs_g = G(a_abs)
    x = [jnp.concatenate([a_abs_g[g], akv[g]], axis=1) for g in gs]
    pw = a_ab
    steps = int(math.log2(c))
    for it in range(steps):
        x = [x[g] + _mm(pw[g], _expand2(x[g], head0_2)) for g in gs]
        if it + 1 < steps:
            pw = [_mm(pw[g], _expand2(pw[g], head0)) for g in gs]
    wm = [x[g][:, 0:w] for g in gs]
    u0 = [x[g][:, w:2 * w] for g in gs]

    bh_g, kh_g, wc_g, rabs_g = G(bh), G(kh), G(w_c), G(r_abs)
    m_mat = [jnp.where(same_head, _mm_tn(bh_g[g], wm[g]), 0.0) + eye * wc_g[g] for g in gs]
    n_mat = [jnp.where(same_head, _mm_tn(jnp.concatenate([bh_g[g], kh_g[g]], axis=0),
                                         jnp.concatenate([u0[g], xv_g[g]], axis=0)), 0.0) for g in gs]
    p_mat = [rabs_g[g] + _mm(a_rb[g], _expand2(wm[g], head0)) for g in gs]
    q_mat = [_mm(jnp.concatenate([a_rb[g], a_rk[g]], axis=1),
                 jnp.concatenate([_expand2(u0[g], head0), xv_e[g]], axis=0)) for g in gs]

    h0 = [_bf(s_ref[g]) for g in gs]
    y = jnp.concatenate([_dot(_bf(p_mat[g]), h0[g]) + q_mat[g] for g in gs], axis=1)
    for g in gs:
        s_ref[g] = _dot(_bf(m_mat[g]), h0[g]) + n_mat[g]

    inv_n = 1.0 / RWKV_HEAD
    d = y - head_sums(y) * inv_n
    var = head_sums(d * d) * inv_n
    yn = d * lax.rsqrt(var + RWKV_GN_EPS) * gn_w + gn_b
    bonus = head_sums(xr * k2 * r_k) * xv
    o_ref[...] = ((yn + bonus) * gate).astype(o_ref.dtype)


def _rwkv_mix(p, cp, gp, mixm, w_up_p, a_up_p, g_up, batch, seq, lw=2048):
    c = CHUNK
    nc = seq // c
    nb = RW // lw
    rows8 = seq // V7X_SUBLANES

    def cur(off):
        assert off % lw == 0
        return pl.BlockSpec((c, lw), lambda b, h, t: (b * nc + t, off // lw + h))

    def prev(off):
        return pl.BlockSpec((V7X_SUBLANES, lw),
                            lambda b, h, t: (jnp.maximum(b * rows8 + t * (c // V7X_SUBLANES) - 1, 0), off // lw + h))

    def par(rows):
        return pl.BlockSpec((rows, lw), lambda b, h, t: (0, h))

    return pl.pallas_call(
        _rwkv_kernel,
        grid=(batch, nb, nc),
        in_specs=[
            cur(OFF_R), cur(OFF_K), cur(OFF_V), prev(OFF_R), prev(OFF_K), prev(OFF_V),
            pl.BlockSpec((c, MISC_W), lambda b, h, t: (b * nc + t, OFF_MISC // MISC_W)),
            pl.BlockSpec((V7X_SUBLANES, MISC_W),
                         lambda b, h, t: (jnp.maximum(b * rows8 + t * (c // V7X_SUBLANES) - 1, 0), OFF_MISC // MISC_W)),
            par(8), par(8),
            pl.BlockSpec((1, MISC_W), lambda b, h, t: (0, 0)),
            par(128), par(128), par(256),
        ],
        out_specs=pl.BlockSpec((c, lw), lambda b, h, t: (b * nc + t, h)),
        out_shape=jax.ShapeDtypeStruct((batch * seq, RW), BF16),
        scratch_shapes=[pltpu.VMEM((lw // V7X_LANES, V7X_LANES, V7X_LANES), F32)]
        + [pltpu.VMEM((V7X_SUBLANES + c, lw), F32)] * 3 + [pltpu.VMEM((V7X_SUBLANES + c, MISC_W), F32)],
        compiler_params=_cparams(("parallel", "parallel", "arbitrary")),
        name="rwkv7_mix",
    )(p, p, p, p, p, p, p, p, cp, gp, mixm, w_up_p, a_up_p, g_up)


def _gdn_kernel(q_ref, k_ref, v_ref, pq_ref, pk_ref, pv_ref, z_ref, misc_ref,
                cwq_ref, cwk_ref, cwv_ref, alog_ref, dtb_ref, nw_ref, o_ref, s_ref):
    c = CHUNK
    w = V7X_LANES
    ci = pl.program_id(2)
    heads = q_ref.shape[1] // w
    hs = range(heads)
    G = _lane_groups

    @pl.when(ci == 0)
    def _():
        s_ref[...] = jnp.zeros_like(s_ref)

    first = ci == 0

    def conv_silu(x_ref, prev_ref, cw_ref):
        x = x_ref[...]
        prev8 = jnp.where(first, 0.0, prev_ref[...])
        row = _iota(x.shape, 0)
        acc = x * cw_ref[3:4, :]
        for s in (1, 2, 3):
            tail = jnp.tile(pltpu.roll(prev8, s, axis=0), (c // V7X_SUBLANES, 1))
            acc = acc + jnp.where(row < s, tail, pltpu.roll(x, s, axis=0)) * cw_ref[3 - s:4 - s, :]
        return acc * _sigmoid(acc)

    def head_sums(x):
        return jnp.concatenate([jnp.broadcast_to(jnp.sum(g, axis=-1, keepdims=True), g.shape) for g in G(x)], axis=1)

    q = conv_silu(q_ref, pq_ref, cwq_ref)
    k = conv_silu(k_ref, pk_ref, cwk_ref)
    v = conv_silu(v_ref, pv_ref, cwv_ref)
    q = q * lax.rsqrt(head_sums(q * q) + L2_EPS) * (GDN_HEAD ** -0.5)
    k = k * lax.rsqrt(head_sums(k * k) + L2_EPS)

    bm = misc_ref[:, 0:128]
    beta = _sigmoid(bm)
    g_full = -jnp.exp(alog_ref[...]) * _softplus(bm + dtb_ref[...])
    gc = _mm_xl(_tril_ones_x2(c), g_full)
    gc_t = jnp.concatenate([gc, gc], axis=0).T

    assert heads == N_GDN_HEADS
    bcol = [jnp.broadcast_to(beta[:, BETA_LANE + h:BETA_LANE + h + 1], (c, w)) for h in hs]
    gcol = [jnp.broadcast_to(gc[:, ALPHA_LANE + h:ALPHA_LANE + h + 1], (c, w)) for h in hs]
    grow = [jnp.broadcast_to(gc_t[ALPHA_LANE + h:ALPHA_LANE + h + 1, 0:c], (c, c)) for h in hs]

    ti = _iota((c, c), 0)
    tj = _iota((c, c), 1)
    causal = ti >= tj
    strict = ti > tj
    decay = [jnp.where(causal, jnp.exp(jnp.where(causal, gcol[h][:, 0:c] - grow[h], 0.0)), 0.0) for h in hs]

    q_g, k_g, v_g = G(q), G(k), G(v)
    kb = [k_g[h] * bcol[h] for h in hs]
    kk = [_mm_nt(jnp.concatenate([kb[h], q_g[h]], axis=0), k_g[h]) for h in hs]
    lneg = [jnp.where(strict, -kk[h][0:c] * decay[h], 0.0) for h in hs]
    a_intra = [kk[h][c:2 * c] * decay[h] for h in hs]

    egc = [jnp.exp(gcol[h]) for h in hs]
    g_last = [gcol[h][c - 1:c, :] for h in hs]
    x = [jnp.concatenate([v_g[h] * bcol[h], kb[h] * egc[h]], axis=1) for h in hs]
    pw = lneg
    steps = int(math.log2(c))
    for it in range(steps):
        x = [x[h] + _mm(pw[h], x[h]) for h in hs]
        if it + 1 < steps:
            pw = [_mm(pw[h], pw[h]) for h in hs]
    uu = [x[h][:, 0:w] for h in hs]
    ww = [x[h][:, w:2 * w] for h in hs]

    kd = [k_g[h] * jnp.exp(g_last[h] - gcol[h]) for h in hs]
    eye = jnp.where(_iota((w, w), 0) == _iota((w, w), 1), 1.0, 0.0)
    m_mat = [eye * jnp.exp(g_last[h]) - _mm_tn(kd[h], ww[h]) for h in hs]
    n_mat = [_mm_tn(kd[h], uu[h]) for h in hs]
    p_mat = [q_g[h] * egc[h] - _mm(a_intra[h], ww[h]) for h in hs]
    q_mat = [_mm(a_intra[h], uu[h]) for h in hs]

    s0 = [_bf(s_ref[h]) for h in hs]
    o = jnp.concatenate([_dot(_bf(p_mat[h]), s0[h]) + q_mat[h] for h in hs], axis=1)
    for h in hs:
        s_ref[h] = _dot(_bf(m_mat[h]), s0[h]) + n_mat[h]

    o = o * lax.rsqrt(head_sums(o * o) * (1.0 / GDN_HEAD) + NORM_EPS) * jnp.tile(nw_ref[...], (1, heads))
    z = z_ref[...]
    o_ref[...] = (o * (z * _sigmoid(z))).astype(o_ref.dtype)


def _gdn_mix(p, conv_w, alog_l, dtb_l, norm_w, batch, seq, lw=2048):
    c = CHUNK
    nc = seq // c
    nb = RW // lw
    rows8 = seq // V7X_SUBLANES

    def cur(off):
        assert off % lw == 0
        return pl.BlockSpec((c, lw), lambda b, h, t: (b * nc + t, off // lw + h))

    def prev(off):
        return pl.BlockSpec((V7X_SUBLANES, lw),
                            lambda b, h, t: (jnp.maximum(b * rows8 + t * (c // V7X_SUBLANES) - 1, 0), off // lw + h))

    def cw(off):
        return pl.BlockSpec((4, lw), lambda b, h, t: (0, off // lw + h))

    lane_vec = pl.BlockSpec((1, V7X_LANES), lambda b, h, t: (0, 0))
    return pl.pallas_call(
        _gdn_kernel,
        grid=(batch, nb, nc),
        in_specs=[
            cur(OFF_GQ), cur(OFF_GK), cur(OFF_GV), prev(OFF_GQ), prev(OFF_GK), prev(OFF_GV), cur(OFF_Z),
            pl.BlockSpec((c, MISC_W), lambda b, h, t: (b * nc + t, OFF_MISC // MISC_W)),
            cw(0), cw(RW), cw(2 * RW), lane_vec, lane_vec, lane_vec,
        ],
        out_specs=pl.BlockSpec((c, lw), lambda b, h, t: (b * nc + t, h)),
        out_shape=jax.ShapeDtypeStruct((batch * seq, RW), BF16),
        scratch_shapes=[pltpu.VMEM((lw // V7X_LANES, V7X_LANES, V7X_LANES), F32)],
        compiler_params=_cparams(("parallel", "parallel", "arbitrary")),
        name="gdn_mix",
    )(p, p, p, p, p, p, p, p, conv_w, conv_w, conv_w, alog_l, dtb_l, norm_w)


def _xattn_kernel(x_ref, ymix_ref, gmix_ref, k_ref, v_ref, wq_ref, wo_ref, gpre_ref, gpost_ref, gnext_ref,
                  h_out_ref, f_out_ref):
    ym = ymix_ref[...]
    h = x_ref[...] + ym * lax.rsqrt(jnp.mean(ym * ym, axis=-1, keepdims=True) + NORM_EPS) * gmix_ref[...]
    cn =(h * lax.rsqrt(jnp.mean(h * h, axis=-1, keepdims=True) + NORM_EPS) * gpre_ref[...]).astype(BF16)
    q = _dot(cn, wq_ref[...]).astype(BF16)
    kk = k_ref[0]
    vv = v_ref[0]
    outs = []
    for hh in range(XA_HEADS):
        sl = slice(hh * XA_HEAD, (hh + 1) * XA_HEAD)
        s = _dot_nt(q[:, sl], kk[:, sl]) * (XA_HEAD ** -0.5)
        s = s - jnp.max(s, axis=-1, keepdims=True)
        e = jnp.exp(s)
        pr = e / jnp.sum(e, axis=-1, keepdims=True)
        outs.append(_dot(pr.astype(BF16), vv[:, sl]))
    o = jnp.concatenate(outs, axis=1).astype(BF16)
    y = _dot(o, wo_ref[...])
    h2 = h + y * lax.rsqrt(jnp.mean(y * y, axis=-1, keepdims=True) + NORM_EPS) * gpost_ref[...]
    h_out_ref[...] = h2
    f_out_ref[...] = (h2 * lax.rsqrt(jnp.mean(h2 * h2, axis=-1, keepdims=True) + NORM_EPS)
                      * gnext_ref[...]).astype(f_out_ref.dtype)


def _xattn_block(x, y_mix, g_mix, kmem, vmem, w_q, w_o, g_pre, g_post, g_next, batch, seq, tt=256):
    n, d = x.shape
    mlen, xw = kmem.shape[1], kmem.shape[2]
    per_b = seq // tt
    row = pl.BlockSpec((tt, d), lambda i: (i, 0))
    vec = pl.BlockSpec((1, d), lambda i: (0, 0))
    mem = pl.BlockSpec((1, mlen, xw), lambda i: (i // per_b, 0, 0))
    return pl.pallas_call(
        _xattn_kernel,
        grid=(n // tt,),
        in_specs=[row, row, vec, mem, mem, pl.BlockSpec((d, xw), lambda i: (0, 0)),
                  pl.BlockSpec((xw, d), lambda i: (0, 0)), vec, vec, vec],
        out_specs=[row, row],
        out_shape=[jax.ShapeDtypeStruct((n, d), F32), jax.ShapeDtypeStruct((n, d), BF16)],
        compiler_params=_cparams(("parallel",)),
        name="xattn_block",
    )(x, y_mix, g_mix.reshape(1, d), kmem, vmem, w_q, w_o, g_pre.reshape(1, d), g_post.reshape(1, d),
      g_next.reshape(1, d))


SRC_LORA = 3 * RW
SRC_GDN = SRC_LORA + 2 * LORA + 256
SRC_BA = SRC_GDN + 4 * RW
SRC_GATES = SRC_BA + 2 * N_GDN_HEADS
RELAYOUT_ROWS = 512


def _relayout_kernel(x_ref, misc_ref, o_ref):
    @pl.when(pl.program_id(0) != OFF_MISC // RELAYOUT_ROWS)
    def _():
        o_ref[...] = x_ref[...].astype(o_ref.dtype)

    @pl.when(pl.program_id(0) == OFF_MISC // RELAYOUT_ROWS)
    def _():
        o_ref[...] = misc_ref[...].astype(o_ref.dtype)


def _relayout_w_in_t(w_in_t):
    n_in, d = w_in_t.shape
    rb = RELAYOUT_ROWS
    assert OFF_GQ % rb == 0 and OFF_MISC % rb == 0 and OFF_GATE_RW % rb == 0 and MISC_W == rb
    zeros = jnp.zeros((128 - LORA, d), w_in_t.dtype)
    misc = jnp.concatenate([w_in_t[SRC_LORA:SRC_LORA + LORA], w_in_t[SRC_BA:SRC_BA + 2 * N_GDN_HEADS],
                            w_in_t[SRC_LORA + LORA:SRC_LORA + 2 * LORA], zeros,
                            w_in_t[SRC_LORA + 2 * LORA:SRC_GDN]], axis=0)

    def src_row(j):
        r = j * rb
        src = jnp.where(r < OFF_GQ, r, jnp.where(r < OFF_MISC, r + (SRC_GDN - OFF_GQ),
                                                 jnp.where(r < OFF_GATE_RW, 0, r + (SRC_GATES - OFF_GATE_RW))))
        return pl.multiple_of(src, math.gcd(rb, SRC_GDN, SRC_GATES))

    return pl.pallas_call(
        _relayout_kernel,
        grid=(NP // rb,),
        in_specs=[pl.BlockSpec((pl.Element(rb), pl.Element(d)), lambda j: (src_row(j), 0)),
                  pl.BlockSpec((rb, d), lambda j: (0, 0))],
        out_specs=pl.BlockSpec((rb, d), lambda j: (j, 0)),
        out_shape=jax.ShapeDtypeStruct((NP, d), BF16),
        compiler_params=_cparams(("parallel",)),
        name="relayout_w_in",
    )(w_in_t, misc)


def _layer(x, mem, mix_norm_pre, mix_norm_post, w_in, rwkv_shift_mix, rwkv_w0, rwkv_w_up, rwkv_a0, rwkv_a_up,
           rwkv_g_up, rwkv_k_k, rwkv_k_a, rwkv_r_k, rwkv_gn_w, rwkv_gn_b, gdn_conv_w, gdn_a_log, gdn_dt_bias,
           gdn_norm_w, w_branch_rwkv, w_branch_gdn, w_mix_out, xa_norm_pre, xa_norm_mem, xa_norm_post, xa_w_q,
           xa_w_kv, xa_w_o, mlp_norm_pre, mlp_norm_post, mlp_w_up, mlp_w_down):
    batch, seq, d = x.shape
    n = batch * seq
    h = x.reshape(n, d)

    w_in_b = _relayout_w_in_t(jnp.swapaxes(w_in, 0, 1))
    mix = rwkv_shift_mix
    o = 3 * RW
    cp = jnp.stack([mix[0:RW], mix[RW:2 * RW], mix[2 * RW:o], rwkv_w0, rwkv_a0, rwkv_k_k, rwkv_k_a,
                    rwkv_r_k.reshape(RW)], axis=0)
    gp = jnp.concatenate([rwkv_gn_w[None], rwkv_gn_b[None], jnp.zeros((6, RW), F32)], axis=0)
    zeros = lambda k: jnp.zeros((k,), F32)
    mixm = jnp.concatenate([mix[o:o + LORA], zeros(32), mix[o + LORA:o + 2 * LORA], zeros(32),
                            mix[o + 2 * LORA:o + 2 * LORA + 256]])[None]
    w_up_p = jnp.concatenate([rwkv_w_up, jnp.zeros((128 - LORA, RW), F32)], axis=0).astype(BF16)
    a_up_p = jnp.concatenate([rwkv_a_up, jnp.zeros((128 - LORA, RW), F32)], axis=0).astype(BF16)
    g_up_b = rwkv_g_up.astype(BF16)
    alog_l = jnp.concatenate([zeros(ALPHA_LANE), gdn_a_log])[None]
    dtb_l = jnp.concatenate([zeros(ALPHA_LANE), gdn_dt_bias])[None]

    u = _rmsnorm(h, mix_norm_pre, BF16)
    p = _matmul(u, w_in_b, F32, tm=1024, tn=1280, b_is_nk=True, name="in_proj")
    y_rw = _rwkv_mix(p, cp, gp, mixm, w_up_p, a_up_p, g_up_b, batch, seq)
    y_gdn = _gdn_mix(p, gdn_conv_w, alog_l, dtb_l, gdn_norm_w[None], batch, seq)
    merged = _branch_merge(y_rw, y_gdn, w_branch_rwkv, w_branch_gdn, p)
    y_mix = _matmul_wcast(merged, w_mix_out, F32, tm=512, tn=1024, name="mix_out")

    mlen = mem.shape[1]
    m = _rmsnorm(mem.reshape(batch * mlen, d), xa_norm_mem, BF16)
    kv = _matmul(m, xa_w_kv.astype(BF16), BF16, tm=batch * mlen, tn=512, name="xa_kv")
    xw = XA_HEADS * XA_HEAD
    kmem = kv[:, :xw].reshape(batch, mlen, xw)
    vmem = kv[:, xw:].reshape(batch, mlen, xw)
    h, f = _xattn_block(h, y_mix, mix_norm_post, kmem, vmem, xa_w_q.astype(BF16), xa_w_o.astype(BF16),
                        xa_norm_pre, xa_norm_post, mlp_norm_pre, batch, seq)

    hmid = _matmul_wcast(f, mlp_w_up, BF16, tm=512, tn=1024, relu2=True, name="mlp_up")
    y_mlp = _matmul_kacc(hmid, mlp_w_down, tm=2048, tn=1024, tk=1024, name="mlp_down")
    h = _resid_norm(h, y_mlp, mlp_norm_post)
    return h.reshape(batch, seq, d)


def kernel(x, mem, mix_norm_pre, mix_norm_post, w_in, rwkv_shift_mix, rwkv_w0, rwkv_w_up, rwkv_a0, rwkv_a_up,
           rwkv_g_up, rwkv_k_k, rwkv_k_a, rwkv_r_k, rwkv_gn_w, rwkv_gn_b, gdn_conv_w, gdn_a_log, gdn_dt_bias,
           gdn_norm_w, w_branch_rwkv, w_branch_gdn, w_mix_out, xa_norm_pre, xa_norm_mem, xa_norm_post, xa_w_q,
           xa_w_kv, xa_w_o, mlp_norm_pre, mlp_norm_post, mlp_w_up, mlp_w_down):
    args = (mix_norm_pre, mix_norm_post, w_in, rwkv_shift_mix, rwkv_w0, rwkv_w_up, rwkv_a0, rwkv_a_up,
            rwkv_g_up, rwkv_k_k, rwkv_k_a, rwkv_r_k, rwkv_gn_w, rwkv_gn_b, gdn_conv_w, gdn_a_log, gdn_dt_bias,
            gdn_norm_w, w_branch_rwkv, w_branch_gdn, w_mix_out, xa_norm_pre, xa_norm_mem, xa_norm_post, xa_w_q,
            xa_w_kv, xa_w_o, mlp_norm_pre, mlp_norm_post, mlp_w_up, mlp_w_down)
    h = x
    for layer in range(mix_norm_pre.shape[0]):
        h = _layer(h, mem, *(a[layer] for a in args))
    return h
```

```python
import functools
import math

import jax
import jax.numpy as jnp
from jax import lax
from jax.experimental import pallas as pl
from jax.experimental.pallas import tpu as pltpu

F32 = jnp.float32
BF16 = jnp.bfloat16

V7X_LANES = 128
V7X_SUBLANES = 8
V7X_VMEM_LIMIT_BYTES = 56 * 1024 * 1024

NORM_EPS = 1e-6
L2_EPS = 1e-6
RWKV_GN_EPS = 64e-5
RWKV_HEAD = 64
GDN_HEAD = 128
CHUNK = 64
XA_HEADS = 4
XA_HEAD = 128

RW = 2048
OFF_R, OFF_K, OFF_V = 0, RW, 2 * RW
OFF_GQ, OFF_GK, OFF_GV = 3 * RW, 4 * RW, 5 * RW
OFF_Z = 6 * RW
OFF_MISC = 7 * RW
MISC_W = 512
OFF_GATE_RW = OFF_MISC + MISC_W
D_MODEL = 4096
OFF_GATE_GDN = OFF_GATE_RW + D_MODEL
NP = OFF_GATE_GDN + D_MODEL
LORA = 96
N_GDN_HEADS = RW // GDN_HEAD
BETA_LANE = LORA
ALPHA_LANE = LORA + N_GDN_HEADS


def _cparams(sem):
    return pltpu.CompilerParams(dimension_semantics=sem, vmem_limit_bytes=V7X_VMEM_LIMIT_BYTES)


def _dot(a, b):
    return jnp.dot(a, b, preferred_element_type=F32)


def _dot_nt(a, b):
    return lax.dot_general(a, b, (((1,), (1,)), ((), ())), preferred_element_type=F32)


def _dot_tn(a, b):
    return lax.dot_general(a, b, (((0,), (0,)), ((), ())), preferred_element_type=F32)


def _bf(x):
    return x.astype(BF16)


def _hi_lo(x):
    hi = x.astype(BF16)
    return hi, (x - hi.astype(F32)).astype(BF16)


def _mm(a, b):
    return _dot(_bf(a), _bf(b))


def _mm_nt(a, b):
    return _dot_nt(_bf(a), _bf(b))


def _mm_tn(a, b):
    return _dot_tn(_bf(a), _bf(b))


def _mm_xr(a, b01x2):
    hi, lo = _hi_lo(a)
    return _dot(jnp.concatenate([hi, lo], axis=1), b01x2)


def _mm_xl(a01x2, b):
    hi, lo = _hi_lo(b)
    return _dot(a01x2, jnp.concatenate([hi, lo], axis=0))


def _sigmoid(x):
    return 1.0 / (1.0 + jnp.exp(-x))


def _softplus(x):
    return jnp.maximum(x, 0.0) + jnp.log1p(jnp.exp(-jnp.abs(x)))


def _iota(shape, dim):
    return lax.broadcasted_iota(jnp.int32, shape, dim)


def _lane_groups(x):
    return [x[:, g * V7X_LANES:(g + 1) * V7X_LANES] for g in range(x.shape[1] // V7X_LANES)]


def _tril_ones_x2(c):
    return jnp.where(_iota((c, 2 * c), 0) >= _iota((c, 2 * c), 1) % c, 1.0, 0.0).astype(BF16)


def _rmsnorm_kernel(x_ref, g_ref, o_ref):
    x = x_ref[...]
    inv = lax.rsqrt(jnp.mean(x * x, axis=-1, keepdims=True) + NORM_EPS)
    o_ref[...] = (x * inv * g_ref[...]).astype(o_ref.dtype)


def _rmsnorm(x, gain, out_dtype, rows=256):
    n, d = x.shape
    return pl.pallas_call(
        _rmsnorm_kernel,
        grid=(n // rows,),
        in_specs=[pl.BlockSpec((rows, d), lambda i: (i, 0)), pl.BlockSpec((1, d), lambda i: (0, 0))],
        out_specs=pl.BlockSpec((rows, d), lambda i: (i, 0)),
        out_shape=jax.ShapeDtypeStruct((n, d), out_dtype),
        compiler_params=_cparams(("parallel",)),
        name="rmsnorm",
    )(x, gain.reshape(1, d))


def _resid_norm_kernel(h_ref, y_ref, g_ref, o_ref):
    y = y_ref[...]
    inv = lax.rsqrt(jnp.mean(y * y, axis=-1, keepdims=True) + NORM_EPS)
    o_ref[...] = h_ref[...] + y * inv * g_ref[...]


def _resid_norm(h, y, gain, rows=256):
    n, d = h.shape
    row = pl.BlockSpec((rows, d), lambda i: (i, 0))
    return pl.pallas_call(
        _resid_norm_kernel,
        grid=(n // rows,),
        in_specs=[row, row, pl.BlockSpec((1, d), lambda i: (0, 0))],
        out_specs=row,
        out_shape=jax.ShapeDtypeStruct((n, d), F32),
        compiler_params=_cparams(("parallel",)),
        name="resid_norm",
    )(h, y, gain.reshape(1, d))


def _mm_kernel(a_ref, b_ref, o_ref, *, relu2, b_is_nk):
    acc = _dot_nt(a_ref[...], b_ref[...]) if b_is_nk else _dot(a_ref[...], b_ref[...])
    if relu2:
        acc = jnp.square(jnp.maximum(acc, 0.0))
    o_ref[...] = acc.astype(o_ref.dtype)


def _matmul(a, b, out_dtype, tm, tn, relu2=False, b_is_nk=False, name="matmul"):
    m, k = a.shape
    n = b.shape[0] if b_is_nk else b.shape[1]
    b_spec = pl.BlockSpec((tn, k), lambda i, j: (j, 0)) if b_is_nk else pl.BlockSpec((k, tn), lambda i, j: (0, j))
    return pl.pallas_call(
        functools.partial(_mm_kernel, relu2=relu2, b_is_nk=b_is_nk),
        grid=(m // tm, n // tn),
        in_specs=[pl.BlockSpec((tm, k), lambda i, j: (i, 0)), b_spec],
        out_specs=pl.BlockSpec((tm, tn), lambda i, j: (i, j)),
        out_shape=jax.ShapeDtypeStruct((m, n), out_dtype),
        compiler_params=_cparams(("parallel", "arbitrary")),
        name=name,
    )(a, b)


def _mm_wcast_kernel(a_ref, b_ref, o_ref, b_bf16, *, relu2):
    @pl.when(pl.program_id(1) == 0)
    def _():
        b_bf16[...] = b_ref[...].astype(BF16)

    acc = _dot(a_ref[...], b_bf16[...])
    if relu2:
        acc = jnp.square(jnp.maximum(acc, 0.0))
    o_ref[...] = acc.astype(o_ref.dtype)


def _matmul_wcast(a, b, out_dtype, tm, tn, relu2=False, name="matmul_wcast"):
    m, k = a.shape
    _, n = b.shape
    return pl.pallas_call(
        functools.partial(_mm_wcast_kernel, relu2=relu2),
        grid=(n // tn, m // tm),
        in_specs=[pl.BlockSpec((tm, k), lambda j, i: (i, 0)), pl.BlockSpec((k, tn), lambda j, i: (0, j))],
        out_specs=pl.BlockSpec((tm, tn), lambda j, i: (i, j)),
        out_shape=jax.ShapeDtypeStruct((m, n), out_dtype),
        scratch_shapes=[pltpu.VMEM((k, tn), BF16)],
        compiler_params=_cparams(("parallel", "arbitrary")),
        name=name,
    )(a, b)


def _mm_kacc_kernel(a_ref, b_ref, o_ref, *, k_chunk):
    @pl.when(pl.program_id(2) == 0)
    def _():
        o_ref[...] = jnp.zeros_like(o_ref)

    acc = None
    for k0 in range(0, a_ref.shape[1], k_chunk):
        part = _dot(a_ref[:, k0:k0 + k_chunk], b_ref[k0:k0 + k_chunk, :].astype(BF16))
        acc = part if acc is None else acc + part
    o_ref[...] += acc


def _matmul_kacc(a, b, tm, tn, tk, k_chunk=512, name="matmul_kacc"):
    m, k = a.shape
    _, n = b.shape
    return pl.pallas_call(
        functools.partial(_mm_kacc_kernel, k_chunk=k_chunk),
        grid=(m // tm, n // tn, k // tk),
        in_specs=[pl.BlockSpec((tm, tk), lambda i, j, l: (i, l)), pl.BlockSpec((tk, tn), lambda i, j, l: (l, j))],
        out_specs=pl.BlockSpec((tm, tn), lambda i, j, l: (i, j)),
        out_shape=jax.ShapeDtypeStruct((m, n), F32),
        compiler_params=_cparams(("parallel", "parallel", "arbitrary")),
        name=name,
    )(a, b)


def _branch_kernel(yr_ref, yg_ref, wr_ref, wg_ref, gr_ref, gg_ref, o_ref, wr_bf16, wg_bf16):
    @pl.when(pl.program_id(1) == 0)
    def _():
        wr_bf16[...] = wr_ref[...].astype(BF16)
        wg_bf16[...] = wg_ref[...].astype(BF16)

    br = _dot(yr_ref[...], wr_bf16[...])
    bg = _dot(yg_ref[...], wg_bf16[...])
    o_ref[...] = (_sigmoid(gr_ref[...]) * br + _sigmoid(gg_ref[...]) * bg).astype(o_ref.dtype)


def _branch_merge(y_rw, y_gdn, w_rw, w_gdn, p, tm=1024, tn=512):
    m, k = y_rw.shape
    n = w_rw.shape[1]
    jr, jg = OFF_GATE_RW // tn, OFF_GATE_GDN // tn
    return pl.pallas_call(
        _branch_kernel,
        grid=(n // tn, m // tm),
        in_specs=[
            pl.BlockSpec((tm, k), lambda j, i: (i, 0)),
            pl.BlockSpec((tm, k), lambda j, i: (i, 0)),
            pl.BlockSpec((k, tn), lambda j, i: (0, j)),
            pl.BlockSpec((k, tn), lambda j, i: (0, j)),
            pl.BlockSpec((tm, tn), lambda j, i: (i, jr + j)),
            pl.BlockSpec((tm, tn), lambda j, i: (i, jg + j)),
        ],
        out_specs=pl.BlockSpec((tm, tn), lambda j, i: (i, j)),
        out_shape=jax.ShapeDtypeStruct((m, n), BF16),
        scratch_shapes=[pltpu.VMEM((k, tn), BF16)] * 2,
        compiler_params=_cparams(("parallel", "arbitrary")),
        name="branch_merge",
    )(y_rw, y_gdn, w_rw, w_gdn, p, p)


def _stage_rows(buf_ref, x_ref, prev_ref, first):
    buf_ref[0:V7X_SUBLANES, :] = jnp.where(first, 0.0, prev_ref[...])
    buf_ref[V7X_SUBLANES:, :] = x_ref[...]


def _shifted(buf_ref, s):
    return buf_ref[V7X_SUBLANES - s:buf_ref.shape[0] - s, :]


def _token_shift(buf_ref, x_ref, prev_ref, first, mix):
    _stage_rows(buf_ref, x_ref, prev_ref, first)
    x = x_ref[...]
    return x + (_shifted(buf_ref, 1) - x) * mix


def _expand2(x, head0):
    return jnp.concatenate([jnp.where(head0, x, 0.0), jnp.where(head0, 0.0, x)], axis=0)


def _rwkv_kernel(r_ref, k_ref, v_ref, pr_ref, pk_ref, pv_ref, misc_ref, pmisc_ref,
                 cp_ref, gp_ref, mixm_ref, wup_ref, aup_ref, gup_ref, o_ref, s_ref, rbuf, kbuf, vbuf, mbuf):
    c = CHUNK
    w = V7X_LANES
    ci = pl.program_id(2)

    @pl.when(ci == 0)
    def _():
        s_ref[...] = jnp.zeros_like(s_ref)

    first = ci == 0
    mix_r, mix_k, mix_v, w0, a0, k_k, k_a, r_k = (cp_ref[i:i + 1, :] for i in range(8))
    gn_w, gn_b = gp_ref[0:1, :], gp_ref[1:2, :]

    misc_m = _token_shift(mbuf, misc_ref, pmisc_ref, first, mixm_ref[...])
    xr = _token_shift(rbuf, r_ref, pr_ref, first, mix_r)
    xk = _token_shift(kbuf, k_ref, pk_ref, first, mix_k)
    xv = _token_shift(vbuf, v_ref, pv_ref, first, mix_v)

    wlin = w0 + _dot(_bf(jnp.tanh(misc_m[:, 0:128])), wup_ref[...])
    logw = -math.exp(-0.5) * _sigmoid(wlin)
    a = _sigmoid(a0 + _dot(_bf(misc_m[:, 128:256]), aup_ref[...]))
    gate = _dot(_bf(_sigmoid(misc_m[:, 256:512])), gup_ref[...])

    same_head = (_iota((w, w), 0) // RWKV_HEAD) == (_iota((w, w), 1) // RWKV_HEAD)
    bones = jnp.where(same_head, 1.0, 0.0).astype(BF16)
    eye = jnp.where(_iota((w, w), 0) == _iota((w, w), 1), 1.0, 0.0)

    def head_sums(x):
        n = x.shape[1] // w
        s = _dot(_bf(jnp.concatenate(_lane_groups(x), axis=0)), bones)
        return jnp.concatenate([s[g * c:(g + 1) * c] for g in range(n)], axis=1)

    kkr = xk * k_k
    kkn = kkr * lax.rsqrt(head_sums(kkr * kkr) + L2_EPS)
    k2 = xk * (1.0 + (a - 1.0) * k_a)
    am = -kkn
    bm = kkn * a

    cum = _mm_xl(_tril_ones_x2(c), logw)
    cum_ex = cum - logw
    c_last = cum[c - 1:c, :]
    c_mid = cum[c // 2 - 1:c // 2, :]
    e_out = jnp.exp(c_mid - cum)
    e_last = jnp.exp(c_last - cum)
    rt = xr * jnp.exp(cum - c_mid)
    at = am * jnp.exp(cum_ex - c_mid)
    bt = bm * e_out
    kt = k2 * e_out
    a_abs = am * jnp.exp(cum_ex)
    r_abs = xr * jnp.exp(cum)
    bh = bm * e_last
    kh = k2 * e_last
    w_c = jnp.exp(c_last)

    head0 = _iota((c, w), 1) < RWKV_HEAD
    head0_2 = _iota((c, 2 * w), 1) % w < RWKV_HEAD
    wi = _iota((c, 2 * c), 0)
    wj = _iota((c, 2 * c), 1) % c
    strict = wi > wj
    incl = wi >= wj

    gs = range(r_ref.shape[1] // w)
    G = _lane_groups
    at_g, rt_g, bt_g, kt_g, xv_g = G(at), G(rt), G(bt), G(kt), G(xv)
    aa = [_mm_nt(jnp.concatenate([at_g[g], rt_g[g]], axis=0),
                 jnp.concatenate([_expand2(bt_g[g], head0), _expand2(kt_g[g], head0)], axis=0)) for g in gs]
    a_ab = [jnp.where(strict, aa[g][0:c, 0:2 * c], 0.0) for g in gs]
    a_ak = [jnp.where(strict, aa[g][0:c, 2 * c:4 * c], 0.0) for g in gs]
    a_rb = [jnp.where(incl, aa[g][c:2 * c, 0:2 * c], 0.0) for g in gs]
    a_rk = [jnp.where(incl, aa[g][c:2 * c, 2 * c:4 * c], 0.0) for g in gs]

    xv_e = [_expand2(xv_g[g], head0) for g in gs]
    akv = [_mm(a_ak[g], xv_e[g]) for g in gs]
    eye_w = jnp.where(wi == wj, 1.0, 0.0)
    tw = [eye_w + a_ab[g] for g in gs]
    pw = [_mm(a_ab[g], _expand2(a_ab[g], head0)) for g in gs]
    levels = int(math.log2(c))
    for lvl in range(1, levels):
        if lvl + 1 < levels:
            pt = [_mm(pw[g], _expand2(jnp.concatenate([pw[g], tw[g]], axis=1), head0_2)) for g in gs]
            pw = [pt[g][:, 0:w] for g in gs]
            tw = [tw[g] + pt[g][:, w:2 * w] for g in gs]
        else:
            tw = [tw[g] + _mm(pw[g], _expand2(tw[g], head0)) for g in gs]
    a_abs_g = G(a_abs)
    x = [_mm(tw[g], _expand2(jnp.concatenate([a_abs_g[g], akv[g]], axis=1), head0_2)) for g in gs]
    wm = [x[g][:, 0:w] for g in gs]
    u0 = [x[g][:, w:2 * w] for g in gs]

    bh_g, kh_g, wc_g, rabs_g = G(bh), G(kh), G(w_c), G(r_abs)
    m_mat = [jnp.where(same_head, _mm_tn(bh_g[g], wm[g]), 0.0) + eye * wc_g[g] for g in gs]
    n_mat = [jnp.where(same_head, _mm_tn(jnp.concatenate([bh_g[g], kh_g[g]], axis=0),
                                         jnp.concatenate([u0[g], xv_g[g]], axis=0)), 0.0) for g in gs]
    p_mat = [rabs_g[g] + _mm(a_rb[g], _expand2(wm[g], head0)) for g in gs]
    q_mat = [_mm(jnp.concatenate([a_rb[g], a_rk[g]], axis=1),
                 jnp.concatenate([_expand2(u0[g], head0), xv_e[g]], axis=0)) for g in gs]

    h0 = [_bf(s_ref[g]) for g in gs]
    y = jnp.concatenate([_dot(_bf(p_mat[g]), h0[g]) + q_mat[g] for g in gs], axis=1)
    for g in gs:
        s_ref[g] = _dot(_bf(m_mat[g]), h0[g]) + n_mat[g]

    inv_n = 1.0 / RWKV_HEAD
    d = y - head_sums(y) * inv_n
    var = head_sums(d * d) * inv_n
    yn = d * lax.rsqrt(var + RWKV_GN_EPS) * gn_w + gn_b
    bonus = head_sums(xr * k2 * r_k) * xv
    o_ref[...] = ((yn + bonus) * gate).astype(o_ref.dtype)


def _rwkv_mix(p, cp, gp, mixm, w_up_p, a_up_p, g_up, batch, seq, lw=2048):
    c = CHUNK
    nc = seq // c
    nb = RW // lw
    rows8 = seq // V7X_SUBLANES

    def cur(off):
        assert off % lw == 0
        return pl.BlockSpec((c, lw), lambda b, h, t: (b * nc + t, off // lw + h))

    def prev(off):
        return pl.BlockSpec((V7X_SUBLANES, lw),
                            lambda b, h, t: (jnp.maximum(b * rows8 + t * (c // V7X_SUBLANES) - 1, 0), off // lw + h))

    def par(rows):
        return pl.BlockSpec((rows, lw), lambda b, h, t: (0, h))

    return pl.pallas_call(
        _rwkv_kernel,
        grid=(batch, nb, nc),
        in_specs=[
            cur(OFF_R), cur(OFF_K), cur(OFF_V), prev(OFF_R), prev(OFF_K), prev(OFF_V),
            pl.BlockSpec((c, MISC_W), lambda b, h, t: (b * nc + t, OFF_MISC // MISC_W)),
            pl.BlockSpec((V7X_SUBLANES, MISC_W),
                         lambda b, h, t: (jnp.maximum(b * rows8 + t * (c // V7X_SUBLANES) - 1, 0), OFF_MISC // MISC_W)),
            par(8), par(8),
            pl.BlockSpec((1, MISC_W), lambda b, h, t: (0, 0)),
            par(128), par(128), par(256),
        ],
        out_specs=pl.BlockSpec((c, lw), lambda b, h, t: (b * nc + t, h)),
        out_shape=jax.ShapeDtypeStruct((batch * seq, RW), BF16),
        scratch_shapes=[pltpu.VMEM((lw // V7X_LANES, V7X_LANES, V7X_LANES), F32)]
        + [pltpu.VMEM((V7X_SUBLANES + c, lw), F32)] * 3 + [pltpu.VMEM((V7X_SUBLANES + c, MISC_W), F32)],
        compiler_params=_cparams(("parallel", "parallel", "arbitrary")),
        name="rwkv7_mix",
    )(p, p, p, p, p, p, p, p, cp, gp, mixm, w_up_p, a_up_p, g_up)


def _gdn_kernel(q_ref, k_ref, v_ref, pq_ref, pk_ref, pv_ref, z_ref, misc_ref,
                cwq_ref, cwk_ref, cwv_ref, alog_ref, dtb_ref, nw_ref, o_ref, s_ref):
    c = CHUNK
    w = V7X_LANES
    ci = pl.program_id(2)
    heads = q_ref.shape[1] // w
    hs = range(heads)
    G = _lane_groups

    @pl.when(ci == 0)
    def _():
        s_ref[...] = jnp.zeros_like(s_ref)

    first = ci == 0

    def conv_silu(x_ref, prev_ref, cw_ref):
        x = x_ref[...]
        prev8 = jnp.where(first, 0.0, prev_ref[...])
        row = _iota(x.shape, 0)
        acc = x * cw_ref[3:4, :]
        for s in (1, 2, 3):
            tail = jnp.tile(pltpu.roll(prev8, s, axis=0), (c // V7X_SUBLANES, 1))
            acc = acc + jnp.where(row < s, tail, pltpu.roll(x, s, axis=0)) * cw_ref[3 - s:4 - s, :]
        return acc * _sigmoid(acc)

    def head_sums(x):
        return jnp.concatenate([jnp.broadcast_to(jnp.sum(g, axis=-1, keepdims=True), g.shape) for g in G(x)], axis=1)

    q = conv_silu(q_ref, pq_ref, cwq_ref)
    k = conv_silu(k_ref, pk_ref, cwk_ref)
    v = conv_silu(v_ref, pv_ref, cwv_ref)
    q = q * lax.rsqrt(head_sums(q * q) + L2_EPS) * (GDN_HEAD ** -0.5)
    k = k * lax.rsqrt(head_sums(k * k) + L2_EPS)

    bm = misc_ref[:, 0:128]
    beta_hl = jnp.concatenate(_hi_lo(_sigmoid(bm)), axis=1)
    g_full = -jnp.exp(alog_ref[...]) * _softplus(bm + dtb_ref[...])
    gc_hl = jnp.concatenate(_hi_lo(_mm_xl(_tril_ones_x2(c), g_full)), axis=1)

    assert heads == N_GDN_HEADS
    lane2 = _iota((2 * w, w), 0) % w
    lane2r = _iota((c, 2 * w), 1) % w
    bcol = [_dot(beta_hl, jnp.where(lane2 == BETA_LANE + h, 1.0, 0.0).astype(BF16)) for h in hs]
    gcol = [_dot(gc_hl, jnp.where(lane2 == ALPHA_LANE + h, 1.0, 0.0).astype(BF16)) for h in hs]
    grow = [_dot_nt(jnp.where(lane2r == ALPHA_LANE + h, 1.0, 0.0).astype(BF16), gc_hl) for h in hs]

    ti = _iota((c, c), 0)
    tj = _iota((c, c), 1)
    causal = ti >= tj
    strict = ti > tj
    decay = [jnp.where(causal, jnp.exp(jnp.where(causal, gcol[h][:, 0:c] - grow[h], 0.0)), 0.0) for h in hs]

    q_g, k_g, v_g = G(q), G(k), G(v)
    kb = [k_g[h] * bcol[h] for h in hs]
    kk = [_mm_nt(jnp.concatenate([kb[h], q_g[h]], axis=0), k_g[h]) for h in hs]
    lneg = [jnp.where(strict, -kk[h][0:c] * decay[h], 0.0) for h in hs]
    a_intra = [kk[h][c:2 * c] * decay[h] for h in hs]

    egc = [jnp.exp(gcol[h]) for h in hs]
    g_last = [gcol[h][c - 1:c, :] for h in hs]
    lane_t = _iota((c, w), 1) >= c
    eye_t = jnp.where(_iota((c, w), 1) == _iota((c, w), 0) + c, 1.0, 0.0)
    zt = [jnp.concatenate([lneg[h], jnp.zeros((c, c), F32)], axis=1) + eye_t for h in hs]
    for _ in range(int(math.log2(c))):
        zt = [_mm(zt[h][:, 0:c], zt[h]) + jnp.where(lane_t, zt[h], 0.0) for h in hs]
    zeros_x = jnp.zeros((c, 2 * w), F32)
    x = [_mm(zt[h], jnp.concatenate([zeros_x, jnp.concatenate([v_g[h] * bcol[h], kb[h] * egc[h]], axis=1)], axis=0))
         for h in hs]
    uu = [x[h][:, 0:w] for h in hs]
    ww = [x[h][:, w:2 * w] for h in hs]

    kd = [k_g[h] * jnp.exp(g_last[h] - gcol[h]) for h in hs]
    eye = jnp.where(_iota((w, w), 0) == _iota((w, w), 1), 1.0, 0.0)
    m_mat = [eye * jnp.exp(g_last[h]) - _mm_tn(kd[h], ww[h]) for h in hs]
    n_mat = [_mm_tn(kd[h], uu[h]) for h in hs]
    p_mat = [q_g[h] * egc[h] - _mm(a_intra[h], ww[h]) for h in hs]
    q_mat = [_mm(a_intra[h], uu[h]) for h in hs]

    s0 = [_bf(s_ref[h]) for h in hs]
    o = jnp.concatenate([_dot(_bf(p_mat[h]), s0[h]) + q_mat[h] for h in hs], axis=1)
    for h in hs:
        s_ref[h] = _dot(_bf(m_mat[h]), s0[h]) + n_mat[h]

    o = o * lax.rsqrt(head_sums(o * o) * (1.0 / GDN_HEAD) + NORM_EPS) * jnp.tile(nw_ref[...], (1, heads))
    z = z_ref[...]
    o_ref[...] = (o * (z * _sigmoid(z))).astype(o_ref.dtype)


def _gdn_mix(p, conv_w, alog_l, dtb_l, norm_w, batch, seq, lw=2048):
    c = CHUNK
    nc = seq // c
    nb = RW // lw
    rows8 = seq // V7X_SUBLANES

    def cur(off):
        assert off % lw == 0
        return pl.BlockSpec((c, lw), lambda b, h, t: (b * nc + t, off // lw + h))

    def prev(off):
        return pl.BlockSpec((V7X_SUBLANES, lw),
                            lambda b, h, t: (jnp.maximum(b * rows8 + t * (c // V7X_SUBLANES) - 1, 0), off // lw + h))

    def cw(off):
        return pl.BlockSpec((4, lw), lambda b, h, t: (0, off // lw + h))

    lane_vec = pl.BlockSpec((1, V7X_LANES), lambda b, h, t: (0, 0))
    return pl.pallas_call(
        _gdn_kernel,
        grid=(batch, nb, nc),
        in_specs=[
            cur(OFF_GQ), cur(OFF_GK), cur(OFF_GV), prev(OFF_GQ), prev(OFF_GK), prev(OFF_GV), cur(OFF_Z),
            pl.BlockSpec((c, MISC_W), lambda b, h, t: (b * nc + t, OFF_MISC // MISC_W)),
            cw(0), cw(RW), cw(2 * RW), lane_vec, lane_vec, lane_vec,
        ],
        out_specs=pl.BlockSpec((c, lw), lambda b, h, t: (b * nc + t, h)),
        out_shape=jax.ShapeDtypeStruct((batch * seq, RW), BF16),
        scratch_shapes=[pltpu.VMEM((lw // V7X_LANES, V7X_LANES, V7X_LANES), F32)],
        compiler_params=_cparams(("parallel", "parallel", "arbitrary")),
        name="gdn_mix",
    )(p, p, p, p, p, p, p, p, conv_w, conv_w, conv_w, alog_l, dtb_l, norm_w)


def _xattn_kernel(x_ref, ymix_ref, gmix_ref, k_ref, v_ref, wq_ref, wo_ref, gpre_ref, gpost_ref, gnext_ref,
                  h_out_ref, f_out_ref):
    ym = ymix_ref[...]
    h = x_ref[...] + ym * lax.rsqrt(jnp.mean(ym * ym, axis=-1, keepdims=True) + NORM_EPS) * gmix_ref[...]
    cn =(h * lax.rsqrt(jnp.mean(h * h, axis=-1, keepdims=True) + NORM_EPS) * gpre_ref[...]).astype(BF16)
    q = _dot(cn, wq_ref[...]).astype(BF16)
    kk = k_ref[0]
    vv = v_ref[0]
    outs = []
    for hh in range(XA_HEADS):
        sl = slice(hh * XA_HEAD, (hh + 1) * XA_HEAD)
        s = _dot_nt(q[:, sl], kk[:, sl]) * (XA_HEAD ** -0.5)
        s = s - jnp.max(s, axis=-1, keepdims=True)
        e = jnp.exp(s)
        pr = e / jnp.sum(e, axis=-1, keepdims=True)
        outs.append(_dot(pr.astype(BF16), vv[:, sl]))
    o = jnp.concatenate(outs, axis=1).astype(BF16)
    y = _dot(o, wo_ref[...])
    h2 = h + y * lax.rsqrt(jnp.mean(y * y, axis=-1, keepdims=True) + NORM_EPS) * gpost_ref[...]
    h_out_ref[...] = h2
    f_out_ref[...] = (h2 * lax.rsqrt(jnp.mean(h2 * h2, axis=-1, keepdims=True) + NORM_EPS)
                      * gnext_ref[...]).astype(f_out_ref.dtype)


def _xattn_block(x, y_mix, g_mix, kmem, vmem, w_q, w_o, g_pre, g_post, g_next, batch, seq, tt=256):
    n, d = x.shape
    mlen, xw = kmem.shape[1], kmem.shape[2]
    per_b = seq // tt
    row = pl.BlockSpec((tt, d), lambda i: (i, 0))
    vec = pl.BlockSpec((1, d), lambda i: (0, 0))
    mem = pl.BlockSpec((1, mlen, xw), lambda i: (i // per_b, 0, 0))
    return pl.pallas_call(
        _xattn_kernel,
        grid=(n // tt,),
        in_specs=[row, row, vec, mem, mem, pl.BlockSpec((d, xw), lambda i: (0, 0)),
                  pl.BlockSpec((xw, d), lambda i: (0, 0)), vec, vec, vec],
        out_specs=[row, row],
        out_shape=[jax.ShapeDtypeStruct((n, d), F32), jax.ShapeDtypeStruct((n, d), BF16)],
        compiler_params=_cparams(("parallel",)),
        name="xattn_block",
    )(x, y_mix, g_mix.reshape(1, d), kmem, vmem, w_q, w_o, g_pre.reshape(1, d), g_post.reshape(1, d),
      g_next.reshape(1, d))


SRC_LORA = 3 * RW
SRC_GDN = SRC_LORA + 2 * LORA + 256
SRC_BA = SRC_GDN + 4 * RW
SRC_GATES = SRC_BA + 2 * N_GDN_HEADS
RELAYOUT_ROWS = 512


def _relayout_kernel(x_ref, misc_ref, o_ref):
    @pl.when(pl.program_id(0) != OFF_MISC // RELAYOUT_ROWS)
    def _():
        o_ref[...] = x_ref[...].astype(o_ref.dtype)

    @pl.when(pl.program_id(0) == OFF_MISC // RELAYOUT_ROWS)
    def _():
        o_ref[...] = misc_ref[...].astype(o_ref.dtype)


def _relayout_w_in_t(w_in_t):
    n_in, d = w_in_t.shape
    rb = RELAYOUT_ROWS
    assert OFF_GQ % rb == 0 and OFF_MISC % rb == 0 and OFF_GATE_RW % rb == 0 and MISC_W == rb
    zeros = jnp.zeros((128 - LORA, d), w_in_t.dtype)
    misc = jnp.concatenate([w_in_t[SRC_LORA:SRC_LORA + LORA], w_in_t[SRC_BA:SRC_BA + 2 * N_GDN_HEADS],
                            w_in_t[SRC_LORA + LORA:SRC_LORA + 2 * LORA], zeros,
                            w_in_t[SRC_LORA + 2 * LORA:SRC_GDN]], axis=0)

    def src_row(j):
        r = j * rb
        src = jnp.where(r < OFF_GQ, r, jnp.where(r < OFF_MISC, r + (SRC_GDN - OFF_GQ),
                                                 jnp.where(r < OFF_GATE_RW, 0, r + (SRC_GATES - OFF_GATE_RW))))
        return pl.multiple_of(src, math.gcd(rb, SRC_GDN, SRC_GATES))

    return pl.pallas_call(
        _relayout_kernel,
        grid=(NP // rb,),
        in_specs=[pl.BlockSpec((pl.Element(rb), pl.Element(d)), lambda j: (src_row(j), 0)),
                  pl.BlockSpec((rb, d), lambda j: (0, 0))],
        out_specs=pl.BlockSpec((rb, d), lambda j: (j, 0)),
        out_shape=jax.ShapeDtypeStruct((NP, d), BF16),
        compiler_params=_cparams(("parallel",)),
        name="relayout_w_in",
    )(w_in_t, misc)


def _layer(x, mem, mix_norm_pre, mix_norm_post, w_in, rwkv_shift_mix, rwkv_w0, rwkv_w_up, rwkv_a0, rwkv_a_up,
           rwkv_g_up, rwkv_k_k, rwkv_k_a, rwkv_r_k, rwkv_gn_w, rwkv_gn_b, gdn_conv_w, gdn_a_log, gdn_dt_bias,
           gdn_norm_w, w_branch_rwkv, w_branch_gdn, w_mix_out, xa_norm_pre, xa_norm_mem, xa_norm_post, xa_w_q,
           xa_w_kv, xa_w_o, mlp_norm_pre, mlp_norm_post, mlp_w_up, mlp_w_down):
    batch, seq, d = x.shape
    n = batch * seq
    h = x.reshape(n, d)

    w_in_b = _relayout_w_in_t(jnp.swapaxes(w_in, 0, 1))
    mix = rwkv_shift_mix
    o = 3 * RW
    cp = jnp.stack([mix[0:RW], mix[RW:2 * RW], mix[2 * RW:o], rwkv_w0, rwkv_a0, rwkv_k_k, rwkv_k_a,
                    rwkv_r_k.reshape(RW)], axis=0)
    gp = jnp.concatenate([rwkv_gn_w[None], rwkv_gn_b[None], jnp.zeros((6, RW), F32)], axis=0)
    zeros = lambda k: jnp.zeros((k,), F32)
    mixm = jnp.concatenate([mix[o:o + LORA], zeros(32), mix[o + LORA:o + 2 * LORA], zeros(32),
                            mix[o + 2 * LORA:o + 2 * LORA + 256]])[None]
    w_up_p = jnp.concatenate([rwkv_w_up, jnp.zeros((128 - LORA, RW), F32)], axis=0).astype(BF16)
    a_up_p = jnp.concatenate([rwkv_a_up, jnp.zeros((128 - LORA, RW), F32)], axis=0).astype(BF16)
    g_up_b = rwkv_g_up.astype(BF16)
    alog_l = jnp.concatenate([zeros(ALPHA_LANE), gdn_a_log])[None]
    dtb_l = jnp.concatenate([zeros(ALPHA_LANE), gdn_dt_bias])[None]

    u = _rmsnorm(h, mix_norm_pre, BF16)
    p = _matmul(u, w_in_b, F32, tm=1024, tn=1280, b_is_nk=True, name="in_proj")
    y_rw = _rwkv_mix(p, cp, gp, mixm, w_up_p, a_up_p, g_up_b, batch, seq)
    y_gdn = _gdn_mix(p, gdn_conv_w, alog_l, dtb_l, gdn_norm_w[None], batch, seq)
    merged = _branch_merge(y_rw, y_gdn, w_branch_rwkv, w_branch_gdn, p)
    y_mix = _matmul_wcast(merged, w_mix_out, F32, tm=512, tn=1024, name="mix_out")

    mlen = mem.shape[1]
    m = _rmsnorm(mem.reshape(batch * mlen, d), xa_norm_mem, BF16)
    kv = _matmul(m, xa_w_kv.astype(BF16), BF16, tm=batch * mlen, tn=512, name="xa_kv")
    xw = XA_HEADS * XA_HEAD
    kmem = kv[:, :xw].reshape(batch, mlen, xw)
    vmem = kv[:, xw:].reshape(batch, mlen, xw)
    h, f = _xattn_block(h, y_mix, mix_norm_post, kmem, vmem, xa_w_q.astype(BF16), xa_w_o.astype(BF16),
                        xa_norm_pre, xa_norm_post, mlp_norm_pre, batch, seq)

    hmid = _matmul_wcast(f, mlp_w_up, BF16, tm=512, tn=1024, relu2=True, name="mlp_up")
    y_mlp = _matmul_kacc(hmid, mlp_w_down, tm=2048, tn=1024, tk=1024, name="mlp_down")
    h = _resid_norm(h, y_mlp, mlp_norm_post)
    return h.reshape(batch, seq, d)


def kernel(x, mem, mix_norm_pre, mix_norm_post, w_in, rwkv_shift_mix, rwkv_w0, rwkv_w_up, rwkv_a0, rwkv_a_up,
           rwkv_g_up, rwkv_k_k, rwkv_k_a, rwkv_r_k, rwkv_gn_w, rwkv_gn_b, gdn_conv_w, gdn_a_log, gdn_dt_bias,
           gdn_norm_w, w_branch_rwkv, w_branch_gdn, w_mix_out, xa_norm_pre, xa_norm_mem, xa_norm_post, xa_w_q,
           xa_w_kv, xa_w_o, mlp_norm_pre, mlp_norm_post, mlp_w_up, mlp_w_down):
    args = (mix_norm_pre, mix_norm_post, w_in, rwkv_shift_mix, rwkv_w0, rwkv_w_up, rwkv_a0, rwkv_a_up,
            rwkv_g_up, rwkv_k_k, rwkv_k_a, rwkv_r_k, rwkv_gn_w, rwkv_gn_b, gdn_conv_w, gdn_a_log, gdn_dt_bias,
            gdn_norm_w, w_branch_rwkv, w_branch_gdn, w_mix_out, xa_norm_pre, xa_norm_mem, xa_norm_post, xa_w_q,
            xa_w_kv, xa_w_o, mlp_norm_pre, mlp_norm_post, mlp_w_up, mlp_w_down)
    h = x
    for layer in range(mix_norm_pre.shape[0]):
        h = _layer(h, mem, *(a[layer] for a in args))
    return h
```

```python
import functools
import math

import jax
import jax.numpy as jnp
from jax import lax
from jax.experimental import pallas as pl
from jax.experimental.pallas import tpu as pltpu

F32 = jnp.float32
BF16 = jnp.bfloat16

V7X_LANES = 128
V7X_SUBLANES = 8
V7X_VMEM_LIMIT_BYTES = 56 * 1024 * 1024

NORM_EPS = 1e-6
L2_EPS = 1e-6
RWKV_GN_EPS = 64e-5
RWKV_HEAD = 64
GDN_HEAD = 128
CHUNK = 64
XA_HEADS = 4
XA_HEAD = 128

RW = 2048
OFF_R, OFF_K, OFF_V = 0, RW, 2 * RW
OFF_GQ, OFF_GK, OFF_GV = 3 * RW, 4 * RW, 5 * RW
OFF_Z = 6 * RW
OFF_MISC = 7 * RW
MISC_W = 512
OFF_GATE_RW = OFF_MISC + MISC_W
D_MODEL = 4096
OFF_GATE_GDN = OFF_GATE_RW + D_MODEL
NP = OFF_GATE_GDN + D_MODEL
LORA = 96
N_GDN_HEADS = RW // GDN_HEAD
BETA_LANE = LORA
ALPHA_LANE = LORA + N_GDN_HEADS


def _cparams(sem):
    return pltpu.CompilerParams(dimension_semantics=sem, vmem_limit_bytes=V7X_VMEM_LIMIT_BYTES)


def _dot(a, b):
    return jnp.dot(a, b, preferred_element_type=F32)


def _dot_nt(a, b):
    return lax.dot_general(a, b, (((1,), (1,)), ((), ())), preferred_element_type=F32)


def _dot_tn(a, b):
    return lax.dot_general(a, b, (((0,), (0,)), ((), ())), preferred_element_type=F32)


def _bf(x):
    return x.astype(BF16)


def _hi_lo(x):
    hi = x.astype(BF16)
    return hi, (x - hi.astype(F32)).astype(BF16)


def _mm(a, b):
    return _dot(_bf(a), _bf(b))


def _mm_nt(a, b):
    return _dot_nt(_bf(a), _bf(b))


def _mm_tn(a, b):
    return _dot_tn(_bf(a), _bf(b))


def _mm_xr(a, b01x2):
    hi, lo = _hi_lo(a)
    return _dot(jnp.concatenate([hi, lo], axis=1), b01x2)


def _mm_xl(a01x2, b):
    hi, lo = _hi_lo(b)
    return _dot(a01x2, jnp.concatenate([hi, lo], axis=0))


def _sigmoid(x):
    return 1.0 / (1.0 + jnp.exp(-x))


def _softplus(x):
    return jnp.maximum(x, 0.0) + jnp.log1p(jnp.exp(-jnp.abs(x)))


def _iota(shape, dim):
    return lax.broadcasted_iota(jnp.int32, shape, dim)


def _lane_groups(x):
    return [x[:, g * V7X_LANES:(g + 1) * V7X_LANES] for g in range(x.shape[1] // V7X_LANES)]


def _tril_ones_x2(c):
    return jnp.where(_iota((c, 2 * c), 0) >= _iota((c, 2 * c), 1) % c, 1.0, 0.0).astype(BF16)


def _rmsnorm_kernel(x_ref, g_ref, o_ref):
    x = x_ref[...]
    inv = lax.rsqrt(jnp.mean(x * x, axis=-1, keepdims=True) + NORM_EPS)
    o_ref[...] = (x * inv * g_ref[...]).astype(o_ref.dtype)


def _rmsnorm(x, gain, out_dtype, rows=256):
    n, d = x.shape
    return pl.pallas_call(
        _rmsnorm_kernel,
        grid=(n // rows,),
        in_specs=[pl.BlockSpec((rows, d), lambda i: (i, 0)), pl.BlockSpec((1, d), lambda i: (0, 0))],
        out_specs=pl.BlockSpec((rows, d), lambda i: (i, 0)),
        out_shape=jax.ShapeDtypeStruct((n, d), out_dtype),
        compiler_params=_cparams(("parallel",)),
        name="rmsnorm",
    )(x, gain.reshape(1, d))


def _resid_norm_kernel(h_ref, y_ref, g_ref, o_ref):
    y = y_ref[...]
    inv = lax.rsqrt(jnp.mean(y * y, axis=-1, keepdims=True) + NORM_EPS)
    o_ref[...] = h_ref[...] + y * inv * g_ref[...]


def _resid_norm(h, y, gain, rows=256):
    n, d = h.shape
    row = pl.BlockSpec((rows, d), lambda i: (i, 0))
    return pl.pallas_call(
        _resid_norm_kernel,
        grid=(n // rows,),
        in_specs=[row, row, pl.BlockSpec((1, d), lambda i: (0, 0))],
        out_specs=row,
        out_shape=jax.ShapeDtypeStruct((n, d), F32),
        compiler_params=_cparams(("parallel",)),
        name="resid_norm",
    )(h, y, gain.reshape(1, d))


def _mm_kernel(a_ref, b_ref, o_ref, *, relu2, b_is_nk):
    acc = _dot_nt(a_ref[...], b_ref[...]) if b_is_nk else _dot(a_ref[...], b_ref[...])
    if relu2:
        acc = jnp.square(jnp.maximum(acc, 0.0))
    o_ref[...] = acc.astype(o_ref.dtype)


def _matmul(a, b, out_dtype, tm, tn, relu2=False, b_is_nk=False, name="matmul"):
    m, k = a.shape
    n = b.shape[0] if b_is_nk else b.shape[1]
    b_spec = pl.BlockSpec((tn, k), lambda i, j: (j, 0)) if b_is_nk else pl.BlockSpec((k, tn), lambda i, j: (0, j))
    return pl.pallas_call(
        functools.partial(_mm_kernel, relu2=relu2, b_is_nk=b_is_nk),
        grid=(m // tm, n // tn),
        in_specs=[pl.BlockSpec((tm, k), lambda i, j: (i, 0)), b_spec],
        out_specs=pl.BlockSpec((tm, tn), lambda i, j: (i, j)),
        out_shape=jax.ShapeDtypeStruct((m, n), out_dtype),
        compiler_params=_cparams(("parallel", "arbitrary")),
        name=name,
    )(a, b)


def _mm_wcast_kernel(a_ref, b_ref, o_ref, b_bf16, *, relu2):
    @pl.when(pl.program_id(1) == 0)
    def _():
        b_bf16[...] = b_ref[...].astype(BF16)

    acc = _dot(a_ref[...], b_bf16[...])
    if relu2:
        acc = jnp.square(jnp.maximum(acc, 0.0))
    o_ref[...] = acc.astype(o_ref.dtype)


def _matmul_wcast(a, b, out_dtype, tm, tn, relu2=False, name="matmul_wcast"):
    m, k = a.shape
    _, n = b.shape
    return pl.pallas_call(
        functools.partial(_mm_wcast_kernel, relu2=relu2),
        grid=(n // tn, m // tm),
        in_specs=[pl.BlockSpec((tm, k), lambda j, i: (i, 0)), pl.BlockSpec((k, tn), lambda j, i: (0, j))],
        out_specs=pl.BlockSpec((tm, tn), lambda j, i: (i, j)),
        out_shape=jax.ShapeDtypeStruct((m, n), out_dtype),
        scratch_shapes=[pltpu.VMEM((k, tn), BF16)],
        compiler_params=_cparams(("parallel", "arbitrary")),
        name=name,
    )(a, b)


def _mm_kacc_kernel(a_ref, b_ref, o_ref, *, k_chunk):
    @pl.when(pl.program_id(2) == 0)
    def _():
        o_ref[...] = jnp.zeros_like(o_ref)

    acc = None
    for k0 in range(0, a_ref.shape[1], k_chunk):
        part = _dot(a_ref[:, k0:k0 + k_chunk], b_ref[k0:k0 + k_chunk, :].astype(BF16))
        acc = part if acc is None else acc + part
    o_ref[...] += acc


def _matmul_kacc(a, b, tm, tn, tk, k_chunk=512, name="matmul_kacc"):
    m, k = a.shape
    _, n = b.shape
    return pl.pallas_call(
        functools.partial(_mm_kacc_kernel, k_chunk=k_chunk),
        grid=(m // tm, n // tn, k // tk),
        in_specs=[pl.BlockSpec((tm, tk), lambda i, j, l: (i, l)), pl.BlockSpec((tk, tn), lambda i, j, l: (l, j))],
        out_specs=pl.BlockSpec((tm, tn), lambda i, j, l: (i, j)),
        out_shape=jax.ShapeDtypeStruct((m, n), F32),
        compiler_params=_cparams(("parallel", "parallel", "arbitrary")),
        name=name,
    )(a, b)


def _branch_kernel(yr_ref, yg_ref, wr_ref, wg_ref, gr_ref, gg_ref, o_ref, wr_bf16, wg_bf16):
    @pl.when(pl.program_id(1) == 0)
    def _():
        wr_bf16[...] = wr_ref[...].astype(BF16)
        wg_bf16[...] = wg_ref[...].astype(BF16)

    br = _dot(yr_ref[...], wr_bf16[...])
    bg = _dot(yg_ref[...], wg_bf16[...])
    sig_r = 0.5 * jnp.tanh(0.5 * gr_ref[...]) + 0.5
    sig_g = 0.5 * jnp.tanh(0.5 * gg_ref[...]) + 0.5
    o_ref[...] = (sig_r * br + sig_g * bg).astype(o_ref.dtype)


def _branch_merge(y_rw, y_gdn, w_rw, w_gdn, p, tm=1024, tn=512):
    m, k = y_rw.shape
    n = w_rw.shape[1]
    jr, jg = OFF_GATE_RW // tn, OFF_GATE_GDN // tn
    return pl.pallas_call(
        _branch_kernel,
        grid=(n // tn, m // tm),
        in_specs=[
            pl.BlockSpec((tm, k), lambda j, i: (i, 0)),
            pl.BlockSpec((tm, k), lambda j, i: (i, 0)),
            pl.BlockSpec((k, tn), lambda j, i: (0, j)),
            pl.BlockSpec((k, tn), lambda j, i: (0, j)),
            pl.BlockSpec((tm, tn), lambda j, i: (i, jr + j)),
            pl.BlockSpec((tm, tn), lambda j, i: (i, jg + j)),
        ],
        out_specs=pl.BlockSpec((tm, tn), lambda j, i: (i, j)),
        out_shape=jax.ShapeDtypeStruct((m, n), BF16),
        scratch_shapes=[pltpu.VMEM((k, tn), BF16)] * 2,
        compiler_params=_cparams(("parallel", "arbitrary")),
        name="branch_merge",
    )(y_rw, y_gdn, w_rw, w_gdn, p, p)


def _batch_lanes(ref):
    return jnp.concatenate([ref[b] for b in range(ref.shape[0])], axis=1)


def _tile_lanes(x, n):
    return jnp.concatenate([x] * n, axis=1)


def _token_shift(buf_ref, x_ref, prev_ref, first, mix):
    batch, _, width = x_ref.shape
    for b in range(batch):
        buf_ref[0:V7X_SUBLANES, b * width:(b + 1) * width] = jnp.where(first, 0.0, prev_ref[b])
        buf_ref[V7X_SUBLANES:, b * width:(b + 1) * width] = x_ref[b]
    x = buf_ref[V7X_SUBLANES:, :]
    return x + (buf_ref[V7X_SUBLANES - 1:buf_ref.shape[0] - 1, :] - x) * mix


def _expand2(x, head0):
    return jnp.concatenate([jnp.where(head0, x, 0.0), jnp.where(head0, 0.0, x)], axis=0)


def _rwkv_kernel(r_ref, k_ref, v_ref, pr_ref, pk_ref, pv_ref, misc_ref, pmisc_ref,
                 cp_ref, gp_ref, mixm_ref, wup_ref, aup_ref, gup_ref, o_ref, s_ref, rbuf, kbuf, vbuf, mbuf):
    c = CHUNK
    w = V7X_LANES
    batch, _, lw = r_ref.shape
    ci = pl.program_id(0)

    @pl.when(ci == 0)
    def _():
        s_ref[...] = jnp.zeros_like(s_ref)

    first = ci == 0
    mix_r, mix_k, mix_v, w0, a0, k_k, k_a, r_k = (_tile_lanes(cp_ref[i:i + 1, :], batch) for i in range(8))
    gn_w, gn_b = _tile_lanes(gp_ref[0:1, :], batch), _tile_lanes(gp_ref[1:2, :], batch)

    misc_m = _token_shift(mbuf, misc_ref, pmisc_ref, first, _tile_lanes(mixm_ref[...], batch))
    xr = _token_shift(rbuf, r_ref, pr_ref, first, mix_r)
    xk = _token_shift(kbuf, k_ref, pk_ref, first, mix_k)
    xv = _token_shift(vbuf, v_ref, pv_ref, first, mix_v)

    def lora(fn, lo, hi, w_ref):
        return jnp.concatenate([_dot(_bf(fn(misc_m[:, b * MISC_W + lo:b * MISC_W + hi])), w_ref[...])
                                for b in range(batch)], axis=1)

    wlin = w0 + lora(jnp.tanh, 0, 128, wup_ref)
    logw = -math.exp(-0.5) * _sigmoid(wlin)
    a = _sigmoid(a0 + lora(lambda t: t, 128, 256, aup_ref))
    gate = lora(_sigmoid, 256, 512, gup_ref)

    same_head = (_iota((w, w), 0) // RWKV_HEAD) == (_iota((w, w), 1) // RWKV_HEAD)
    bones = jnp.where(same_head, 1.0, 0.0).astype(BF16)
    eye = jnp.where(_iota((w, w), 0) == _iota((w, w), 1), 1.0, 0.0)

    def head_sums(x):
        n = x.shape[1] // w
        s = _dot(_bf(jnp.concatenate(_lane_groups(x), axis=0)), bones)
        return jnp.concatenate([s[g * c:(g + 1) * c] for g in range(n)], axis=1)

    kkr = xk * k_k
    kkn = kkr * lax.rsqrt(head_sums(kkr * kkr) + L2_EPS)
    k2 = xk * (1.0 + (a - 1.0) * k_a)
    am = -kkn
    bm = kkn * a

    cum = _mm_xl(_tril_ones_x2(c), logw)
    cum_ex = cum - logw
    c_last = cum[c - 1:c, :]
    c_mid = cum[c // 2 - 1:c // 2, :]
    e_out = jnp.exp(c_mid - cum)
    e_last = jnp.exp(c_last - cum)
    rt = xr * jnp.exp(cum - c_mid)
    at = am * jnp.exp(cum_ex - c_mid)
    bt = bm * e_out
    kt = k2 * e_out
    a_abs = am * jnp.exp(cum_ex)
    r_abs = xr * jnp.exp(cum)
    bh = bm * e_last
    kh = k2 * e_last
    w_c = jnp.exp(c_last)

    head0 = _iota((c, w), 1) < RWKV_HEAD
    head0_2 = _iota((c, 2 * w), 1) % w < RWKV_HEAD
    wi = _iota((c, 2 * c), 0)
    wj = _iota((c, 2 * c), 1) % c
    strict = wi > wj
    incl = wi >= wj

    gs = range(batch * lw // w)
    G = _lane_groups
    at_g, rt_g, bt_g, kt_g, xv_g = G(at), G(rt), G(bt), G(kt), G(xv)
    aa = [_mm_nt(jnp.concatenate([at_g[g], rt_g[g]], axis=0),
                 jnp.concatenate([_expand2(bt_g[g], head0), _expand2(kt_g[g], head0)], axis=0)) for g in gs]
    a_ab = [jnp.where(strict, aa[g][0:c, 0:2 * c], 0.0) for g in gs]
    a_ak = [jnp.where(strict, aa[g][0:c, 2 * c:4 * c], 0.0) for g in gs]
    a_rb = [jnp.where(incl, aa[g][c:2 * c, 0:2 * c], 0.0) for g in gs]
    a_rk = [jnp.where(incl, aa[g][c:2 * c, 2 * c:4 * c], 0.0) for g in gs]

    xv_e = [_expand2(xv_g[g], head0) for g in gs]
    akv = [_mm(a_ak[g], xv_e[g]) for g in gs]
    eye_w = jnp.where(wi == wj, 1.0, 0.0)
    tw = [eye_w + a_ab[g] for g in gs]
    pw = [_mm(a_ab[g], _expand2(a_ab[g], head0)) for g in gs]
    levels = int(math.log2(c))
    for lvl in range(1, levels):
        if lvl + 1 < levels:
            pt = [_mm(pw[g], _expand2(jnp.concatenate([pw[g], tw[g]], axis=1), head0_2)) for g in gs]
            pw = [pt[g][:, 0:w] for g in gs]
            tw = [tw[g] + pt[g][:, w:2 * w] for g in gs]
        else:
            tw = [tw[g] + _mm(pw[g], _expand2(tw[g], head0)) for g in gs]
    a_abs_g = G(a_abs)
    x = [_mm(tw[g], _expand2(jnp.concatenate([a_abs_g[g], akv[g]], axis=1), head0_2)) for g in gs]
    wm = [x[g][:, 0:w] for g in gs]
    u0 = [x[g][:, w:2 * w] for g in gs]

    bh_g, kh_g, wc_g, rabs_g = G(bh), G(kh), G(w_c), G(r_abs)
    m_mat = [jnp.where(same_head, _mm_tn(bh_g[g], wm[g]), 0.0) + eye * wc_g[g] for g in gs]
    n_mat = [jnp.where(same_head, _mm_tn(jnp.concatenate([bh_g[g], kh_g[g]], axis=0),
                                         jnp.concatenate([u0[g], xv_g[g]], axis=0)), 0.0) for g in gs]
    p_mat = [rabs_g[g] + _mm(a_rb[g], _expand2(wm[g], head0)) for g in gs]
    q_mat = [_mm(jnp.concatenate([a_rb[g], a_rk[g]], axis=1),
                 jnp.concatenate([_expand2(u0[g], head0), xv_e[g]], axis=0)) for g in gs]

    h0 = [_bf(s_ref[g]) for g in gs]
    y = jnp.concatenate([_dot(_bf(p_mat[g]), h0[g]) + q_mat[g] for g in gs], axis=1)
    for g in gs:
        s_ref[g] = _dot(_bf(m_mat[g]), h0[g]) + n_mat[g]

    inv_n = 1.0 / RWKV_HEAD
    d = y - head_sums(y) * inv_n
    var = head_sums(d * d) * inv_n
    yn = d * lax.rsqrt(var + RWKV_GN_EPS) * gn_w + gn_b
    bonus = head_sums(xr * k2 * r_k) * xv
    out = ((yn + bonus) * gate).astype(o_ref.dtype)
    for b in range(batch):
        o_ref[b] = out[:, b * lw:(b + 1) * lw]


def _chunk_specs(batch, c):
    def cur(off, width):
        assert off % width == 0
        return pl.BlockSpec((batch, c, width), lambda t: (0, t, off // width))

    def prev(off, width):
        return pl.BlockSpec((batch, V7X_SUBLANES, width),
                            lambda t: (0, jnp.maximum(t * (c // V7X_SUBLANES) - 1, 0), off // width))

    return cur, prev


def _rwkv_mix(p, cp, gp, mixm, w_up_p, a_up_p, g_up):
    batch, seq, _ = p.shape
    c = CHUNK
    cur, prev = _chunk_specs(batch, c)
    whole = lambda x: pl.BlockSpec(x.shape, lambda t: (0, 0))
    return pl.pallas_call(
        _rwkv_kernel,
        grid=(seq // c,),
        in_specs=[
            cur(OFF_R, RW), cur(OFF_K, RW), cur(OFF_V, RW), prev(OFF_R, RW), prev(OFF_K, RW), prev(OFF_V, RW),
            cur(OFF_MISC, MISC_W), prev(OFF_MISC, MISC_W),
            whole(cp), whole(gp), whole(mixm), whole(w_up_p), whole(a_up_p), whole(g_up),
        ],
        out_specs=pl.BlockSpec((batch, c, RW), lambda t: (0, t, 0)),
        out_shape=jax.ShapeDtypeStruct((batch, seq, RW), BF16),
        scratch_shapes=[pltpu.VMEM((batch * RW // V7X_LANES, V7X_LANES, V7X_LANES), F32)]
        + [pltpu.VMEM((V7X_SUBLANES + c, batch * RW), F32)] * 3
        + [pltpu.VMEM((V7X_SUBLANES + c, batch * MISC_W), F32)],
        compiler_params=_cparams(("arbitrary",)),
        name="rwkv7_mix",
    )(p, p, p, p, p, p, p, p, cp, gp, mixm, w_up_p, a_up_p, g_up)


def _gdn_kernel(q_ref, k_ref, v_ref, pq_ref, pk_ref, pv_ref, z_ref, misc_ref,
                cwq_ref, cwk_ref, cwv_ref, alog_ref, dtb_ref, nw_ref, o_ref, s_ref):
    c = CHUNK
    w = V7X_LANES
    batch, _, lw = q_ref.shape
    assert lw == N_GDN_HEADS * w
    ci = pl.program_id(0)
    heads = batch * N_GDN_HEADS
    hs = range(heads)
    G = _lane_groups

    @pl.when(ci == 0)
    def _():
        s_ref[...] = jnp.zeros_like(s_ref)

    first = ci == 0

    def conv_silu(x_ref, prev_ref, cw_ref):
        x = _batch_lanes(x_ref)
        prev8 = jnp.where(first, 0.0, _batch_lanes(prev_ref))
        cw = _tile_lanes(cw_ref[...], batch)
        row = _iota(x.shape, 0)
        acc = x * cw[3:4, :]
        for s in (1, 2, 3):
            tail = jnp.tile(pltpu.roll(prev8, s, axis=0), (c // V7X_SUBLANES, 1))
            acc = acc + jnp.where(row < s, tail, pltpu.roll(x, s, axis=0)) * cw[3 - s:4 - s, :]
        return acc * _sigmoid(acc)

    def head_sums(x):
        return jnp.concatenate([jnp.broadcast_to(jnp.sum(g, axis=-1, keepdims=True), g.shape) for g in G(x)], axis=1)

    q = conv_silu(q_ref, pq_ref, cwq_ref)
    k = conv_silu(k_ref, pk_ref, cwk_ref)
    v = conv_silu(v_ref, pv_ref, cwv_ref)
    q = q * lax.rsqrt(head_sums(q * q) + L2_EPS) * (GDN_HEAD ** -0.5)
    k = k * lax.rsqrt(head_sums(k * k) + L2_EPS)

    lane2 = _iota((2 * w, w), 0) % w
    lane2r = _iota((c, 2 * w), 1) % w
    bcol, gcol, grow = [], [], []
    for b in range(batch):
        bm = misc_ref[b, :, 0:128]
        beta_hl = jnp.concatenate(_hi_lo(_sigmoid(bm)), axis=1)
        g_full = -jnp.exp(alog_ref[...]) * _softplus(bm + dtb_ref[...])
        gc_hl = jnp.concatenate(_hi_lo(_mm_xl(_tril_ones_x2(c), g_full)), axis=1)
        for h in range(N_GDN_HEADS):
            bcol.append(_dot(beta_hl, jnp.where(lane2 == BETA_LANE + h, 1.0, 0.0).astype(BF16)))
            gcol.append(_dot(gc_hl, jnp.where(lane2 == ALPHA_LANE + h, 1.0, 0.0).astype(BF16)))
            grow.append(_dot_nt(jnp.where(lane2r == ALPHA_LANE + h, 1.0, 0.0).astype(BF16), gc_hl))

    ti = _iota((c, c), 0)
    tj = _iota((c, c), 1)
    causal = ti >= tj
    strict = ti > tj
    decay = [jnp.where(causal, jnp.exp(jnp.where(causal, gcol[h][:, 0:c] - grow[h], 0.0)), 0.0) for h in hs]

    q_g, k_g, v_g = G(q), G(k), G(v)
    kb = [k_g[h] * bcol[h] for h in hs]
    kk = [_mm_nt(jnp.concatenate([kb[h], q_g[h]], axis=0), k_g[h]) for h in hs]
    lneg = [jnp.where(strict, -kk[h][0:c] * decay[h], 0.0) for h in hs]
    a_intra = [kk[h][c:2 * c] * decay[h] for h in hs]

    egc = [jnp.exp(gcol[h]) for h in hs]
    g_last = [gcol[h][c - 1:c, :] for h in hs]
    lane_t = _iota((c, w), 1) >= c
    eye_t = jnp.where(_iota((c, w), 1) == _iota((c, w), 0) + c, 1.0, 0.0)
    zt = [jnp.concatenate([lneg[h], jnp.zeros((c, c), F32)], axis=1) + eye_t for h in hs]
    for _ in range(int(math.log2(c))):
        zb = [_bf(zt[h]) for h in hs]
        zt = [_dot(zb[h][:, 0:c], zb[h]) + jnp.where(lane_t, zt[h], 0.0) for h in hs]
    zeros_x = jnp.zeros((c, 2 * w), F32)
    x = [_mm(zt[h], jnp.concatenate([zeros_x, jnp.concatenate([v_g[h] * bcol[h], kb[h] * egc[h]], axis=1)], axis=0))
         for h in hs]
    uu = [x[h][:, 0:w] for h in hs]
    ww = [x[h][:, w:2 * w] for h in hs]

    kd = [k_g[h] * jnp.exp(g_last[h] - gcol[h]) for h in hs]
    eye = jnp.where(_iota((w, w), 0) == _iota((w, w), 1), 1.0, 0.0)
    m_mat = [eye * jnp.exp(g_last[h]) - _mm_tn(kd[h], ww[h]) for h in hs]
    n_mat = [_mm_tn(kd[h], uu[h]) for h in hs]
    p_mat = [q_g[h] * egc[h] - _mm(a_intra[h], ww[h]) for h in hs]
    q_mat = [_mm(a_intra[h], uu[h]) for h in hs]

    s0 = [_bf(s_ref[h]) for h in hs]
    o = jnp.concatenate([_dot(_bf(p_mat[h]), s0[h]) + q_mat[h] for h in hs], axis=1)
    for h in hs:
        s_ref[h] = _dot(_bf(m_mat[h]), s0[h]) + n_mat[h]

    o = o * lax.rsqrt(head_sums(o * o) * (1.0 / GDN_HEAD) + NORM_EPS) * _tile_lanes(nw_ref[...], heads)
    z = _batch_lanes(z_ref)
    out = (o * (z * _sigmoid(z))).astype(o_ref.dtype)
    for b in range(batch):
        o_ref[b] = out[:, b * lw:(b + 1) * lw]


def _gdn_mix(p, conv_w, alog_l, dtb_l, norm_w):
    batch, seq, _ = p.shape
    c = CHUNK
    cur, prev = _chunk_specs(batch, c)

    def cw(part):
        return pl.BlockSpec((4, RW), lambda t: (0, part))

    lane_vec = pl.BlockSpec((1, V7X_LANES), lambda t: (0, 0))
    return pl.pallas_call(
        _gdn_kernel,
        grid=(seq // c,),
        in_specs=[
            cur(OFF_GQ, RW), cur(OFF_GK, RW), cur(OFF_GV, RW), prev(OFF_GQ, RW), prev(OFF_GK, RW), prev(OFF_GV, RW),
            cur(OFF_Z, RW), cur(OFF_MISC, MISC_W),
            cw(0), cw(1), cw(2), lane_vec, lane_vec, lane_vec,
        ],
        out_specs=pl.BlockSpec((batch, c, RW), lambda t: (0, t, 0)),
        out_shape=jax.ShapeDtypeStruct((batch, seq, RW), BF16),
        scratch_shapes=[pltpu.VMEM((batch * N_GDN_HEADS, V7X_LANES, V7X_LANES), F32)],
        compiler_params=_cparams(("arbitrary",)),
        name="gdn_mix",
    )(p, p, p, p, p, p, p, p, conv_w, conv_w, conv_w, alog_l, dtb_l, norm_w)


def _xattn_kernel(x_ref, ymix_ref, gmix_ref, k_ref, v_ref, wq_ref, wo_ref, gpre_ref, gpost_ref, gnext_ref,
                  h_out_ref, f_out_ref):
    ym = ymix_ref[...]
    h = x_ref[...] + ym * lax.rsqrt(jnp.mean(ym * ym, axis=-1, keepdims=True) + NORM_EPS) * gmix_ref[...]
    cn =(h * lax.rsqrt(jnp.mean(h * h, axis=-1, keepdims=True) + NORM_EPS) * gpre_ref[...]).astype(BF16)
    q = _dot(cn, wq_ref[...]).astype(BF16)
    kk = k_ref[0]
    vv = v_ref[0]
    outs = []
    for hh in range(XA_HEADS):
        sl = slice(hh * XA_HEAD, (hh + 1) * XA_HEAD)
        s = _dot_nt(q[:, sl], kk[:, sl]) * (XA_HEAD ** -0.5)
        s = s - jnp.max(s, axis=-1, keepdims=True)
        e = jnp.exp(s)
        pr = e / jnp.sum(e, axis=-1, keepdims=True)
        outs.append(_dot(pr.astype(BF16), vv[:, sl]))
    o = jnp.concatenate(outs, axis=1).astype(BF16)
    y = _dot(o, wo_ref[...])
    h2 = h + y * lax.rsqrt(jnp.mean(y * y, axis=-1, keepdims=True) + NORM_EPS) * gpost_ref[...]
    h_out_ref[...] = h2
    f_out_ref[...] = (h2 * lax.rsqrt(jnp.mean(h2 * h2, axis=-1, keepdims=True) + NORM_EPS)
                      * gnext_ref[...]).astype(f_out_ref.dtype)


def _xattn_block(x, y_mix, g_mix, kmem, vmem, w_q, w_o, g_pre, g_post, g_next, batch, seq, tt=256):
    n, d = x.shape
    mlen, xw = kmem.shape[1], kmem.shape[2]
    per_b = seq // tt
    row = pl.BlockSpec((tt, d), lambda i: (i, 0))
    vec = pl.BlockSpec((1, d), lambda i: (0, 0))
    mem = pl.BlockSpec((1, mlen, xw), lambda i: (i // per_b, 0, 0))
    return pl.pallas_call(
        _xattn_kernel,
        grid=(n // tt,),
        in_specs=[row, row, vec, mem, mem, pl.BlockSpec((d, xw), lambda i: (0, 0)),
                  pl.BlockSpec((xw, d), lambda i: (0, 0)), vec, vec, vec],
        out_specs=[row, row],
        out_shape=[jax.ShapeDtypeStruct((n, d), F32), jax.ShapeDtypeStruct((n, d), BF16)],
        compiler_params=_cparams(("parallel",)),
        name="xattn_block",
    )(x, y_mix, g_mix.reshape(1, d), kmem, vmem, w_q, w_o, g_pre.reshape(1, d), g_post.reshape(1, d),
      g_next.reshape(1, d))


SRC_LORA = 3 * RW
SRC_GDN = SRC_LORA + 2 * LORA + 256
SRC_BA = SRC_GDN + 4 * RW
SRC_GATES = SRC_BA + 2 * N_GDN_HEADS
RELAYOUT_ROWS = 512


def _relayout_kernel(x_ref, misc_ref, o_ref):
    @pl.when(pl.program_id(0) != OFF_MISC // RELAYOUT_ROWS)
    def _():
        o_ref[...] = x_ref[...].astype(o_ref.dtype)

    @pl.when(pl.program_id(0) == OFF_MISC // RELAYOUT_ROWS)
    def _():
        o_ref[...] = misc_ref[...].astype(o_ref.dtype)


def _relayout_w_in_t(w_in_t):
    n_in, d = w_in_t.shape
    rb = RELAYOUT_ROWS
    assert OFF_GQ % rb == 0 and OFF_MISC % rb == 0 and OFF_GATE_RW % rb == 0 and MISC_W == rb
    zeros = jnp.zeros((128 - LORA, d), w_in_t.dtype)
    misc = jnp.concatenate([w_in_t[SRC_LORA:SRC_LORA + LORA], w_in_t[SRC_BA:SRC_BA + 2 * N_GDN_HEADS],
                            w_in_t[SRC_LORA + LORA:SRC_LORA + 2 * LORA], zeros,
                            w_in_t[SRC_LORA + 2 * LORA:SRC_GDN]], axis=0)

    def src_row(j):
        r = j * rb
        src = jnp.where(r < OFF_GQ, r, jnp.where(r < OFF_MISC, r + (SRC_GDN - OFF_GQ),
                                                 jnp.where(r < OFF_GATE_RW, 0, r + (SRC_GATES - OFF_GATE_RW))))
        return pl.multiple_of(src, math.gcd(rb, SRC_GDN, SRC_GATES))

    return pl.pallas_call(
        _relayout_kernel,
        grid=(NP // rb,),
        in_specs=[pl.BlockSpec((pl.Element(rb), pl.Element(d)), lambda j: (src_row(j), 0)),
                  pl.BlockSpec((rb, d), lambda j: (0, 0))],
        out_specs=pl.BlockSpec((rb, d), lambda j: (j, 0)),
        out_shape=jax.ShapeDtypeStruct((NP, d), BF16),
        compiler_params=_cparams(("parallel",)),
        name="relayout_w_in",
    )(w_in_t, misc)


def _layer(x, mem, mix_norm_pre, mix_norm_post, w_in, rwkv_shift_mix, rwkv_w0, rwkv_w_up, rwkv_a0, rwkv_a_up,
           rwkv_g_up, rwkv_k_k, rwkv_k_a, rwkv_r_k, rwkv_gn_w, rwkv_gn_b, gdn_conv_w, gdn_a_log, gdn_dt_bias,
           gdn_norm_w, w_branch_rwkv, w_branch_gdn, w_mix_out, xa_norm_pre, xa_norm_mem, xa_norm_post, xa_w_q,
           xa_w_kv, xa_w_o, mlp_norm_pre, mlp_norm_post, mlp_w_up, mlp_w_down):
    batch, seq, d = x.shape
    n = batch * seq
    h = x.reshape(n, d)

    w_in_b = _relayout_w_in_t(jnp.swapaxes(w_in, 0, 1))
    mix = rwkv_shift_mix
    o = 3 * RW
    cp = jnp.stack([mix[0:RW], mix[RW:2 * RW], mix[2 * RW:o], rwkv_w0, rwkv_a0, rwkv_k_k, rwkv_k_a,
                    rwkv_r_k.reshape(RW)], axis=0)
    gp = jnp.concatenate([rwkv_gn_w[None], rwkv_gn_b[None], jnp.zeros((6, RW), F32)], axis=0)
    zeros = lambda k: jnp.zeros((k,), F32)
    mixm = jnp.concatenate([mix[o:o + LORA], zeros(32), mix[o + LORA:o + 2 * LORA], zeros(32),
                            mix[o + 2 * LORA:o + 2 * LORA + 256]])[None]
    w_up_p = jnp.concatenate([rwkv_w_up, jnp.zeros((128 - LORA, RW), F32)], axis=0).astype(BF16)
    a_up_p = jnp.concatenate([rwkv_a_up, jnp.zeros((128 - LORA, RW), F32)], axis=0).astype(BF16)
    g_up_b = rwkv_g_up.astype(BF16)
    alog_l = jnp.concatenate([zeros(ALPHA_LANE), gdn_a_log])[None]
    dtb_l = jnp.concatenate([zeros(ALPHA_LANE), gdn_dt_bias])[None]

    u = _rmsnorm(h, mix_norm_pre, BF16)
    p = _matmul(u, w_in_b, F32, tm=1024, tn=1280, b_is_nk=True, name="in_proj")
    p3 = p.reshape(batch, seq, NP)
    y_rw = _rwkv_mix(p3, cp, gp, mixm, w_up_p, a_up_p, g_up_b).reshape(n, RW)
    y_gdn = _gdn_mix(p3, gdn_conv_w, alog_l, dtb_l, gdn_norm_w[None]).reshape(n, RW)
    merged = _branch_merge(y_rw, y_gdn, w_branch_rwkv, w_branch_gdn, p)
    y_mix = _matmul_wcast(merged, w_mix_out, F32, tm=512, tn=1024, name="mix_out")

    mlen = mem.shape[1]
    m = _rmsnorm(mem.reshape(batch * mlen, d), xa_norm_mem, BF16)
    kv = _matmul(m, xa_w_kv.astype(BF16), BF16, tm=batch * mlen, tn=512, name="xa_kv")
    xw = XA_HEADS * XA_HEAD
    kmem = kv[:, :xw].reshape(batch, mlen, xw)
    vmem = kv[:, xw:].reshape(batch, mlen, xw)
    h, f = _xattn_block(h, y_mix, mix_norm_post, kmem, vmem, xa_w_q.astype(BF16), xa_w_o.astype(BF16),
                        xa_norm_pre, xa_norm_post, mlp_norm_pre, batch, seq)

    hmid = _matmul_wcast(f, mlp_w_up, BF16, tm=512, tn=1024, relu2=True, name="mlp_up")
    y_mlp = _matmul_kacc(hmid, mlp_w_down, tm=2048, tn=1024, tk=1024, name="mlp_down")
    h = _resid_norm(h, y_mlp, mlp_norm_post)
    return h.reshape(batch, seq, d)


def kernel(x, mem, mix_norm_pre, mix_norm_post, w_in, rwkv_shift_mix, rwkv_w0, rwkv_w_up, rwkv_a0, rwkv_a_up,
           rwkv_g_up, rwkv_k_k, rwkv_k_a, rwkv_r_k, rwkv_gn_w, rwkv_gn_b, gdn_conv_w, gdn_a_log, gdn_dt_bias,
           gdn_norm_w, w_branch_rwkv, w_branch_gdn, w_mix_out, xa_norm_pre, xa_norm_mem, xa_norm_post, xa_w_q,
           xa_w_kv, xa_w_o, mlp_norm_pre, mlp_norm_post, mlp_w_up, mlp_w_down):
    args = (mix_norm_pre, mix_norm_post, w_in, rwkv_shift_mix, rwkv_w0, rwkv_w_up, rwkv_a0, rwkv_a_up,
            rwkv_g_up, rwkv_k_k, rwkv_k_a, rwkv_r_k, rwkv_gn_w, rwkv_gn_b, gdn_conv_w, gdn_a_log, gdn_dt_bias,
            gdn_norm_w, w_branch_rwkv, w_branch_gdn, w_mix_out, xa_norm_pre, xa_norm_mem, xa_norm_post, xa_w_q,
            xa_w_kv, xa_w_o, mlp_norm_pre, mlp_norm_post, mlp_w_up, mlp_w_down)
    h = x
    for layer in range(mix_norm_pre.shape[0]):
        h = _layer(h, mem, *(a[layer] for a in args))
    return h
```

```python
import functools
import math

import jax
import jax.numpy as jnp
from jax import lax
from jax.experimental import pallas as pl
from jax.experimental.pallas import tpu as pltpu

F32 = jnp.float32
BF16 = jnp.bfloat16

V7X_LANES = 128
V7X_SUBLANES = 8
V7X_VMEM_LIMIT_BYTES = 60 * 1024 * 1024

NORM_EPS = 1e-6
L2_EPS = 1e-6
RWKV_GN_EPS = 64e-5
RWKV_HEAD = 64
GDN_HEAD = 128
CHUNK = 64
XA_HEADS = 4
XA_HEAD = 128

RW = 2048
OFF_R, OFF_K, OFF_V = 0, RW, 2 * RW
OFF_GQ, OFF_GK, OFF_GV = 3 * RW, 4 * RW, 5 * RW
OFF_Z = 6 * RW
OFF_MISC = 7 * RW
MISC_W = 512
OFF_GATE_RW = OFF_MISC + MISC_W
D_MODEL = 4096
OFF_GATE_GDN = OFF_GATE_RW + D_MODEL
NP = OFF_GATE_GDN + D_MODEL
LORA = 96
N_GDN_HEADS = RW // GDN_HEAD
BETA_LANE = LORA
ALPHA_LANE = LORA + N_GDN_HEADS


def _cparams(sem):
    return pltpu.CompilerParams(dimension_semantics=sem, vmem_limit_bytes=V7X_VMEM_LIMIT_BYTES)


def _dot(a, b):
    return jnp.dot(a, b, preferred_element_type=F32)


def _dot_nt(a, b):
    return lax.dot_general(a, b, (((1,), (1,)), ((), ())), preferred_element_type=F32)


def _dot_tn(a, b):
    return lax.dot_general(a, b, (((0,), (0,)), ((), ())), preferred_element_type=F32)


def _bf(x):
    return x.astype(BF16)


def _hi_lo(x):
    hi = x.astype(BF16)
    return hi, (x - hi.astype(F32)).astype(BF16)


def _mm(a, b):
    return _dot(_bf(a), _bf(b))


def _mm_nt(a, b):
    return _dot_nt(_bf(a), _bf(b))


def _mm_tn(a, b):
    return _dot_tn(_bf(a), _bf(b))


def _mm_xr(a, b01x2):
    hi, lo = _hi_lo(a)
    return _dot(jnp.concatenate([hi, lo], axis=1), b01x2)


def _mm_xl(a01x2, b):
    hi, lo = _hi_lo(b)
    return _dot(a01x2, jnp.concatenate([hi, lo], axis=0))


def _sigmoid(x):
    return 1.0 / (1.0 + jnp.exp(-x))


def _softplus(x):
    return jnp.maximum(x, 0.0) + jnp.log1p(jnp.exp(-jnp.abs(x)))


def _iota(shape, dim):
    return lax.broadcasted_iota(jnp.int32, shape, dim)


def _lane_groups(x):
    return [x[:, g * V7X_LANES:(g + 1) * V7X_LANES] for g in range(x.shape[1] // V7X_LANES)]


def _tril_ones_x2(c):
    return jnp.where(_iota((c, 2 * c), 0) >= _iota((c, 2 * c), 1) % c, 1.0, 0.0).astype(BF16)


def _rmsnorm_kernel(x_ref, g_ref, o_ref):
    x = x_ref[...]
    inv = lax.rsqrt(jnp.mean(x * x, axis=-1, keepdims=True) + NORM_EPS)
    o_ref[...] = (x * inv * g_ref[...]).astype(o_ref.dtype)


def _rmsnorm(x, gain, out_dtype, rows=256):
    n, d = x.shape
    return pl.pallas_call(
        _rmsnorm_kernel,
        grid=(n // rows,),
        in_specs=[pl.BlockSpec((rows, d), lambda i: (i, 0)), pl.BlockSpec((1, d), lambda i: (0, 0))],
        out_specs=pl.BlockSpec((rows, d), lambda i: (i, 0)),
        out_shape=jax.ShapeDtypeStruct((n, d), out_dtype),
        compiler_params=_cparams(("parallel",)),
        name="rmsnorm",
    )(x, gain.reshape(1, d))


def _resid_norm_kernel(h_ref, y_ref, g_ref, o_ref):
    y = y_ref[...]
    inv = lax.rsqrt(jnp.mean(y * y, axis=-1, keepdims=True) + NORM_EPS)
    o_ref[...] = h_ref[...] + y * inv * g_ref[...]


def _resid_norm(h, y, gain, rows=256):
    n, d = h.shape
    row = pl.BlockSpec((rows, d), lambda i: (i, 0))
    return pl.pallas_call(
        _resid_norm_kernel,
        grid=(n // rows,),
        in_specs=[row, row, pl.BlockSpec((1, d), lambda i: (0, 0))],
        out_specs=row,
        out_shape=jax.ShapeDtypeStruct((n, d), F32),
        compiler_params=_cparams(("parallel",)),
        name="resid_norm",
    )(h, y, gain.reshape(1, d))


def _mm_kernel(a_ref, b_ref, o_ref, *, relu2, b_is_nk):
    acc = _dot_nt(a_ref[...], b_ref[...]) if b_is_nk else _dot(a_ref[...], b_ref[...])
    if relu2:
        acc = jnp.square(jnp.maximum(acc, 0.0))
    o_ref[...] = acc.astype(o_ref.dtype)


def _matmul(a, b, out_dtype, tm, tn, relu2=False, b_is_nk=False, name="matmul"):
    m, k = a.shape
    n = b.shape[0] if b_is_nk else b.shape[1]
    b_spec = pl.BlockSpec((tn, k), lambda i, j: (j, 0)) if b_is_nk else pl.BlockSpec((k, tn), lambda i, j: (0, j))
    return pl.pallas_call(
        functools.partial(_mm_kernel, relu2=relu2, b_is_nk=b_is_nk),
        grid=(m // tm, n // tn),
        in_specs=[pl.BlockSpec((tm, k), lambda i, j: (i, 0)), b_spec],
        out_specs=pl.BlockSpec((tm, tn), lambda i, j: (i, j)),
        out_shape=jax.ShapeDtypeStruct((m, n), out_dtype),
        compiler_params=_cparams(("parallel", "arbitrary")),
        name=name,
    )(a, b)


def _mm_wcast_kernel(a_ref, b_ref, o_ref, b_bf16, *, relu2):
    @pl.when(pl.program_id(1) == 0)
    def _():
        b_bf16[...] = b_ref[...].astype(BF16)

    acc = _dot(a_ref[...], b_bf16[...])
    if relu2:
        acc = jnp.square(jnp.maximum(acc, 0.0))
    o_ref[...] = acc.astype(o_ref.dtype)


def _matmul_wcast(a, b, out_dtype, tm, tn, relu2=False, name="matmul_wcast"):
    m, k = a.shape
    _, n = b.shape
    return pl.pallas_call(
        functools.partial(_mm_wcast_kernel, relu2=relu2),
        grid=(n // tn, m // tm),
        in_specs=[pl.BlockSpec((tm, k), lambda j, i: (i, 0)), pl.BlockSpec((k, tn), lambda j, i: (0, j))],
        out_specs=pl.BlockSpec((tm, tn), lambda j, i: (i, j)),
        out_shape=jax.ShapeDtypeStruct((m, n), out_dtype),
        scratch_shapes=[pltpu.VMEM((k, tn), BF16)],
        compiler_params=_cparams(("parallel", "arbitrary")),
        name=name,
    )(a, b)


def _mm_kacc_kernel(a_ref, b_ref, o_ref, *, k_chunk):
    @pl.when(pl.program_id(2) == 0)
    def _():
        o_ref[...] = jnp.zeros_like(o_ref)

    acc = None
    for k0 in range(0, a_ref.shape[1], k_chunk):
        part = _dot(a_ref[:, k0:k0 + k_chunk], b_ref[k0:k0 + k_chunk, :].astype(BF16))
        acc = part if acc is None else acc + part
    o_ref[...] += acc


def _matmul_kacc(a, b, tm, tn, tk, k_chunk=512, name="matmul_kacc"):
    m, k = a.shape
    _, n = b.shape
    return pl.pallas_call(
        functools.partial(_mm_kacc_kernel, k_chunk=k_chunk),
        grid=(m // tm, n // tn, k // tk),
        in_specs=[pl.BlockSpec((tm, tk), lambda i, j, l: (i, l)), pl.BlockSpec((tk, tn), lambda i, j, l: (l, j))],
        out_specs=pl.BlockSpec((tm, tn), lambda i, j, l: (i, j)),
        out_shape=jax.ShapeDtypeStruct((m, n), F32),
        compiler_params=_cparams(("parallel", "parallel", "arbitrary")),
        name=name,
    )(a, b)


def _branch_kernel(yr_ref, yg_ref, wr_ref, wg_ref, gr_ref, gg_ref, o_ref, wr_bf16, wg_bf16):
    @pl.when(pl.program_id(1) == 0)
    def _():
        wr_bf16[...] = wr_ref[...].astype(BF16)
        wg_bf16[...] = wg_ref[...].astype(BF16)

    br = _dot(yr_ref[...], wr_bf16[...])
    bg = _dot(yg_ref[...], wg_bf16[...])
    sig_r = 0.5 * jnp.tanh(0.5 * gr_ref[...]) + 0.5
    sig_g = 0.5 * jnp.tanh(0.5 * gg_ref[...]) + 0.5
    o_ref[...] = (sig_r * br + sig_g * bg).astype(o_ref.dtype)


def _branch_merge(y_rw, y_gdn, w_rw, w_gdn, p, tm=1024, tn=512):
    m, k = y_rw.shape
    n = w_rw.shape[1]
    jr, jg = OFF_GATE_RW // tn, OFF_GATE_GDN // tn
    return pl.pallas_call(
        _branch_kernel,
        grid=(n // tn, m // tm),
        in_specs=[
            pl.BlockSpec((tm, k), lambda j, i: (i, 0)),
            pl.BlockSpec((tm, k), lambda j, i: (i, 0)),
            pl.BlockSpec((k, tn), lambda j, i: (0, j)),
            pl.BlockSpec((k, tn), lambda j, i: (0, j)),
            pl.BlockSpec((tm, tn), lambda j, i: (i, jr + j)),
            pl.BlockSpec((tm, tn), lambda j, i: (i, jg + j)),
        ],
        out_specs=pl.BlockSpec((tm, tn), lambda j, i: (i, j)),
        out_shape=jax.ShapeDtypeStruct((m, n), BF16),
        scratch_shapes=[pltpu.VMEM((k, tn), BF16)] * 2,
        compiler_params=_cparams(("parallel", "arbitrary")),
        name="branch_merge",
    )(y_rw, y_gdn, w_rw, w_gdn, p, p)


def _batch_lanes(ref):
    return jnp.concatenate([ref[b] for b in range(ref.shape[0])], axis=1)


def _tile_lanes(x, n):
    return jnp.concatenate([x] * n, axis=1)


def _token_shift(buf_ref, x_ref, prev_ref, first, mix):
    batch, _, width = x_ref.shape
    for b in range(batch):
        buf_ref[0:V7X_SUBLANES, b * width:(b + 1) * width] = jnp.where(first, 0.0, prev_ref[b])
        buf_ref[V7X_SUBLANES:, b * width:(b + 1) * width] = x_ref[b]
    x = buf_ref[V7X_SUBLANES:, :]
    return x + (buf_ref[V7X_SUBLANES - 1:buf_ref.shape[0] - 1, :] - x) * mix


def _expand2(x, head0):
    return jnp.concatenate([jnp.where(head0, x, 0.0), jnp.where(head0, 0.0, x)], axis=0)


def _rwkv_kernel(r_ref, k_ref, v_ref, pr_ref, pk_ref, pv_ref, misc_ref, pmisc_ref,
                 cp_ref, gp_ref, mixm_ref, wup_ref, aup_ref, gup_ref, o_ref, s_ref, rbuf, kbuf, vbuf, mbuf):
    c = CHUNK
    w = V7X_LANES
    batch, _, lw = r_ref.shape
    ci = pl.program_id(0)

    @pl.when(ci == 0)
    def _():
        s_ref[...] = jnp.zeros_like(s_ref)

    first = ci == 0
    mix_r, mix_k, mix_v, w0, a0, k_k, k_a, r_k = (_tile_lanes(cp_ref[i:i + 1, :], batch) for i in range(8))
    gn_w, gn_b = _tile_lanes(gp_ref[0:1, :], batch), _tile_lanes(gp_ref[1:2, :], batch)

    misc_m = _token_shift(mbuf, misc_ref, pmisc_ref, first, _tile_lanes(mixm_ref[...], batch))
    xr = _token_shift(rbuf, r_ref, pr_ref, first, mix_r)
    xk = _token_shift(kbuf, k_ref, pk_ref, first, mix_k)
    xv = _token_shift(vbuf, v_ref, pv_ref, first, mix_v)

    def lora(fn, lo, hi, w_ref):
        return jnp.concatenate([_dot(_bf(fn(misc_m[:, b * MISC_W + lo:b * MISC_W + hi])), w_ref[...])
                                for b in range(batch)], axis=1)

    wlin = w0 + lora(jnp.tanh, 0, 128, wup_ref)
    logw = -math.exp(-0.5) * _sigmoid(wlin)
    a = _sigmoid(a0 + lora(lambda t: t, 128, 256, aup_ref))
    gate = lora(_sigmoid, 256, 512, gup_ref)

    same_head = (_iota((w, w), 0) // RWKV_HEAD) == (_iota((w, w), 1) // RWKV_HEAD)
    bones = jnp.where(same_head, 1.0, 0.0).astype(BF16)
    eye = jnp.where(_iota((w, w), 0) == _iota((w, w), 1), 1.0, 0.0)

    def head_sums(x):
        n = x.shape[1] // w
        s = _dot(_bf(jnp.concatenate(_lane_groups(x), axis=0)), bones)
        return jnp.concatenate([s[g * c:(g + 1) * c] for g in range(n)], axis=1)

    kkr = xk * k_k
    kkn = kkr * lax.rsqrt(head_sums(kkr * kkr) + L2_EPS)
    k2 = xk * (1.0 + (a - 1.0) * k_a)
    am = -kkn
    bm = kkn * a

    cum = _mm_xl(_tril_ones_x2(c), logw)
    cum_ex = cum - logw
    c_last = cum[c - 1:c, :]
    c_mid = cum[c // 2 - 1:c // 2, :]
    e_out = jnp.exp(c_mid - cum)
    e_last = jnp.exp(c_last - cum)
    rt = xr * jnp.exp(cum - c_mid)
    at = am * jnp.exp(cum_ex - c_mid)
    bt = bm * e_out
    kt = k2 * e_out
    a_abs = am * jnp.exp(cum_ex)
    r_abs = xr * jnp.exp(cum)
    bh = bm * e_last
    kh = k2 * e_last
    w_c = jnp.exp(c_last)

    head0 = _iota((c, w), 1) < RWKV_HEAD
    head0_2 = _iota((c, 2 * w), 1) % w < RWKV_HEAD
    wi = _iota((c, 2 * c), 0)
    wj = _iota((c, 2 * c), 1) % c
    strict = wi > wj
    incl = wi >= wj

    gs = range(batch * lw // w)
    G = _lane_groups
    at_g, rt_g, bt_g, kt_g, xv_g = G(at), G(rt), G(bt), G(kt), G(xv)
    aa = [_mm_nt(jnp.concatenate([at_g[g], rt_g[g]], axis=0),
                 jnp.concatenate([_expand2(bt_g[g], head0), _expand2(kt_g[g], head0)], axis=0)) for g in gs]
    a_ab = [jnp.where(strict, aa[g][0:c, 0:2 * c], 0.0) for g in gs]
    a_ak = [jnp.where(strict, aa[g][0:c, 2 * c:4 * c], 0.0) for g in gs]
    a_rb = [jnp.where(incl, aa[g][c:2 * c, 0:2 * c], 0.0) for g in gs]
    a_rk = [jnp.where(incl, aa[g][c:2 * c, 2 * c:4 * c], 0.0) for g in gs]

    xv_e = [_expand2(xv_g[g], head0) for g in gs]
    akv = [_mm(a_ak[g], xv_e[g]) for g in gs]
    eye_w = jnp.where(wi == wj, 1.0, 0.0)
    tw = [eye_w + a_ab[g] for g in gs]
    pw = [_mm(a_ab[g], _expand2(a_ab[g], head0)) for g in gs]
    levels = int(math.log2(c))
    for lvl in range(1, levels):
        if lvl + 1 < levels:
            pt = [_mm(pw[g], _expand2(jnp.concatenate([pw[g], tw[g]], axis=1), head0_2)) for g in gs]
            pw = [pt[g][:, 0:w] for g in gs]
            tw = [tw[g] + pt[g][:, w:2 * w] for g in gs]
        else:
            tw = [tw[g] + _mm(pw[g], _expand2(tw[g], head0)) for g in gs]
    a_abs_g = G(a_abs)
    x = [_mm(tw[g], _expand2(jnp.concatenate([a_abs_g[g], akv[g]], axis=1), head0_2)) for g in gs]
    wm = [x[g][:, 0:w] for g in gs]
    u0 = [x[g][:, w:2 * w] for g in gs]

    bh_g, kh_g, wc_g, rabs_g = G(bh), G(kh), G(w_c), G(r_abs)
    m_mat = [jnp.where(same_head, _mm_tn(bh_g[g], wm[g]), 0.0) + eye * wc_g[g] for g in gs]
    n_mat = [jnp.where(same_head, _mm_tn(jnp.concatenate([bh_g[g], kh_g[g]], axis=0),
                                         jnp.concatenate([u0[g], xv_g[g]], axis=0)), 0.0) for g in gs]
    p_mat = [rabs_g[g] + _mm(a_rb[g], _expand2(wm[g], head0)) for g in gs]
    q_mat = [_mm(jnp.concatenate([a_rb[g], a_rk[g]], axis=1),
                 jnp.concatenate([_expand2(u0[g], head0), xv_e[g]], axis=0)) for g in gs]

    h0 = [_bf(s_ref[g]) for g in gs]
    y = jnp.concatenate([_dot(_bf(p_mat[g]), h0[g]) + q_mat[g] for g in gs], axis=1)
    for g in gs:
        s_ref[g] = _dot(_bf(m_mat[g]), h0[g]) + n_mat[g]

    inv_n = 1.0 / RWKV_HEAD
    d = y - head_sums(y) * inv_n
    var = head_sums(d * d) * inv_n
    yn = d * lax.rsqrt(var + RWKV_GN_EPS) * gn_w + gn_b
    bonus = head_sums(xr * k2 * r_k) * xv
    out = ((yn + bonus) * gate).astype(o_ref.dtype)
    for b in range(batch):
        o_ref[b] = out[:, b * lw:(b + 1) * lw]


def _chunk_specs(batch, c):
    def cur(off, width):
        assert off % width == 0
        return pl.BlockSpec((batch, c, width), lambda t: (0, t, off // width))

    def prev(off, width):
        return pl.BlockSpec((batch, V7X_SUBLANES, width),
                            lambda t: (0, jnp.maximum(t * (c // V7X_SUBLANES) - 1, 0), off // width))

    return cur, prev


def _rwkv_mix(p, cp, gp, mixm, w_up_p, a_up_p, g_up):
    batch, seq, _ = p.shape
    c = CHUNK
    cur, prev = _chunk_specs(batch, c)
    whole = lambda x: pl.BlockSpec(x.shape, lambda t: (0, 0))
    return pl.pallas_call(
        _rwkv_kernel,
        grid=(seq // c,),
        in_specs=[
            cur(OFF_R, RW), cur(OFF_K, RW), cur(OFF_V, RW), prev(OFF_R, RW), prev(OFF_K, RW), prev(OFF_V, RW),
            cur(OFF_MISC, MISC_W), prev(OFF_MISC, MISC_W),
            whole(cp), whole(gp), whole(mixm), whole(w_up_p), whole(a_up_p), whole(g_up),
        ],
        out_specs=pl.BlockSpec((batch, c, RW), lambda t: (0, t, 0)),
        out_shape=jax.ShapeDtypeStruct((batch, seq, RW), BF16),
        scratch_shapes=[pltpu.VMEM((batch * RW // V7X_LANES, V7X_LANES, V7X_LANES), F32)]
        + [pltpu.VMEM((V7X_SUBLANES + c, batch * RW), F32)] * 3
        + [pltpu.VMEM((V7X_SUBLANES + c, batch * MISC_W), F32)],
        compiler_params=_cparams(("arbitrary",)),
        name="rwkv7_mix",
    )(p, p, p, p, p, p, p, p, cp, gp, mixm, w_up_p, a_up_p, g_up)


def _gdn_kernel(q_ref, k_ref, v_ref, pq_ref, pk_ref, pv_ref, z_ref, misc_ref,
                cwq_ref, cwk_ref, cwv_ref, alog_ref, dtb_ref, nw_ref, o_ref, s_ref):
    c = CHUNK
    w = V7X_LANES
    batch, _, lw = q_ref.shape
    assert lw == N_GDN_HEADS * w
    ci = pl.program_id(0)
    heads = batch * N_GDN_HEADS
    hs = range(heads)
    G = _lane_groups

    @pl.when(ci == 0)
    def _():
        s_ref[...] = jnp.zeros_like(s_ref)

    first = ci == 0

    def conv_silu(x_ref, prev_ref, cw_ref):
        x = _batch_lanes(x_ref)
        prev8 = jnp.where(first, 0.0, _batch_lanes(prev_ref))
        cw = _tile_lanes(cw_ref[...], batch)
        row = _iota(x.shape, 0)
        acc = x * cw[3:4, :]
        for s in (1, 2, 3):
            tail = jnp.tile(pltpu.roll(prev8, s, axis=0), (c // V7X_SUBLANES, 1))
            acc = acc + jnp.where(row < s, tail, pltpu.roll(x, s, axis=0)) * cw[3 - s:4 - s, :]
        return acc * _sigmoid(acc)

    def head_sums(x):
        return jnp.concatenate([jnp.broadcast_to(jnp.sum(g, axis=-1, keepdims=True), g.shape) for g in G(x)], axis=1)

    q = conv_silu(q_ref, pq_ref, cwq_ref)
    k = conv_silu(k_ref, pk_ref, cwk_ref)
    v = conv_silu(v_ref, pv_ref, cwv_ref)
    q = q * lax.rsqrt(head_sums(q * q) + L2_EPS) * (GDN_HEAD ** -0.5)
    k = k * lax.rsqrt(head_sums(k * k) + L2_EPS)

    lane2 = _iota((2 * w, w), 0) % w
    lane2r = _iota((c, 2 * w), 1) % w
    bcol, gcol, grow = [], [], []
    for b in range(batch):
        bm = misc_ref[b, :, 0:128]
        beta_hl = jnp.concatenate(_hi_lo(_sigmoid(bm)), axis=1)
        g_full = -jnp.exp(alog_ref[...]) * _softplus(bm + dtb_ref[...])
        gc_hl = jnp.concatenate(_hi_lo(_mm_xl(_tril_ones_x2(c), g_full)), axis=1)
        for h in range(N_GDN_HEADS):
            bcol.append(_dot(beta_hl, jnp.where(lane2 == BETA_LANE + h, 1.0, 0.0).astype(BF16)))
            gcol.append(_dot(gc_hl, jnp.where(lane2 == ALPHA_LANE + h, 1.0, 0.0).astype(BF16)))
            grow.append(_dot_nt(jnp.where(lane2r == ALPHA_LANE + h, 1.0, 0.0).astype(BF16), gc_hl))

    ti = _iota((c, c), 0)
    tj = _iota((c, c), 1)
    causal = ti >= tj
    strict = ti > tj
    decay = [jnp.where(causal, jnp.exp(jnp.where(causal, gcol[h][:, 0:c] - grow[h], 0.0)), 0.0) for h in hs]

    q_g, k_g, v_g = G(q), G(k), G(v)
    kb = [k_g[h] * bcol[h] for h in hs]
    kk = [_mm_nt(jnp.concatenate([kb[h], q_g[h]], axis=0), k_g[h]) for h in hs]
    lneg = [jnp.where(strict, -kk[h][0:c] * decay[h], 0.0) for h in hs]
    a_intra = [kk[h][c:2 * c] * decay[h] for h in hs]

    egc = [jnp.exp(gcol[h]) for h in hs]
    g_last = [gcol[h][c - 1:c, :] for h in hs]
    lane_t = _iota((c, w), 1) >= c
    eye_t = jnp.where(_iota((c, w), 1) == _iota((c, w), 0) + c, 1.0, 0.0)
    zt = [jnp.concatenate([lneg[h], jnp.zeros((c, c), F32)], axis=1) + eye_t for h in hs]
    for _ in range(int(math.log2(c))):
        zb = [_bf(zt[h]) for h in hs]
        zt = [_dot(zb[h][:, 0:c], zb[h]) + jnp.where(lane_t, zt[h], 0.0) for h in hs]
    zeros_x = jnp.zeros((c, 2 * w), F32)
    x = [_mm(zt[h], jnp.concatenate([zeros_x, jnp.concatenate([v_g[h] * bcol[h], kb[h] * egc[h]], axis=1)], axis=0))
         for h in hs]
    uu = [x[h][:, 0:w] for h in hs]
    ww = [x[h][:, w:2 * w] for h in hs]

    kd = [k_g[h] * jnp.exp(g_last[h] - gcol[h]) for h in hs]
    eye = jnp.where(_iota((w, w), 0) == _iota((w, w), 1), 1.0, 0.0)
    m_mat = [eye * jnp.exp(g_last[h]) - _mm_tn(kd[h], ww[h]) for h in hs]
    n_mat = [_mm_tn(kd[h], uu[h]) for h in hs]
    p_mat = [q_g[h] * egc[h] - _mm(a_intra[h], ww[h]) for h in hs]
    q_mat = [_mm(a_intra[h], uu[h]) for h in hs]

    s0 = [_bf(s_ref[h]) for h in hs]
    o = jnp.concatenate([_dot(_bf(p_mat[h]), s0[h]) + q_mat[h] for h in hs], axis=1)
    for h in hs:
        s_ref[h] = _dot(_bf(m_mat[h]), s0[h]) + n_mat[h]

    o = o * lax.rsqrt(head_sums(o * o) * (1.0 / GDN_HEAD) + NORM_EPS) * _tile_lanes(nw_ref[...], heads)
    z = _batch_lanes(z_ref)
    out = (o * (z * _sigmoid(z))).astype(o_ref.dtype)
    for b in range(batch):
        o_ref[b] = out[:, b * lw:(b + 1) * lw]


def _gdn_mix(p, conv_w, alog_l, dtb_l, norm_w):
    batch, seq, _ = p.shape
    c = CHUNK
    cur, prev = _chunk_specs(batch, c)

    def cw(part):
        return pl.BlockSpec((4, RW), lambda t: (0, part))

    lane_vec = pl.BlockSpec((1, V7X_LANES), lambda t: (0, 0))
    return pl.pallas_call(
        _gdn_kernel,
        grid=(seq // c,),
        in_specs=[
            cur(OFF_GQ, RW), cur(OFF_GK, RW), cur(OFF_GV, RW), prev(OFF_GQ, RW), prev(OFF_GK, RW), prev(OFF_GV, RW),
            cur(OFF_Z, RW), cur(OFF_MISC, MISC_W),
            cw(0), cw(1), cw(2), lane_vec, lane_vec, lane_vec,
        ],
        out_specs=pl.BlockSpec((batch, c, RW), lambda t: (0, t, 0)),
        out_shape=jax.ShapeDtypeStruct((batch, seq, RW), BF16),
        scratch_shapes=[pltpu.VMEM((batch * N_GDN_HEADS, V7X_LANES, V7X_LANES), F32)],
        compiler_params=_cparams(("arbitrary",)),
        name="gdn_mix",
    )(p, p, p, p, p, p, p, p, conv_w, conv_w, conv_w, alog_l, dtb_l, norm_w)


def _xattn_kernel(x_ref, ymix_ref, gmix_ref, k_ref, v_ref, wq_ref, wo_ref, gpre_ref, gpost_ref, gnext_ref,
                  h_out_ref, f_out_ref):
    ym = ymix_ref[...]
    h = x_ref[...] + ym * lax.rsqrt(jnp.mean(ym * ym, axis=-1, keepdims=True) + NORM_EPS) * gmix_ref[...]
    cn =(h * lax.rsqrt(jnp.mean(h * h, axis=-1, keepdims=True) + NORM_EPS) * gpre_ref[...]).astype(BF16)
    q = _dot(cn, wq_ref[...]).astype(BF16)
    kk = k_ref[0]
    vv = v_ref[0]
    outs = []
    for hh in range(XA_HEADS):
        sl = slice(hh * XA_HEAD, (hh + 1) * XA_HEAD)
        s = _dot_nt(q[:, sl], kk[:, sl]) * (XA_HEAD ** -0.5)
        s = s - jnp.max(s, axis=-1, keepdims=True)
        e = jnp.exp(s)
        pr = e / jnp.sum(e, axis=-1, keepdims=True)
        outs.append(_dot(pr.astype(BF16), vv[:, sl]))
    o = jnp.concatenate(outs, axis=1).astype(BF16)
    y = _dot(o, wo_ref[...])
    h2 = h + y * lax.rsqrt(jnp.mean(y * y, axis=-1, keepdims=True) + NORM_EPS) * gpost_ref[...]
    h_out_ref[...] = h2
    f_out_ref[...] = (h2 * lax.rsqrt(jnp.mean(h2 * h2, axis=-1, keepdims=True) + NORM_EPS)
                      * gnext_ref[...]).astype(f_out_ref.dtype)


def _xattn_block(x, y_mix, g_mix, kmem, vmem, w_q, w_o, g_pre, g_post, g_next, batch, seq, tt=256):
    n, d = x.shape
    mlen, xw = kmem.shape[1], kmem.shape[2]
    per_b = seq // tt
    row = pl.BlockSpec((tt, d), lambda i: (i, 0))
    vec = pl.BlockSpec((1, d), lambda i: (0, 0))
    mem = pl.BlockSpec((1, mlen, xw), lambda i: (i // per_b, 0, 0))
    return pl.pallas_call(
        _xattn_kernel,
        grid=(n // tt,),
        in_specs=[row, row, vec, mem, mem, pl.BlockSpec((d, xw), lambda i: (0, 0)),
                  pl.BlockSpec((xw, d), lambda i: (0, 0)), vec, vec, vec],
        out_specs=[row, row],
        out_shape=[jax.ShapeDtypeStruct((n, d), F32), jax.ShapeDtypeStruct((n, d), BF16)],
        compiler_params=_cparams(("parallel",)),
        name="xattn_block",
    )(x, y_mix, g_mix.reshape(1, d), kmem, vmem, w_q, w_o, g_pre.reshape(1, d), g_post.reshape(1, d),
      g_next.reshape(1, d))


SRC_LORA = 3 * RW
SRC_GDN = SRC_LORA + 2 * LORA + 256
SRC_BA = SRC_GDN + 4 * RW
SRC_GATES = SRC_BA + 2 * N_GDN_HEADS
RELAYOUT_ROWS = 512


def _relayout_kernel(x_ref, misc_ref, o_ref):
    @pl.when(pl.program_id(0) != OFF_MISC // RELAYOUT_ROWS)
    def _():
        o_ref[...] = x_ref[...].astype(o_ref.dtype)

    @pl.when(pl.program_id(0) == OFF_MISC // RELAYOUT_ROWS)
    def _():
        o_ref[...] = misc_ref[...].astype(o_ref.dtype)


def _relayout_w_in_t(w_in_t):
    n_in, d = w_in_t.shape
    rb = RELAYOUT_ROWS
    assert OFF_GQ % rb == 0 and OFF_MISC % rb == 0 and OFF_GATE_RW % rb == 0 and MISC_W == rb
    zeros = jnp.zeros((128 - LORA, d), w_in_t.dtype)
    misc = jnp.concatenate([w_in_t[SRC_LORA:SRC_LORA + LORA], w_in_t[SRC_BA:SRC_BA + 2 * N_GDN_HEADS],
                            w_in_t[SRC_LORA + LORA:SRC_LORA + 2 * LORA], zeros,
                            w_in_t[SRC_LORA + 2 * LORA:SRC_GDN]], axis=0)

    def src_row(j):
        r = j * rb
        src = jnp.where(r < OFF_GQ, r, jnp.where(r < OFF_MISC, r + (SRC_GDN - OFF_GQ),
                                                 jnp.where(r < OFF_GATE_RW, 0, r + (SRC_GATES - OFF_GATE_RW))))
        return pl.multiple_of(src, math.gcd(rb, SRC_GDN, SRC_GATES))

    return pl.pallas_call(
        _relayout_kernel,
        grid=(NP // rb,),
        in_specs=[pl.BlockSpec((pl.Element(rb), pl.Element(d)), lambda j: (src_row(j), 0)),
                  pl.BlockSpec((rb, d), lambda j: (0, 0))],
        out_specs=pl.BlockSpec((rb, d), lambda j: (j, 0)),
        out_shape=jax.ShapeDtypeStruct((NP, d), BF16),
        compiler_params=_cparams(("parallel",)),
        name="relayout_w_in",
    )(w_in_t, misc)


def _layer(x, mem, mix_norm_pre, mix_norm_post, w_in, rwkv_shift_mix, rwkv_w0, rwkv_w_up, rwkv_a0, rwkv_a_up,
           rwkv_g_up, rwkv_k_k, rwkv_k_a, rwkv_r_k, rwkv_gn_w, rwkv_gn_b, gdn_conv_w, gdn_a_log, gdn_dt_bias,
           gdn_norm_w, w_branch_rwkv, w_branch_gdn, w_mix_out, xa_norm_pre, xa_norm_mem, xa_norm_post, xa_w_q,
           xa_w_kv, xa_w_o, mlp_norm_pre, mlp_norm_post, mlp_w_up, mlp_w_down):
    batch, seq, d = x.shape
    n = batch * seq
    h = x.reshape(n, d)

    w_in_b = _relayout_w_in_t(jnp.swapaxes(w_in, 0, 1))
    mix = rwkv_shift_mix
    o = 3 * RW
    cp = jnp.stack([mix[0:RW], mix[RW:2 * RW], mix[2 * RW:o], rwkv_w0, rwkv_a0, rwkv_k_k, rwkv_k_a,
                    rwkv_r_k.reshape(RW)], axis=0)
    gp = jnp.concatenate([rwkv_gn_w[None], rwkv_gn_b[None], jnp.zeros((6, RW), F32)], axis=0)
    zeros = lambda k: jnp.zeros((k,), F32)
    mixm = jnp.concatenate([mix[o:o + LORA], zeros(32), mix[o + LORA:o + 2 * LORA], zeros(32),
                            mix[o + 2 * LORA:o + 2 * LORA + 256]])[None]
    w_up_p = jnp.concatenate([rwkv_w_up, jnp.zeros((128 - LORA, RW), F32)], axis=0).astype(BF16)
    a_up_p = jnp.concatenate([rwkv_a_up, jnp.zeros((128 - LORA, RW), F32)], axis=0).astype(BF16)
    g_up_b = rwkv_g_up.astype(BF16)
    alog_l = jnp.concatenate([zeros(ALPHA_LANE), gdn_a_log])[None]
    dtb_l = jnp.concatenate([zeros(ALPHA_LANE), gdn_dt_bias])[None]

    u = _rmsnorm(h, mix_norm_pre, BF16)
    p = _matmul(u, w_in_b, F32, tm=1024, tn=1280, b_is_nk=True, name="in_proj")
    p3 = p.reshape(batch, seq, NP)
    y_rw = _rwkv_mix(p3, cp, gp, mixm, w_up_p, a_up_p, g_up_b).reshape(n, RW)
    y_gdn = _gdn_mix(p3, gdn_conv_w, alog_l, dtb_l, gdn_norm_w[None]).reshape(n, RW)
    merged = _branch_merge(y_rw, y_gdn, w_branch_rwkv, w_branch_gdn, p)
    y_mix = _matmul_wcast(merged, w_mix_out, F32, tm=512, tn=1024, name="mix_out")

    mlen = mem.shape[1]
    m = _rmsnorm(mem.reshape(batch * mlen, d), xa_norm_mem, BF16)
    kv = _matmul(m, xa_w_kv.astype(BF16), BF16, tm=batch * mlen, tn=512, name="xa_kv")
    xw = XA_HEADS * XA_HEAD
    kmem = kv[:, :xw].reshape(batch, mlen, xw)
    vmem = kv[:, xw:].reshape(batch, mlen, xw)
    h, f = _xattn_block(h, y_mix, mix_norm_post, kmem, vmem, xa_w_q.astype(BF16), xa_w_o.astype(BF16),
                        xa_norm_pre, xa_norm_post, mlp_norm_pre, batch, seq)

    hmid = _matmul_wcast(f, mlp_w_up, BF16, tm=2048, tn=512, relu2=True, name="mlp_up")
    y_mlp = _matmul_kacc(hmid, mlp_w_down, tm=2048, tn=1024, tk=2048, k_chunk=1024, name="mlp_down")
    h = _resid_norm(h, y_mlp, mlp_norm_post)
    return h.reshape(batch, seq, d)


def kernel(x, mem, mix_norm_pre, mix_norm_post, w_in, rwkv_shift_mix, rwkv_w0, rwkv_w_up, rwkv_a0, rwkv_a_up,
           rwkv_g_up, rwkv_k_k, rwkv_k_a, rwkv_r_k, rwkv_gn_w, rwkv_gn_b, gdn_conv_w, gdn_a_log, gdn_dt_bias,
           gdn_norm_w, w_branch_rwkv, w_branch_gdn, w_mix_out, xa_norm_pre, xa_norm_mem, xa_norm_post, xa_w_q,
           xa_w_kv, xa_w_o, mlp_norm_pre, mlp_norm_post, mlp_w_up, mlp_w_down):
    args = (mix_norm_pre, mix_norm_post, w_in, rwkv_shift_mix, rwkv_w0, rwkv_w_up, rwkv_a0, rwkv_a_up,
            rwkv_g_up, rwkv_k_k, rwkv_k_a, rwkv_r_k, rwkv_gn_w, rwkv_gn_b, gdn_conv_w, gdn_a_log, gdn_dt_bias,
            gdn_norm_w, w_branch_rwkv, w_branch_gdn, w_mix_out, xa_norm_pre, xa_norm_mem, xa_norm_post, xa_w_q,
            xa_w_kv, xa_w_o, mlp_norm_pre, mlp_norm_post, mlp_w_up, mlp_w_down)
    h = x
    for layer in range(mix_norm_pre.shape[0]):
        h = _layer(h, mem, *(a[layer] for a in args))
    return h
```

```python
import functools
import math

import jax
import jax.numpy as jnp
from jax import lax
from jax.experimental import pallas as pl
from jax.experimental.pallas import tpu as pltpu

F32 = jnp.float32
BF16 = jnp.bfloat16

V7X_LANES = 128
V7X_SUBLANES = 8
V7X_VMEM_LIMIT_BYTES = 60 * 1024 * 1024

NORM_EPS = 1e-6
L2_EPS = 1e-6
RWKV_GN_EPS = 64e-5
RWKV_HEAD = 64
GDN_HEAD = 128
CHUNK = 64
XA_HEADS = 4
XA_HEAD = 128

RW = 2048
OFF_R, OFF_K, OFF_V = 0, RW, 2 * RW
OFF_GQ, OFF_GK, OFF_GV = 3 * RW, 4 * RW, 5 * RW
OFF_Z = 6 * RW
OFF_MISC = 7 * RW
MISC_W = 512
OFF_GATE_RW = OFF_MISC + MISC_W
D_MODEL = 4096
OFF_GATE_GDN = OFF_GATE_RW + D_MODEL
NP = OFF_GATE_GDN + D_MODEL
LORA = 96
N_GDN_HEADS = RW // GDN_HEAD
BETA_LANE = LORA
ALPHA_LANE = LORA + N_GDN_HEADS


def _cparams(sem):
    return pltpu.CompilerParams(dimension_semantics=sem, vmem_limit_bytes=V7X_VMEM_LIMIT_BYTES)


def _dot(a, b):
    return jnp.dot(a, b, preferred_element_type=F32)


def _dot_nt(a, b):
    return lax.dot_general(a, b, (((1,), (1,)), ((), ())), preferred_element_type=F32)


def _dot_tn(a, b):
    return lax.dot_general(a, b, (((0,), (0,)), ((), ())), preferred_element_type=F32)


def _bf(x):
    return x.astype(BF16)


def _hi_lo(x):
    hi = x.astype(BF16)
    return hi, (x - hi.astype(F32)).astype(BF16)


def _mm(a, b):
    return _dot(_bf(a), _bf(b))


def _mm_nt(a, b):
    return _dot_nt(_bf(a), _bf(b))


def _mm_tn(a, b):
    return _dot_tn(_bf(a), _bf(b))


def _mm_xr(a, b01x2):
    hi, lo = _hi_lo(a)
    return _dot(jnp.concatenate([hi, lo], axis=1), b01x2)


def _mm_xl(a01x2, b):
    hi, lo = _hi_lo(b)
    return _dot(a01x2, jnp.concatenate([hi, lo], axis=0))


def _sigmoid(x):
    return 1.0 / (1.0 + jnp.exp(-x))


def _softplus(x):
    return jnp.maximum(x, 0.0) + jnp.log1p(jnp.exp(-jnp.abs(x)))


def _iota(shape, dim):
    return lax.broadcasted_iota(jnp.int32, shape, dim)


def _lane_groups(x):
    return [x[:, g * V7X_LANES:(g + 1) * V7X_LANES] for g in range(x.shape[1] // V7X_LANES)]


def _tril_ones_x2(c):
    return jnp.where(_iota((c, 2 * c), 0) >= _iota((c, 2 * c), 1) % c, 1.0, 0.0).astype(BF16)


def _rmsnorm_kernel(x_ref, g_ref, o_ref):
    x = x_ref[...]
    inv = lax.rsqrt(jnp.mean(x * x, axis=-1, keepdims=True) + NORM_EPS)
    o_ref[...] = (x * inv * g_ref[...]).astype(o_ref.dtype)


def _rmsnorm(x, gain, out_dtype, rows=256):
    n, d = x.shape
    return pl.pallas_call(
        _rmsnorm_kernel,
        grid=(n // rows,),
        in_specs=[pl.BlockSpec((rows, d), lambda i: (i, 0)), pl.BlockSpec((1, d), lambda i: (0, 0))],
        out_specs=pl.BlockSpec((rows, d), lambda i: (i, 0)),
        out_shape=jax.ShapeDtypeStruct((n, d), out_dtype),
        compiler_params=_cparams(("parallel",)),
        name="rmsnorm",
    )(x, gain.reshape(1, d))


def _resid_norm_kernel(h_ref, y_ref, g_ref, o_ref):
    y = y_ref[...].astype(F32)
    inv = lax.rsqrt(jnp.mean(y * y, axis=-1, keepdims=True) + NORM_EPS)
    o_ref[...] = h_ref[...] + y * inv * g_ref[...]


def _resid_norm(h, y, gain, rows=256):
    n, d = h.shape
    row = pl.BlockSpec((rows, d), lambda i: (i, 0))
    return pl.pallas_call(
        _resid_norm_kernel,
        grid=(n // rows,),
        in_specs=[row, row, pl.BlockSpec((1, d), lambda i: (0, 0))],
        out_specs=row,
        out_shape=jax.ShapeDtypeStruct((n, d), F32),
        compiler_params=_cparams(("parallel",)),
        name="resid_norm",
    )(h, y, gain.reshape(1, d))


def _mm_kernel(a_ref, b_ref, o_ref, *, relu2, b_is_nk):
    acc = _dot_nt(a_ref[...], b_ref[...]) if b_is_nk else _dot(a_ref[...], b_ref[...])
    if relu2:
        acc = jnp.square(jnp.maximum(acc, 0.0))
    o_ref[...] = acc.astype(o_ref.dtype)


def _matmul(a, b, out_dtype, tm, tn, relu2=False, b_is_nk=False, name="matmul"):
    m, k = a.shape
    n = b.shape[0] if b_is_nk else b.shape[1]
    b_spec = pl.BlockSpec((tn, k), lambda i, j: (j, 0)) if b_is_nk else pl.BlockSpec((k, tn), lambda i, j: (0, j))
    return pl.pallas_call(
        functools.partial(_mm_kernel, relu2=relu2, b_is_nk=b_is_nk),
        grid=(m // tm, n // tn),
        in_specs=[pl.BlockSpec((tm, k), lambda i, j: (i, 0)), b_spec],
        out_specs=pl.BlockSpec((tm, tn), lambda i, j: (i, j)),
        out_shape=jax.ShapeDtypeStruct((m, n), out_dtype),
        compiler_params=_cparams(("parallel", "arbitrary")),
        name=name,
    )(a, b)


def _mm_wcast_kernel(a_ref, b_ref, o_ref, b_bf16, *, relu2):
    @pl.when(pl.program_id(1) == 0)
    def _():
        b_bf16[...] = b_ref[...].astype(BF16)

    acc = _dot(a_ref[...], b_bf16[...])
    if relu2:
        acc = jnp.square(jnp.maximum(acc, 0.0))
    o_ref[...] = acc.astype(o_ref.dtype)


def _matmul_wcast(a, b, out_dtype, tm, tn, relu2=False, name="matmul_wcast"):
    m, k = a.shape
    _, n = b.shape
    return pl.pallas_call(
        functools.partial(_mm_wcast_kernel, relu2=relu2),
        grid=(n // tn, m // tm),
        in_specs=[pl.BlockSpec((tm, k), lambda j, i: (i, 0)), pl.BlockSpec((k, tn), lambda j, i: (0, j))],
        out_specs=pl.BlockSpec((tm, tn), lambda j, i: (i, j)),
        out_shape=jax.ShapeDtypeStruct((m, n), out_dtype),
        scratch_shapes=[pltpu.VMEM((k, tn), BF16)],
        compiler_params=_cparams(("parallel", "arbitrary")),
        name=name,
    )(a, b)


def _mm_kacc_kernel(a_ref, b_ref, o_ref, acc_ref, *, k_chunk):
    l = pl.program_id(2)

    @pl.when(l == 0)
    def _():
        acc_ref[...] = jnp.zeros_like(acc_ref)

    acc = None
    for k0 in range(0, a_ref.shape[1], k_chunk):
        part = _dot(a_ref[:, k0:k0 + k_chunk], b_ref[k0:k0 + k_chunk, :].astype(BF16))
        acc = part if acc is None else acc + part
    acc_ref[...] += acc

    @pl.when(l == pl.num_programs(2) - 1)
    def _():
        o_ref[...] = acc_ref[...].astype(o_ref.dtype)


def _matmul_kacc(a, b, out_dtype, tm, tn, tk, k_chunk=512, name="matmul_kacc"):
    m, k = a.shape
    _, n = b.shape
    return pl.pallas_call(
        functools.partial(_mm_kacc_kernel, k_chunk=k_chunk),
        grid=(m // tm, n // tn, k // tk),
        in_specs=[pl.BlockSpec((tm, tk), lambda i, j, l: (i, l)), pl.BlockSpec((tk, tn), lambda i, j, l: (l, j))],
        out_specs=pl.BlockSpec((tm, tn), lambda i, j, l: (i, j)),
        out_shape=jax.ShapeDtypeStruct((m, n), out_dtype),
        scratch_shapes=[pltpu.VMEM((tm, tn), F32)],
        compiler_params=_cparams(("parallel", "parallel", "arbitrary")),
        name=name,
    )(a, b)


def _branch_kernel(yr_ref, yg_ref, wr_ref, wg_ref, gr_ref, gg_ref, o_ref, wr_bf16, wg_bf16):
    @pl.when(pl.program_id(1) == 0)
    def _():
        wr_bf16[...] = wr_ref[...].astype(BF16)
        wg_bf16[...] = wg_ref[...].astype(BF16)

    br = _dot(yr_ref[...], wr_bf16[...])
    bg = _dot(yg_ref[...], wg_bf16[...])
    sig_r = 0.5 * jnp.tanh(0.5 * gr_ref[...]) + 0.5
    sig_g = 0.5 * jnp.tanh(0.5 * gg_ref[...]) + 0.5
    o_ref[...] = (sig_r * br + sig_g * bg).astype(o_ref.dtype)


def _branch_merge(y_rw, y_gdn, w_rw, w_gdn, p, tm=1024, tn=512):
    m, k = y_rw.shape
    n = w_rw.shape[1]
    jr, jg = OFF_GATE_RW // tn, OFF_GATE_GDN // tn
    return pl.pallas_call(
        _branch_kernel,
        grid=(n // tn, m // tm),
        in_specs=[
            pl.BlockSpec((tm, k), lambda j, i: (i, 0)),
            pl.BlockSpec((tm, k), lambda j, i: (i, 0)),
            pl.BlockSpec((k, tn), lambda j, i: (0, j)),
            pl.BlockSpec((k, tn), lambda j, i: (0, j)),
            pl.BlockSpec((tm, tn), lambda j, i: (i, jr + j)),
            pl.BlockSpec((tm, tn), lambda j, i: (i, jg + j)),
        ],
        out_specs=pl.BlockSpec((tm, tn), lambda j, i: (i, j)),
        out_shape=jax.ShapeDtypeStruct((m, n), BF16),
        scratch_shapes=[pltpu.VMEM((k, tn), BF16)] * 2,
        compiler_params=_cparams(("parallel", "arbitrary")),
        name="branch_merge",
    )(y_rw, y_gdn, w_rw, w_gdn, p, p)


def _batch_lanes(ref):
    return jnp.concatenate([ref[b] for b in range(ref.shape[0])], axis=1)


def _tile_lanes(x, n):
    return jnp.concatenate([x] * n, axis=1)


def _token_shift(buf_ref, x_ref, prev_ref, first, mix):
    batch, _, width = x_ref.shape
    for b in range(batch):
        buf_ref[0:V7X_SUBLANES, b * width:(b + 1) * width] = jnp.where(first, 0.0, prev_ref[b])
        buf_ref[V7X_SUBLANES:, b * width:(b + 1) * width] = x_ref[b]
    x = buf_ref[V7X_SUBLANES:, :]
    return x + (buf_ref[V7X_SUBLANES - 1:buf_ref.shape[0] - 1, :] - x) * mix


def _expand2(x, head0):
    return jnp.concatenate([jnp.where(head0, x, 0.0), jnp.where(head0, 0.0, x)], axis=0)


def _rwkv_kernel(r_ref, k_ref, v_ref, pr_ref, pk_ref, pv_ref, misc_ref, pmisc_ref,
                 cp_ref, gp_ref, mixm_ref, wup_ref, aup_ref, gup_ref, o_ref, s_ref, rbuf, kbuf, vbuf, mbuf):
    c = CHUNK
    w = V7X_LANES
    batch, _, lw = r_ref.shape
    ci = pl.program_id(0)

    @pl.when(ci == 0)
    def _():
        s_ref[...] = jnp.zeros_like(s_ref)

    first = ci == 0
    mix_r, mix_k, mix_v, w0, a0, k_k, k_a, r_k = (_tile_lanes(cp_ref[i:i + 1, :], batch) for i in range(8))
    gn_w, gn_b = _tile_lanes(gp_ref[0:1, :], batch), _tile_lanes(gp_ref[1:2, :], batch)

    misc_m = _token_shift(mbuf, misc_ref, pmisc_ref, first, _tile_lanes(mixm_ref[...], batch))
    xr = _token_shift(rbuf, r_ref, pr_ref, first, mix_r)
    xk = _token_shift(kbuf, k_ref, pk_ref, first, mix_k)
    xv = _token_shift(vbuf, v_ref, pv_ref, first, mix_v)

    def lora(fn, lo, hi, w_ref):
        return jnp.concatenate([_dot(_bf(fn(misc_m[:, b * MISC_W + lo:b * MISC_W + hi])), w_ref[...])
                                for b in range(batch)], axis=1)

    wlin = w0 + lora(jnp.tanh, 0, 128, wup_ref)
    logw = -math.exp(-0.5) * _sigmoid(wlin)
    a = _sigmoid(a0 + lora(lambda t: t, 128, 256, aup_ref))
    gate = lora(_sigmoid, 256, 512, gup_ref)

    same_head = (_iota((w, w), 0) // RWKV_HEAD) == (_iota((w, w), 1) // RWKV_HEAD)
    bones = jnp.where(same_head, 1.0, 0.0).astype(BF16)
    eye = jnp.where(_iota((w, w), 0) == _iota((w, w), 1), 1.0, 0.0)

    def head_sums(x):
        n = x.shape[1] // w
        s = _dot(_bf(jnp.concatenate(_lane_groups(x), axis=0)), bones)
        return jnp.concatenate([s[g * c:(g + 1) * c] for g in range(n)], axis=1)

    kkr = xk * k_k
    kkn = kkr * lax.rsqrt(head_sums(kkr * kkr) + L2_EPS)
    k2 = xk * (1.0 + (a - 1.0) * k_a)
    am = -kkn
    bm = kkn * a

    cum = _mm_xl(_tril_ones_x2(c), logw)
    cum_ex = cum - logw
    c_last = cum[c - 1:c, :]
    c_mid = cum[c // 2 - 1:c // 2, :]
    e_out = jnp.exp(c_mid - cum)
    e_last = jnp.exp(c_last - cum)
    rt = xr * jnp.exp(cum - c_mid)
    at = am * jnp.exp(cum_ex - c_mid)
    bt = bm * e_out
    kt = k2 * e_out
    a_abs = am * jnp.exp(cum_ex)
    r_abs = xr * jnp.exp(cum)
    bh = bm * e_last
    kh = k2 * e_last
    w_c = jnp.exp(c_last)

    head0 = _iota((c, w), 1) < RWKV_HEAD
    head0_2 = _iota((c, 2 * w), 1) % w < RWKV_HEAD
    wi = _iota((c, 2 * c), 0)
    wj = _iota((c, 2 * c), 1) % c
    strict = wi > wj
    incl = wi >= wj

    gs = range(batch * lw // w)
    G = _lane_groups
    at_g, rt_g, bt_g, kt_g, xv_g = G(at), G(rt), G(bt), G(kt), G(xv)
    aa = [_mm_nt(jnp.concatenate([at_g[g], rt_g[g]], axis=0),
                 jnp.concatenate([_expand2(bt_g[g], head0), _expand2(kt_g[g], head0)], axis=0)) for g in gs]
    a_ab = [jnp.where(strict, aa[g][0:c, 0:2 * c], 0.0) for g in gs]
    a_ak = [jnp.where(strict, aa[g][0:c, 2 * c:4 * c], 0.0) for g in gs]
    a_rb = [jnp.where(incl, aa[g][c:2 * c, 0:2 * c], 0.0) for g in gs]
    a_rk = [jnp.where(incl, aa[g][c:2 * c, 2 * c:4 * c], 0.0) for g in gs]

    xv_e = [_expand2(xv_g[g], head0) for g in gs]
    akv = [_mm(a_ak[g], xv_e[g]) for g in gs]
    eye_w = jnp.where(wi == wj, 1.0, 0.0)
    tw = [eye_w + a_ab[g] for g in gs]
    pw = [_mm(a_ab[g], _expand2(a_ab[g], head0)) for g in gs]
    levels = int(math.log2(c))
    for lvl in range(1, levels):
        if lvl + 1 < levels:
            pt = [_mm(pw[g], _expand2(jnp.concatenate([pw[g], tw[g]], axis=1), head0_2)) for g in gs]
            pw = [pt[g][:, 0:w] for g in gs]
            tw = [tw[g] + pt[g][:, w:2 * w] for g in gs]
        else:
            tw = [tw[g] + _mm(pw[g], _expand2(tw[g], head0)) for g in gs]
    a_abs_g = G(a_abs)
    x = [_mm(tw[g], _expand2(jnp.concatenate([a_abs_g[g], akv[g]], axis=1), head0_2)) for g in gs]
    wm = [x[g][:, 0:w] for g in gs]
    u0 = [x[g][:, w:2 * w] for g in gs]

    bh_g, kh_g, wc_g, rabs_g = G(bh), G(kh), G(w_c), G(r_abs)
    m_mat = [jnp.where(same_head, _mm_tn(bh_g[g], wm[g]), 0.0) + eye * wc_g[g] for g in gs]
    n_mat = [jnp.where(same_head, _mm_tn(jnp.concatenate([bh_g[g], kh_g[g]], axis=0),
                                         jnp.concatenate([u0[g], xv_g[g]], axis=0)), 0.0) for g in gs]
    p_mat = [rabs_g[g] + _mm(a_rb[g], _expand2(wm[g], head0)) for g in gs]
    q_mat = [_mm(jnp.concatenate([a_rb[g], a_rk[g]], axis=1),
                 jnp.concatenate([_expand2(u0[g], head0), xv_e[g]], axis=0)) for g in gs]

    h0 = [_bf(s_ref[g]) for g in gs]
    y = jnp.concatenate([_dot(_bf(p_mat[g]), h0[g]) + q_mat[g] for g in gs], axis=1)
    for g in gs:
        s_ref[g] = _dot(_bf(m_mat[g]), h0[g]) + n_mat[g]

    inv_n = 1.0 / RWKV_HEAD
    d = y - head_sums(y) * inv_n
    var = head_sums(d * d) * inv_n
    yn = d * lax.rsqrt(var + RWKV_GN_EPS) * gn_w + gn_b
    bonus = head_sums(xr * k2 * r_k) * xv
    out = ((yn + bonus) * gate).astype(o_ref.dtype)
    for b in range(batch):
        o_ref[b] = out[:, b * lw:(b + 1) * lw]


def _chunk_specs(batch, c):
    def cur(off, width):
        assert off % width == 0
        return pl.BlockSpec((batch, c, width), lambda t: (0, t, off // width))

    def prev(off, width):
        return pl.BlockSpec((batch, V7X_SUBLANES, width),
                            lambda t: (0, jnp.maximum(t * (c // V7X_SUBLANES) - 1, 0), off // width))

    return cur, prev


def _rwkv_mix(p, cp, gp, mixm, w_up_p, a_up_p, g_up):
    batch, seq, _ = p.shape
    c = CHUNK
    cur, prev = _chunk_specs(batch, c)
    whole = lambda x: pl.BlockSpec(x.shape, lambda t: (0, 0))
    return pl.pallas_call(
        _rwkv_kernel,
        grid=(seq // c,),
        in_specs=[
            cur(OFF_R, RW), cur(OFF_K, RW), cur(OFF_V, RW), prev(OFF_R, RW), prev(OFF_K, RW), prev(OFF_V, RW),
            cur(OFF_MISC, MISC_W), prev(OFF_MISC, MISC_W),
            whole(cp), whole(gp), whole(mixm), whole(w_up_p), whole(a_up_p), whole(g_up),
        ],
        out_specs=pl.BlockSpec((batch, c, RW), lambda t: (0, t, 0)),
        out_shape=jax.ShapeDtypeStruct((batch, seq, RW), BF16),
        scratch_shapes=[pltpu.VMEM((batch * RW // V7X_LANES, V7X_LANES, V7X_LANES), F32)]
        + [pltpu.VMEM((V7X_SUBLANES + c, batch * RW), F32)] * 3
        + [pltpu.VMEM((V7X_SUBLANES + c, batch * MISC_W), F32)],
        compiler_params=_cparams(("arbitrary",)),
        name="rwkv7_mix",
    )(p, p, p, p, p, p, p, p, cp, gp, mixm, w_up_p, a_up_p, g_up)


def _gdn_kernel(q_ref, k_ref, v_ref, pq_ref, pk_ref, pv_ref, z_ref, misc_ref,
                cwq_ref, cwk_ref, cwv_ref, alog_ref, dtb_ref, nw_ref, o_ref, s_ref):
    c = CHUNK
    w = V7X_LANES
    batch, _, lw = q_ref.shape
    assert lw == N_GDN_HEADS * w
    ci = pl.program_id(0)
    heads = batch * N_GDN_HEADS
    hs = range(heads)
    G = _lane_groups

    @pl.when(ci == 0)
    def _():
        s_ref[...] = jnp.zeros_like(s_ref)

    first = ci == 0

    def conv_silu(x_ref, prev_ref, cw_ref):
        x = _batch_lanes(x_ref)
        prev8 = jnp.where(first, 0.0, _batch_lanes(prev_ref))
        cw = _tile_lanes(cw_ref[...], batch)
        row = _iota(x.shape, 0)
        acc = x * cw[3:4, :]
        for s in (1, 2, 3):
            tail = jnp.tile(pltpu.roll(prev8, s, axis=0), (c // V7X_SUBLANES, 1))
            acc = acc + jnp.where(row < s, tail, pltpu.roll(x, s, axis=0)) * cw[3 - s:4 - s, :]
        return acc * _sigmoid(acc)

    def head_sums(x):
        return jnp.concatenate([jnp.broadcast_to(jnp.sum(g, axis=-1, keepdims=True), g.shape) for g in G(x)], axis=1)

    q = conv_silu(q_ref, pq_ref, cwq_ref)
    k = conv_silu(k_ref, pk_ref, cwk_ref)
    v = conv_silu(v_ref, pv_ref, cwv_ref)
    q = q * lax.rsqrt(head_sums(q * q) + L2_EPS) * (GDN_HEAD ** -0.5)
    k = k * lax.rsqrt(head_sums(k * k) + L2_EPS)

    lane2 = _iota((2 * w, w), 0) % w
    lane2r = _iota((c, 2 * w), 1) % w
    bcol, gcol, grow = [], [], []
    for b in range(batch):
        bm = misc_ref[b, :, 0:128]
        beta_hl = jnp.concatenate(_hi_lo(_sigmoid(bm)), axis=1)
        g_full = -jnp.exp(alog_ref[...]) * _softplus(bm + dtb_ref[...])
        gc_hl = jnp.concatenate(_hi_lo(_mm_xl(_tril_ones_x2(c), g_full)), axis=1)
        for h in range(N_GDN_HEADS):
            bcol.append(_dot(beta_hl, jnp.where(lane2 == BETA_LANE + h, 1.0, 0.0).astype(BF16)))
            gcol.append(_dot(gc_hl, jnp.where(lane2 == ALPHA_LANE + h, 1.0, 0.0).astype(BF16)))
            grow.append(_dot_nt(jnp.where(lane2r == ALPHA_LANE + h, 1.0, 0.0).astype(BF16), gc_hl))

    ti = _iota((c, c), 0)
    tj = _iota((c, c), 1)
    causal = ti >= tj
    strict = ti > tj
    decay = [jnp.where(causal, jnp.exp(jnp.where(causal, gcol[h][:, 0:c] - grow[h], 0.0)), 0.0) for h in hs]

    q_g, k_g, v_g = G(q), G(k), G(v)
    kb = [k_g[h] * bcol[h] for h in hs]
    kk = [_mm_nt(jnp.concatenate([kb[h], q_g[h]], axis=0), k_g[h]) for h in hs]
    lneg = [jnp.where(strict, -kk[h][0:c] * decay[h], 0.0) for h in hs]
    a_intra = [kk[h][c:2 * c] * decay[h] for h in hs]

    egc = [jnp.exp(gcol[h]) for h in hs]
    g_last = [gcol[h][c - 1:c, :] for h in hs]
    lane_t = _iota((c, w), 1) >= c
    eye_t = jnp.where(_iota((c, w), 1) == _iota((c, w), 0) + c, 1.0, 0.0)
    zt = [jnp.concatenate([lneg[h], jnp.zeros((c, c), F32)], axis=1) + eye_t for h in hs]
    for _ in range(int(math.log2(c))):
        zb = [_bf(zt[h]) for h in hs]
        zt = [_dot(zb[h][:, 0:c], zb[h]) + jnp.where(lane_t, zt[h], 0.0) for h in hs]
    zeros_x = jnp.zeros((c, 2 * w), F32)
    x = [_mm(zt[h], jnp.concatenate([zeros_x, jnp.concatenate([v_g[h] * bcol[h], kb[h] * egc[h]], axis=1)], axis=0))
         for h in hs]
    uu = [x[h][:, 0:w] for h in hs]
    ww = [x[h][:, w:2 * w] for h in hs]

    kd = [k_g[h] * jnp.exp(g_last[h] - gcol[h]) for h in hs]
    eye = jnp.where(_iota((w, w), 0) == _iota((w, w), 1), 1.0, 0.0)
    m_mat = [eye * jnp.exp(g_last[h]) - _mm_tn(kd[h], ww[h]) for h in hs]
    n_mat = [_mm_tn(kd[h], uu[h]) for h in hs]
    p_mat = [q_g[h] * egc[h] - _mm(a_intra[h], ww[h]) for h in hs]
    q_mat = [_mm(a_intra[h], uu[h]) for h in hs]

    s0 = [_bf(s_ref[h]) for h in hs]
    o = jnp.concatenate([_dot(_bf(p_mat[h]), s0[h]) + q_mat[h] for h in hs], axis=1)
    for h in hs:
        s_ref[h] = _dot(_bf(m_mat[h]), s0[h]) + n_mat[h]

    o = o * lax.rsqrt(head_sums(o * o) * (1.0 / GDN_HEAD) + NORM_EPS) * _tile_lanes(nw_ref[...], heads)
    z = _batch_lanes(z_ref)
    out = (o * (z * _sigmoid(z))).astype(o_ref.dtype)
    for b in range(batch):
        o_ref[b] = out[:, b * lw:(b + 1) * lw]


def _gdn_mix(p, conv_w, alog_l, dtb_l, norm_w):
    batch, seq, _ = p.shape
    c = CHUNK
    cur, prev = _chunk_specs(batch, c)

    def cw(part):
        return pl.BlockSpec((4, RW), lambda t: (0, part))

    lane_vec = pl.BlockSpec((1, V7X_LANES), lambda t: (0, 0))
    return pl.pallas_call(
        _gdn_kernel,
        grid=(seq // c,),
        in_specs=[
            cur(OFF_GQ, RW), cur(OFF_GK, RW), cur(OFF_GV, RW), prev(OFF_GQ, RW), prev(OFF_GK, RW), prev(OFF_GV, RW),
            cur(OFF_Z, RW), cur(OFF_MISC, MISC_W),
            cw(0), cw(1), cw(2), lane_vec, lane_vec, lane_vec,
        ],
        out_specs=pl.BlockSpec((batch, c, RW), lambda t: (0, t, 0)),
        out_shape=jax.ShapeDtypeStruct((batch, seq, RW), BF16),
        scratch_shapes=[pltpu.VMEM((batch * N_GDN_HEADS, V7X_LANES, V7X_LANES), F32)],
        compiler_params=_cparams(("arbitrary",)),
        name="gdn_mix",
    )(p, p, p, p, p, p, p, p, conv_w, conv_w, conv_w, alog_l, dtb_l, norm_w)


def _xattn_kernel(x_ref, ymix_ref, gmix_ref, k_ref, v_ref, wq_ref, wo_ref, gpre_ref, gpost_ref, gnext_ref,
                  h_out_ref, f_out_ref):
    ym = ymix_ref[...].astype(F32)
    h = x_ref[...] + ym * lax.rsqrt(jnp.mean(ym * ym, axis=-1, keepdims=True) + NORM_EPS) * gmix_ref[...]
    cn =(h * lax.rsqrt(jnp.mean(h * h, axis=-1, keepdims=True) + NORM_EPS) * gpre_ref[...]).astype(BF16)
    q = _dot(cn, wq_ref[...]).astype(BF16)
    kk = k_ref[0]
    vv = v_ref[0]
    outs = []
    for hh in range(XA_HEADS):
        sl = slice(hh * XA_HEAD, (hh + 1) * XA_HEAD)
        s = _dot_nt(q[:, sl], kk[:, sl]) * (XA_HEAD ** -0.5)
        s = s - jnp.max(s, axis=-1, keepdims=True)
        e = jnp.exp(s)
        pr = e / jnp.sum(e, axis=-1, keepdims=True)
        outs.append(_dot(pr.astype(BF16), vv[:, sl]))
    o = jnp.concatenate(outs, axis=1).astype(BF16)
    y = _dot(o, wo_ref[...])
    h2 = h + y * lax.rsqrt(jnp.mean(y * y, axis=-1, keepdims=True) + NORM_EPS) * gpost_ref[...]
    h_out_ref[...] = h2
    f_out_ref[...] = (h2 * lax.rsqrt(jnp.mean(h2 * h2, axis=-1, keepdims=True) + NORM_EPS)
                      * gnext_ref[...]).astype(f_out_ref.dtype)


def _xattn_block(x, y_mix, g_mix, kmem, vmem, w_q, w_o, g_pre, g_post, g_next, batch, seq, tt=256):
    n, d = x.shape
    mlen, xw = kmem.shape[1], kmem.shape[2]
    per_b = seq // tt
    row = pl.BlockSpec((tt, d), lambda i: (i, 0))
    vec = pl.BlockSpec((1, d), lambda i: (0, 0))
    mem = pl.BlockSpec((1, mlen, xw), lambda i: (i // per_b, 0, 0))
    once = pl.Buffered(1)
    return pl.pallas_call(
        _xattn_kernel,
        grid=(n // tt,),
        in_specs=[row, row, vec, mem, mem, pl.BlockSpec((d, xw), lambda i: (0, 0), pipeline_mode=once),
                  pl.BlockSpec((xw, d), lambda i: (0, 0), pipeline_mode=once), vec, vec, vec],
        out_specs=[row, row],
        out_shape=[jax.ShapeDtypeStruct((n, d), F32), jax.ShapeDtypeStruct((n, d), BF16)],
        compiler_params=_cparams(("parallel",)),
        name="xattn_block",
    )(x, y_mix, g_mix.reshape(1, d), kmem, vmem, w_q, w_o, g_pre.reshape(1, d), g_post.reshape(1, d),
      g_next.reshape(1, d))


SRC_LORA = 3 * RW
SRC_GDN = SRC_LORA + 2 * LORA + 256
SRC_BA = SRC_GDN + 4 * RW
SRC_GATES = SRC_BA + 2 * N_GDN_HEADS
RELAYOUT_ROWS = 512


def _relayout_kernel(x_ref, misc_ref, o_ref):
    @pl.when(pl.program_id(0) != OFF_MISC // RELAYOUT_ROWS)
    def _():
        o_ref[...] = x_ref[...].astype(o_ref.dtype)

    @pl.when(pl.program_id(0) == OFF_MISC // RELAYOUT_ROWS)
    def _():
        o_ref[...] = misc_ref[...].astype(o_ref.dtype)


def _relayout_w_in_t(w_in_t):
    n_in, d = w_in_t.shape
    rb = RELAYOUT_ROWS
    assert OFF_GQ % rb == 0 and OFF_MISC % rb == 0 and OFF_GATE_RW % rb == 0 and MISC_W == rb
    zeros = jnp.zeros((128 - LORA, d), w_in_t.dtype)
    misc = jnp.concatenate([w_in_t[SRC_LORA:SRC_LORA + LORA], w_in_t[SRC_BA:SRC_BA + 2 * N_GDN_HEADS],
                            w_in_t[SRC_LORA + LORA:SRC_LORA + 2 * LORA], zeros,
                            w_in_t[SRC_LORA + 2 * LORA:SRC_GDN]], axis=0)

    def src_row(j):
        r = j * rb
        src = jnp.where(r < OFF_GQ, r, jnp.where(r < OFF_MISC, r + (SRC_GDN - OFF_GQ),
                                                 jnp.where(r < OFF_GATE_RW, 0, r + (SRC_GATES - OFF_GATE_RW))))
        return pl.multiple_of(src, math.gcd(rb, SRC_GDN, SRC_GATES))

    return pl.pallas_call(
        _relayout_kernel,
        grid=(NP // rb,),
        in_specs=[pl.BlockSpec((pl.Element(rb), pl.Element(d)), lambda j: (src_row(j), 0)),
                  pl.BlockSpec((rb, d), lambda j: (0, 0))],
        out_specs=pl.BlockSpec((rb, d), lambda j: (j, 0)),
        out_shape=jax.ShapeDtypeStruct((NP, d), BF16),
        compiler_params=_cparams(("parallel",)),
        name="relayout_w_in",
    )(w_in_t, misc)


def _layer(x, mem, mix_norm_pre, mix_norm_post, w_in, rwkv_shift_mix, rwkv_w0, rwkv_w_up, rwkv_a0, rwkv_a_up,
           rwkv_g_up, rwkv_k_k, rwkv_k_a, rwkv_r_k, rwkv_gn_w, rwkv_gn_b, gdn_conv_w, gdn_a_log, gdn_dt_bias,
           gdn_norm_w, w_branch_rwkv, w_branch_gdn, w_mix_out, xa_norm_pre, xa_norm_mem, xa_norm_post, xa_w_q,
           xa_w_kv, xa_w_o, mlp_norm_pre, mlp_norm_post, mlp_w_up, mlp_w_down):
    batch, seq, d = x.shape
    n = batch * seq
    h = x.reshape(n, d)

    w_in_b = _relayout_w_in_t(jnp.swapaxes(w_in, 0, 1))
    mix = rwkv_shift_mix
    o = 3 * RW
    cp = jnp.stack([mix[0:RW], mix[RW:2 * RW], mix[2 * RW:o], rwkv_w0, rwkv_a0, rwkv_k_k, rwkv_k_a,
                    rwkv_r_k.reshape(RW)], axis=0)
    gp = jnp.concatenate([rwkv_gn_w[None], rwkv_gn_b[None], jnp.zeros((6, RW), F32)], axis=0)
    zeros = lambda k: jnp.zeros((k,), F32)
    mixm = jnp.concatenate([mix[o:o + LORA], zeros(32), mix[o + LORA:o + 2 * LORA], zeros(32),
                            mix[o + 2 * LORA:o + 2 * LORA + 256]])[None]
    w_up_p = jnp.concatenate([rwkv_w_up, jnp.zeros((128 - LORA, RW), F32)], axis=0).astype(BF16)
    a_up_p = jnp.concatenate([rwkv_a_up, jnp.zeros((128 - LORA, RW), F32)], axis=0).astype(BF16)
    g_up_b = rwkv_g_up.astype(BF16)
    alog_l = jnp.concatenate([zeros(ALPHA_LANE), gdn_a_log])[None]
    dtb_l = jnp.concatenate([zeros(ALPHA_LANE), gdn_dt_bias])[None]

    u = _rmsnorm(h, mix_norm_pre, BF16)
    p = _matmul(u, w_in_b, F32, tm=1024, tn=1280, b_is_nk=True, name="in_proj")
    p3 = p.reshape(batch, seq, NP)
    y_rw = _rwkv_mix(p3, cp, gp, mixm, w_up_p, a_up_p, g_up_b).reshape(n, RW)
    y_gdn = _gdn_mix(p3, gdn_conv_w, alog_l, dtb_l, gdn_norm_w[None]).reshape(n, RW)
    merged = _branch_merge(y_rw, y_gdn, w_branch_rwkv, w_branch_gdn, p)
    y_mix = _matmul_wcast(merged, w_mix_out, BF16, tm=2048, tn=512, name="mix_out")

    mlen = mem.shape[1]
    m = _rmsnorm(mem.reshape(batch * mlen, d), xa_norm_mem, BF16)
    kv = _matmul(m, xa_w_kv.astype(BF16), BF16, tm=batch * mlen, tn=512, name="xa_kv")
    xw = XA_HEADS * XA_HEAD
    kmem = kv[:, :xw].reshape(batch, mlen, xw)
    vmem = kv[:, xw:].reshape(batch, mlen, xw)
    h, f = _xattn_block(h, y_mix, mix_norm_post, kmem, vmem, xa_w_q.astype(BF16), xa_w_o.astype(BF16),
                        xa_norm_pre, xa_norm_post, mlp_norm_pre, batch, seq)

    hmid = _matmul_wcast(f, mlp_w_up, BF16, tm=2048, tn=512, relu2=True, name="mlp_up")
    y_mlp = _matmul_kacc(hmid, mlp_w_down, BF16, tm=2048, tn=1024, tk=2048, k_chunk=1024, name="mlp_down")
    h = _resid_norm(h, y_mlp, mlp_norm_post)
    return h.reshape(batch, seq, d)


def kernel(x, mem, mix_norm_pre, mix_norm_post, w_in, rwkv_shift_mix, rwkv_w0, rwkv_w_up, rwkv_a0, rwkv_a_up,
           rwkv_g_up, rwkv_k_k, rwkv_k_a, rwkv_r_k, rwkv_gn_w, rwkv_gn_b, gdn_conv_w, gdn_a_log, gdn_dt_bias,
           gdn_norm_w, w_branch_rwkv, w_branch_gdn, w_mix_out, xa_norm_pre, xa_norm_mem, xa_norm_post, xa_w_q,
           xa_w_kv, xa_w_o, mlp_norm_pre, mlp_norm_post, mlp_w_up, mlp_w_down):
    args = (mix_norm_pre, mix_norm_post, w_in, rwkv_shift_mix, rwkv_w0, rwkv_w_up, rwkv_a0, rwkv_a_up,
            rwkv_g_up, rwkv_k_k, rwkv_k_a, rwkv_r_k, rwkv_gn_w, rwkv_gn_b, gdn_conv_w, gdn_a_log, gdn_dt_bias,
            gdn_norm_w, w_branch_rwkv, w_branch_gdn, w_mix_out, xa_norm_pre, xa_norm_mem, xa_norm_post, xa_w_q,
            xa_w_kv, xa_w_o, mlp_norm_pre, mlp_norm_post, mlp_w_up, mlp_w_down)
    h = x
    for layer in range(mix_norm_pre.shape[0]):
        h = _layer(h, mem, *(a[layer] for a in args))
    return h
```

```python
import functools
import math

import jax
import jax.numpy as jnp
from jax import lax
from jax.experimental import pallas as pl
from jax.experimental.pallas import tpu as pltpu

F32 = jnp.float32
BF16 = jnp.bfloat16

V7X_LANES = 128
V7X_SUBLANES = 8
V7X_VMEM_LIMIT_BYTES = 60 * 1024 * 1024

NORM_EPS = 1e-6
L2_EPS = 1e-6
RWKV_GN_EPS = 64e-5
RWKV_HEAD = 64
GDN_HEAD = 128
CHUNK = 64
XA_HEADS = 4
XA_HEAD = 128

RW = 2048
OFF_R, OFF_K, OFF_V = 0, RW, 2 * RW
OFF_GQ, OFF_GK, OFF_GV = 3 * RW, 4 * RW, 5 * RW
OFF_Z = 6 * RW
OFF_MISC = 7 * RW
MISC_W = 512
OFF_GATE_RW = OFF_MISC + MISC_W
D_MODEL = 4096
OFF_GATE_GDN = OFF_GATE_RW + D_MODEL
NP = OFF_GATE_GDN + D_MODEL
LORA = 96
N_GDN_HEADS = RW // GDN_HEAD
BETA_LANE = LORA
ALPHA_LANE = LORA + N_GDN_HEADS


def _cparams(sem):
    return pltpu.CompilerParams(dimension_semantics=sem, vmem_limit_bytes=V7X_VMEM_LIMIT_BYTES)


def _dot(a, b):
    return jnp.dot(a, b, preferred_element_type=F32)


def _dot_nt(a, b):
    return lax.dot_general(a, b, (((1,), (1,)), ((), ())), preferred_element_type=F32)


def _dot_tn(a, b):
    return lax.dot_general(a, b, (((0,), (0,)), ((), ())), preferred_element_type=F32)


def _bf(x):
    return x.astype(BF16)


def _hi_lo(x):
    hi = x.astype(BF16)
    return hi, (x - hi.astype(F32)).astype(BF16)


def _mm(a, b):
    return _dot(_bf(a), _bf(b))


def _mm_nt(a, b):
    return _dot_nt(_bf(a), _bf(b))


def _mm_tn(a, b):
    return _dot_tn(_bf(a), _bf(b))


def _mm_xr(a, b01x2):
    hi, lo = _hi_lo(a)
    return _dot(jnp.concatenate([hi, lo], axis=1), b01x2)


def _mm_xl(a01x2, b):
    hi, lo = _hi_lo(b)
    return _dot(a01x2, jnp.concatenate([hi, lo], axis=0))


def _sigmoid(x):
    return 0.5 * jnp.tanh(0.5 * x) + 0.5


def _softplus(x):
    return jnp.maximum(x, 0.0) + jnp.log1p(jnp.exp(-jnp.abs(x)))


def _iota(shape, dim):
    return lax.broadcasted_iota(jnp.int32, shape, dim)


def _lane_groups(x):
    return [x[:, g * V7X_LANES:(g + 1) * V7X_LANES] for g in range(x.shape[1] // V7X_LANES)]


def _tril_ones_x2(c):
    return jnp.where(_iota((c, 2 * c), 0) >= _iota((c, 2 * c), 1) % c, 1.0, 0.0).astype(BF16)


def _rmsnorm_kernel(x_ref, g_ref, o_ref):
    x = x_ref[...]
    inv = lax.rsqrt(jnp.mean(x * x, axis=-1, keepdims=True) + NORM_EPS)
    o_ref[...] = (x * inv * g_ref[...]).astype(o_ref.dtype)


def _rmsnorm(x, gain, out_dtype, rows=256):
    n, d = x.shape
    return pl.pallas_call(
        _rmsnorm_kernel,
        grid=(n // rows,),
        in_specs=[pl.BlockSpec((rows, d), lambda i: (i, 0)), pl.BlockSpec((1, d), lambda i: (0, 0))],
        out_specs=pl.BlockSpec((rows, d), lambda i: (i, 0)),
        out_shape=jax.ShapeDtypeStruct((n, d), out_dtype),
        compiler_params=_cparams(("parallel",)),
        name="rmsnorm",
    )(x, gain.reshape(1, d))


def _resid_norm_kernel(h_ref, y_ref, g_ref, o_ref):
    y = y_ref[...].astype(F32)
    inv = lax.rsqrt(jnp.mean(y * y, axis=-1, keepdims=True) + NORM_EPS)
    o_ref[...] = h_ref[...] + y * inv * g_ref[...]


def _resid_norm(h, y, gain, rows=256):
    n, d = h.shape
    row = pl.BlockSpec((rows, d), lambda i: (i, 0))
    return pl.pallas_call(
        _resid_norm_kernel,
        grid=(n // rows,),
        in_specs=[row, row, pl.BlockSpec((1, d), lambda i: (0, 0))],
        out_specs=row,
        out_shape=jax.ShapeDtypeStruct((n, d), F32),
        compiler_params=_cparams(("parallel",)),
        name="resid_norm",
    )(h, y, gain.reshape(1, d))


def _mm_kernel(a_ref, b_ref, o_ref, *, relu2, b_is_nk):
    acc = _dot_nt(a_ref[...], b_ref[...]) if b_is_nk else _dot(a_ref[...], b_ref[...])
    if relu2:
        acc = jnp.square(jnp.maximum(acc, 0.0))
    o_ref[...] = acc.astype(o_ref.dtype)


def _matmul(a, b, out_dtype, tm, tn, relu2=False, b_is_nk=False, name="matmul"):
    m, k = a.shape
    n = b.shape[0] if b_is_nk else b.shape[1]
    b_spec = pl.BlockSpec((tn, k), lambda i, j: (j, 0)) if b_is_nk else pl.BlockSpec((k, tn), lambda i, j: (0, j))
    return pl.pallas_call(
        functools.partial(_mm_kernel, relu2=relu2, b_is_nk=b_is_nk),
        grid=(m // tm, n // tn),
        in_specs=[pl.BlockSpec((tm, k), lambda i, j: (i, 0)), b_spec],
        out_specs=pl.BlockSpec((tm, tn), lambda i, j: (i, j)),
        out_shape=jax.ShapeDtypeStruct((m, n), out_dtype),
        compiler_params=_cparams(("parallel", "arbitrary")),
        name=name,
    )(a, b)


def _mm_wcast_kernel(a_ref, b_ref, o_ref, b_bf16, *, relu2):
    @pl.when(pl.program_id(1) == 0)
    def _():
        b_bf16[...] = b_ref[...].astype(BF16)

    acc = _dot(a_ref[...], b_bf16[...])
    if relu2:
        acc = jnp.square(jnp.maximum(acc, 0.0))
    o_ref[...] = acc.astype(o_ref.dtype)


def _matmul_wcast(a, b, out_dtype, tm, tn, relu2=False, name="matmul_wcast"):
    m, k = a.shape
    _, n = b.shape
    return pl.pallas_call(
        functools.partial(_mm_wcast_kernel, relu2=relu2),
        grid=(n // tn, m // tm),
        in_specs=[pl.BlockSpec((tm, k), lambda j, i: (i, 0)), pl.BlockSpec((k, tn), lambda j, i: (0, j))],
        out_specs=pl.BlockSpec((tm, tn), lambda j, i: (i, j)),
        out_shape=jax.ShapeDtypeStruct((m, n), out_dtype),
        scratch_shapes=[pltpu.VMEM((k, tn), BF16)],
        compiler_params=_cparams(("parallel", "arbitrary")),
        name=name,
    )(a, b)


def _mm_kacc_kernel(a_ref, b_ref, o_ref, acc_ref, *, k_chunk):
    l = pl.program_id(2)

    @pl.when(l == 0)
    def _():
        acc_ref[...] = jnp.zeros_like(acc_ref)

    acc = None
    for k0 in range(0, a_ref.shape[1], k_chunk):
        part = _dot(a_ref[:, k0:k0 + k_chunk], b_ref[k0:k0 + k_chunk, :].astype(BF16))
        acc = part if acc is None else acc + part
    acc_ref[...] += acc

    @pl.when(l == pl.num_programs(2) - 1)
    def _():
        o_ref[...] = acc_ref[...].astype(o_ref.dtype)


def _matmul_kacc(a, b, out_dtype, tm, tn, tk, k_chunk=512, name="matmul_kacc"):
    m, k = a.shape
    _, n = b.shape
    return pl.pallas_call(
        functools.partial(_mm_kacc_kernel, k_chunk=k_chunk),
        grid=(m // tm, n // tn, k // tk),
        in_specs=[pl.BlockSpec((tm, tk), lambda i, j, l: (i, l)), pl.BlockSpec((tk, tn), lambda i, j, l: (l, j))],
        out_specs=pl.BlockSpec((tm, tn), lambda i, j, l: (i, j)),
        out_shape=jax.ShapeDtypeStruct((m, n), out_dtype),
        scratch_shapes=[pltpu.VMEM((tm, tn), F32)],
        compiler_params=_cparams(("parallel", "parallel", "arbitrary")),
        name=name,
    )(a, b)


def _branch_kernel(yr_ref, yg_ref, wr_ref, wg_ref, gr_ref, gg_ref, o_ref, wr_bf16, wg_bf16):
    @pl.when(pl.program_id(1) == 0)
    def _():
        wr_bf16[...] = wr_ref[...].astype(BF16)
        wg_bf16[...] = wg_ref[...].astype(BF16)

    br = _dot(yr_ref[...], wr_bf16[...])
    bg = _dot(yg_ref[...], wg_bf16[...])
    o_ref[...] = (_sigmoid(gr_ref[...]) * br + _sigmoid(gg_ref[...]) * bg).astype(o_ref.dtype)


def _branch_merge(y_rw, y_gdn, w_rw, w_gdn, p, tm=1024, tn=512):
    m, k = y_rw.shape
    n = w_rw.shape[1]
    jr, jg = OFF_GATE_RW // tn, OFF_GATE_GDN // tn
    return pl.pallas_call(
        _branch_kernel,
        grid=(n // tn, m // tm),
        in_specs=[
            pl.BlockSpec((tm, k), lambda j, i: (i, 0)),
            pl.BlockSpec((tm, k), lambda j, i: (i, 0)),
            pl.BlockSpec((k, tn), lambda j, i: (0, j)),
            pl.BlockSpec((k, tn), lambda j, i: (0, j)),
            pl.BlockSpec((tm, tn), lambda j, i: (i, jr + j)),
            pl.BlockSpec((tm, tn), lambda j, i: (i, jg + j)),
        ],
        out_specs=pl.BlockSpec((tm, tn), lambda j, i: (i, j)),
        out_shape=jax.ShapeDtypeStruct((m, n), BF16),
        scratch_shapes=[pltpu.VMEM((k, tn), BF16)] * 2,
        compiler_params=_cparams(("parallel", "arbitrary")),
        name="branch_merge",
    )(y_rw, y_gdn, w_rw, w_gdn, p, p)


def _batch_lanes(ref):
    return jnp.concatenate([ref[b] for b in range(ref.shape[0])], axis=1)


def _tile_lanes(x, n):
    return jnp.concatenate([x] * n, axis=1)


def _token_shift(x_ref, prev_ref, first, mix):
    x = _batch_lanes(x_ref)
    before = jnp.where(first, 0.0, _batch_lanes(prev_ref)[V7X_SUBLANES - 1:V7X_SUBLANES, :])
    prev = jnp.where(_iota(x.shape, 0) == 0, before, pltpu.roll(x, 1, axis=0))
    return x + (prev - x) * mix


def _expand2(x, head0):
    return jnp.concatenate([jnp.where(head0, x, 0.0), jnp.where(head0, 0.0, x)], axis=0)


def _rwkv_kernel(r_ref, k_ref, v_ref, pr_ref, pk_ref, pv_ref, misc_ref, pmisc_ref,
                 cp_ref, gp_ref, mixm_ref, wup_ref, aup_ref, gup_ref, o_ref, s_ref):
    c = CHUNK
    w = V7X_LANES
    batch, _, lw = r_ref.shape
    ci = pl.program_id(0)

    @pl.when(ci == 0)
    def _():
        s_ref[...] = jnp.zeros_like(s_ref)

    first = ci == 0
    mix_r, mix_k, mix_v, w0, a0, k_k, k_a, r_k = (_tile_lanes(cp_ref[i:i + 1, :], batch) for i in range(8))
    gn_w, gn_b = _tile_lanes(gp_ref[0:1, :], batch), _tile_lanes(gp_ref[1:2, :], batch)

    misc_m = _token_shift(misc_ref, pmisc_ref, first, _tile_lanes(mixm_ref[...], batch))
    xr = _token_shift(r_ref, pr_ref, first, mix_r)
    xk = _token_shift(k_ref, pk_ref, first, mix_k)
    xv = _token_shift(v_ref, pv_ref, first, mix_v)

    def lora(fn, lo, hi, w_ref):
        return jnp.concatenate([_dot(_bf(fn(misc_m[:, b * MISC_W + lo:b * MISC_W + hi])), w_ref[...])
                                for b in range(batch)], axis=1)

    wlin = w0 + lora(jnp.tanh, 0, 128, wup_ref)
    logw = -math.exp(-0.5) * _sigmoid(wlin)
    a = _sigmoid(a0 + lora(lambda t: t, 128, 256, aup_ref))
    gate = lora(_sigmoid, 256, 512, gup_ref)

    same_head = (_iota((w, w), 0) // RWKV_HEAD) == (_iota((w, w), 1) // RWKV_HEAD)
    bones = jnp.where(same_head, 1.0, 0.0).astype(BF16)
    eye = jnp.where(_iota((w, w), 0) == _iota((w, w), 1), 1.0, 0.0)

    def head_sums(x):
        n = x.shape[1] // w
        s = _dot(_bf(jnp.concatenate(_lane_groups(x), axis=0)), bones)
        return jnp.concatenate([s[g * c:(g + 1) * c] for g in range(n)], axis=1)

    kkr = xk * k_k
    kkn = kkr * lax.rsqrt(head_sums(kkr * kkr) + L2_EPS)
    k2 = xk * (1.0 + (a - 1.0) * k_a)
    am = -kkn
    bm = kkn * a

    cum = _mm_xl(_tril_ones_x2(c), logw)
    cum_ex = cum - logw
    c_last = cum[c - 1:c, :]
    c_mid = cum[c // 2 - 1:c // 2, :]
    e_out = jnp.exp(c_mid - cum)
    e_last = jnp.exp(c_last - cum)
    rt = xr * jnp.exp(cum - c_mid)
    at = am * jnp.exp(cum_ex - c_mid)
    bt = bm * e_out
    kt = k2 * e_out
    a_abs = am * jnp.exp(cum_ex)
    r_abs = xr * jnp.exp(cum)
    bh = bm * e_last
    kh = k2 * e_last
    w_c = jnp.exp(c_last)

    head0 = _iota((c, w), 1) < RWKV_HEAD
    head0_2 = _iota((c, 2 * w), 1) % w < RWKV_HEAD
    wi = _iota((c, 2 * c), 0)
    wj = _iota((c, 2 * c), 1) % c
    strict = wi > wj
    incl = wi >= wj

    gs = range(batch * lw // w)
    G = _lane_groups
    at_g, rt_g, bt_g, kt_g, xv_g = G(at), G(rt), G(bt), G(kt), G(xv)
    aa = [_mm_nt(jnp.concatenate([at_g[g], rt_g[g]], axis=0),
                 jnp.concatenate([_expand2(bt_g[g], head0), _expand2(kt_g[g], head0)], axis=0)) for g in gs]
    a_ab = [jnp.where(strict, aa[g][0:c, 0:2 * c], 0.0) for g in gs]
    a_ak = [jnp.where(strict, aa[g][0:c, 2 * c:4 * c], 0.0) for g in gs]
    a_rb = [jnp.where(incl, aa[g][c:2 * c, 0:2 * c], 0.0) for g in gs]
    a_rk = [jnp.where(incl, aa[g][c:2 * c, 2 * c:4 * c], 0.0) for g in gs]

    xv_e = [_expand2(xv_g[g], head0) for g in gs]
    akv = [_mm(a_ak[g], xv_e[g]) for g in gs]
    eye_w = jnp.where(wi == wj, 1.0, 0.0)
    tw = [eye_w + a_ab[g] for g in gs]
    pw = [_mm(a_ab[g], _expand2(a_ab[g], head0)) for g in gs]
    levels = int(math.log2(c))
    for lvl in range(1, levels):
        if lvl + 1 < levels:
            pt = [_mm(pw[g], _expand2(jnp.concatenate([pw[g], tw[g]], axis=1), head0_2)) for g in gs]
            pw = [pt[g][:, 0:w] for g in gs]
            tw = [tw[g] + pt[g][:, w:2 * w] for g in gs]
        else:
            tw = [tw[g] + _mm(pw[g], _expand2(tw[g], head0)) for g in gs]
    a_abs_g = G(a_abs)
    x = [_mm(tw[g], _expand2(jnp.concatenate([a_abs_g[g], akv[g]], axis=1), head0_2)) for g in gs]
    wm = [x[g][:, 0:w] for g in gs]
    u0 = [x[g][:, w:2 * w] for g in gs]

    bh_g, kh_g, wc_g, rabs_g = G(bh), G(kh), G(w_c), G(r_abs)
    m_mat = [jnp.where(same_head, _mm_tn(bh_g[g], wm[g]), 0.0) + eye * wc_g[g] for g in gs]
    n_mat = [jnp.where(same_head, _mm_tn(jnp.concatenate([bh_g[g], kh_g[g]], axis=0),
                                         jnp.concatenate([u0[g], xv_g[g]], axis=0)), 0.0) for g in gs]
    p_mat = [rabs_g[g] + _mm(a_rb[g], _expand2(wm[g], head0)) for g in gs]
    q_mat = [_mm(jnp.concatenate([a_rb[g], a_rk[g]], axis=1),
                 jnp.concatenate([_expand2(u0[g], head0), xv_e[g]], axis=0)) for g in gs]

    h0 = [_bf(s_ref[g]) for g in gs]
    y = jnp.concatenate([_dot(_bf(p_mat[g]), h0[g]) + q_mat[g] for g in gs], axis=1)
    for g in gs:
        s_ref[g] = _dot(_bf(m_mat[g]), h0[g]) + n_mat[g]

    inv_n = 1.0 / RWKV_HEAD
    d = y - head_sums(y) * inv_n
    var = head_sums(d * d) * inv_n
    yn = d * lax.rsqrt(var + RWKV_GN_EPS) * gn_w + gn_b
    bonus = head_sums(xr * k2 * r_k) * xv
    out = ((yn + bonus) * gate).astype(o_ref.dtype)
    for b in range(batch):
        o_ref[b] = out[:, b * lw:(b + 1) * lw]


def _chunk_specs(batch, c):
    def cur(off, width):
        assert off % width == 0
        return pl.BlockSpec((batch, c, width), lambda t: (0, t, off // width))

    def prev(off, width):
        return pl.BlockSpec((batch, V7X_SUBLANES, width),
                            lambda t: (0, jnp.maximum(t * (c // V7X_SUBLANES) - 1, 0), off // width))

    return cur, prev


def _rwkv_mix(p, cp, gp, mixm, w_up_p, a_up_p, g_up):
    batch, seq, _ = p.shape
    c = CHUNK
    cur, prev = _chunk_specs(batch, c)
    whole = lambda x: pl.BlockSpec(x.shape, lambda t: (0, 0))
    return pl.pallas_call(
        _rwkv_kernel,
        grid=(seq // c,),
        in_specs=[
            cur(OFF_R, RW), cur(OFF_K, RW), cur(OFF_V, RW), prev(OFF_R, RW), prev(OFF_K, RW), prev(OFF_V, RW),
            cur(OFF_MISC, MISC_W), prev(OFF_MISC, MISC_W),
            whole(cp), whole(gp), whole(mixm), whole(w_up_p), whole(a_up_p), whole(g_up),
        ],
        out_specs=pl.BlockSpec((batch, c, RW), lambda t: (0, t, 0)),
        out_shape=jax.ShapeDtypeStruct((batch, seq, RW), BF16),
        scratch_shapes=[pltpu.VMEM((batch * RW // V7X_LANES, V7X_LANES, V7X_LANES), F32)],
        compiler_params=_cparams(("arbitrary",)),
        name="rwkv7_mix",
    )(p, p, p, p, p, p, p, p, cp, gp, mixm, w_up_p, a_up_p, g_up)


def _gdn_kernel(q_ref, k_ref, v_ref, pq_ref, pk_ref, pv_ref, z_ref, misc_ref,
                cwq_ref, cwk_ref, cwv_ref, alog_ref, dtb_ref, nw_ref, o_ref, s_ref):
    c = CHUNK
    w = V7X_LANES
    batch, _, lw = q_ref.shape
    assert lw == N_GDN_HEADS * w
    ci = pl.program_id(0)
    heads = batch * N_GDN_HEADS
    hs = range(heads)
    G = _lane_groups

    @pl.when(ci == 0)
    def _():
        s_ref[...] = jnp.zeros_like(s_ref)

    first = ci == 0

    def conv_silu(x_ref, prev_ref, cw_ref):
        x = _batch_lanes(x_ref)
        prev8 = jnp.where(first, 0.0, _batch_lanes(prev_ref))
        cw = _tile_lanes(cw_ref[...], batch)
        row = _iota(x.shape, 0)
        acc = x * cw[3:4, :]
        for s in (1, 2, 3):
            tail = jnp.tile(pltpu.roll(prev8, s, axis=0), (c // V7X_SUBLANES, 1))
            acc = acc + jnp.where(row < s, tail, pltpu.roll(x, s, axis=0)) * cw[3 - s:4 - s, :]
        return acc * _sigmoid(acc)

    def head_sums(x):
        return jnp.concatenate([jnp.broadcast_to(jnp.sum(g, axis=-1, keepdims=True), g.shape) for g in G(x)], axis=1)

    q = conv_silu(q_ref, pq_ref, cwq_ref)
    k = conv_silu(k_ref, pk_ref, cwk_ref)
    v = conv_silu(v_ref, pv_ref, cwv_ref)
    q = q * lax.rsqrt(head_sums(q * q) + L2_EPS) * (GDN_HEAD ** -0.5)
    k = k * lax.rsqrt(head_sums(k * k) + L2_EPS)

    lane2 = _iota((2 * w, w), 0) % w
    lane2r = _iota((c, 2 * w), 1) % w
    bcol, gcol, grow = [], [], []
    for b in range(batch):
        bm = misc_ref[b, :, 0:128]
        beta_hl = jnp.concatenate(_hi_lo(_sigmoid(bm)), axis=1)
        g_full = -jnp.exp(alog_ref[...]) * _softplus(bm + dtb_ref[...])
        gc_hl = jnp.concatenate(_hi_lo(_mm_xl(_tril_ones_x2(c), g_full)), axis=1)
        for h in range(N_GDN_HEADS):
            bcol.append(_dot(beta_hl, jnp.where(lane2 == BETA_LANE + h, 1.0, 0.0).astype(BF16)))
            gcol.append(_dot(gc_hl, jnp.where(lane2 == ALPHA_LANE + h, 1.0, 0.0).astype(BF16)))
            grow.append(_dot_nt(jnp.where(lane2r == ALPHA_LANE + h, 1.0, 0.0).astype(BF16), gc_hl))

    ti = _iota((c, c), 0)
    tj = _iota((c, c), 1)
    causal = ti >= tj
    strict = ti > tj
    decay = [jnp.where(causal, jnp.exp(jnp.where(causal, gcol[h][:, 0:c] - grow[h], 0.0)), 0.0) for h in hs]

    q_g, k_g, v_g = G(q), G(k), G(v)
    kb = [k_g[h] * bcol[h] for h in hs]
    kk = [_mm_nt(jnp.concatenate([kb[h], q_g[h]], axis=0), k_g[h]) for h in hs]
    lneg = [jnp.where(strict, -kk[h][0:c] * decay[h], 0.0) for h in hs]
    a_intra = [kk[h][c:2 * c] * decay[h] for h in hs]

    egc = [jnp.exp(gcol[h]) for h in hs]
    g_last = [gcol[h][c - 1:c, :] for h in hs]
    lane_t = _iota((c, w), 1) >= c
    eye_t = jnp.where(_iota((c, w), 1) == _iota((c, w), 0) + c, 1.0, 0.0)
    zt = [jnp.concatenate([lneg[h], jnp.zeros((c, c), F32)], axis=1) + eye_t for h in hs]
    for _ in range(int(math.log2(c))):
        zb = [_bf(zt[h]) for h in hs]
        zt = [_dot(zb[h][:, 0:c], zb[h]) + jnp.where(lane_t, zt[h], 0.0) for h in hs]
    zeros_x = jnp.zeros((c, 2 * w), F32)
    x = [_mm(zt[h], jnp.concatenate([zeros_x, jnp.concatenate([v_g[h] * bcol[h], kb[h] * egc[h]], axis=1)], axis=0))
         for h in hs]
    uu = [x[h][:, 0:w] for h in hs]
    ww = [x[h][:, w:2 * w] for h in hs]

    kd = [k_g[h] * jnp.exp(g_last[h] - gcol[h]) for h in hs]
    eye = jnp.where(_iota((w, w), 0) == _iota((w, w), 1), 1.0, 0.0)
    m_mat = [eye * jnp.exp(g_last[h]) - _mm_tn(kd[h], ww[h]) for h in hs]
    n_mat = [_mm_tn(kd[h], uu[h]) for h in hs]
    p_mat = [q_g[h] * egc[h] - _mm(a_intra[h], ww[h]) for h in hs]
    q_mat = [_mm(a_intra[h], uu[h]) for h in hs]

    s0 = [_bf(s_ref[h]) for h in hs]
    o = jnp.concatenate([_dot(_bf(p_mat[h]), s0[h]) + q_mat[h] for h in hs], axis=1)
    for h in hs:
        s_ref[h] = _dot(_bf(m_mat[h]), s0[h]) + n_mat[h]

    o = o * lax.rsqrt(head_sums(o * o) * (1.0 / GDN_HEAD) + NORM_EPS) * _tile_lanes(nw_ref[...], heads)
    z = _batch_lanes(z_ref)
    out = (o * (z * _sigmoid(z))).astype(o_ref.dtype)
    for b in range(batch):
        o_ref[b] = out[:, b * lw:(b + 1) * lw]


def _gdn_mix(p, conv_w, alog_l, dtb_l, norm_w):
    batch, seq, _ = p.shape
    c = CHUNK
    cur, prev = _chunk_specs(batch, c)

    def cw(part):
        return pl.BlockSpec((4, RW), lambda t: (0, part))

    lane_vec = pl.BlockSpec((1, V7X_LANES), lambda t: (0, 0))
    return pl.pallas_call(
        _gdn_kernel,
        grid=(seq // c,),
        in_specs=[
            cur(OFF_GQ, RW), cur(OFF_GK, RW), cur(OFF_GV, RW), prev(OFF_GQ, RW), prev(OFF_GK, RW), prev(OFF_GV, RW),
            cur(OFF_Z, RW), cur(OFF_MISC, MISC_W),
            cw(0), cw(1), cw(2), lane_vec, lane_vec, lane_vec,
        ],
        out_specs=pl.BlockSpec((batch, c, RW), lambda t: (0, t, 0)),
        out_shape=jax.ShapeDtypeStruct((batch, seq, RW), BF16),
        scratch_shapes=[pltpu.VMEM((batch * N_GDN_HEADS, V7X_LANES, V7X_LANES), F32)],
        compiler_params=_cparams(("arbitrary",)),
        name="gdn_mix",
    )(p, p, p, p, p, p, p, p, conv_w, conv_w, conv_w, alog_l, dtb_l, norm_w)


def _xattn_kernel(x_ref, ymix_ref, gmix_ref, k_ref, v_ref, wq_ref, wo_ref, gpre_ref, gpost_ref, gnext_ref,
                  h_out_ref, f_out_ref):
    ym = ymix_ref[...].astype(F32)
    h = x_ref[...] + ym * lax.rsqrt(jnp.mean(ym * ym, axis=-1, keepdims=True) + NORM_EPS) * gmix_ref[...]
    cn =(h * lax.rsqrt(jnp.mean(h * h, axis=-1, keepdims=True) + NORM_EPS) * gpre_ref[...]).astype(BF16)
    q = _dot(cn, wq_ref[...]).astype(BF16)
    kk = k_ref[0]
    vv = v_ref[0]
    outs = []
    for hh in range(XA_HEADS):
        sl = slice(hh * XA_HEAD, (hh + 1) * XA_HEAD)
        s = _dot_nt(q[:, sl], kk[:, sl]) * (XA_HEAD ** -0.5)
        s = s - jnp.max(s, axis=-1, keepdims=True)
        e = jnp.exp(s)
        pr = e / jnp.sum(e, axis=-1, keepdims=True)
        outs.append(_dot(pr.astype(BF16), vv[:, sl]))
    o = jnp.concatenate(outs, axis=1).astype(BF16)
    y = _dot(o, wo_ref[...])
    h2 = h + y * lax.rsqrt(jnp.mean(y * y, axis=-1, keepdims=True) + NORM_EPS) * gpost_ref[...]
    h_out_ref[...] = h2
    f_out_ref[...] = (h2 * lax.rsqrt(jnp.mean(h2 * h2, axis=-1, keepdims=True) + NORM_EPS)
                      * gnext_ref[...]).astype(f_out_ref.dtype)


def _xattn_block(x, y_mix, g_mix, kmem, vmem, w_q, w_o, g_pre, g_post, g_next, batch, seq, tt=256):
    n, d = x.shape
    mlen, xw = kmem.shape[1], kmem.shape[2]
    per_b = seq // tt
    row = pl.BlockSpec((tt, d), lambda i: (i, 0))
    vec = pl.BlockSpec((1, d), lambda i: (0, 0))
    mem = pl.BlockSpec((1, mlen, xw), lambda i: (i // per_b, 0, 0))
    once = pl.Buffered(1)
    return pl.pallas_call(
        _xattn_kernel,
        grid=(n // tt,),
        in_specs=[row, row, vec, mem, mem, pl.BlockSpec((d, xw), lambda i: (0, 0), pipeline_mode=once),
                  pl.BlockSpec((xw, d), lambda i: (0, 0), pipeline_mode=once), vec, vec, vec],
        out_specs=[row, row],
        out_shape=[jax.ShapeDtypeStruct((n, d), F32), jax.ShapeDtypeStruct((n, d), BF16)],
        compiler_params=_cparams(("parallel",)),
        name="xattn_block",
    )(x, y_mix, g_mix.reshape(1, d), kmem, vmem, w_q, w_o, g_pre.reshape(1, d), g_post.reshape(1, d),
      g_next.reshape(1, d))


SRC_LORA = 3 * RW
SRC_GDN = SRC_LORA + 2 * LORA + 256
SRC_BA = SRC_GDN + 4 * RW
SRC_GATES = SRC_BA + 2 * N_GDN_HEADS
RELAYOUT_ROWS = 512


def _relayout_kernel(x_ref, misc_ref, o_ref):
    @pl.when(pl.program_id(0) != OFF_MISC // RELAYOUT_ROWS)
    def _():
        o_ref[...] = x_ref[...].astype(o_ref.dtype)

    @pl.when(pl.program_id(0) == OFF_MISC // RELAYOUT_ROWS)
    def _():
        o_ref[...] = misc_ref[...].astype(o_ref.dtype)


def _relayout_w_in_t(w_in_t):
    n_in, d = w_in_t.shape
    rb = RELAYOUT_ROWS
    assert OFF_GQ % rb == 0 and OFF_MISC % rb == 0 and OFF_GATE_RW % rb == 0 and MISC_W == rb
    zeros = jnp.zeros((128 - LORA, d), w_in_t.dtype)
    misc = jnp.concatenate([w_in_t[SRC_LORA:SRC_LORA + LORA], w_in_t[SRC_BA:SRC_BA + 2 * N_GDN_HEADS],
                            w_in_t[SRC_LORA + LORA:SRC_LORA + 2 * LORA], zeros,
                            w_in_t[SRC_LORA + 2 * LORA:SRC_GDN]], axis=0)

    def src_row(j):
        r = j * rb
        src = jnp.where(r < OFF_GQ, r, jnp.where(r < OFF_MISC, r + (SRC_GDN - OFF_GQ),
                                                 jnp.where(r < OFF_GATE_RW, 0, r + (SRC_GATES - OFF_GATE_RW))))
        return pl.multiple_of(src, math.gcd(rb, SRC_GDN, SRC_GATES))

    return pl.pallas_call(
        _relayout_kernel,
        grid=(NP // rb,),
        in_specs=[pl.BlockSpec((pl.Element(rb), pl.Element(d)), lambda j: (src_row(j), 0)),
                  pl.BlockSpec((rb, d), lambda j: (0, 0))],
        out_specs=pl.BlockSpec((rb, d), lambda j: (j, 0)),
        out_shape=jax.ShapeDtypeStruct((NP, d), BF16),
        compiler_params=_cparams(("parallel",)),
        name="relayout_w_in",
    )(w_in_t, misc)


def _layer(x, mem, mix_norm_pre, mix_norm_post, w_in, rwkv_shift_mix, rwkv_w0, rwkv_w_up, rwkv_a0, rwkv_a_up,
           rwkv_g_up, rwkv_k_k, rwkv_k_a, rwkv_r_k, rwkv_gn_w, rwkv_gn_b, gdn_conv_w, gdn_a_log, gdn_dt_bias,
           gdn_norm_w, w_branch_rwkv, w_branch_gdn, w_mix_out, xa_norm_pre, xa_norm_mem, xa_norm_post, xa_w_q,
           xa_w_kv, xa_w_o, mlp_norm_pre, mlp_norm_post, mlp_w_up, mlp_w_down):
    batch, seq, d = x.shape
    n = batch * seq
    h = x.reshape(n, d)

    w_in_b = _relayout_w_in_t(jnp.swapaxes(w_in, 0, 1))
    mix = rwkv_shift_mix
    o = 3 * RW
    cp = jnp.stack([mix[0:RW], mix[RW:2 * RW], mix[2 * RW:o], rwkv_w0, rwkv_a0, rwkv_k_k, rwkv_k_a,
                    rwkv_r_k.reshape(RW)], axis=0)
    gp = jnp.concatenate([rwkv_gn_w[None], rwkv_gn_b[None], jnp.zeros((6, RW), F32)], axis=0)
    zeros = lambda k: jnp.zeros((k,), F32)
    mixm = jnp.concatenate([mix[o:o + LORA], zeros(32), mix[o + LORA:o + 2 * LORA], zeros(32),
                            mix[o + 2 * LORA:o + 2 * LORA + 256]])[None]
    w_up_p = jnp.concatenate([rwkv_w_up, jnp.zeros((128 - LORA, RW), F32)], axis=0).astype(BF16)
    a_up_p = jnp.concatenate([rwkv_a_up, jnp.zeros((128 - LORA, RW), F32)], axis=0).astype(BF16)
    g_up_b = rwkv_g_up.astype(BF16)
    alog_l = jnp.concatenate([zeros(ALPHA_LANE), gdn_a_log])[None]
    dtb_l = jnp.concatenate([zeros(ALPHA_LANE), gdn_dt_bias])[None]

    u = _rmsnorm(h, mix_norm_pre, BF16)
    p = _matmul(u, w_in_b, F32, tm=1024, tn=1280, b_is_nk=True, name="in_proj")
    p3 = p.reshape(batch, seq, NP)
    y_rw = _rwkv_mix(p3, cp, gp, mixm, w_up_p, a_up_p, g_up_b).reshape(n, RW)
    y_gdn = _gdn_mix(p3, gdn_conv_w, alog_l, dtb_l, gdn_norm_w[None]).reshape(n, RW)
    merged = _branch_merge(y_rw, y_gdn, w_branch_rwkv, w_branch_gdn, p)
    y_mix = _matmul_wcast(merged, w_mix_out, BF16, tm=2048, tn=512, name="mix_out")

    mlen = mem.shape[1]
    m = _rmsnorm(mem.reshape(batch * mlen, d), xa_norm_mem, BF16)
    kv = _matmul(m, xa_w_kv.astype(BF16), BF16, tm=batch * mlen, tn=512, name="xa_kv")
    xw = XA_HEADS * XA_HEAD
    kmem = kv[:, :xw].reshape(batch, mlen, xw)
    vmem = kv[:, xw:].reshape(batch, mlen, xw)
    h, f = _xattn_block(h, y_mix, mix_norm_post, kmem, vmem, xa_w_q.astype(BF16), xa_w_o.astype(BF16),
                        xa_norm_pre, xa_norm_post, mlp_norm_pre, batch, seq)

    hmid = _matmul_wcast(f, mlp_w_up, BF16, tm=2048, tn=512, relu2=True, name="mlp_up")
    y_mlp = _matmul_kacc(hmid, mlp_w_down, BF16, tm=2048, tn=1024, tk=2048, k_chunk=1024, name="mlp_down")
    h = _resid_norm(h, y_mlp, mlp_norm_post)
    return h.reshape(batch, seq, d)


def kernel(x, mem, mix_norm_pre, mix_norm_post, w_in, rwkv_shift_mix, rwkv_w0, rwkv_w_up, rwkv_a0, rwkv_a_up,
           rwkv_g_up, rwkv_k_k, rwkv_k_a, rwkv_r_k, rwkv_gn_w, rwkv_gn_b, gdn_conv_w, gdn_a_log, gdn_dt_bias,
           gdn_norm_w, w_branch_rwkv, w_branch_gdn, w_mix_out, xa_norm_pre, xa_norm_mem, xa_norm_post, xa_w_q,
           xa_w_kv, xa_w_o, mlp_norm_pre, mlp_norm_post, mlp_w_up, mlp_w_down):
    args = (mix_norm_pre, mix_norm_post, w_in, rwkv_shift_mix, rwkv_w0, rwkv_w_up, rwkv_a0, rwkv_a_up,
            rwkv_g_up, rwkv_k_k, rwkv_k_a, rwkv_r_k, rwkv_gn_w, rwkv_gn_b, gdn_conv_w, gdn_a_log, gdn_dt_bias,
            gdn_norm_w, w_branch_rwkv, w_branch_gdn, w_mix_out, xa_norm_pre, xa_norm_mem, xa_norm_post, xa_w_q,
            xa_w_kv, xa_w_o, mlp_norm_pre, mlp_norm_post, mlp_w_up, mlp_w_down)
    h = x
    for layer in range(mix_norm_pre.shape[0]):
        h = _layer(h, mem, *(a[layer] for a in args))
    return h
```

```python
import functools
import math

import jax
import jax.numpy as jnp
from jax import lax
from jax.experimental import pallas as pl
from jax.experimental.pallas import tpu as pltpu

F32 = jnp.float32
BF16 = jnp.bfloat16

V7X_LANES = 128
V7X_SUBLANES = 8
V7X_VMEM_BYTES = 64 * 1024 * 1024
V7X_VMEM_LIMIT_BYTES = V7X_VMEM_BYTES - 4 * 1024 * 1024

NORM_EPS = 1e-6
L2_EPS = 1e-6
RWKV_GN_EPS = 64e-5
RWKV_HEAD = 64
GDN_HEAD = 128
CHUNK = 64
XA_HEADS = 4
XA_HEAD = 128

RW = 2048
D_MODEL = 4096
LORA = 96
N_GDN_HEADS = RW // GDN_HEAD
OFF_R, OFF_K, OFF_V = 0, RW, 2 * RW
OFF_GQ, OFF_GK, OFF_GV = 3 * RW, 4 * RW, 5 * RW
OFF_Z = 6 * RW
OFF_MISC = 7 * RW
MISC_W = 512
MISC_LW, MISC_LA, MISC_LG = 0, 128, 256
BETA_LANE = LORA
ALPHA_LANE = LORA + N_GDN_HEADS
OFF_GATE_RW = OFF_MISC + MISC_W
OFF_GATE_GDN = OFF_GATE_RW + D_MODEL
NP = OFF_GATE_GDN + D_MODEL

SRC_LORA = 3 * RW
SRC_GDN = SRC_LORA + 2 * LORA + 256
SRC_BA = SRC_GDN + 4 * RW
SRC_GATES = SRC_BA + 2 * N_GDN_HEADS
RELAYOUT_ROWS = 512

TILE_IN_PROJ = (1024, 1536)
TILE_PANEL = (2048, 512)
TILE_BRANCH = (1024, 512)
TILE_MLP_DOWN = (2048, 1024, 2048)
K_CHUNK = 1024
ROW_TILE = 256


def _cparams(sem):
    return pltpu.CompilerParams(dimension_semantics=sem, vmem_limit_bytes=V7X_VMEM_LIMIT_BYTES)


def _dot(a, b):
    return jnp.dot(a, b, preferred_element_type=F32)


def _dot_nt(a, b):
    return lax.dot_general(a, b, (((1,), (1,)), ((), ())), preferred_element_type=F32)


def _dot_tn(a, b):
    return lax.dot_general(a, b, (((0,), (0,)), ((), ())), preferred_element_type=F32)


def _bf(x):
    return x.astype(BF16)


def _hi_lo(x):
    hi = x.astype(BF16)
    return hi, (x - hi.astype(F32)).astype(BF16)


def _mm(a, b):
    return _dot(_bf(a), _bf(b))


def _mm_nt(a, b):
    return _dot_nt(_bf(a), _bf(b))


def _mm_tn(a, b):
    return _dot_tn(_bf(a), _bf(b))


def _mm_xl(a01x2, b):
    hi, lo = _hi_lo(b)
    return _dot(a01x2, jnp.concatenate([hi, lo], axis=0))


def _sigmoid(x):
    return 0.5 * jnp.tanh(0.5 * x) + 0.5


def _softplus(x):
    return jnp.maximum(x, 0.0) + jnp.log1p(jnp.exp(-jnp.abs(x)))


def _iota(shape, dim):
    return lax.broadcasted_iota(jnp.int32, shape, dim)


def _lane_groups(x):
    return [x[:, g * V7X_LANES:(g + 1) * V7X_LANES] for g in range(x.shape[1] // V7X_LANES)]


def _tril_ones_x2(c):
    return jnp.where(_iota((c, 2 * c), 0) >= _iota((c, 2 * c), 1) % c, 1.0, 0.0).astype(BF16)


def _rmsnorm_kernel(x_ref, g_ref, o_ref):
    x = x_ref[...]
    inv = lax.rsqrt(jnp.mean(x * x, axis=-1, keepdims=True) + NORM_EPS)
    o_ref[...] = (x * inv * g_ref[...]).astype(o_ref.dtype)


def _rmsnorm(x, gain, out_dtype):
    n, d = x.shape
    rows = min(ROW_TILE, n)
    return pl.pallas_call(
        _rmsnorm_kernel,
        grid=(n // rows,),
        in_specs=[pl.BlockSpec((rows, d), lambda i: (i, 0)), pl.BlockSpec((1, d), lambda i: (0, 0))],
        out_specs=pl.BlockSpec((rows, d), lambda i: (i, 0)),
        out_shape=jax.ShapeDtypeStruct((n, d), out_dtype),
        compiler_params=_cparams(("parallel",)),
        name="rmsnorm",
    )(x, gain.reshape(1, d))


def _resid_norm_kernel(h_ref, y_ref, g_ref, o_ref):
    y = y_ref[...].astype(F32)
    inv = lax.rsqrt(jnp.mean(y * y, axis=-1, keepdims=True) + NORM_EPS)
    o_ref[...] = h_ref[...] + y * inv * g_ref[...]


def _resid_norm(h, y, gain):
    n, d = h.shape
    row = pl.BlockSpec((ROW_TILE, d), lambda i: (i, 0))
    return pl.pallas_call(
        _resid_norm_kernel,
        grid=(n // ROW_TILE,),
        in_specs=[row, row, pl.BlockSpec((1, d), lambda i: (0, 0))],
        out_specs=row,
        out_shape=jax.ShapeDtypeStruct((n, d), F32),
        compiler_params=_cparams(("parallel",)),
        name="resid_norm",
    )(h, y, gain.reshape(1, d))


def _mm_kernel(a_ref, b_ref, o_ref, *, b_is_nk):
    acc = _dot_nt(a_ref[...], b_ref[...]) if b_is_nk else _dot(a_ref[...], b_ref[...])
    o_ref[...] = acc.astype(o_ref.dtype)


def _matmul(a, b, out_dtype, tm, tn, b_is_nk=False, name="matmul"):
    m, k = a.shape
    n = b.shape[0] if b_is_nk else b.shape[1]
    b_spec = pl.BlockSpec((tn, k), lambda i, j: (j, 0)) if b_is_nk else pl.BlockSpec((k, tn), lambda i, j: (0, j))
    return pl.pallas_call(
        functools.partial(_mm_kernel, b_is_nk=b_is_nk),
        grid=(m // tm, n // tn),
        in_specs=[pl.BlockSpec((tm, k), lambda i, j: (i, 0)), b_spec],
        out_specs=pl.BlockSpec((tm, tn), lambda i, j: (i, j)),
        out_shape=jax.ShapeDtypeStruct((m, n), out_dtype),
        compiler_params=_cparams(("parallel", "arbitrary")),
        name=name,
    )(a, b)


def _mm_wcast_kernel(a_ref, b_ref, o_ref, b_bf16, *, relu2):
    @pl.when(pl.program_id(1) == 0)
    def _():
        b_bf16[...] = b_ref[...].astype(BF16)

    acc = _dot(a_ref[...], b_bf16[...])
    if relu2:
        acc = jnp.square(jnp.maximum(acc, 0.0))
    o_ref[...] = acc.astype(o_ref.dtype)


def _matmul_wcast(a, b, out_dtype, relu2=False, name="matmul_wcast"):
    m, k = a.shape
    _, n = b.shape
    tm, tn = TILE_PANEL
    return pl.pallas_call(
        functools.partial(_mm_wcast_kernel, relu2=relu2),
        grid=(n // tn, m // tm),
        in_specs=[pl.BlockSpec((tm, k), lambda j, i: (i, 0)), pl.BlockSpec((k, tn), lambda j, i: (0, j))],
        out_specs=pl.BlockSpec((tm, tn), lambda j, i: (i, j)),
        out_shape=jax.ShapeDtypeStruct((m, n), out_dtype),
        scratch_shapes=[pltpu.VMEM((k, tn), BF16)],
        compiler_params=_cparams(("parallel", "arbitrary")),
        name=name,
    )(a, b)


def _mm_kacc_kernel(a_ref, b_ref, o_ref, acc_ref):
    l = pl.program_id(2)

    @pl.when(l == 0)
    def _():
        acc_ref[...] = jnp.zeros_like(acc_ref)

    acc = None
    for k0 in range(0, a_ref.shape[1], K_CHUNK):
        part = _dot(a_ref[:, k0:k0 + K_CHUNK], b_ref[k0:k0 + K_CHUNK, :].astype(BF16))
        acc = part if acc is None else acc + part
    acc_ref[...] += acc

    @pl.when(l == pl.num_programs(2) - 1)
    def _():
        o_ref[...] = acc_ref[...].astype(o_ref.dtype)


def _matmul_kacc(a, b, out_dtype, name="matmul_kacc"):
    m, k = a.shape
    _, n = b.shape
    tm, tn, tk = TILE_MLP_DOWN
    return pl.pallas_call(
        _mm_kacc_kernel,
        grid=(m // tm, n // tn, k // tk),
        in_specs=[pl.BlockSpec((tm, tk), lambda i, j, l: (i, l)), pl.BlockSpec((tk, tn), lambda i, j, l: (l, j))],
        out_specs=pl.BlockSpec((tm, tn), lambda i, j, l: (i, j)),
        out_shape=jax.ShapeDtypeStruct((m, n), out_dtype),
        scratch_shapes=[pltpu.VMEM((tm, tn), F32)],
        compiler_params=_cparams(("parallel", "parallel", "arbitrary")),
        name=name,
    )(a, b)


def _branch_kernel(yr_ref, yg_ref, wr_ref, wg_ref, gr_ref, gg_ref, o_ref, wr_bf16, wg_bf16):
    @pl.when(pl.program_id(1) == 0)
    def _():
        wr_bf16[...] = wr_ref[...].astype(BF16)
        wg_bf16[...] = wg_ref[...].astype(BF16)

    br = _dot(yr_ref[...], wr_bf16[...])
    bg = _dot(yg_ref[...], wg_bf16[...])
    o_ref[...] = (_sigmoid(gr_ref[...]) * br + _sigmoid(gg_ref[...]) * bg).astype(o_ref.dtype)


def _branch_merge(y_rw, y_gdn, w_rw, w_gdn, p):
    m, k = y_rw.shape
    n = w_rw.shape[1]
    tm, tn = TILE_BRANCH
    jr, jg = OFF_GATE_RW // tn, OFF_GATE_GDN // tn
    return pl.pallas_call(
        _branch_kernel,
        grid=(n // tn, m // tm),
        in_specs=[
            pl.BlockSpec((tm, k), lambda j, i: (i, 0)),
            pl.BlockSpec((tm, k), lambda j, i: (i, 0)),
            pl.BlockSpec((k, tn), lambda j, i: (0, j)),
            pl.BlockSpec((k, tn), lambda j, i: (0, j)),
            pl.BlockSpec((tm, tn), lambda j, i: (i, jr + j)),
            pl.BlockSpec((tm, tn), lambda j, i: (i, jg + j)),
        ],
        out_specs=pl.BlockSpec((tm, tn), lambda j, i: (i, j)),
        out_shape=jax.ShapeDtypeStruct((m, n), BF16),
        scratch_shapes=[pltpu.VMEM((k, tn), BF16)] * 2,
        compiler_params=_cparams(("parallel", "arbitrary")),
        name="branch_merge",
    )(y_rw, y_gdn, w_rw, w_gdn, p, p)


def _batch_lanes(ref):
    return jnp.concatenate([ref[b] for b in range(ref.shape[0])], axis=1)


def _tile_lanes(x, n):
    return jnp.concatenate([x] * n, axis=1)


def _token_shift(x_ref, prev_ref, first, mix):
    x = _batch_lanes(x_ref)
    before = jnp.where(first, 0.0, _batch_lanes(prev_ref)[V7X_SUBLANES - 1:V7X_SUBLANES, :])
    prev = jnp.where(_iota(x.shape, 0) == 0, before, pltpu.roll(x, 1, axis=0))
    return x + (prev - x) * mix


def _expand2(x, head0):
    return jnp.concatenate([jnp.where(head0, x, 0.0), jnp.where(head0, 0.0, x)], axis=0)


def _rwkv_kernel(r_ref, k_ref, v_ref, pr_ref, pk_ref, pv_ref, misc_ref, pmisc_ref,
                 cp_ref, gp_ref, mixm_ref, wup_ref, aup_ref, gup_ref, o_ref, s_ref):
    c = CHUNK
    w = V7X_LANES
    batch, _, lw = r_ref.shape
    ci = pl.program_id(0)

    @pl.when(ci == 0)
    def _():
        s_ref[...] = jnp.zeros_like(s_ref)

    first = ci == 0
    mix_r, mix_k, mix_v, w0, a0, k_k, k_a, r_k = (_tile_lanes(cp_ref[i:i + 1, :], batch) for i in range(8))
    gn_w, gn_b = _tile_lanes(gp_ref[0:1, :], batch), _tile_lanes(gp_ref[1:2, :], batch)

    misc_m = _token_shift(misc_ref, pmisc_ref, first, _tile_lanes(mixm_ref[...], batch))
    xr = _token_shift(r_ref, pr_ref, first, mix_r)
    xk = _token_shift(k_ref, pk_ref, first, mix_k)
    xv = _token_shift(v_ref, pv_ref, first, mix_v)

    def lora(fn, lo, hi, w_ref):
        return jnp.concatenate([_dot(_bf(fn(misc_m[:, b * MISC_W + lo:b * MISC_W + hi])), w_ref[...])
                                for b in range(batch)], axis=1)

    wlin = w0 + lora(jnp.tanh, MISC_LW, MISC_LA, wup_ref)
    logw = -math.exp(-0.5) * _sigmoid(wlin)
    a = _sigmoid(a0 + lora(lambda t: t, MISC_LA, MISC_LG, aup_ref))
    gate = lora(_sigmoid, MISC_LG, MISC_W, gup_ref)

    same_head = (_iota((w, w), 0) // RWKV_HEAD) == (_iota((w, w), 1) // RWKV_HEAD)
    bones = jnp.where(same_head, 1.0, 0.0).astype(BF16)
    eye = jnp.where(_iota((w, w), 0) == _iota((w, w), 1), 1.0, 0.0)

    def head_sums(x):
        n = x.shape[1] // w
        s = _dot(_bf(jnp.concatenate(_lane_groups(x), axis=0)), bones)
        return jnp.concatenate([s[g * c:(g + 1) * c] for g in range(n)], axis=1)

    kkr = xk * k_k
    kkn = kkr * lax.rsqrt(head_sums(kkr * kkr) + L2_EPS)
    k2 = xk * (1.0 + (a - 1.0) * k_a)
    am = -kkn
    bm = kkn * a

    cum = _mm_xl(_tril_ones_x2(c), logw)
    cum_ex = cum - logw
    c_last = cum[c - 1:c, :]
    c_mid = cum[c // 2 - 1:c // 2, :]
    e_out = jnp.exp(c_mid - cum)
    e_last = jnp.exp(c_last - cum)
    rt = xr * jnp.exp(cum - c_mid)
    at = am * jnp.exp(cum_ex - c_mid)
    bt = bm * e_out
    kt = k2 * e_out
    a_abs = am * jnp.exp(cum_ex)
    r_abs = xr * jnp.exp(cum)
    bh = bm * e_last
    kh = k2 * e_last
    w_c = jnp.exp(c_last)

    head0 = _iota((c, w), 1) < RWKV_HEAD
    head0_2 = _iota((c, 2 * w), 1) % w < RWKV_HEAD
    wi = _iota((c, 2 * c), 0)
    wj = _iota((c, 2 * c), 1) % c
    strict = wi > wj
    incl = wi >= wj

    gs = range(batch * lw // w)
    G = _lane_groups
    at_g, rt_g, bt_g, kt_g, xv_g = G(at), G(rt), G(bt), G(kt), G(xv)
    aa = [_mm_nt(jnp.concatenate([at_g[g], rt_g[g]], axis=0),
                 jnp.concatenate([_expand2(bt_g[g], head0), _expand2(kt_g[g], head0)], axis=0)) for g in gs]
    a_ab = [jnp.where(strict, aa[g][0:c, 0:2 * c], 0.0) for g in gs]
    a_ak = [jnp.where(strict, aa[g][0:c, 2 * c:4 * c], 0.0) for g in gs]
    a_rb = [jnp.where(incl, aa[g][c:2 * c, 0:2 * c], 0.0) for g in gs]
    a_rk = [jnp.where(incl, aa[g][c:2 * c, 2 * c:4 * c], 0.0) for g in gs]

    xv_e = [_expand2(xv_g[g], head0) for g in gs]
    akv = [_mm(a_ak[g], xv_e[g]) for g in gs]
    eye_w = jnp.where(wi == wj, 1.0, 0.0)
    tw = [eye_w + a_ab[g] for g in gs]
    pw = [_mm(a_ab[g], _expand2(a_ab[g], head0)) for g in gs]
    levels = int(math.log2(c))
    for lvl in range(1, levels):
        if lvl + 1 < levels:
            pt = [_mm(pw[g], _expand2(jnp.concatenate([pw[g], tw[g]], axis=1), head0_2)) for g in gs]
            pw = [pt[g][:, 0:w] for g in gs]
            tw = [tw[g] + pt[g][:, w:2 * w] for g in gs]
        else:
            tw = [tw[g] + _mm(pw[g], _expand2(tw[g], head0)) for g in gs]
    a_abs_g = G(a_abs)
    x = [_mm(tw[g], _expand2(jnp.concatenate([a_abs_g[g], akv[g]], axis=1), head0_2)) for g in gs]
    wm = [x[g][:, 0:w] for g in gs]
    u0 = [x[g][:, w:2 * w] for g in gs]

    bh_g, kh_g, wc_g, rabs_g = G(bh), G(kh), G(w_c), G(r_abs)
    m_mat = [jnp.where(same_head, _mm_tn(bh_g[g], wm[g]), 0.0) + eye * wc_g[g] for g in gs]
    n_mat = [jnp.where(same_head, _mm_tn(jnp.concatenate([bh_g[g], kh_g[g]], axis=0),
                                         jnp.concatenate([u0[g], xv_g[g]], axis=0)), 0.0) for g in gs]
    p_mat = [rabs_g[g] + _mm(a_rb[g], _expand2(wm[g], head0)) for g in gs]
    q_mat = [_mm(jnp.concatenate([a_rb[g], a_rk[g]], axis=1),
                 jnp.concatenate([_expand2(u0[g], head0), xv_e[g]], axis=0)) for g in gs]

    h0 = [_bf(s_ref[g]) for g in gs]
    y = jnp.concatenate([_dot(_bf(p_mat[g]), h0[g]) + q_mat[g] for g in gs], axis=1)
    for g in gs:
        s_ref[g] = _dot(_bf(m_mat[g]), h0[g]) + n_mat[g]

    inv_n = 1.0 / RWKV_HEAD
    d = y - head_sums(y) * inv_n
    var = head_sums(d * d) * inv_n
    yn = d * lax.rsqrt(var + RWKV_GN_EPS) * gn_w + gn_b
    bonus = head_sums(xr * k2 * r_k) * xv
    out = ((yn + bonus) * gate).astype(o_ref.dtype)
    for b in range(batch):
        o_ref[b] = out[:, b * lw:(b + 1) * lw]


def _chunk_specs(batch, c):
    def cur(off, width):
        assert off % width == 0
        return pl.BlockSpec((batch, c, width), lambda t: (0, t, off // width))

    def prev(off, width):
        return pl.BlockSpec((batch, V7X_SUBLANES, width),
                            lambda t: (0, jnp.maximum(t * (c // V7X_SUBLANES) - 1, 0), off // width))

    return cur, prev


def _rwkv_mix(p, cp, gp, mixm, w_up_p, a_up_p, g_up):
    batch, seq, _ = p.shape
    c = CHUNK
    cur, prev = _chunk_specs(batch, c)
    whole = lambda x: pl.BlockSpec(x.shape, lambda t: (0, 0))
    return pl.pallas_call(
        _rwkv_kernel,
        grid=(seq // c,),
        in_specs=[
            cur(OFF_R, RW), cur(OFF_K, RW), cur(OFF_V, RW), prev(OFF_R, RW), prev(OFF_K, RW), prev(OFF_V, RW),
            cur(OFF_MISC, MISC_W), prev(OFF_MISC, MISC_W),
            whole(cp), whole(gp), whole(mixm), whole(w_up_p), whole(a_up_p), whole(g_up),
        ],
        out_specs=pl.BlockSpec((batch, c, RW), lambda t: (0, t, 0)),
        out_shape=jax.ShapeDtypeStruct((batch, seq, RW), BF16),
        scratch_shapes=[pltpu.VMEM((batch * RW // V7X_LANES, V7X_LANES, V7X_LANES), F32)],
        compiler_params=_cparams(("arbitrary",)),
        name="rwkv7_mix",
    )(p, p, p, p, p, p, p, p, cp, gp, mixm, w_up_p, a_up_p, g_up)


def _gdn_kernel(q_ref, k_ref, v_ref, pq_ref, pk_ref, pv_ref, z_ref, misc_ref,
                cwq_ref, cwk_ref, cwv_ref, alog_ref, dtb_ref, nw_ref, o_ref, s_ref):
    c = CHUNK
    w = V7X_LANES
    batch, _, lw = q_ref.shape
    assert lw == N_GDN_HEADS * w
    ci = pl.program_id(0)
    heads = batch * N_GDN_HEADS
    hs = range(heads)
    G = _lane_groups

    @pl.when(ci == 0)
    def _():
        s_ref[...] = jnp.zeros_like(s_ref)

    first = ci == 0

    def conv_silu(x_ref, prev_ref, cw_ref):
        x = _batch_lanes(x_ref)
        prev8 = jnp.where(first, 0.0, _batch_lanes(prev_ref))
        cw = _tile_lanes(cw_ref[...], batch)
        row = _iota(x.shape, 0)
        taps = cw.shape[0]
        acc = x * cw[taps - 1:taps, :]
        for s in range(1, taps):
            tail = jnp.tile(pltpu.roll(prev8, s, axis=0), (c // V7X_SUBLANES, 1))
            acc = acc + jnp.where(row < s, tail, pltpu.roll(x, s, axis=0)) * cw[taps - 1 - s:taps - s, :]
        return acc * _sigmoid(acc)

    def head_sums(x):
        return jnp.concatenate([jnp.broadcast_to(jnp.sum(g, axis=-1, keepdims=True), g.shape) for g in G(x)], axis=1)

    q = conv_silu(q_ref, pq_ref, cwq_ref)
    k = conv_silu(k_ref, pk_ref, cwk_ref)
    v = conv_silu(v_ref, pv_ref, cwv_ref)
    q = q * lax.rsqrt(head_sums(q * q) + L2_EPS) * (GDN_HEAD ** -0.5)
    k = k * lax.rsqrt(head_sums(k * k) + L2_EPS)

    lane2 = _iota((2 * w, w), 0) % w
    lane2r = _iota((c, 2 * w), 1) % w
    bcol, gcol, grow = [], [], []
    for b in range(batch):
        bm = misc_ref[b, :, MISC_LW:MISC_LA]
        beta_hl = jnp.concatenate(_hi_lo(_sigmoid(bm)), axis=1)
        g_full = -jnp.exp(alog_ref[...]) * _softplus(bm + dtb_ref[...])
        gc_hl = jnp.concatenate(_hi_lo(_mm_xl(_tril_ones_x2(c), g_full)), axis=1)
        for h in range(N_GDN_HEADS):
            bcol.append(_dot(beta_hl, jnp.where(lane2 == BETA_LANE + h, 1.0, 0.0).astype(BF16)))
            gcol.append(_dot(gc_hl, jnp.where(lane2 == ALPHA_LANE + h, 1.0, 0.0).astype(BF16)))
            grow.append(_dot_nt(jnp.where(lane2r == ALPHA_LANE + h, 1.0, 0.0).astype(BF16), gc_hl))

    ti = _iota((c, c), 0)
    tj = _iota((c, c), 1)
    causal = ti >= tj
    strict = ti > tj
    decay = [jnp.where(causal, jnp.exp(jnp.where(causal, gcol[h][:, 0:c] - grow[h], 0.0)), 0.0) for h in hs]

    q_g, k_g, v_g = G(q), G(k), G(v)
    kb = [k_g[h] * bcol[h] for h in hs]
    kk = [_mm_nt(jnp.concatenate([kb[h], q_g[h]], axis=0), k_g[h]) for h in hs]
    lneg = [jnp.where(strict, -kk[h][0:c] * decay[h], 0.0) for h in hs]
    a_intra = [kk[h][c:2 * c] * decay[h] for h in hs]

    egc = [jnp.exp(gcol[h]) for h in hs]
    g_last = [gcol[h][c - 1:c, :] for h in hs]
    lane_t = _iota((c, w), 1) >= c
    eye_t = jnp.where(_iota((c, w), 1) == _iota((c, w), 0) + c, 1.0, 0.0)
    zt = [jnp.concatenate([lneg[h], jnp.zeros((c, c), F32)], axis=1) + eye_t for h in hs]
    for _ in range(int(math.log2(c))):
        zb = [_bf(zt[h]) for h in hs]
        zt = [_dot(zb[h][:, 0:c], zb[h]) + jnp.where(lane_t, zt[h], 0.0) for h in hs]
    zeros_x = jnp.zeros((c, 2 * w), F32)
    x = [_mm(zt[h], jnp.concatenate([zeros_x, jnp.concatenate([v_g[h] * bcol[h], kb[h] * egc[h]], axis=1)], axis=0))
         for h in hs]
    uu = [x[h][:, 0:w] for h in hs]
    ww = [x[h][:, w:2 * w] for h in hs]

    kd = [k_g[h] * jnp.exp(g_last[h] - gcol[h]) for h in hs]
    eye = jnp.where(_iota((w, w), 0) == _iota((w, w), 1), 1.0, 0.0)
    m_mat = [eye * jnp.exp(g_last[h]) - _mm_tn(kd[h], ww[h]) for h in hs]
    n_mat = [_mm_tn(kd[h], uu[h]) for h in hs]
    p_mat = [q_g[h] * egc[h] - _mm(a_intra[h], ww[h]) for h in hs]
    q_mat = [_mm(a_intra[h], uu[h]) for h in hs]

    s0 = [_bf(s_ref[h]) for h in hs]
    o = jnp.concatenate([_dot(_bf(p_mat[h]), s0[h]) + q_mat[h] for h in hs], axis=1)
    for h in hs:
        s_ref[h] = _dot(_bf(m_mat[h]), s0[h]) + n_mat[h]

    o = o * lax.rsqrt(head_sums(o * o) * (1.0 / GDN_HEAD) + NORM_EPS) * _tile_lanes(nw_ref[...], heads)
    z = _batch_lanes(z_ref)
    out = (o * (z * _sigmoid(z))).astype(o_ref.dtype)
    for b in range(batch):
        o_ref[b] = out[:, b * lw:(b + 1) * lw]


def _gdn_mix(p, conv_w, alog_l, dtb_l, norm_w):
    batch, seq, _ = p.shape
    c = CHUNK
    cur, prev = _chunk_specs(batch, c)

    def cw(part):
        return pl.BlockSpec((conv_w.shape[0], RW), lambda t: (0, part))

    lane_vec = pl.BlockSpec((1, V7X_LANES), lambda t: (0, 0))
    return pl.pallas_call(
        _gdn_kernel,
        grid=(seq // c,),
        in_specs=[
            cur(OFF_GQ, RW), cur(OFF_GK, RW), cur(OFF_GV, RW), prev(OFF_GQ, RW), prev(OFF_GK, RW), prev(OFF_GV, RW),
            cur(OFF_Z, RW), cur(OFF_MISC, MISC_W),
            cw(0), cw(1), cw(2), lane_vec, lane_vec, lane_vec,
        ],
        out_specs=pl.BlockSpec((batch, c, RW), lambda t: (0, t, 0)),
        out_shape=jax.ShapeDtypeStruct((batch, seq, RW), BF16),
        scratch_shapes=[pltpu.VMEM((batch * N_GDN_HEADS, V7X_LANES, V7X_LANES), F32)],
        compiler_params=_cparams(("arbitrary",)),
        name="gdn_mix",
    )(p, p, p, p, p, p, p, p, conv_w, conv_w, conv_w, alog_l, dtb_l, norm_w)


def _xattn_kernel(x_ref, ymix_ref, gmix_ref, k_ref, v_ref, wq_ref, wo_ref, gpre_ref, gpost_ref, gnext_ref,
                  h_out_ref, f_out_ref):
    ym = ymix_ref[...].astype(F32)
    h = x_ref[...] + ym * lax.rsqrt(jnp.mean(ym * ym, axis=-1, keepdims=True) + NORM_EPS) * gmix_ref[...]
    cn = (h * lax.rsqrt(jnp.mean(h * h, axis=-1, keepdims=True) + NORM_EPS) * gpre_ref[...]).astype(BF16)
    q = _dot(cn, wq_ref[...]).astype(BF16)
    kk = k_ref[0]
    vv = v_ref[0]
    outs = []
    for hh in range(XA_HEADS):
        sl = slice(hh * XA_HEAD, (hh + 1) * XA_HEAD)
        s = _dot_nt(q[:, sl], kk[:, sl]) * (XA_HEAD ** -0.5)
        s = s - jnp.max(s, axis=-1, keepdims=True)
        e = jnp.exp(s)
        pr = e / jnp.sum(e, axis=-1, keepdims=True)
        outs.append(_dot(pr.astype(BF16), vv[:, sl]))
    o = jnp.concatenate(outs, axis=1).astype(BF16)
    y = _dot(o, wo_ref[...])
    h2 = h + y * lax.rsqrt(jnp.mean(y * y, axis=-1, keepdims=True) + NORM_EPS) * gpost_ref[...]
    h_out_ref[...] = h2
    f_out_ref[...] = (h2 * lax.rsqrt(jnp.mean(h2 * h2, axis=-1, keepdims=True) + NORM_EPS)
                      * gnext_ref[...]).astype(f_out_ref.dtype)


def _xattn_block(x, y_mix, g_mix, kmem, vmem, w_q, w_o, g_pre, g_post, g_next, seq):
    n, d = x.shape
    mlen, xw = kmem.shape[1], kmem.shape[2]
    per_b = seq // ROW_TILE
    row = pl.BlockSpec((ROW_TILE, d), lambda i: (i, 0))
    vec = pl.BlockSpec((1, d), lambda i: (0, 0))
    mem = pl.BlockSpec((1, mlen, xw), lambda i: (i // per_b, 0, 0))
    once = pl.Buffered(1)
    return pl.pallas_call(
        _xattn_kernel,
        grid=(n // ROW_TILE,),
        in_specs=[row, row, vec, mem, mem, pl.BlockSpec((d, xw), lambda i: (0, 0), pipeline_mode=once),
                  pl.BlockSpec((xw, d), lambda i: (0, 0), pipeline_mode=once), vec, vec, vec],
        out_specs=[row, row],
        out_shape=[jax.ShapeDtypeStruct((n, d), F32), jax.ShapeDtypeStruct((n, d), BF16)],
        compiler_params=_cparams(("parallel",)),
        name="xattn_block",
    )(x, y_mix, g_mix.reshape(1, d), kmem, vmem, w_q, w_o, g_pre.reshape(1, d), g_post.reshape(1, d),
      g_next.reshape(1, d))


def _relayout_kernel(x_ref, misc_ref, o_ref):
    @pl.when(pl.program_id(0) != OFF_MISC // RELAYOUT_ROWS)
    def _():
        o_ref[...] = x_ref[...].astype(o_ref.dtype)

    @pl.when(pl.program_id(0) == OFF_MISC // RELAYOUT_ROWS)
    def _():
        o_ref[...] = misc_ref[...].astype(o_ref.dtype)


def _relayout_w_in_t(w_in_t):
    n_in, d = w_in_t.shape
    rb = RELAYOUT_ROWS
    assert OFF_GQ % rb == 0 and OFF_MISC % rb == 0 and OFF_GATE_RW % rb == 0 and MISC_W == rb
    zeros = jnp.zeros((MISC_LG - MISC_LA - LORA, d), w_in_t.dtype)
    misc = jnp.concatenate([w_in_t[SRC_LORA:SRC_LORA + LORA], w_in_t[SRC_BA:SRC_BA + 2 * N_GDN_HEADS],
                            w_in_t[SRC_LORA + LORA:SRC_LORA + 2 * LORA], zeros,
                            w_in_t[SRC_LORA + 2 * LORA:SRC_GDN]], axis=0)

    def src_row(j):
        r = j * rb
        src = jnp.where(r < OFF_GQ, r, jnp.where(r < OFF_MISC, r + (SRC_GDN - OFF_GQ),
                                                 jnp.where(r < OFF_GATE_RW, 0, r + (SRC_GATES - OFF_GATE_RW))))
        return pl.multiple_of(src, math.gcd(rb, SRC_GDN, SRC_GATES))

    return pl.pallas_call(
        _relayout_kernel,
        grid=(NP // rb,),
        in_specs=[pl.BlockSpec((pl.Element(rb), pl.Element(d)), lambda j: (src_row(j), 0)),
                  pl.BlockSpec((rb, d), lambda j: (0, 0))],
        out_specs=pl.BlockSpec((rb, d), lambda j: (j, 0)),
        out_shape=jax.ShapeDtypeStruct((NP, d), BF16),
        compiler_params=_cparams(("parallel",)),
        name="relayout_w_in",
    )(w_in_t, misc)


def _layer(x, mem, mix_norm_pre, mix_norm_post, w_in, rwkv_shift_mix, rwkv_w0, rwkv_w_up, rwkv_a0, rwkv_a_up,
           rwkv_g_up, rwkv_k_k, rwkv_k_a, rwkv_r_k, rwkv_gn_w, rwkv_gn_b, gdn_conv_w, gdn_a_log, gdn_dt_bias,
           gdn_norm_w, w_branch_rwkv, w_branch_gdn, w_mix_out, xa_norm_pre, xa_norm_mem, xa_norm_post, xa_w_q,
           xa_w_kv, xa_w_o, mlp_norm_pre, mlp_norm_post, mlp_w_up, mlp_w_down):
    batch, seq, d = x.shape
    n = batch * seq
    h = x.reshape(n, d)

    w_in_b = _relayout_w_in_t(jnp.swapaxes(w_in, 0, 1))
    mix = rwkv_shift_mix
    o = 3 * RW
    cp = jnp.stack([mix[0:RW], mix[RW:2 * RW], mix[2 * RW:o], rwkv_w0, rwkv_a0, rwkv_k_k, rwkv_k_a,
                    rwkv_r_k.reshape(RW)], axis=0)
    gp = jnp.concatenate([rwkv_gn_w[None], rwkv_gn_b[None], jnp.zeros((V7X_SUBLANES - 2, RW), F32)], axis=0)
    zeros = lambda k: jnp.zeros((k,), F32)
    lora_pad = MISC_LA - LORA
    mixm = jnp.concatenate([mix[o:o + LORA], zeros(lora_pad), mix[o + LORA:o + 2 * LORA], zeros(lora_pad),
                            mix[o + 2 * LORA:o + 2 * LORA + 256]])[None]
    w_up_p = jnp.concatenate([rwkv_w_up, jnp.zeros((lora_pad, RW), F32)], axis=0).astype(BF16)
    a_up_p = jnp.concatenate([rwkv_a_up, jnp.zeros((lora_pad, RW), F32)], axis=0).astype(BF16)
    g_up_b = rwkv_g_up.astype(BF16)
    alog_l = jnp.concatenate([zeros(ALPHA_LANE), gdn_a_log])[None]
    dtb_l = jnp.concatenate([zeros(ALPHA_LANE), gdn_dt_bias])[None]

    u = _rmsnorm(h, mix_norm_pre, BF16)
    p = _matmul(u, w_in_b, F32, *TILE_IN_PROJ, b_is_nk=True, name="in_proj")
    p3 = p.reshape(batch, seq, NP)
    y_rw = _rwkv_mix(p3, cp, gp, mixm, w_up_p, a_up_p, g_up_b).reshape(n, RW)
    y_gdn = _gdn_mix(p3, gdn_conv_w, alog_l, dtb_l, gdn_norm_w[None]).reshape(n, RW)
    merged = _branch_merge(y_rw, y_gdn, w_branch_rwkv, w_branch_gdn, p)
    y_mix = _matmul_wcast(merged, w_mix_out, BF16, name="mix_out")

    mlen = mem.shape[1]
    m = _rmsnorm(mem.reshape(batch * mlen, d), xa_norm_mem, BF16)
    xw = XA_HEADS * XA_HEAD
    kv = _matmul(m, xa_w_kv.astype(BF16), BF16, batch * mlen, xw, name="xa_kv")
    kmem = kv[:, :xw].reshape(batch, mlen, xw)
    vmem = kv[:, xw:].reshape(batch, mlen, xw)
    h, f = _xattn_block(h, y_mix, mix_norm_post, kmem, vmem, xa_w_q.astype(BF16), xa_w_o.astype(BF16),
                        xa_norm_pre, xa_norm_post, mlp_norm_pre, seq)

    hmid = _matmul_wcast(f, mlp_w_up, BF16, relu2=True, name="mlp_up")
    y_mlp = _matmul_kacc(hmid, mlp_w_down, BF16, name="mlp_down")
    h = _resid_norm(h, y_mlp, mlp_norm_post)
    return h.reshape(batch, seq, d)


def kernel(x, mem, mix_norm_pre, mix_norm_post, w_in, rwkv_shift_mix, rwkv_w0, rwkv_w_up, rwkv_a0, rwkv_a_up,
           rwkv_g_up, rwkv_k_k, rwkv_k_a, rwkv_r_k, rwkv_gn_w, rwkv_gn_b, gdn_conv_w, gdn_a_log, gdn_dt_bias,
           gdn_norm_w, w_branch_rwkv, w_branch_gdn, w_mix_out, xa_norm_pre, xa_norm_mem, xa_norm_post, xa_w_q,
           xa_w_kv, xa_w_o, mlp_norm_pre, mlp_norm_post, mlp_w_up, mlp_w_down):
    args = (mix_norm_pre, mix_norm_post, w_in, rwkv_shift_mix, rwkv_w0, rwkv_w_up, rwkv_a0, rwkv_a_up,
            rwkv_g_up, rwkv_k_k, rwkv_k_a, rwkv_r_k, rwkv_gn_w, rwkv_gn_b, gdn_conv_w, gdn_a_log, gdn_dt_bias,
            gdn_norm_w, w_branch_rwkv, w_branch_gdn, w_mix_out, xa_norm_pre, xa_norm_mem, xa_norm_post, xa_w_q,
            xa_w_kv, xa_w_o, mlp_norm_pre, mlp_norm_post, mlp_w_up, mlp_w_down)
    h = x
    for layer in range(mix_norm_pre.shape[0]):
        h = _layer(h, mem, *(a[layer] for a in args))
    return h
```

```python
import functools
import math

import jax
import jax.numpy as jnp
from jax import lax
from jax.experimental import pallas as pl
from jax.experimental.pallas import tpu as pltpu

F32 = jnp.float32
BF16 = jnp.bfloat16

V7X_LANES = 128
V7X_SUBLANES = 8
V7X_VMEM_BYTES = 64 * 1024 * 1024
V7X_VMEM_LIMIT_BYTES = V7X_VMEM_BYTES - 4 * 1024 * 1024

NORM_EPS = 1e-6
L2_EPS = 1e-6
RWKV_GN_EPS = 64e-5
RWKV_HEAD = 64
GDN_HEAD = 128
CHUNK = 64
XA_HEADS = 4
XA_HEAD = 128

RW = 2048
D_MODEL = 4096
LORA = 96
N_GDN_HEADS = RW // GDN_HEAD
OFF_R, OFF_K, OFF_V = 0, RW, 2 * RW
OFF_GQ, OFF_GK, OFF_GV = 3 * RW, 4 * RW, 5 * RW
OFF_Z = 6 * RW
OFF_MISC = 7 * RW
MISC_W = 512
MISC_LW, MISC_LA, MISC_LG = 0, 128, 256
BETA_LANE = LORA
ALPHA_LANE = LORA + N_GDN_HEADS
OFF_GATE_RW = OFF_MISC + MISC_W
OFF_GATE_GDN = OFF_GATE_RW + D_MODEL
NP = OFF_GATE_GDN + D_MODEL

SRC_LORA = 3 * RW
SRC_GDN = SRC_LORA + 2 * LORA + 256
SRC_BA = SRC_GDN + 4 * RW
SRC_GATES = SRC_BA + 2 * N_GDN_HEADS
RELAYOUT_ROWS = 512

TILE_IN_PROJ = (1024, 1536)
TILE_PANEL = (2048, 512)
TILE_BRANCH = (1024, 512)
TILE_MLP_DOWN = (2048, 1024, 2048)
K_CHUNK = 1024
ROW_TILE = 256


def _cparams(sem):
    return pltpu.CompilerParams(dimension_semantics=sem, vmem_limit_bytes=V7X_VMEM_LIMIT_BYTES)


def _dot(a, b):
    return jnp.dot(a, b, preferred_element_type=F32)


def _dot_nt(a, b):
    return lax.dot_general(a, b, (((1,), (1,)), ((), ())), preferred_element_type=F32)


def _dot_tn(a, b):
    return lax.dot_general(a, b, (((0,), (0,)), ((), ())), preferred_element_type=F32)


def _bf(x):
    return x.astype(BF16)


def _hi_lo(x):
    hi = x.astype(BF16)
    return hi, (x - hi.astype(F32)).astype(BF16)


def _mm(a, b):
    return _dot(_bf(a), _bf(b))


def _mm_nt(a, b):
    return _dot_nt(_bf(a), _bf(b))


def _mm_tn(a, b):
    return _dot_tn(_bf(a), _bf(b))


def _mm_xl(a01x2, b):
    hi, lo = _hi_lo(b)
    return _dot(a01x2, jnp.concatenate([hi, lo], axis=0))


def _sigmoid(x):
    return 0.5 * jnp.tanh(0.5 * x) + 0.5


def _softplus(x):
    return jnp.maximum(x, 0.0) + jnp.log1p(jnp.exp(-jnp.abs(x)))


def _iota(shape, dim):
    return lax.broadcasted_iota(jnp.int32, shape, dim)


def _lane_groups(x):
    return [x[:, g * V7X_LANES:(g + 1) * V7X_LANES] for g in range(x.shape[1] // V7X_LANES)]


def _tril_ones_x2(c):
    return jnp.where(_iota((c, 2 * c), 0) >= _iota((c, 2 * c), 1) % c, 1.0, 0.0).astype(BF16)


def _rmsnorm_kernel(x_ref, g_ref, o_ref):
    x = x_ref[...]
    inv = lax.rsqrt(jnp.mean(x * x, axis=-1, keepdims=True) + NORM_EPS)
    o_ref[...] = (x * inv * g_ref[...]).astype(o_ref.dtype)


def _rmsnorm(x, gain, out_dtype):
    n, d = x.shape
    rows = min(ROW_TILE, n)
    return pl.pallas_call(
        _rmsnorm_kernel,
        grid=(n // rows,),
        in_specs=[pl.BlockSpec((rows, d), lambda i: (i, 0)), pl.BlockSpec((1, d), lambda i: (0, 0))],
        out_specs=pl.BlockSpec((rows, d), lambda i: (i, 0)),
        out_shape=jax.ShapeDtypeStruct((n, d), out_dtype),
        compiler_params=_cparams(("parallel",)),
        name="rmsnorm",
    )(x, gain.reshape(1, d))


def _resid_norm_kernel(h_ref, y_ref, g_ref, o_ref):
    y = y_ref[...].astype(F32)
    inv = lax.rsqrt(jnp.mean(y * y, axis=-1, keepdims=True) + NORM_EPS)
    o_ref[...] = h_ref[...] + y * inv * g_ref[...]


def _resid_norm(h, y, gain):
    n, d = h.shape
    row = pl.BlockSpec((ROW_TILE, d), lambda i: (i, 0))
    return pl.pallas_call(
        _resid_norm_kernel,
        grid=(n // ROW_TILE,),
        in_specs=[row, row, pl.BlockSpec((1, d), lambda i: (0, 0))],
        out_specs=row,
        out_shape=jax.ShapeDtypeStruct((n, d), F32),
        compiler_params=_cparams(("parallel",)),
        name="resid_norm",
    )(h, y, gain.reshape(1, d))


def _mm_kernel(a_ref, b_ref, o_ref, *, b_is_nk):
    acc = _dot_nt(a_ref[...], b_ref[...]) if b_is_nk else _dot(a_ref[...], b_ref[...])
    o_ref[...] = acc.astype(o_ref.dtype)


def _matmul(a, b, out_dtype, tm, tn, b_is_nk=False, name="matmul"):
    m, k = a.shape
    n = b.shape[0] if b_is_nk else b.shape[1]
    b_spec = pl.BlockSpec((tn, k), lambda i, j: (j, 0)) if b_is_nk else pl.BlockSpec((k, tn), lambda i, j: (0, j))
    return pl.pallas_call(
        functools.partial(_mm_kernel, b_is_nk=b_is_nk),
        grid=(m // tm, n // tn),
        in_specs=[pl.BlockSpec((tm, k), lambda i, j: (i, 0)), b_spec],
        out_specs=pl.BlockSpec((tm, tn), lambda i, j: (i, j)),
        out_shape=jax.ShapeDtypeStruct((m, n), out_dtype),
        compiler_params=_cparams(("parallel", "arbitrary")),
        name=name,
    )(a, b)


def _mm_wcast_kernel(a_ref, b_ref, o_ref, b_bf16, *, relu2):
    def finish(acc):
        if relu2:
            acc = jnp.square(jnp.maximum(acc, 0.0))
        o_ref[...] = acc.astype(o_ref.dtype)

    @pl.when(pl.program_id(1) == 0)
    def _():
        acc = None
        for k0 in range(0, a_ref.shape[1], K_CHUNK):
            b_bf16[k0:k0 + K_CHUNK, :] = b_ref[k0:k0 + K_CHUNK, :].astype(BF16)
            part = _dot(a_ref[:, k0:k0 + K_CHUNK], b_bf16[k0:k0 + K_CHUNK, :])
            acc = part if acc is None else acc + part
        finish(acc)

    @pl.when(pl.program_id(1) != 0)
    def _():
        finish(_dot(a_ref[...], b_bf16[...]))


def _matmul_wcast(a, b, out_dtype, relu2=False, name="matmul_wcast"):
    m, k = a.shape
    _, n = b.shape
    tm, tn = TILE_PANEL
    return pl.pallas_call(
        functools.partial(_mm_wcast_kernel, relu2=relu2),
        grid=(n // tn, m // tm),
        in_specs=[pl.BlockSpec((tm, k), lambda j, i: (i, 0)), pl.BlockSpec((k, tn), lambda j, i: (0, j))],
        out_specs=pl.BlockSpec((tm, tn), lambda j, i: (i, j)),
        out_shape=jax.ShapeDtypeStruct((m, n), out_dtype),
        scratch_shapes=[pltpu.VMEM((k, tn), BF16)],
        compiler_params=_cparams(("parallel", "arbitrary")),
        name=name,
    )(a, b)


def _mm_kacc_kernel(a_ref, b_ref, o_ref, acc_ref):
    l = pl.program_id(2)

    @pl.when(l == 0)
    def _():
        acc_ref[...] = jnp.zeros_like(acc_ref)

    acc = None
    for k0 in range(0, a_ref.shape[1], K_CHUNK):
        part = _dot(a_ref[:, k0:k0 + K_CHUNK], b_ref[k0:k0 + K_CHUNK, :].astype(BF16))
        acc = part if acc is None else acc + part
    acc_ref[...] += acc

    @pl.when(l == pl.num_programs(2) - 1)
    def _():
        o_ref[...] = acc_ref[...].astype(o_ref.dtype)


def _matmul_kacc(a, b, out_dtype, name="matmul_kacc"):
    m, k = a.shape
    _, n = b.shape
    tm, tn, tk = TILE_MLP_DOWN
    return pl.pallas_call(
        _mm_kacc_kernel,
        grid=(m // tm, n // tn, k // tk),
        in_specs=[pl.BlockSpec((tm, tk), lambda i, j, l: (i, l)), pl.BlockSpec((tk, tn), lambda i, j, l: (l, j))],
        out_specs=pl.BlockSpec((tm, tn), lambda i, j, l: (i, j)),
        out_shape=jax.ShapeDtypeStruct((m, n), out_dtype),
        scratch_shapes=[pltpu.VMEM((tm, tn), F32)],
        compiler_params=_cparams(("parallel", "parallel", "arbitrary")),
        name=name,
    )(a, b)


def _branch_kernel(yr_ref, yg_ref, wr_ref, wg_ref, gr_ref, gg_ref, o_ref, wr_bf16, wg_bf16):
    @pl.when(pl.program_id(1) == 0)
    def _():
        wr_bf16[...] = wr_ref[...].astype(BF16)
        wg_bf16[...] = wg_ref[...].astype(BF16)

    br = _dot(yr_ref[...], wr_bf16[...])
    bg = _dot(yg_ref[...], wg_bf16[...])
    o_ref[...] = (_sigmoid(gr_ref[...]) * br + _sigmoid(gg_ref[...]) * bg).astype(o_ref.dtype)


def _branch_merge(y_rw, y_gdn, w_rw, w_gdn, p):
    m, k = y_rw.shape
    n = w_rw.shape[1]
    tm, tn = TILE_BRANCH
    jr, jg = OFF_GATE_RW // tn, OFF_GATE_GDN // tn
    return pl.pallas_call(
        _branch_kernel,
        grid=(n // tn, m // tm),
        in_specs=[
            pl.BlockSpec((tm, k), lambda j, i: (i, 0)),
            pl.BlockSpec((tm, k), lambda j, i: (i, 0)),
            pl.BlockSpec((k, tn), lambda j, i: (0, j)),
            pl.BlockSpec((k, tn), lambda j, i: (0, j)),
            pl.BlockSpec((tm, tn), lambda j, i: (i, jr + j)),
            pl.BlockSpec((tm, tn), lambda j, i: (i, jg + j)),
        ],
        out_specs=pl.BlockSpec((tm, tn), lambda j, i: (i, j)),
        out_shape=jax.ShapeDtypeStruct((m, n), BF16),
        scratch_shapes=[pltpu.VMEM((k, tn), BF16)] * 2,
        compiler_params=_cparams(("parallel", "arbitrary")),
        name="branch_merge",
    )(y_rw, y_gdn, w_rw, w_gdn, p, p)


def _batch_lanes(ref):
    return jnp.concatenate([ref[b] for b in range(ref.shape[0])], axis=1)


def _tile_lanes(x, n):
    return jnp.concatenate([x] * n, axis=1)


def _token_shift(x_ref, prev_ref, first, mix):
    x = _batch_lanes(x_ref)
    before = jnp.where(first, 0.0, _batch_lanes(prev_ref)[V7X_SUBLANES - 1:V7X_SUBLANES, :])
    prev = jnp.where(_iota(x.shape, 0) == 0, before, pltpu.roll(x, 1, axis=0))
    return x + (prev - x) * mix


def _expand2(x, head0):
    return jnp.concatenate([jnp.where(head0, x, 0.0), jnp.where(head0, 0.0, x)], axis=0)


def _rwkv_kernel(r_ref, k_ref, v_ref, pr_ref, pk_ref, pv_ref, misc_ref, pmisc_ref,
                 cp_ref, gp_ref, mixm_ref, wup_ref, aup_ref, gup_ref, o_ref, s_ref):
    c = CHUNK
    w = V7X_LANES
    batch, _, lw = r_ref.shape
    ci = pl.program_id(0)

    @pl.when(ci == 0)
    def _():
        s_ref[...] = jnp.zeros_like(s_ref)

    first = ci == 0
    mix_r, mix_k, mix_v, w0, a0, k_k, k_a, r_k = (_tile_lanes(cp_ref[i:i + 1, :], batch) for i in range(8))
    gn_w, gn_b = _tile_lanes(gp_ref[0:1, :], batch), _tile_lanes(gp_ref[1:2, :], batch)

    misc_m = _token_shift(misc_ref, pmisc_ref, first, _tile_lanes(mixm_ref[...], batch))
    xr = _token_shift(r_ref, pr_ref, first, mix_r)
    xk = _token_shift(k_ref, pk_ref, first, mix_k)
    xv = _token_shift(v_ref, pv_ref, first, mix_v)

    def lora(fn, lo, hi, w_ref):
        return jnp.concatenate([_dot(_bf(fn(misc_m[:, b * MISC_W + lo:b * MISC_W + hi])), w_ref[...])
                                for b in range(batch)], axis=1)

    wlin = w0 + lora(jnp.tanh, MISC_LW, MISC_LA, wup_ref)
    logw = -math.exp(-0.5) * _sigmoid(wlin)
    a = _sigmoid(a0 + lora(lambda t: t, MISC_LA, MISC_LG, aup_ref))
    gate = lora(_sigmoid, MISC_LG, MISC_W, gup_ref)

    same_head = (_iota((w, w), 0) // RWKV_HEAD) == (_iota((w, w), 1) // RWKV_HEAD)
    bones = jnp.where(same_head, 1.0, 0.0).astype(BF16)
    eye = jnp.where(_iota((w, w), 0) == _iota((w, w), 1), 1.0, 0.0)

    def head_sums(x):
        n = x.shape[1] // w
        s = _dot(_bf(jnp.concatenate(_lane_groups(x), axis=0)), bones)
        return jnp.concatenate([s[g * c:(g + 1) * c] for g in range(n)], axis=1)

    kkr = xk * k_k
    kkn = kkr * lax.rsqrt(head_sums(kkr * kkr) + L2_EPS)
    k2 = xk * (1.0 + (a - 1.0) * k_a)
    am = -kkn
    bm = kkn * a

    cum = _mm_xl(_tril_ones_x2(c), logw)
    cum_ex = cum - logw
    c_last = cum[c - 1:c, :]
    c_mid = cum[c // 2 - 1:c // 2, :]
    e_out = jnp.exp(c_mid - cum)
    e_last = jnp.exp(c_last - cum)
    rt = xr * jnp.exp(cum - c_mid)
    at = am * jnp.exp(cum_ex - c_mid)
    bt = bm * e_out
    kt = k2 * e_out
    a_abs = am * jnp.exp(cum_ex)
    r_abs = xr * jnp.exp(cum)
    bh = bm * e_last
    kh = k2 * e_last
    w_c = jnp.exp(c_last)

    head0 = _iota((c, w), 1) < RWKV_HEAD
    head0_2 = _iota((c, 2 * w), 1) % w < RWKV_HEAD
    wi = _iota((c, 2 * c), 0)
    wj = _iota((c, 2 * c), 1) % c
    strict = wi > wj
    incl = wi >= wj

    gs = range(batch * lw // w)
    G = _lane_groups
    at_g, rt_g, bt_g, kt_g, xv_g = G(at), G(rt), G(bt), G(kt), G(xv)
    aa = [_mm_nt(jnp.concatenate([at_g[g], rt_g[g]], axis=0),
                 jnp.concatenate([_expand2(bt_g[g], head0), _expand2(kt_g[g], head0)], axis=0)) for g in gs]
    a_ab = [jnp.where(strict, aa[g][0:c, 0:2 * c], 0.0) for g in gs]
    a_ak = [jnp.where(strict, aa[g][0:c, 2 * c:4 * c], 0.0) for g in gs]
    a_rb = [jnp.where(incl, aa[g][c:2 * c, 0:2 * c], 0.0) for g in gs]
    a_rk = [jnp.where(incl, aa[g][c:2 * c, 2 * c:4 * c], 0.0) for g in gs]

    xv_e = [_expand2(xv_g[g], head0) for g in gs]
    akv = [_mm(a_ak[g], xv_e[g]) for g in gs]
    eye_w = jnp.where(wi == wj, 1.0, 0.0)
    tw = [eye_w + a_ab[g] for g in gs]
    pw = [_mm(a_ab[g], _expand2(a_ab[g], head0)) for g in gs]
    levels = int(math.log2(c))
    for lvl in range(1, levels):
        if lvl + 1 < levels:
            pt = [_mm(pw[g], _expand2(jnp.concatenate([pw[g], tw[g]], axis=1), head0_2)) for g in gs]
            pw = [pt[g][:, 0:w] for g in gs]
            tw = [tw[g] + pt[g][:, w:2 * w] for g in gs]
        else:
            tw = [tw[g] + _mm(pw[g], _expand2(tw[g], head0)) for g in gs]
    a_abs_g = G(a_abs)
    x = [_mm(tw[g], _expand2(jnp.concatenate([a_abs_g[g], akv[g]], axis=1), head0_2)) for g in gs]
    wm = [x[g][:, 0:w] for g in gs]
    u0 = [x[g][:, w:2 * w] for g in gs]

    bh_g, kh_g, wc_g, rabs_g = G(bh), G(kh), G(w_c), G(r_abs)
    m_mat = [jnp.where(same_head, _mm_tn(bh_g[g], wm[g]), 0.0) + eye * wc_g[g] for g in gs]
    n_mat = [jnp.where(same_head, _mm_tn(jnp.concatenate([bh_g[g], kh_g[g]], axis=0),
                                         jnp.concatenate([u0[g], xv_g[g]], axis=0)), 0.0) for g in gs]
    p_mat = [rabs_g[g] + _mm(a_rb[g], _expand2(wm[g], head0)) for g in gs]
    q_mat = [_mm(jnp.concatenate([a_rb[g], a_rk[g]], axis=1),
                 jnp.concatenate([_expand2(u0[g], head0), xv_e[g]], axis=0)) for g in gs]

    h0 = [_bf(s_ref[g]) for g in gs]
    y = jnp.concatenate([_dot(_bf(p_mat[g]), h0[g]) + q_mat[g] for g in gs], axis=1)
    for g in gs:
        s_ref[g] = _dot(_bf(m_mat[g]), h0[g]) + n_mat[g]

    inv_n = 1.0 / RWKV_HEAD
    d = y - head_sums(y) * inv_n
    var = head_sums(d * d) * inv_n
    yn = d * lax.rsqrt(var + RWKV_GN_EPS) * gn_w + gn_b
    bonus = head_sums(xr * k2 * r_k) * xv
    out = ((yn + bonus) * gate).astype(o_ref.dtype)
    for b in range(batch):
        o_ref[b] = out[:, b * lw:(b + 1) * lw]


def _chunk_specs(batch, c):
    def cur(off, width):
        assert off % width == 0
        return pl.BlockSpec((batch, c, width), lambda t: (0, t, off // width))

    def prev(off, width):
        return pl.BlockSpec((batch, V7X_SUBLANES, width),
                            lambda t: (0, jnp.maximum(t * (c // V7X_SUBLANES) - 1, 0), off // width))

    return cur, prev


def _rwkv_mix(p, cp, gp, mixm, w_up_p, a_up_p, g_up):
    batch, seq, _ = p.shape
    c = CHUNK
    cur, prev = _chunk_specs(batch, c)
    whole = lambda x: pl.BlockSpec(x.shape, lambda t: (0, 0))
    return pl.pallas_call(
        _rwkv_kernel,
        grid=(seq // c,),
        in_specs=[
            cur(OFF_R, RW), cur(OFF_K, RW), cur(OFF_V, RW), prev(OFF_R, RW), prev(OFF_K, RW), prev(OFF_V, RW),
            cur(OFF_MISC, MISC_W), prev(OFF_MISC, MISC_W),
            whole(cp), whole(gp), whole(mixm), whole(w_up_p), whole(a_up_p), whole(g_up),
        ],
        out_specs=pl.BlockSpec((batch, c, RW), lambda t: (0, t, 0)),
        out_shape=jax.ShapeDtypeStruct((batch, seq, RW), BF16),
        scratch_shapes=[pltpu.VMEM((batch * RW // V7X_LANES, V7X_LANES, V7X_LANES), F32)],
        compiler_params=_cparams(("arbitrary",)),
        name="rwkv7_mix",
    )(p, p, p, p, p, p, p, p, cp, gp, mixm, w_up_p, a_up_p, g_up)


def _gdn_kernel(q_ref, k_ref, v_ref, pq_ref, pk_ref, pv_ref, z_ref, misc_ref,
                cwq_ref, cwk_ref, cwv_ref, alog_ref, dtb_ref, nw_ref, o_ref, s_ref):
    c = CHUNK
    w = V7X_LANES
    batch, _, lw = q_ref.shape
    assert lw == N_GDN_HEADS * w
    ci = pl.program_id(0)
    heads = batch * N_GDN_HEADS
    hs = range(heads)
    G = _lane_groups

    @pl.when(ci == 0)
    def _():
        s_ref[...] = jnp.zeros_like(s_ref)

    first = ci == 0

    def conv_silu(x_ref, prev_ref, cw_ref):
        x = _batch_lanes(x_ref)
        prev8 = jnp.where(first, 0.0, _batch_lanes(prev_ref))
        cw = _tile_lanes(cw_ref[...], batch)
        row = _iota(x.shape, 0)
        taps = cw.shape[0]
        acc = x * cw[taps - 1:taps, :]
        for s in range(1, taps):
            tail = jnp.tile(pltpu.roll(prev8, s, axis=0), (c // V7X_SUBLANES, 1))
            acc = acc + jnp.where(row < s, tail, pltpu.roll(x, s, axis=0)) * cw[taps - 1 - s:taps - s, :]
        return acc * _sigmoid(acc)

    def head_sums(x):
        return jnp.concatenate([jnp.broadcast_to(jnp.sum(g, axis=-1, keepdims=True), g.shape) for g in G(x)], axis=1)

    q = conv_silu(q_ref, pq_ref, cwq_ref)
    k = conv_silu(k_ref, pk_ref, cwk_ref)
    v = conv_silu(v_ref, pv_ref, cwv_ref)
    q = q * lax.rsqrt(head_sums(q * q) + L2_EPS) * (GDN_HEAD ** -0.5)
    k = k * lax.rsqrt(head_sums(k * k) + L2_EPS)

    lane2 = _iota((2 * w, w), 0) % w
    lane2r = _iota((c, 2 * w), 1) % w
    bcol, gcol, grow = [], [], []
    for b in range(batch):
        bm = misc_ref[b, :, MISC_LW:MISC_LA]
        beta_hl = jnp.concatenate(_hi_lo(_sigmoid(bm)), axis=1)
        g_full = -jnp.exp(alog_ref[...]) * _softplus(bm + dtb_ref[...])
        gc_hl = jnp.concatenate(_hi_lo(_mm_xl(_tril_ones_x2(c), g_full)), axis=1)
        for h in range(N_GDN_HEADS):
            bcol.append(_dot(beta_hl, jnp.where(lane2 == BETA_LANE + h, 1.0, 0.0).astype(BF16)))
            gcol.append(_dot(gc_hl, jnp.where(lane2 == ALPHA_LANE + h, 1.0, 0.0).astype(BF16)))
            grow.append(_dot_nt(jnp.where(lane2r == ALPHA_LANE + h, 1.0, 0.0).astype(BF16), gc_hl))

    ti = _iota((c, c), 0)
    tj = _iota((c, c), 1)
    causal = ti >= tj
    strict = ti > tj
    decay = [jnp.where(causal, jnp.exp(jnp.where(causal, gcol[h][:, 0:c] - grow[h], 0.0)), 0.0) for h in hs]

    q_g, k_g, v_g = G(q), G(k), G(v)
    kb = [k_g[h] * bcol[h] for h in hs]
    kk = [_mm_nt(jnp.concatenate([kb[h], q_g[h]], axis=0), k_g[h]) for h in hs]
    lneg = [jnp.where(strict, -kk[h][0:c] * decay[h], 0.0) for h in hs]
    a_intra = [kk[h][c:2 * c] * decay[h] for h in hs]

    egc = [jnp.exp(gcol[h]) for h in hs]
    g_last = [gcol[h][c - 1:c, :] for h in hs]
    lane_t = _iota((c, w), 1) >= c
    eye_t = jnp.where(_iota((c, w), 1) == _iota((c, w), 0) + c, 1.0, 0.0)
    zt = [jnp.concatenate([lneg[h], jnp.zeros((c, c), F32)], axis=1) + eye_t for h in hs]
    for _ in range(int(math.log2(c))):
        zb = [_bf(zt[h]) for h in hs]
        zt = [_dot(zb[h][:, 0:c], zb[h]) + jnp.where(lane_t, zt[h], 0.0) for h in hs]
    zeros_x = jnp.zeros((c, 2 * w), F32)
    x = [_mm(zt[h], jnp.concatenate([zeros_x, jnp.concatenate([v_g[h] * bcol[h], kb[h] * egc[h]], axis=1)], axis=0))
         for h in hs]
    uu = [x[h][:, 0:w] for h in hs]
    ww = [x[h][:, w:2 * w] for h in hs]

    kd = [k_g[h] * jnp.exp(g_last[h] - gcol[h]) for h in hs]
    eye = jnp.where(_iota((w, w), 0) == _iota((w, w), 1), 1.0, 0.0)
    m_mat = [eye * jnp.exp(g_last[h]) - _mm_tn(kd[h], ww[h]) for h in hs]
    n_mat = [_mm_tn(kd[h], uu[h]) for h in hs]
    p_mat = [q_g[h] * egc[h] - _mm(a_intra[h], ww[h]) for h in hs]
    q_mat = [_mm(a_intra[h], uu[h]) for h in hs]

    s0 = [_bf(s_ref[h]) for h in hs]
    o = jnp.concatenate([_dot(_bf(p_mat[h]), s0[h]) + q_mat[h] for h in hs], axis=1)
    for h in hs:
        s_ref[h] = _dot(_bf(m_mat[h]), s0[h]) + n_mat[h]

    o = o * lax.rsqrt(head_sums(o * o) * (1.0 / GDN_HEAD) + NORM_EPS) * _tile_lanes(nw_ref[...], heads)
    z = _batch_lanes(z_ref)
    out = (o * (z * _sigmoid(z))).astype(o_ref.dtype)
    for b in range(batch):
        o_ref[b] = out[:, b * lw:(b + 1) * lw]


def _gdn_mix(p, conv_w, alog_l, dtb_l, norm_w):
    batch, seq, _ = p.shape
    c = CHUNK
    cur, prev = _chunk_specs(batch, c)

    def cw(part):
        return pl.BlockSpec((conv_w.shape[0], RW), lambda t: (0, part))

    lane_vec = pl.BlockSpec((1, V7X_LANES), lambda t: (0, 0))
    return pl.pallas_call(
        _gdn_kernel,
        grid=(seq // c,),
        in_specs=[
            cur(OFF_GQ, RW), cur(OFF_GK, RW), cur(OFF_GV, RW), prev(OFF_GQ, RW), prev(OFF_GK, RW), prev(OFF_GV, RW),
            cur(OFF_Z, RW), cur(OFF_MISC, MISC_W),
            cw(0), cw(1), cw(2), lane_vec, lane_vec, lane_vec,
        ],
        out_specs=pl.BlockSpec((batch, c, RW), lambda t: (0, t, 0)),
        out_shape=jax.ShapeDtypeStruct((batch, seq, RW), BF16),
        scratch_shapes=[pltpu.VMEM((batch * N_GDN_HEADS, V7X_LANES, V7X_LANES), F32)],
        compiler_params=_cparams(("arbitrary",)),
        name="gdn_mix",
    )(p, p, p, p, p, p, p, p, conv_w, conv_w, conv_w, alog_l, dtb_l, norm_w)


def _xattn_kernel(x_ref, ymix_ref, gmix_ref, k_ref, v_ref, wq_ref, wo_ref, gpre_ref, gpost_ref, gnext_ref,
                  h_out_ref, f_out_ref):
    ym = ymix_ref[...].astype(F32)
    h = x_ref[...] + ym * lax.rsqrt(jnp.mean(ym * ym, axis=-1, keepdims=True) + NORM_EPS) * gmix_ref[...]
    cn = (h * lax.rsqrt(jnp.mean(h * h, axis=-1, keepdims=True) + NORM_EPS) * gpre_ref[...]).astype(BF16)
    q = _dot(cn, wq_ref[...]).astype(BF16)
    kk = k_ref[0]
    vv = v_ref[0]
    outs = []
    for hh in range(XA_HEADS):
        sl = slice(hh * XA_HEAD, (hh + 1) * XA_HEAD)
        s = _dot_nt(q[:, sl], kk[:, sl]) * (XA_HEAD ** -0.5)
        s = s - jnp.max(s, axis=-1, keepdims=True)
        e = jnp.exp(s)
        pr = e / jnp.sum(e, axis=-1, keepdims=True)
        outs.append(_dot(pr.astype(BF16), vv[:, sl]))
    o = jnp.concatenate(outs, axis=1).astype(BF16)
    y = _dot(o, wo_ref[...])
    h2 = h + y * lax.rsqrt(jnp.mean(y * y, axis=-1, keepdims=True) + NORM_EPS) * gpost_ref[...]
    h_out_ref[...] = h2
    f_out_ref[...] = (h2 * lax.rsqrt(jnp.mean(h2 * h2, axis=-1, keepdims=True) + NORM_EPS)
                      * gnext_ref[...]).astype(f_out_ref.dtype)


def _xattn_block(x, y_mix, g_mix, kmem, vmem, w_q, w_o, g_pre, g_post, g_next, seq):
    n, d = x.shape
    mlen, xw = kmem.shape[1], kmem.shape[2]
    per_b = seq // ROW_TILE
    row = pl.BlockSpec((ROW_TILE, d), lambda i: (i, 0))
    vec = pl.BlockSpec((1, d), lambda i: (0, 0))
    mem = pl.BlockSpec((1, mlen, xw), lambda i: (i // per_b, 0, 0))
    once = pl.Buffered(1)
    return pl.pallas_call(
        _xattn_kernel,
        grid=(n // ROW_TILE,),
        in_specs=[row, row, vec, mem, mem, pl.BlockSpec((d, xw), lambda i: (0, 0), pipeline_mode=once),
                  pl.BlockSpec((xw, d), lambda i: (0, 0), pipeline_mode=once), vec, vec, vec],
        out_specs=[row, row],
        out_shape=[jax.ShapeDtypeStruct((n, d), F32), jax.ShapeDtypeStruct((n, d), BF16)],
        compiler_params=_cparams(("parallel",)),
        name="xattn_block",
    )(x, y_mix, g_mix.reshape(1, d), kmem, vmem, w_q, w_o, g_pre.reshape(1, d), g_post.reshape(1, d),
      g_next.reshape(1, d))


def _relayout_kernel(x_ref, misc_ref, o_ref):
    @pl.when(pl.program_id(0) != OFF_MISC // RELAYOUT_ROWS)
    def _():
        o_ref[...] = x_ref[...].astype(o_ref.dtype)

    @pl.when(pl.program_id(0) == OFF_MISC // RELAYOUT_ROWS)
    def _():
        o_ref[...] = misc_ref[...].astype(o_ref.dtype)


def _relayout_w_in_t(w_in_t):
    n_in, d = w_in_t.shape
    rb = RELAYOUT_ROWS
    assert OFF_GQ % rb == 0 and OFF_MISC % rb == 0 and OFF_GATE_RW % rb == 0 and MISC_W == rb
    zeros = jnp.zeros((MISC_LG - MISC_LA - LORA, d), w_in_t.dtype)
    misc = jnp.concatenate([w_in_t[SRC_LORA:SRC_LORA + LORA], w_in_t[SRC_BA:SRC_BA + 2 * N_GDN_HEADS],
                            w_in_t[SRC_LORA + LORA:SRC_LORA + 2 * LORA], zeros,
                            w_in_t[SRC_LORA + 2 * LORA:SRC_GDN]], axis=0)

    def src_row(j):
        r = j * rb
        src = jnp.where(r < OFF_GQ, r, jnp.where(r < OFF_MISC, r + (SRC_GDN - OFF_GQ),
                                                 jnp.where(r < OFF_GATE_RW, 0, r + (SRC_GATES - OFF_GATE_RW))))
        return pl.multiple_of(src, math.gcd(rb, SRC_GDN, SRC_GATES))

    return pl.pallas_call(
        _relayout_kernel,
        grid=(NP // rb,),
        in_specs=[pl.BlockSpec((pl.Element(rb), pl.Element(d)), lambda j: (src_row(j), 0)),
                  pl.BlockSpec((rb, d), lambda j: (0, 0))],
        out_specs=pl.BlockSpec((rb, d), lambda j: (j, 0)),
        out_shape=jax.ShapeDtypeStruct((NP, d), BF16),
        compiler_params=_cparams(("parallel",)),
        name="relayout_w_in",
    )(w_in_t, misc)


def _layer(x, mem, mix_norm_pre, mix_norm_post, w_in, rwkv_shift_mix, rwkv_w0, rwkv_w_up, rwkv_a0, rwkv_a_up,
           rwkv_g_up, rwkv_k_k, rwkv_k_a, rwkv_r_k, rwkv_gn_w, rwkv_gn_b, gdn_conv_w, gdn_a_log, gdn_dt_bias,
           gdn_norm_w, w_branch_rwkv, w_branch_gdn, w_mix_out, xa_norm_pre, xa_norm_mem, xa_norm_post, xa_w_q,
           xa_w_kv, xa_w_o, mlp_norm_pre, mlp_norm_post, mlp_w_up, mlp_w_down):
    batch, seq, d = x.shape
    n = batch * seq
    h = x.reshape(n, d)

    w_in_b = _relayout_w_in_t(jnp.swapaxes(w_in, 0, 1))
    mix = rwkv_shift_mix
    o = 3 * RW
    cp = jnp.stack([mix[0:RW], mix[RW:2 * RW], mix[2 * RW:o], rwkv_w0, rwkv_a0, rwkv_k_k, rwkv_k_a,
                    rwkv_r_k.reshape(RW)], axis=0)
    gp = jnp.concatenate([rwkv_gn_w[None], rwkv_gn_b[None], jnp.zeros((V7X_SUBLANES - 2, RW), F32)], axis=0)
    zeros = lambda k: jnp.zeros((k,), F32)
    lora_pad = MISC_LA - LORA
    mixm = jnp.concatenate([mix[o:o + LORA], zeros(lora_pad), mix[o + LORA:o + 2 * LORA], zeros(lora_pad),
                            mix[o + 2 * LORA:o + 2 * LORA + 256]])[None]
    w_up_p = jnp.concatenate([rwkv_w_up, jnp.zeros((lora_pad, RW), F32)], axis=0).astype(BF16)
    a_up_p = jnp.concatenate([rwkv_a_up, jnp.zeros((lora_pad, RW), F32)], axis=0).astype(BF16)
    g_up_b = rwkv_g_up.astype(BF16)
    alog_l = jnp.concatenate([zeros(ALPHA_LANE), gdn_a_log])[None]
    dtb_l = jnp.concatenate([zeros(ALPHA_LANE), gdn_dt_bias])[None]

    u = _rmsnorm(h, mix_norm_pre, BF16)
    p = _matmul(u, w_in_b, F32, *TILE_IN_PROJ, b_is_nk=True, name="in_proj")
    p3 = p.reshape(batch, seq, NP)
    y_rw = _rwkv_mix(p3, cp, gp, mixm, w_up_p, a_up_p, g_up_b).reshape(n, RW)
    y_gdn = _gdn_mix(p3, gdn_conv_w, alog_l, dtb_l, gdn_norm_w[None]).reshape(n, RW)
    merged = _branch_merge(y_rw, y_gdn, w_branch_rwkv, w_branch_gdn, p)
    y_mix = _matmul_wcast(merged, w_mix_out, BF16, name="mix_out")

    mlen = mem.shape[1]
    m = _rmsnorm(mem.reshape(batch * mlen, d), xa_norm_mem, BF16)
    xw = XA_HEADS * XA_HEAD
    kv = _matmul(m, xa_w_kv.astype(BF16), BF16, batch * mlen, xw, name="xa_kv")
    kmem = kv[:, :xw].reshape(batch, mlen, xw)
    vmem = kv[:, xw:].reshape(batch, mlen, xw)
    h, f = _xattn_block(h, y_mix, mix_norm_post, kmem, vmem, xa_w_q.astype(BF16), xa_w_o.astype(BF16),
                        xa_norm_pre, xa_norm_post, mlp_norm_pre, seq)

    hmid = _matmul_wcast(f, mlp_w_up, BF16, relu2=True, name="mlp_up")
    y_mlp = _matmul_kacc(hmid, mlp_w_down, BF16, name="mlp_down")
    h = _resid_norm(h, y_mlp, mlp_norm_post)
    return h.reshape(batch, seq, d)


def kernel(x, mem, mix_norm_pre, mix_norm_post, w_in, rwkv_shift_mix, rwkv_w0, rwkv_w_up, rwkv_a0, rwkv_a_up,
           rwkv_g_up, rwkv_k_k, rwkv_k_a, rwkv_r_k, rwkv_gn_w, rwkv_gn_b, gdn_conv_w, gdn_a_log, gdn_dt_bias,
           gdn_norm_w, w_branch_rwkv, w_branch_gdn, w_mix_out, xa_norm_pre, xa_norm_mem, xa_norm_post, xa_w_q,
           xa_w_kv, xa_w_o, mlp_norm_pre, mlp_norm_post, mlp_w_up, mlp_w_down):
    args = (mix_norm_pre, mix_norm_post, w_in, rwkv_shift_mix, rwkv_w0, rwkv_w_up, rwkv_a0, rwkv_a_up,
            rwkv_g_up, rwkv_k_k, rwkv_k_a, rwkv_r_k, rwkv_gn_w, rwkv_gn_b, gdn_conv_w, gdn_a_log, gdn_dt_bias,
            gdn_norm_w, w_branch_rwkv, w_branch_gdn, w_mix_out, xa_norm_pre, xa_norm_mem, xa_norm_post, xa_w_q,
            xa_w_kv, xa_w_o, mlp_norm_pre, mlp_norm_post, mlp_w_up, mlp_w_down)
    h = x
    for layer in range(mix_norm_pre.shape[0]):
        h = _layer(h, mem, *(a[layer] for a in args))
    return h
```

```python
import functools
import math

import jax
import jax.numpy as jnp
from jax import lax
from jax.experimental import pallas as pl
from jax.experimental.pallas import tpu as pltpu

F32 = jnp.float32
BF16 = jnp.bfloat16

V7X_LANES = 128
V7X_SUBLANES = 8
V7X_VMEM_BYTES = 64 * 1024 * 1024
V7X_VMEM_LIMIT_BYTES = V7X_VMEM_BYTES - 4 * 1024 * 1024

NORM_EPS = 1e-6
L2_EPS = 1e-6
RWKV_GN_EPS = 64e-5
RWKV_HEAD = 64
GDN_HEAD = 128
CHUNK = 64
XA_HEADS = 4
XA_HEAD = 128

RW = 2048
D_MODEL = 4096
LORA = 96
N_GDN_HEADS = RW // GDN_HEAD
OFF_R, OFF_K, OFF_V = 0, RW, 2 * RW
OFF_GQ, OFF_GK, OFF_GV = 3 * RW, 4 * RW, 5 * RW
OFF_Z = 6 * RW
OFF_MISC = 7 * RW
MISC_W = 512
MISC_LW, MISC_LA, MISC_LG = 0, 128, 256
BETA_LANE = LORA
ALPHA_LANE = LORA + N_GDN_HEADS
OFF_GATE_RW = OFF_MISC + MISC_W
OFF_GATE_GDN = OFF_GATE_RW + D_MODEL
NP = OFF_GATE_GDN + D_MODEL

SRC_LORA = 3 * RW
SRC_GDN = SRC_LORA + 2 * LORA + 256
SRC_BA = SRC_GDN + 4 * RW
SRC_GATES = SRC_BA + 2 * N_GDN_HEADS
RELAYOUT_ROWS = 512

TILE_IN_PROJ = (1024, 1536)
TILE_PANEL = (2048, 512)
TILE_BRANCH = (1024, 512)
TILE_MLP_DOWN = (2048, 1024, 2048)
K_CHUNK = 1024
ROW_TILE = 256


def _cparams(sem):
    return pltpu.CompilerParams(dimension_semantics=sem, vmem_limit_bytes=V7X_VMEM_LIMIT_BYTES)


def _dot(a, b):
    return jnp.dot(a, b, preferred_element_type=F32)


def _dot_nt(a, b):
    return lax.dot_general(a, b, (((1,), (1,)), ((), ())), preferred_element_type=F32)


def _dot_tn(a, b):
    return lax.dot_general(a, b, (((0,), (0,)), ((), ())), preferred_element_type=F32)


def _bf(x):
    return x.astype(BF16)


def _hi_lo(x):
    hi = x.astype(BF16)
    return hi, (x - hi.astype(F32)).astype(BF16)


def _mm(a, b):
    return _dot(_bf(a), _bf(b))


def _mm_nt(a, b):
    return _dot_nt(_bf(a), _bf(b))


def _mm_tn(a, b):
    return _dot_tn(_bf(a), _bf(b))


def _mm_xl(a01x2, b):
    hi, lo = _hi_lo(b)
    return _dot(a01x2, jnp.concatenate([hi, lo], axis=0))


def _sigmoid(x):
    return 0.5 * jnp.tanh(0.5 * x) + 0.5


def _softplus(x):
    return jnp.maximum(x, 0.0) + jnp.log1p(jnp.exp(-jnp.abs(x)))


def _iota(shape, dim):
    return lax.broadcasted_iota(jnp.int32, shape, dim)


def _lane_groups(x):
    return [x[:, g * V7X_LANES:(g + 1) * V7X_LANES] for g in range(x.shape[1] // V7X_LANES)]


def _tril_ones_x2(c):
    return jnp.where(_iota((c, 2 * c), 0) >= _iota((c, 2 * c), 1) % c, 1.0, 0.0).astype(BF16)


def _rmsnorm_kernel(x_ref, g_ref, o_ref):
    x = x_ref[...]
    inv = lax.rsqrt(jnp.mean(x * x, axis=-1, keepdims=True) + NORM_EPS)
    o_ref[...] = (x * inv * g_ref[...]).astype(o_ref.dtype)


def _rmsnorm(x, gain, out_dtype):
    n, d = x.shape
    rows = min(ROW_TILE, n)
    return pl.pallas_call(
        _rmsnorm_kernel,
        grid=(n // rows,),
        in_specs=[pl.BlockSpec((rows, d), lambda i: (i, 0)), pl.BlockSpec((1, d), lambda i: (0, 0))],
        out_specs=pl.BlockSpec((rows, d), lambda i: (i, 0)),
        out_shape=jax.ShapeDtypeStruct((n, d), out_dtype),
        compiler_params=_cparams(("parallel",)),
        name="rmsnorm",
    )(x, gain.reshape(1, d))


def _resid_norm_kernel(h_ref, y_ref, g_ref, o_ref):
    y = y_ref[...].astype(F32)
    inv = lax.rsqrt(jnp.mean(y * y, axis=-1, keepdims=True) + NORM_EPS)
    o_ref[...] = h_ref[...] + y * inv * g_ref[...]


def _resid_norm(h, y, gain):
    n, d = h.shape
    row = pl.BlockSpec((ROW_TILE, d), lambda i: (i, 0))
    return pl.pallas_call(
        _resid_norm_kernel,
        grid=(n // ROW_TILE,),
        in_specs=[row, row, pl.BlockSpec((1, d), lambda i: (0, 0))],
        out_specs=row,
        out_shape=jax.ShapeDtypeStruct((n, d), F32),
        compiler_params=_cparams(("parallel",)),
        name="resid_norm",
    )(h, y, gain.reshape(1, d))


def _mm_kernel(a_ref, b_ref, o_ref, *, b_is_nk):
    acc = _dot_nt(a_ref[...], b_ref[...]) if b_is_nk else _dot(a_ref[...], b_ref[...])
    o_ref[...] = acc.astype(o_ref.dtype)


def _matmul(a, b, out_dtype, tm, tn, b_is_nk=False, name="matmul"):
    m, k = a.shape
    n = b.shape[0] if b_is_nk else b.shape[1]
    b_spec = pl.BlockSpec((tn, k), lambda i, j: (j, 0)) if b_is_nk else pl.BlockSpec((k, tn), lambda i, j: (0, j))
    return pl.pallas_call(
        functools.partial(_mm_kernel, b_is_nk=b_is_nk),
        grid=(m // tm, n // tn),
        in_specs=[pl.BlockSpec((tm, k), lambda i, j: (i, 0)), b_spec],
        out_specs=pl.BlockSpec((tm, tn), lambda i, j: (i, j)),
        out_shape=jax.ShapeDtypeStruct((m, n), out_dtype),
        compiler_params=_cparams(("parallel", "arbitrary")),
        name=name,
    )(a, b)


def _round_and_dot(a_ref, b_ref, b_bf16):
    acc = None
    for k0 in range(0, a_ref.shape[1], K_CHUNK):
        b_bf16[k0:k0 + K_CHUNK, :] = b_ref[k0:k0 + K_CHUNK, :].astype(BF16)
        part = _dot(a_ref[:, k0:k0 + K_CHUNK], b_bf16[k0:k0 + K_CHUNK, :])
        acc = part if acc is None else acc + part
    return acc


def _mm_wcast_kernel(a_ref, b_ref, o_ref, b_bf16, *, relu2):
    def finish(acc):
        if relu2:
            acc = jnp.square(jnp.maximum(acc, 0.0))
        o_ref[...] = acc.astype(o_ref.dtype)

    @pl.when(pl.program_id(1) == 0)
    def _():
        finish(_round_and_dot(a_ref, b_ref, b_bf16))

    @pl.when(pl.program_id(1) != 0)
    def _():
        finish(_dot(a_ref[...], b_bf16[...]))


def _matmul_wcast(a, b, out_dtype, relu2=False, name="matmul_wcast"):
    m, k = a.shape
    _, n = b.shape
    tm, tn = TILE_PANEL
    return pl.pallas_call(
        functools.partial(_mm_wcast_kernel, relu2=relu2),
        grid=(n // tn, m // tm),
        in_specs=[pl.BlockSpec((tm, k), lambda j, i: (i, 0)), pl.BlockSpec((k, tn), lambda j, i: (0, j))],
        out_specs=pl.BlockSpec((tm, tn), lambda j, i: (i, j)),
        out_shape=jax.ShapeDtypeStruct((m, n), out_dtype),
        scratch_shapes=[pltpu.VMEM((k, tn), BF16)],
        compiler_params=_cparams(("parallel", "arbitrary")),
        name=name,
    )(a, b)


def _mm_kacc_kernel(a_ref, b_ref, o_ref, acc_ref):
    l = pl.program_id(2)

    @pl.when(l == 0)
    def _():
        acc_ref[...] = jnp.zeros_like(acc_ref)

    acc = None
    for k0 in range(0, a_ref.shape[1], K_CHUNK):
        part = _dot(a_ref[:, k0:k0 + K_CHUNK], b_ref[k0:k0 + K_CHUNK, :].astype(BF16))
        acc = part if acc is None else acc + part
    acc_ref[...] += acc

    @pl.when(l == pl.num_programs(2) - 1)
    def _():
        o_ref[...] = acc_ref[...].astype(o_ref.dtype)


def _matmul_kacc(a, b, out_dtype, name="matmul_kacc"):
    m, k = a.shape
    _, n = b.shape
    tm, tn, tk = TILE_MLP_DOWN
    return pl.pallas_call(
        _mm_kacc_kernel,
        grid=(m // tm, n // tn, k // tk),
        in_specs=[pl.BlockSpec((tm, tk), lambda i, j, l: (i, l)), pl.BlockSpec((tk, tn), lambda i, j, l: (l, j))],
        out_specs=pl.BlockSpec((tm, tn), lambda i, j, l: (i, j)),
        out_shape=jax.ShapeDtypeStruct((m, n), out_dtype),
        scratch_shapes=[pltpu.VMEM((tm, tn), F32)],
        compiler_params=_cparams(("parallel", "parallel", "arbitrary")),
        name=name,
    )(a, b)


def _branch_kernel(yr_ref, yg_ref, wr_ref, wg_ref, gr_ref, gg_ref, o_ref, wr_bf16, wg_bf16):
    def finish(br, bg):
        o_ref[...] = (_sigmoid(gr_ref[...]) * br + _sigmoid(gg_ref[...]) * bg).astype(o_ref.dtype)

    @pl.when(pl.program_id(1) == 0)
    def _():
        finish(_round_and_dot(yr_ref, wr_ref, wr_bf16), _round_and_dot(yg_ref, wg_ref, wg_bf16))

    @pl.when(pl.program_id(1) != 0)
    def _():
        finish(_dot(yr_ref[...], wr_bf16[...]), _dot(yg_ref[...], wg_bf16[...]))


def _branch_merge(y_rw, y_gdn, w_rw, w_gdn, p):
    m, k = y_rw.shape
    n = w_rw.shape[1]
    tm, tn = TILE_BRANCH
    jr, jg = OFF_GATE_RW // tn, OFF_GATE_GDN // tn
    return pl.pallas_call(
        _branch_kernel,
        grid=(n // tn, m // tm),
        in_specs=[
            pl.BlockSpec((tm, k), lambda j, i: (i, 0)),
            pl.BlockSpec((tm, k), lambda j, i: (i, 0)),
            pl.BlockSpec((k, tn), lambda j, i: (0, j)),
            pl.BlockSpec((k, tn), lambda j, i: (0, j)),
            pl.BlockSpec((tm, tn), lambda j, i: (i, jr + j)),
            pl.BlockSpec((tm, tn), lambda j, i: (i, jg + j)),
        ],
        out_specs=pl.BlockSpec((tm, tn), lambda j, i: (i, j)),
        out_shape=jax.ShapeDtypeStruct((m, n), BF16),
        scratch_shapes=[pltpu.VMEM((k, tn), BF16)] * 2,
        compiler_params=_cparams(("parallel", "arbitrary")),
        name="branch_merge",
    )(y_rw, y_gdn, w_rw, w_gdn, p, p)


def _batch_lanes(ref):
    return jnp.concatenate([ref[b] for b in range(ref.shape[0])], axis=1)


def _tile_lanes(x, n):
    return jnp.concatenate([x] * n, axis=1)


def _token_shift(x_ref, prev_ref, first, mix):
    x = _batch_lanes(x_ref)
    before = jnp.where(first, 0.0, _batch_lanes(prev_ref)[V7X_SUBLANES - 1:V7X_SUBLANES, :])
    prev = jnp.where(_iota(x.shape, 0) == 0, before, pltpu.roll(x, 1, axis=0))
    return x + (prev - x) * mix


def _expand2(x, head0):
    return jnp.concatenate([jnp.where(head0, x, 0.0), jnp.where(head0, 0.0, x)], axis=0)


def _rwkv_kernel(r_ref, k_ref, v_ref, pr_ref, pk_ref, pv_ref, misc_ref, pmisc_ref,
                 cp_ref, gp_ref, mixm_ref, wup_ref, aup_ref, gup_ref, o_ref, s_ref):
    c = CHUNK
    w = V7X_LANES
    batch, _, lw = r_ref.shape
    ci = pl.program_id(0)

    @pl.when(ci == 0)
    def _():
        s_ref[...] = jnp.zeros_like(s_ref)

    first = ci == 0
    mix_r, mix_k, mix_v, w0, a0, k_k, k_a, r_k = (_tile_lanes(cp_ref[i:i + 1, :], batch) for i in range(8))
    gn_w, gn_b = _tile_lanes(gp_ref[0:1, :], batch), _tile_lanes(gp_ref[1:2, :], batch)

    misc_m = _token_shift(misc_ref, pmisc_ref, first, _tile_lanes(mixm_ref[...], batch))
    xr = _token_shift(r_ref, pr_ref, first, mix_r)
    xk = _token_shift(k_ref, pk_ref, first, mix_k)
    xv = _token_shift(v_ref, pv_ref, first, mix_v)

    def lora(fn, lo, hi, w_ref):
        return jnp.concatenate([_dot(_bf(fn(misc_m[:, b * MISC_W + lo:b * MISC_W + hi])), w_ref[...])
                                for b in range(batch)], axis=1)

    wlin = w0 + lora(jnp.tanh, MISC_LW, MISC_LA, wup_ref)
    logw = -math.exp(-0.5) * _sigmoid(wlin)
    a = _sigmoid(a0 + lora(lambda t: t, MISC_LA, MISC_LG, aup_ref))
    gate = lora(_sigmoid, MISC_LG, MISC_W, gup_ref)

    same_head = (_iota((w, w), 0) // RWKV_HEAD) == (_iota((w, w), 1) // RWKV_HEAD)
    bones = jnp.where(same_head, 1.0, 0.0).astype(BF16)
    eye = jnp.where(_iota((w, w), 0) == _iota((w, w), 1), 1.0, 0.0)

    def head_sums(x):
        n = x.shape[1] // w
        s = _dot(_bf(jnp.concatenate(_lane_groups(x), axis=0)), bones)
        return jnp.concatenate([s[g * c:(g + 1) * c] for g in range(n)], axis=1)

    kkr = xk * k_k
    kkn = kkr * lax.rsqrt(head_sums(kkr * kkr) + L2_EPS)
    k2 = xk * (1.0 + (a - 1.0) * k_a)
    am = -kkn
    bm = kkn * a

    cum = _mm_xl(_tril_ones_x2(c), logw)
    cum_ex = cum - logw
    c_last = cum[c - 1:c, :]
    c_mid = cum[c // 2 - 1:c // 2, :]
    e_out = jnp.exp(c_mid - cum)
    e_last = jnp.exp(c_last - cum)
    rt = xr * jnp.exp(cum - c_mid)
    at = am * jnp.exp(cum_ex - c_mid)
    bt = bm * e_out
    kt = k2 * e_out
    a_abs = am * jnp.exp(cum_ex)
    r_abs = xr * jnp.exp(cum)
    bh = bm * e_last
    kh = k2 * e_last
    w_c = jnp.exp(c_last)

    head0 = _iota((c, w), 1) < RWKV_HEAD
    head0_2 = _iota((c, 2 * w), 1) % w < RWKV_HEAD
    wi = _iota((c, 2 * c), 0)
    wj = _iota((c, 2 * c), 1) % c
    strict = wi > wj
    incl = wi >= wj

    gs = range(batch * lw // w)
    G = _lane_groups
    at_g, rt_g, bt_g, kt_g, xv_g = G(at), G(rt), G(bt), G(kt), G(xv)
    aa = [_mm_nt(jnp.concatenate([at_g[g], rt_g[g]], axis=0),
                 jnp.concatenate([_expand2(bt_g[g], head0), _expand2(kt_g[g], head0)], axis=0)) for g in gs]
    a_ab = [jnp.where(strict, aa[g][0:c, 0:2 * c], 0.0) for g in gs]
    a_ak = [jnp.where(strict, aa[g][0:c, 2 * c:4 * c], 0.0) for g in gs]
    a_rb = [jnp.where(incl, aa[g][c:2 * c, 0:2 * c], 0.0) for g in gs]
    a_rk = [jnp.where(incl, aa[g][c:2 * c, 2 * c:4 * c], 0.0) for g in gs]

    xv_e = [_expand2(xv_g[g], head0) for g in gs]
    akv = [_mm(a_ak[g], xv_e[g]) for g in gs]
    eye_w = jnp.where(wi == wj, 1.0, 0.0)
    tw = [eye_w + a_ab[g] for g in gs]
    pw = [_mm(a_ab[g], _expand2(a_ab[g], head0)) for g in gs]
    levels = int(math.log2(c))
    for lvl in range(1, levels):
        if lvl + 1 < levels:
            pt = [_mm(pw[g], _expand2(jnp.concatenate([pw[g], tw[g]], axis=1), head0_2)) for g in gs]
            pw = [pt[g][:, 0:w] for g in gs]
            tw = [tw[g] + pt[g][:, w:2 * w] for g in gs]
        else:
            tw = [tw[g] + _mm(pw[g], _expand2(tw[g], head0)) for g in gs]
    a_abs_g = G(a_abs)
    x = [_mm(tw[g], _expand2(jnp.concatenate([a_abs_g[g], akv[g]], axis=1), head0_2)) for g in gs]
    wm = [x[g][:, 0:w] for g in gs]
    u0 = [x[g][:, w:2 * w] for g in gs]

    bh_g, kh_g, wc_g, rabs_g = G(bh), G(kh), G(w_c), G(r_abs)
    m_mat = [jnp.where(same_head, _mm_tn(bh_g[g], wm[g]), 0.0) + eye * wc_g[g] for g in gs]
    n_mat = [jnp.where(same_head, _mm_tn(jnp.concatenate([bh_g[g], kh_g[g]], axis=0),
                                         jnp.concatenate([u0[g], xv_g[g]], axis=0)), 0.0) for g in gs]
    p_mat = [rabs_g[g] + _mm(a_rb[g], _expand2(wm[g], head0)) for g in gs]
    q_mat = [_mm(jnp.concatenate([a_rb[g], a_rk[g]], axis=1),
                 jnp.concatenate([_expand2(u0[g], head0), xv_e[g]], axis=0)) for g in gs]

    h0 = [_bf(s_ref[g]) for g in gs]
    y = jnp.concatenate([_dot(_bf(p_mat[g]), h0[g]) + q_mat[g] for g in gs], axis=1)
    for g in gs:
        s_ref[g] = _dot(_bf(m_mat[g]), h0[g]) + n_mat[g]

    inv_n = 1.0 / RWKV_HEAD
    d = y - head_sums(y) * inv_n
    var = head_sums(d * d) * inv_n
    yn = d * lax.rsqrt(var + RWKV_GN_EPS) * gn_w + gn_b
    bonus = head_sums(xr * k2 * r_k) * xv
    out = ((yn + bonus) * gate).astype(o_ref.dtype)
    for b in range(batch):
        o_ref[b] = out[:, b * lw:(b + 1) * lw]


def _chunk_specs(batch, c):
    def cur(off, width):
        assert off % width == 0
        return pl.BlockSpec((batch, c, width), lambda t: (0, t, off // width))

    def prev(off, width):
        return pl.BlockSpec((batch, V7X_SUBLANES, width),
                            lambda t: (0, jnp.maximum(t * (c // V7X_SUBLANES) - 1, 0), off // width))

    return cur, prev


def _rwkv_mix(p, cp, gp, mixm, w_up_p, a_up_p, g_up):
    batch, seq, _ = p.shape
    c = CHUNK
    cur, prev = _chunk_specs(batch, c)
    whole = lambda x: pl.BlockSpec(x.shape, lambda t: (0, 0))
    return pl.pallas_call(
        _rwkv_kernel,
        grid=(seq // c,),
        in_specs=[
            cur(OFF_R, RW), cur(OFF_K, RW), cur(OFF_V, RW), prev(OFF_R, RW), prev(OFF_K, RW), prev(OFF_V, RW),
            cur(OFF_MISC, MISC_W), prev(OFF_MISC, MISC_W),
            whole(cp), whole(gp), whole(mixm), whole(w_up_p), whole(a_up_p), whole(g_up),
        ],
        out_specs=pl.BlockSpec((batch, c, RW), lambda t: (0, t, 0)),
        out_shape=jax.ShapeDtypeStruct((batch, seq, RW), BF16),
        scratch_shapes=[pltpu.VMEM((batch * RW // V7X_LANES, V7X_LANES, V7X_LANES), F32)],
        compiler_params=_cparams(("arbitrary",)),
        name="rwkv7_mix",
    )(p, p, p, p, p, p, p, p, cp, gp, mixm, w_up_p, a_up_p, g_up)


def _gdn_kernel(q_ref, k_ref, v_ref, pq_ref, pk_ref, pv_ref, z_ref, misc_ref,
                cwq_ref, cwk_ref, cwv_ref, alog_ref, dtb_ref, nw_ref, o_ref, s_ref):
    c = CHUNK
    w = V7X_LANES
    batch, _, lw = q_ref.shape
    assert lw == N_GDN_HEADS * w
    ci = pl.program_id(0)
    heads = batch * N_GDN_HEADS
    hs = range(heads)
    G = _lane_groups

    @pl.when(ci == 0)
    def _():
        s_ref[...] = jnp.zeros_like(s_ref)

    first = ci == 0

    def conv_silu(x_ref, prev_ref, cw_ref):
        x = _batch_lanes(x_ref)
        prev8 = jnp.where(first, 0.0, _batch_lanes(prev_ref))
        cw = _tile_lanes(cw_ref[...], batch)
        row = _iota(x.shape, 0)
        taps = cw.shape[0]
        acc = x * cw[taps - 1:taps, :]
        for s in range(1, taps):
            tail = jnp.tile(pltpu.roll(prev8, s, axis=0), (c // V7X_SUBLANES, 1))
            acc = acc + jnp.where(row < s, tail, pltpu.roll(x, s, axis=0)) * cw[taps - 1 - s:taps - s, :]
        return acc * _sigmoid(acc)

    def head_sums(x):
        return jnp.concatenate([jnp.broadcast_to(jnp.sum(g, axis=-1, keepdims=True), g.shape) for g in G(x)], axis=1)

    q = conv_silu(q_ref, pq_ref, cwq_ref)
    k = conv_silu(k_ref, pk_ref, cwk_ref)
    v = conv_silu(v_ref, pv_ref, cwv_ref)
    q = q * lax.rsqrt(head_sums(q * q) + L2_EPS) * (GDN_HEAD ** -0.5)
    k = k * lax.rsqrt(head_sums(k * k) + L2_EPS)

    lane2 = _iota((2 * w, w), 0) % w
    lane2r = _iota((c, 2 * w), 1) % w
    bcol, gcol, grow = [], [], []
    for b in range(batch):
        bm = misc_ref[b, :, MISC_LW:MISC_LA]
        beta_hl = jnp.concatenate(_hi_lo(_sigmoid(bm)), axis=1)
        g_full = -jnp.exp(alog_ref[...]) * _softplus(bm + dtb_ref[...])
        gc_hl = jnp.concatenate(_hi_lo(_mm_xl(_tril_ones_x2(c), g_full)), axis=1)
        for h in range(N_GDN_HEADS):
            bcol.append(_dot(beta_hl, jnp.where(lane2 == BETA_LANE + h, 1.0, 0.0).astype(BF16)))
            gcol.append(_dot(gc_hl, jnp.where(lane2 == ALPHA_LANE + h, 1.0, 0.0).astype(BF16)))
            grow.append(_dot_nt(jnp.where(lane2r == ALPHA_LANE + h, 1.0, 0.0).astype(BF16), gc_hl))

    ti = _iota((c, c), 0)
    tj = _iota((c, c), 1)
    causal = ti >= tj
    strict = ti > tj
    decay = [jnp.where(causal, jnp.exp(jnp.where(causal, gcol[h][:, 0:c] - grow[h], 0.0)), 0.0) for h in hs]

    q_g, k_g, v_g = G(q), G(k), G(v)
    kb = [k_g[h] * bcol[h] for h in hs]
    kk = [_mm_nt(jnp.concatenate([kb[h], q_g[h]], axis=0), k_g[h]) for h in hs]
    lneg = [jnp.where(strict, -kk[h][0:c] * decay[h], 0.0) for h in hs]
    a_intra = [kk[h][c:2 * c] * decay[h] for h in hs]

    egc = [jnp.exp(gcol[h]) for h in hs]
    g_last = [gcol[h][c - 1:c, :] for h in hs]
    lane_t = _iota((c, w), 1) >= c
    eye_t = jnp.where(_iota((c, w), 1) == _iota((c, w), 0) + c, 1.0, 0.0)
    zt = [jnp.concatenate([lneg[h], jnp.zeros((c, c), F32)], axis=1) + eye_t for h in hs]
    for _ in range(int(math.log2(c))):
        zb = [_bf(zt[h]) for h in hs]
        zt = [_dot(zb[h][:, 0:c], zb[h]) + jnp.where(lane_t, zt[h], 0.0) for h in hs]
    zeros_x = jnp.zeros((c, 2 * w), F32)
    x = [_mm(zt[h], jnp.concatenate([zeros_x, jnp.concatenate([v_g[h] * bcol[h], kb[h] * egc[h]], axis=1)], axis=0))
         for h in hs]
    uu = [x[h][:, 0:w] for h in hs]
    ww = [x[h][:, w:2 * w] for h in hs]

    kd = [k_g[h] * jnp.exp(g_last[h] - gcol[h]) for h in hs]
    eye = jnp.where(_iota((w, w), 0) == _iota((w, w), 1), 1.0, 0.0)
    m_mat = [eye * jnp.exp(g_last[h]) - _mm_tn(kd[h], ww[h]) for h in hs]
    n_mat = [_mm_tn(kd[h], uu[h]) for h in hs]
    p_mat = [q_g[h] * egc[h] - _mm(a_intra[h], ww[h]) for h in hs]
    q_mat = [_mm(a_intra[h], uu[h]) for h in hs]

    s0 = [_bf(s_ref[h]) for h in hs]
    o = jnp.concatenate([_dot(_bf(p_mat[h]), s0[h]) + q_mat[h] for h in hs], axis=1)
    for h in hs:
        s_ref[h] = _dot(_bf(m_mat[h]), s0[h]) + n_mat[h]

    o = o * lax.rsqrt(head_sums(o * o) * (1.0 / GDN_HEAD) + NORM_EPS) * _tile_lanes(nw_ref[...], heads)
    z = _batch_lanes(z_ref)
    out = (o * (z * _sigmoid(z))).astype(o_ref.dtype)
    for b in range(batch):
        o_ref[b] = out[:, b * lw:(b + 1) * lw]


def _gdn_mix(p, conv_w, alog_l, dtb_l, norm_w):
    batch, seq, _ = p.shape
    c = CHUNK
    cur, prev = _chunk_specs(batch, c)

    def cw(part):
        return pl.BlockSpec((conv_w.shape[0], RW), lambda t: (0, part))

    lane_vec = pl.BlockSpec((1, V7X_LANES), lambda t: (0, 0))
    return pl.pallas_call(
        _gdn_kernel,
        grid=(seq // c,),
        in_specs=[
            cur(OFF_GQ, RW), cur(OFF_GK, RW), cur(OFF_GV, RW), prev(OFF_GQ, RW), prev(OFF_GK, RW), prev(OFF_GV, RW),
            cur(OFF_Z, RW), cur(OFF_MISC, MISC_W),
            cw(0), cw(1), cw(2), lane_vec, lane_vec, lane_vec,
        ],
        out_specs=pl.BlockSpec((batch, c, RW), lambda t: (0, t, 0)),
        out_shape=jax.ShapeDtypeStruct((batch, seq, RW), BF16),
        scratch_shapes=[pltpu.VMEM((batch * N_GDN_HEADS, V7X_LANES, V7X_LANES), F32)],
        compiler_params=_cparams(("arbitrary",)),
        name="gdn_mix",
    )(p, p, p, p, p, p, p, p, conv_w, conv_w, conv_w, alog_l, dtb_l, norm_w)


def _xattn_kernel(x_ref, ymix_ref, gmix_ref, k_ref, v_ref, wq_ref, wo_ref, gpre_ref, gpost_ref, gnext_ref,
                  h_out_ref, f_out_ref):
    ym = ymix_ref[...].astype(F32)
    h = x_ref[...] + ym * lax.rsqrt(jnp.mean(ym * ym, axis=-1, keepdims=True) + NORM_EPS) * gmix_ref[...]
    cn = (h * lax.rsqrt(jnp.mean(h * h, axis=-1, keepdims=True) + NORM_EPS) * gpre_ref[...]).astype(BF16)
    q = _dot(cn, wq_ref[...]).astype(BF16)
    kk = k_ref[0]
    vv = v_ref[0]
    outs = []
    for hh in range(XA_HEADS):
        sl = slice(hh * XA_HEAD, (hh + 1) * XA_HEAD)
        s = _dot_nt(q[:, sl], kk[:, sl]) * (XA_HEAD ** -0.5)
        s = s - jnp.max(s, axis=-1, keepdims=True)
        e = jnp.exp(s)
        pr = e / jnp.sum(e, axis=-1, keepdims=True)
        outs.append(_dot(pr.astype(BF16), vv[:, sl]))
    o = jnp.concatenate(outs, axis=1).astype(BF16)
    y = _dot(o, wo_ref[...])
    h2 = h + y * lax.rsqrt(jnp.mean(y * y, axis=-1, keepdims=True) + NORM_EPS) * gpost_ref[...]
    h_out_ref[...] = h2
    f_out_ref[...] = (h2 * lax.rsqrt(jnp.mean(h2 * h2, axis=-1, keepdims=True) + NORM_EPS)
                      * gnext_ref[...]).astype(f_out_ref.dtype)


def _xattn_block(x, y_mix, g_mix, kmem, vmem, w_q, w_o, g_pre, g_post, g_next, seq):
    n, d = x.shape
    mlen, xw = kmem.shape[1], kmem.shape[2]
    per_b = seq // ROW_TILE
    row = pl.BlockSpec((ROW_TILE, d), lambda i: (i, 0))
    vec = pl.BlockSpec((1, d), lambda i: (0, 0))
    mem = pl.BlockSpec((1, mlen, xw), lambda i: (i // per_b, 0, 0))
    once = pl.Buffered(1)
    return pl.pallas_call(
        _xattn_kernel,
        grid=(n // ROW_TILE,),
        in_specs=[row, row, vec, mem, mem, pl.BlockSpec((d, xw), lambda i: (0, 0), pipeline_mode=once),
                  pl.BlockSpec((xw, d), lambda i: (0, 0), pipeline_mode=once), vec, vec, vec],
        out_specs=[row, row],
        out_shape=[jax.ShapeDtypeStruct((n, d), F32), jax.ShapeDtypeStruct((n, d), BF16)],
        compiler_params=_cparams(("parallel",)),
        name="xattn_block",
    )(x, y_mix, g_mix.reshape(1, d), kmem, vmem, w_q, w_o, g_pre.reshape(1, d), g_post.reshape(1, d),
      g_next.reshape(1, d))


def _relayout_kernel(x_ref, misc_ref, o_ref):
    @pl.when(pl.program_id(0) != OFF_MISC // RELAYOUT_ROWS)
    def _():
        o_ref[...] = x_ref[...].astype(o_ref.dtype)

    @pl.when(pl.program_id(0) == OFF_MISC // RELAYOUT_ROWS)
    def _():
        o_ref[...] = misc_ref[...].astype(o_ref.dtype)


def _relayout_w_in_t(w_in_t):
    n_in, d = w_in_t.shape
    rb = RELAYOUT_ROWS
    assert OFF_GQ % rb == 0 and OFF_MISC % rb == 0 and OFF_GATE_RW % rb == 0 and MISC_W == rb
    zeros = jnp.zeros((MISC_LG - MISC_LA - LORA, d), w_in_t.dtype)
    misc = jnp.concatenate([w_in_t[SRC_LORA:SRC_LORA + LORA], w_in_t[SRC_BA:SRC_BA + 2 * N_GDN_HEADS],
                            w_in_t[SRC_LORA + LORA:SRC_LORA + 2 * LORA], zeros,
                            w_in_t[SRC_LORA + 2 * LORA:SRC_GDN]], axis=0)

    def src_row(j):
        r = j * rb
        src = jnp.where(r < OFF_GQ, r, jnp.where(r < OFF_MISC, r + (SRC_GDN - OFF_GQ),
                                                 jnp.where(r < OFF_GATE_RW, 0, r + (SRC_GATES - OFF_GATE_RW))))
        return pl.multiple_of(src, math.gcd(rb, SRC_GDN, SRC_GATES))

    return pl.pallas_call(
        _relayout_kernel,
        grid=(NP // rb,),
        in_specs=[pl.BlockSpec((pl.Element(rb), pl.Element(d)), lambda j: (src_row(j), 0)),
                  pl.BlockSpec((rb, d), lambda j: (0, 0))],
        out_specs=pl.BlockSpec((rb, d), lambda j: (j, 0)),
        out_shape=jax.ShapeDtypeStruct((NP, d), BF16),
        compiler_params=_cparams(("parallel",)),
        name="relayout_w_in",
    )(w_in_t, misc)


def _layer(x, mem, mix_norm_pre, mix_norm_post, w_in, rwkv_shift_mix, rwkv_w0, rwkv_w_up, rwkv_a0, rwkv_a_up,
           rwkv_g_up, rwkv_k_k, rwkv_k_a, rwkv_r_k, rwkv_gn_w, rwkv_gn_b, gdn_conv_w, gdn_a_log, gdn_dt_bias,
           gdn_norm_w, w_branch_rwkv, w_branch_gdn, w_mix_out, xa_norm_pre, xa_norm_mem, xa_norm_post, xa_w_q,
           xa_w_kv, xa_w_o, mlp_norm_pre, mlp_norm_post, mlp_w_up, mlp_w_down):
    batch, seq, d = x.shape
    n = batch * seq
    h = x.reshape(n, d)

    w_in_b = _relayout_w_in_t(jnp.swapaxes(w_in, 0, 1))
    mix = rwkv_shift_mix
    o = 3 * RW
    cp = jnp.stack([mix[0:RW], mix[RW:2 * RW], mix[2 * RW:o], rwkv_w0, rwkv_a0, rwkv_k_k, rwkv_k_a,
                    rwkv_r_k.reshape(RW)], axis=0)
    gp = jnp.concatenate([rwkv_gn_w[None], rwkv_gn_b[None], jnp.zeros((V7X_SUBLANES - 2, RW), F32)], axis=0)
    zeros = lambda k: jnp.zeros((k,), F32)
    lora_pad = MISC_LA - LORA
    mixm = jnp.concatenate([mix[o:o + LORA], zeros(lora_pad), mix[o + LORA:o + 2 * LORA], zeros(lora_pad),
                            mix[o + 2 * LORA:o + 2 * LORA + 256]])[None]
    w_up_p = jnp.concatenate([rwkv_w_up, jnp.zeros((lora_pad, RW), F32)], axis=0).astype(BF16)
    a_up_p = jnp.concatenate([rwkv_a_up, jnp.zeros((lora_pad, RW), F32)], axis=0).astype(BF16)
    g_up_b = rwkv_g_up.astype(BF16)
    alog_l = jnp.concatenate([zeros(ALPHA_LANE), gdn_a_log])[None]
    dtb_l = jnp.concatenate([zeros(ALPHA_LANE), gdn_dt_bias])[None]

    u = _rmsnorm(h, mix_norm_pre, BF16)
    p = _matmul(u, w_in_b, F32, *TILE_IN_PROJ, b_is_nk=True, name="in_proj")
    p3 = p.reshape(batch, seq, NP)
    y_rw = _rwkv_mix(p3, cp, gp, mixm, w_up_p, a_up_p, g_up_b).reshape(n, RW)
    y_gdn = _gdn_mix(p3, gdn_conv_w, alog_l, dtb_l, gdn_norm_w[None]).reshape(n, RW)
    merged = _branch_merge(y_rw, y_gdn, w_branch_rwkv, w_branch_gdn, p)
    y_mix = _matmul_wcast(merged, w_mix_out, BF16, name="mix_out")

    mlen = mem.shape[1]
    m = _rmsnorm(mem.reshape(batch * mlen, d), xa_norm_mem, BF16)
    xw = XA_HEADS * XA_HEAD
    kv = _matmul(m, xa_w_kv.astype(BF16), BF16, batch * mlen, xw, name="xa_kv")
    kmem = kv[:, :xw].reshape(batch, mlen, xw)
    vmem = kv[:, xw:].reshape(batch, mlen, xw)
    h, f = _xattn_block(h, y_mix, mix_norm_post, kmem, vmem, xa_w_q.astype(BF16), xa_w_o.astype(BF16),
                        xa_norm_pre, xa_norm_post, mlp_norm_pre, seq)

    hmid = _matmul_wcast(f, mlp_w_up, BF16, relu2=True, name="mlp_up")
    y_mlp = _matmul_kacc(hmid, mlp_w_down, BF16, name="mlp_down")
    h = _resid_norm(h, y_mlp, mlp_norm_post)
    return h.reshape(batch, seq, d)


def kernel(x, mem, mix_norm_pre, mix_norm_post, w_in, rwkv_shift_mix, rwkv_w0, rwkv_w_up, rwkv_a0, rwkv_a_up,
           rwkv_g_up, rwkv_k_k, rwkv_k_a, rwkv_r_k, rwkv_gn_w, rwkv_gn_b, gdn_conv_w, gdn_a_log, gdn_dt_bias,
           gdn_norm_w, w_branch_rwkv, w_branch_gdn, w_mix_out, xa_norm_pre, xa_norm_mem, xa_norm_post, xa_w_q,
           xa_w_kv, xa_w_o, mlp_norm_pre, mlp_norm_post, mlp_w_up, mlp_w_down):
    args = (mix_norm_pre, mix_norm_post, w_in, rwkv_shift_mix, rwkv_w0, rwkv_w_up, rwkv_a0, rwkv_a_up,
            rwkv_g_up, rwkv_k_k, rwkv_k_a, rwkv_r_k, rwkv_gn_w, rwkv_gn_b, gdn_conv_w, gdn_a_log, gdn_dt_bias,
            gdn_norm_w, w_branch_rwkv, w_branch_gdn, w_mix_out, xa_norm_pre, xa_norm_mem, xa_norm_post, xa_w_q,
            xa_w_kv, xa_w_o, mlp_norm_pre, mlp_norm_post, mlp_w_up, mlp_w_down)
    h = x
    for layer in range(mix_norm_pre.shape[0]):
        h = _layer(h, mem, *(a[layer] for a in args))
    return h
```

```python
import functools
import math

import jax
import jax.numpy as jnp
from jax import lax
from jax.experimental import pallas as pl
from jax.experimental.pallas import tpu as pltpu

F32 = jnp.float32
BF16 = jnp.bfloat16

V7X_LANES = 128
V7X_SUBLANES = 8
V7X_VMEM_BYTES = 64 * 1024 * 1024
V7X_VMEM_LIMIT_BYTES = V7X_VMEM_BYTES - 4 * 1024 * 1024

NORM_EPS = 1e-6
L2_EPS = 1e-6
RWKV_GN_EPS = 64e-5
RWKV_HEAD = 64
GDN_HEAD = 128
CHUNK = 64
XA_HEADS = 4
XA_HEAD = 128

RW = 2048
D_MODEL = 4096
LORA = 96
N_GDN_HEADS = RW // GDN_HEAD
OFF_R, OFF_K, OFF_V = 0, RW, 2 * RW
OFF_GQ, OFF_GK, OFF_GV = 3 * RW, 4 * RW, 5 * RW
OFF_Z = 6 * RW
OFF_MISC = 7 * RW
MISC_W = 512
MISC_LW, MISC_LA, MISC_LG = 0, 128, 256
BETA_LANE = LORA
ALPHA_LANE = LORA + N_GDN_HEADS
OFF_GATE_RW = OFF_MISC + MISC_W
OFF_GATE_GDN = OFF_GATE_RW + D_MODEL
NP = OFF_GATE_GDN + D_MODEL

SRC_LORA = 3 * RW
SRC_GDN = SRC_LORA + 2 * LORA + 256
SRC_BA = SRC_GDN + 4 * RW
SRC_GATES = SRC_BA + 2 * N_GDN_HEADS
RELAYOUT_ROWS = 512

TILE_IN_PROJ = (1024, 1536)
TILE_PANEL = (2048, 512)
TILE_BRANCH = (1024, 512)
TILE_MLP_DOWN = (2048, 1024, 2048)
K_CHUNK = 1024
ROW_TILE = 256
NORM_ROW_TILE = 512


def _cparams(sem):
    return pltpu.CompilerParams(dimension_semantics=sem, vmem_limit_bytes=V7X_VMEM_LIMIT_BYTES)


def _dot(a, b):
    return jnp.dot(a, b, preferred_element_type=F32)


def _dot_nt(a, b):
    return lax.dot_general(a, b, (((1,), (1,)), ((), ())), preferred_element_type=F32)


def _dot_tn(a, b):
    return lax.dot_general(a, b, (((0,), (0,)), ((), ())), preferred_element_type=F32)


def _bf(x):
    return x.astype(BF16)


def _hi_lo(x):
    hi = x.astype(BF16)
    return hi, (x - hi.astype(F32)).astype(BF16)


def _mm(a, b):
    return _dot(_bf(a), _bf(b))


def _mm_nt(a, b):
    return _dot_nt(_bf(a), _bf(b))


def _mm_tn(a, b):
    return _dot_tn(_bf(a), _bf(b))


def _mm_xl(a01x2, b):
    hi, lo = _hi_lo(b)
    return _dot(a01x2, jnp.concatenate([hi, lo], axis=0))


def _sigmoid(x):
    return 0.5 * jnp.tanh(0.5 * x) + 0.5


def _softplus(x):
    return jnp.maximum(x, 0.0) + jnp.log1p(jnp.exp(-jnp.abs(x)))


def _iota(shape, dim):
    return lax.broadcasted_iota(jnp.int32, shape, dim)


def _lane_groups(x):
    return [x[:, g * V7X_LANES:(g + 1) * V7X_LANES] for g in range(x.shape[1] // V7X_LANES)]


def _tril_ones_x2(c):
    return jnp.where(_iota((c, 2 * c), 0) >= _iota((c, 2 * c), 1) % c, 1.0, 0.0).astype(BF16)


def _rmsnorm_kernel(x_ref, g_ref, o_ref):
    x = x_ref[...]
    inv = lax.rsqrt(jnp.mean(x * x, axis=-1, keepdims=True) + NORM_EPS)
    o_ref[...] = (x * inv * g_ref[...]).astype(o_ref.dtype)


def _rmsnorm(x, gain, out_dtype):
    n, d = x.shape
    rows = min(NORM_ROW_TILE, n)
    return pl.pallas_call(
        _rmsnorm_kernel,
        grid=(n // rows,),
        in_specs=[pl.BlockSpec((rows, d), lambda i: (i, 0)), pl.BlockSpec((1, d), lambda i: (0, 0))],
        out_specs=pl.BlockSpec((rows, d), lambda i: (i, 0)),
        out_shape=jax.ShapeDtypeStruct((n, d), out_dtype),
        compiler_params=_cparams(("parallel",)),
        name="rmsnorm",
    )(x, gain.reshape(1, d))


def _resid_norm_kernel(h_ref, y_ref, g_ref, o_ref):
    y = y_ref[...].astype(F32)
    inv = lax.rsqrt(jnp.mean(y * y, axis=-1, keepdims=True) + NORM_EPS)
    o_ref[...] = h_ref[...] + y * inv * g_ref[...]


def _resid_norm(h, y, gain):
    n, d = h.shape
    row = pl.BlockSpec((NORM_ROW_TILE, d), lambda i: (i, 0))
    return pl.pallas_call(
        _resid_norm_kernel,
        grid=(n // NORM_ROW_TILE,),
        in_specs=[row, row, pl.BlockSpec((1, d), lambda i: (0, 0))],
        out_specs=row,
        out_shape=jax.ShapeDtypeStruct((n, d), F32),
        compiler_params=_cparams(("parallel",)),
        name="resid_norm",
    )(h, y, gain.reshape(1, d))


def _mm_kernel(a_ref, b_ref, o_ref, *, b_is_nk):
    acc = _dot_nt(a_ref[...], b_ref[...]) if b_is_nk else _dot(a_ref[...], b_ref[...])
    o_ref[...] = acc.astype(o_ref.dtype)


def _matmul(a, b, out_dtype, tm, tn, b_is_nk=False, name="matmul"):
    m, k = a.shape
    n = b.shape[0] if b_is_nk else b.shape[1]
    b_spec = pl.BlockSpec((tn, k), lambda i, j: (j, 0)) if b_is_nk else pl.BlockSpec((k, tn), lambda i, j: (0, j))
    return pl.pallas_call(
        functools.partial(_mm_kernel, b_is_nk=b_is_nk),
        grid=(m // tm, n // tn),
        in_specs=[pl.BlockSpec((tm, k), lambda i, j: (i, 0)), b_spec],
        out_specs=pl.BlockSpec((tm, tn), lambda i, j: (i, j)),
        out_shape=jax.ShapeDtypeStruct((m, n), out_dtype),
        compiler_params=_cparams(("parallel", "arbitrary")),
        name=name,
    )(a, b)


def _round_and_dot(a_ref, b_ref, b_bf16):
    acc = None
    for k0 in range(0, a_ref.shape[1], K_CHUNK):
        b_bf16[k0:k0 + K_CHUNK, :] = b_ref[k0:k0 + K_CHUNK, :].astype(BF16)
        part = _dot(a_ref[:, k0:k0 + K_CHUNK], b_bf16[k0:k0 + K_CHUNK, :])
        acc = part if acc is None else acc + part
    return acc


def _mm_wcast_kernel(a_ref, b_ref, o_ref, b_bf16, *, relu2):
    def finish(acc):
        if relu2:
            acc = jnp.square(jnp.maximum(acc, 0.0))
        o_ref[...] = acc.astype(o_ref.dtype)

    @pl.when(pl.program_id(1) == 0)
    def _():
        finish(_round_and_dot(a_ref, b_ref, b_bf16))

    @pl.when(pl.program_id(1) != 0)
    def _():
        finish(_dot(a_ref[...], b_bf16[...]))


def _matmul_wcast(a, b, out_dtype, relu2=False, name="matmul_wcast"):
    m, k = a.shape
    _, n = b.shape
    tm, tn = TILE_PANEL
    return pl.pallas_call(
        functools.partial(_mm_wcast_kernel, relu2=relu2),
        grid=(n // tn, m // tm),
        in_specs=[pl.BlockSpec((tm, k), lambda j, i: (i, 0)), pl.BlockSpec((k, tn), lambda j, i: (0, j))],
        out_specs=pl.BlockSpec((tm, tn), lambda j, i: (i, j)),
        out_shape=jax.ShapeDtypeStruct((m, n), out_dtype),
        scratch_shapes=[pltpu.VMEM((k, tn), BF16)],
        compiler_params=_cparams(("parallel", "arbitrary")),
        name=name,
    )(a, b)


def _mm_kacc_kernel(a_ref, b_ref, o_ref, acc_ref):
    l = pl.program_id(2)

    @pl.when(l == 0)
    def _():
        acc_ref[...] = jnp.zeros_like(acc_ref)

    acc = None
    for k0 in range(0, a_ref.shape[1], K_CHUNK):
        part = _dot(a_ref[:, k0:k0 + K_CHUNK], b_ref[k0:k0 + K_CHUNK, :].astype(BF16))
        acc = part if acc is None else acc + part
    acc_ref[...] += acc

    @pl.when(l == pl.num_programs(2) - 1)
    def _():
        o_ref[...] = acc_ref[...].astype(o_ref.dtype)


def _matmul_kacc(a, b, out_dtype, name="matmul_kacc"):
    m, k = a.shape
    _, n = b.shape
    tm, tn, tk = TILE_MLP_DOWN
    return pl.pallas_call(
        _mm_kacc_kernel,
        grid=(m // tm, n // tn, k // tk),
        in_specs=[pl.BlockSpec((tm, tk), lambda i, j, l: (i, l)), pl.BlockSpec((tk, tn), lambda i, j, l: (l, j))],
        out_specs=pl.BlockSpec((tm, tn), lambda i, j, l: (i, j)),
        out_shape=jax.ShapeDtypeStruct((m, n), out_dtype),
        scratch_shapes=[pltpu.VMEM((tm, tn), F32)],
        compiler_params=_cparams(("parallel", "parallel", "arbitrary")),
        name=name,
    )(a, b)


def _branch_kernel(yr_ref, yg_ref, wr_ref, wg_ref, gr_ref, gg_ref, o_ref, wr_bf16, wg_bf16):
    def finish(br, bg):
        o_ref[...] = (_sigmoid(gr_ref[...]) * br + _sigmoid(gg_ref[...]) * bg).astype(o_ref.dtype)

    @pl.when(pl.program_id(1) == 0)
    def _():
        finish(_round_and_dot(yr_ref, wr_ref, wr_bf16), _round_and_dot(yg_ref, wg_ref, wg_bf16))

    @pl.when(pl.program_id(1) != 0)
    def _():
        finish(_dot(yr_ref[...], wr_bf16[...]), _dot(yg_ref[...], wg_bf16[...]))


def _branch_merge(y_rw, y_gdn, w_rw, w_gdn, p):
    m, k = y_rw.shape
    n = w_rw.shape[1]
    tm, tn = TILE_BRANCH
    jr, jg = OFF_GATE_RW // tn, OFF_GATE_GDN // tn
    return pl.pallas_call(
        _branch_kernel,
        grid=(n // tn, m // tm),
        in_specs=[
            pl.BlockSpec((tm, k), lambda j, i: (i, 0)),
            pl.BlockSpec((tm, k), lambda j, i: (i, 0)),
            pl.BlockSpec((k, tn), lambda j, i: (0, j)),
            pl.BlockSpec((k, tn), lambda j, i: (0, j)),
            pl.BlockSpec((tm, tn), lambda j, i: (i, jr + j)),
            pl.BlockSpec((tm, tn), lambda j, i: (i, jg + j)),
        ],
        out_specs=pl.BlockSpec((tm, tn), lambda j, i: (i, j)),
        out_shape=jax.ShapeDtypeStruct((m, n), BF16),
        scratch_shapes=[pltpu.VMEM((k, tn), BF16)] * 2,
        compiler_params=_cparams(("parallel", "arbitrary")),
        name="branch_merge",
    )(y_rw, y_gdn, w_rw, w_gdn, p, p)


def _batch_lanes(ref):
    return jnp.concatenate([ref[b] for b in range(ref.shape[0])], axis=1)


def _tile_lanes(x, n):
    return jnp.concatenate([x] * n, axis=1)


def _token_shift(x_ref, prev_ref, first, mix):
    x = _batch_lanes(x_ref)
    before = jnp.where(first, 0.0, _batch_lanes(prev_ref)[V7X_SUBLANES - 1:V7X_SUBLANES, :])
    prev = jnp.where(_iota(x.shape, 0) == 0, before, pltpu.roll(x, 1, axis=0))
    return x + (prev - x) * mix


def _expand2(x, head0):
    return jnp.concatenate([jnp.where(head0, x, 0.0), jnp.where(head0, 0.0, x)], axis=0)


def _rwkv_kernel(r_ref, k_ref, v_ref, pr_ref, pk_ref, pv_ref, misc_ref, pmisc_ref,
                 cp_ref, gp_ref, mixm_ref, wup_ref, aup_ref, gup_ref, o_ref, s_ref):
    c = CHUNK
    w = V7X_LANES
    batch, _, lw = r_ref.shape
    ci = pl.program_id(0)

    @pl.when(ci == 0)
    def _():
        s_ref[...] = jnp.zeros_like(s_ref)

    first = ci == 0
    mix_r, mix_k, mix_v, w0, a0, k_k, k_a, r_k = (_tile_lanes(cp_ref[i:i + 1, :], batch) for i in range(8))
    gn_w, gn_b = _tile_lanes(gp_ref[0:1, :], batch), _tile_lanes(gp_ref[1:2, :], batch)

    misc_m = _token_shift(misc_ref, pmisc_ref, first, _tile_lanes(mixm_ref[...], batch))
    xr = _token_shift(r_ref, pr_ref, first, mix_r)
    xk = _token_shift(k_ref, pk_ref, first, mix_k)
    xv = _token_shift(v_ref, pv_ref, first, mix_v)

    def lora(fn, lo, hi, w_ref):
        return jnp.concatenate([_dot(_bf(fn(misc_m[:, b * MISC_W + lo:b * MISC_W + hi])), w_ref[...])
                                for b in range(batch)], axis=1)

    wlin = w0 + lora(jnp.tanh, MISC_LW, MISC_LA, wup_ref)
    logw = -math.exp(-0.5) * _sigmoid(wlin)
    a = _sigmoid(a0 + lora(lambda t: t, MISC_LA, MISC_LG, aup_ref))
    gate = lora(_sigmoid, MISC_LG, MISC_W, gup_ref)

    same_head = (_iota((w, w), 0) // RWKV_HEAD) == (_iota((w, w), 1) // RWKV_HEAD)
    bones = jnp.where(same_head, 1.0, 0.0).astype(BF16)
    eye = jnp.where(_iota((w, w), 0) == _iota((w, w), 1), 1.0, 0.0)

    def head_sums(x):
        n = x.shape[1] // w
        s = _dot(_bf(jnp.concatenate(_lane_groups(x), axis=0)), bones)
        return jnp.concatenate([s[g * c:(g + 1) * c] for g in range(n)], axis=1)

    kkr = xk * k_k
    kkn = kkr * lax.rsqrt(head_sums(kkr * kkr) + L2_EPS)
    k2 = xk * (1.0 + (a - 1.0) * k_a)
    am = -kkn
    bm = kkn * a

    cum = _mm_xl(_tril_ones_x2(c), logw)
    cum_ex = cum - logw
    c_last = cum[c - 1:c, :]
    c_mid = cum[c // 2 - 1:c // 2, :]
    e_out = jnp.exp(c_mid - cum)
    e_last = jnp.exp(c_last - cum)
    rt = xr * jnp.exp(cum - c_mid)
    at = am * jnp.exp(cum_ex - c_mid)
    bt = bm * e_out
    kt = k2 * e_out
    a_abs = am * jnp.exp(cum_ex)
    r_abs = xr * jnp.exp(cum)
    bh = bm * e_last
    kh = k2 * e_last
    w_c = jnp.exp(c_last)

    head0 = _iota((c, w), 1) < RWKV_HEAD
    head0_2 = _iota((c, 2 * w), 1) % w < RWKV_HEAD
    wi = _iota((c, 2 * c), 0)
    wj = _iota((c, 2 * c), 1) % c
    strict = wi > wj
    incl = wi >= wj

    gs = range(batch * lw // w)
    G = _lane_groups
    at_g, rt_g, bt_g, kt_g, xv_g = G(at), G(rt), G(bt), G(kt), G(xv)
    aa = [_mm_nt(jnp.concatenate([at_g[g], rt_g[g]], axis=0),
                 jnp.concatenate([_expand2(bt_g[g], head0), _expand2(kt_g[g], head0)], axis=0)) for g in gs]
    a_ab = [jnp.where(strict, aa[g][0:c, 0:2 * c], 0.0) for g in gs]
    a_ak = [jnp.where(strict, aa[g][0:c, 2 * c:4 * c], 0.0) for g in gs]
    a_rb = [jnp.where(incl, aa[g][c:2 * c, 0:2 * c], 0.0) for g in gs]
    a_rk = [jnp.where(incl, aa[g][c:2 * c, 2 * c:4 * c], 0.0) for g in gs]

    xv_e = [_expand2(xv_g[g], head0) for g in gs]
    akv = [_mm(a_ak[g], xv_e[g]) for g in gs]
    eye_w = jnp.where(wi == wj, 1.0, 0.0)
    tw = [eye_w + a_ab[g] for g in gs]
    pw = [_mm(a_ab[g], _expand2(a_ab[g], head0)) for g in gs]
    levels = int(math.log2(c))
    for lvl in range(1, levels):
        if lvl + 1 < levels:
            pt = [_mm(pw[g], _expand2(jnp.concatenate([pw[g], tw[g]], axis=1), head0_2)) for g in gs]
            pw = [pt[g][:, 0:w] for g in gs]
            tw = [tw[g] + pt[g][:, w:2 * w] for g in gs]
        else:
            tw = [tw[g] + _mm(pw[g], _expand2(tw[g], head0)) for g in gs]
    a_abs_g = G(a_abs)
    x = [_mm(tw[g], _expand2(jnp.concatenate([a_abs_g[g], akv[g]], axis=1), head0_2)) for g in gs]
    wm = [x[g][:, 0:w] for g in gs]
    u0 = [x[g][:, w:2 * w] for g in gs]

    bh_g, kh_g, wc_g, rabs_g = G(bh), G(kh), G(w_c), G(r_abs)
    m_mat = [jnp.where(same_head, _mm_tn(bh_g[g], wm[g]), 0.0) + eye * wc_g[g] for g in gs]
    n_mat = [jnp.where(same_head, _mm_tn(jnp.concatenate([bh_g[g], kh_g[g]], axis=0),
                                         jnp.concatenate([u0[g], xv_g[g]], axis=0)), 0.0) for g in gs]
    p_mat = [rabs_g[g] + _mm(a_rb[g], _expand2(wm[g], head0)) for g in gs]
    q_mat = [_mm(jnp.concatenate([a_rb[g], a_rk[g]], axis=1),
                 jnp.concatenate([_expand2(u0[g], head0), xv_e[g]], axis=0)) for g in gs]

    h0 = [_bf(s_ref[g]) for g in gs]
    y = jnp.concatenate([_dot(_bf(p_mat[g]), h0[g]) + q_mat[g] for g in gs], axis=1)
    for g in gs:
        s_ref[g] = _dot(_bf(m_mat[g]), h0[g]) + n_mat[g]

    inv_n = 1.0 / RWKV_HEAD
    d = y - head_sums(y) * inv_n
    var = head_sums(d * d) * inv_n
    yn = d * lax.rsqrt(var + RWKV_GN_EPS) * gn_w + gn_b
    bonus = head_sums(xr * k2 * r_k) * xv
    out = ((yn + bonus) * gate).astype(o_ref.dtype)
    for b in range(batch):
        o_ref[b] = out[:, b * lw:(b + 1) * lw]


def _chunk_specs(batch, c):
    def cur(off, width):
        assert off % width == 0
        return pl.BlockSpec((batch, c, width), lambda t: (0, t, off // width))

    def prev(off, width):
        return pl.BlockSpec((batch, V7X_SUBLANES, width),
                            lambda t: (0, jnp.maximum(t * (c // V7X_SUBLANES) - 1, 0), off // width))

    return cur, prev


def _rwkv_mix(p, cp, gp, mixm, w_up_p, a_up_p, g_up):
    batch, seq, _ = p.shape
    c = CHUNK
    cur, prev = _chunk_specs(batch, c)
    whole = lambda x: pl.BlockSpec(x.shape, lambda t: (0, 0))
    return pl.pallas_call(
        _rwkv_kernel,
        grid=(seq // c,),
        in_specs=[
            cur(OFF_R, RW), cur(OFF_K, RW), cur(OFF_V, RW), prev(OFF_R, RW), prev(OFF_K, RW), prev(OFF_V, RW),
            cur(OFF_MISC, MISC_W), prev(OFF_MISC, MISC_W),
            whole(cp), whole(gp), whole(mixm), whole(w_up_p), whole(a_up_p), whole(g_up),
        ],
        out_specs=pl.BlockSpec((batch, c, RW), lambda t: (0, t, 0)),
        out_shape=jax.ShapeDtypeStruct((batch, seq, RW), BF16),
        scratch_shapes=[pltpu.VMEM((batch * RW // V7X_LANES, V7X_LANES, V7X_LANES), F32)],
        compiler_params=_cparams(("arbitrary",)),
        name="rwkv7_mix",
    )(p, p, p, p, p, p, p, p, cp, gp, mixm, w_up_p, a_up_p, g_up)


def _gdn_kernel(q_ref, k_ref, v_ref, pq_ref, pk_ref, pv_ref, z_ref, misc_ref,
                cwq_ref, cwk_ref, cwv_ref, alog_ref, dtb_ref, nw_ref, o_ref, s_ref):
    c = CHUNK
    w = V7X_LANES
    batch, _, lw = q_ref.shape
    assert lw == N_GDN_HEADS * w
    ci = pl.program_id(0)
    heads = batch * N_GDN_HEADS
    hs = range(heads)
    G = _lane_groups

    @pl.when(ci == 0)
    def _():
        s_ref[...] = jnp.zeros_like(s_ref)

    first = ci == 0

    def conv_silu(x_ref, prev_ref, cw_ref):
        x = _batch_lanes(x_ref)
        prev8 = jnp.where(first, 0.0, _batch_lanes(prev_ref))
        cw = _tile_lanes(cw_ref[...], batch)
        row = _iota(x.shape, 0)
        taps = cw.shape[0]
        acc = x * cw[taps - 1:taps, :]
        for s in range(1, taps):
            tail = jnp.tile(pltpu.roll(prev8, s, axis=0), (c // V7X_SUBLANES, 1))
            acc = acc + jnp.where(row < s, tail, pltpu.roll(x, s, axis=0)) * cw[taps - 1 - s:taps - s, :]
        return acc * _sigmoid(acc)

    def head_sums(x):
        return jnp.concatenate([jnp.broadcast_to(jnp.sum(g, axis=-1, keepdims=True), g.shape) for g in G(x)], axis=1)

    q = conv_silu(q_ref, pq_ref, cwq_ref)
    k = conv_silu(k_ref, pk_ref, cwk_ref)
    v = conv_silu(v_ref, pv_ref, cwv_ref)
    q = q * lax.rsqrt(head_sums(q * q) + L2_EPS) * (GDN_HEAD ** -0.5)
    k = k * lax.rsqrt(head_sums(k * k) + L2_EPS)

    lane2 = _iota((2 * w, w), 0) % w
    lane2r = _iota((c, 2 * w), 1) % w
    bcol, gcol, grow = [], [], []
    for b in range(batch):
        bm = misc_ref[b, :, MISC_LW:MISC_LA]
        beta_hl = jnp.concatenate(_hi_lo(_sigmoid(bm)), axis=1)
        g_full = -jnp.exp(alog_ref[...]) * _softplus(bm + dtb_ref[...])
        gc_hl = jnp.concatenate(_hi_lo(_mm_xl(_tril_ones_x2(c), g_full)), axis=1)
        for h in range(N_GDN_HEADS):
            bcol.append(_dot(beta_hl, jnp.where(lane2 == BETA_LANE + h, 1.0, 0.0).astype(BF16)))
            gcol.append(_dot(gc_hl, jnp.where(lane2 == ALPHA_LANE + h, 1.0, 0.0).astype(BF16)))
            grow.append(_dot_nt(jnp.where(lane2r == ALPHA_LANE + h, 1.0, 0.0).astype(BF16), gc_hl))

    ti = _iota((c, c), 0)
    tj = _iota((c, c), 1)
    causal = ti >= tj
    strict = ti > tj
    decay = [jnp.where(causal, jnp.exp(jnp.where(causal, gcol[h][:, 0:c] - grow[h], 0.0)), 0.0) for h in hs]

    q_g, k_g, v_g = G(q), G(k), G(v)
    kb = [k_g[h] * bcol[h] for h in hs]
    kk = [_mm_nt(jnp.concatenate([kb[h], q_g[h]], axis=0), k_g[h]) for h in hs]
    lneg = [jnp.where(strict, -kk[h][0:c] * decay[h], 0.0) for h in hs]
    a_intra = [kk[h][c:2 * c] * decay[h] for h in hs]

    egc = [jnp.exp(gcol[h]) for h in hs]
    g_last = [gcol[h][c - 1:c, :] for h in hs]
    lane_t = _iota((c, w), 1) >= c
    eye_t = jnp.where(_iota((c, w), 1) == _iota((c, w), 0) + c, 1.0, 0.0)
    zt = [jnp.concatenate([lneg[h], jnp.zeros((c, c), F32)], axis=1) + eye_t for h in hs]
    for _ in range(int(math.log2(c))):
        zb = [_bf(zt[h]) for h in hs]
        zt = [_dot(zb[h][:, 0:c], zb[h]) + jnp.where(lane_t, zt[h], 0.0) for h in hs]
    zeros_x = jnp.zeros((c, 2 * w), F32)
    x = [_mm(zt[h], jnp.concatenate([zeros_x, jnp.concatenate([v_g[h] * bcol[h], kb[h] * egc[h]], axis=1)], axis=0))
         for h in hs]
    uu = [x[h][:, 0:w] for h in hs]
    ww = [x[h][:, w:2 * w] for h in hs]

    kd = [k_g[h] * jnp.exp(g_last[h] - gcol[h]) for h in hs]
    eye = jnp.where(_iota((w, w), 0) == _iota((w, w), 1), 1.0, 0.0)
    m_mat = [eye * jnp.exp(g_last[h]) - _mm_tn(kd[h], ww[h]) for h in hs]
    n_mat = [_mm_tn(kd[h], uu[h]) for h in hs]
    p_mat = [q_g[h] * egc[h] - _mm(a_intra[h], ww[h]) for h in hs]
    q_mat = [_mm(a_intra[h], uu[h]) for h in hs]

    s0 = [_bf(s_ref[h]) for h in hs]
    o = jnp.concatenate([_dot(_bf(p_mat[h]), s0[h]) + q_mat[h] for h in hs], axis=1)
    for h in hs:
        s_ref[h] = _dot(_bf(m_mat[h]), s0[h]) + n_mat[h]

    o = o * lax.rsqrt(head_sums(o * o) * (1.0 / GDN_HEAD) + NORM_EPS) * _tile_lanes(nw_ref[...], heads)
    z = _batch_lanes(z_ref)
    out = (o * (z * _sigmoid(z))).astype(o_ref.dtype)
    for b in range(batch):
        o_ref[b] = out[:, b * lw:(b + 1) * lw]


def _gdn_mix(p, conv_w, alog_l, dtb_l, norm_w):
    batch, seq, _ = p.shape
    c = CHUNK
    cur, prev = _chunk_specs(batch, c)

    def cw(part):
        return pl.BlockSpec((conv_w.shape[0], RW), lambda t: (0, part))

    lane_vec = pl.BlockSpec((1, V7X_LANES), lambda t: (0, 0))
    return pl.pallas_call(
        _gdn_kernel,
        grid=(seq // c,),
        in_specs=[
            cur(OFF_GQ, RW), cur(OFF_GK, RW), cur(OFF_GV, RW), prev(OFF_GQ, RW), prev(OFF_GK, RW), prev(OFF_GV, RW),
            cur(OFF_Z, RW), cur(OFF_MISC, MISC_W),
            cw(0), cw(1), cw(2), lane_vec, lane_vec, lane_vec,
        ],
        out_specs=pl.BlockSpec((batch, c, RW), lambda t: (0, t, 0)),
        out_shape=jax.ShapeDtypeStruct((batch, seq, RW), BF16),
        scratch_shapes=[pltpu.VMEM((batch * N_GDN_HEADS, V7X_LANES, V7X_LANES), F32)],
        compiler_params=_cparams(("arbitrary",)),
        name="gdn_mix",
    )(p, p, p, p, p, p, p, p, conv_w, conv_w, conv_w, alog_l, dtb_l, norm_w)


def _xattn_kernel(x_ref, ymix_ref, gmix_ref, k_ref, v_ref, wq_ref, wo_ref, gpre_ref, gpost_ref, gnext_ref,
                  h_out_ref, f_out_ref):
    ym = ymix_ref[...].astype(F32)
    h = x_ref[...] + ym * lax.rsqrt(jnp.mean(ym * ym, axis=-1, keepdims=True) + NORM_EPS) * gmix_ref[...]
    cn = (h * lax.rsqrt(jnp.mean(h * h, axis=-1, keepdims=True) + NORM_EPS) * gpre_ref[...]).astype(BF16)
    q = _dot(cn, wq_ref[...]).astype(BF16)
    kk = k_ref[0]
    vv = v_ref[0]
    outs = []
    for hh in range(XA_HEADS):
        sl = slice(hh * XA_HEAD, (hh + 1) * XA_HEAD)
        s = _dot_nt(q[:, sl], kk[:, sl]) * (XA_HEAD ** -0.5)
        s = s - jnp.max(s, axis=-1, keepdims=True)
        e = jnp.exp(s)
        pr = e / jnp.sum(e, axis=-1, keepdims=True)
        outs.append(_dot(pr.astype(BF16), vv[:, sl]))
    o = jnp.concatenate(outs, axis=1).astype(BF16)
    y = _dot(o, wo_ref[...])
    h2 = h + y * lax.rsqrt(jnp.mean(y * y, axis=-1, keepdims=True) + NORM_EPS) * gpost_ref[...]
    h_out_ref[...] = h2
    f_out_ref[...] = (h2 * lax.rsqrt(jnp.mean(h2 * h2, axis=-1, keepdims=True) + NORM_EPS)
                      * gnext_ref[...]).astype(f_out_ref.dtype)


def _xattn_block(x, y_mix, g_mix, kmem, vmem, w_q, w_o, g_pre, g_post, g_next, seq):
    n, d = x.shape
    mlen, xw = kmem.shape[1], kmem.shape[2]
    per_b = seq // ROW_TILE
    row = pl.BlockSpec((ROW_TILE, d), lambda i: (i, 0))
    vec = pl.BlockSpec((1, d), lambda i: (0, 0))
    mem = pl.BlockSpec((1, mlen, xw), lambda i: (i // per_b, 0, 0))
    once = pl.Buffered(1)
    return pl.pallas_call(
        _xattn_kernel,
        grid=(n // ROW_TILE,),
        in_specs=[row, row, vec, mem, mem, pl.BlockSpec((d, xw), lambda i: (0, 0), pipeline_mode=once),
                  pl.BlockSpec((xw, d), lambda i: (0, 0), pipeline_mode=once), vec, vec, vec],
        out_specs=[row, row],
        out_shape=[jax.ShapeDtypeStruct((n, d), F32), jax.ShapeDtypeStruct((n, d), BF16)],
        compiler_params=_cparams(("parallel",)),
        name="xattn_block",
    )(x, y_mix, g_mix.reshape(1, d), kmem, vmem, w_q, w_o, g_pre.reshape(1, d), g_post.reshape(1, d),
      g_next.reshape(1, d))


def _relayout_kernel(x_ref, misc_ref, o_ref):
    @pl.when(pl.program_id(0) != OFF_MISC // RELAYOUT_ROWS)
    def _():
        o_ref[...] = x_ref[...].astype(o_ref.dtype)

    @pl.when(pl.program_id(0) == OFF_MISC // RELAYOUT_ROWS)
    def _():
        o_ref[...] = misc_ref[...].astype(o_ref.dtype)


def _relayout_w_in_t(w_in_t):
    n_in, d = w_in_t.shape
    rb = RELAYOUT_ROWS
    assert OFF_GQ % rb == 0 and OFF_MISC % rb == 0 and OFF_GATE_RW % rb == 0 and MISC_W == rb
    zeros = jnp.zeros((MISC_LG - MISC_LA - LORA, d), w_in_t.dtype)
    misc = jnp.concatenate([w_in_t[SRC_LORA:SRC_LORA + LORA], w_in_t[SRC_BA:SRC_BA + 2 * N_GDN_HEADS],
                            w_in_t[SRC_LORA + LORA:SRC_LORA + 2 * LORA], zeros,
                            w_in_t[SRC_LORA + 2 * LORA:SRC_GDN]], axis=0)

    def src_row(j):
        r = j * rb
        src = jnp.where(r < OFF_GQ, r, jnp.where(r < OFF_MISC, r + (SRC_GDN - OFF_GQ),
                                                 jnp.where(r < OFF_GATE_RW, 0, r + (SRC_GATES - OFF_GATE_RW))))
        return pl.multiple_of(src, math.gcd(rb, SRC_GDN, SRC_GATES))

    return pl.pallas_call(
        _relayout_kernel,
        grid=(NP // rb,),
        in_specs=[pl.BlockSpec((pl.Element(rb), pl.Element(d)), lambda j: (src_row(j), 0)),
                  pl.BlockSpec((rb, d), lambda j: (0, 0))],
        out_specs=pl.BlockSpec((rb, d), lambda j: (j, 0)),
        out_shape=jax.ShapeDtypeStruct((NP, d), BF16),
        compiler_params=_cparams(("parallel",)),
        name="relayout_w_in",
    )(w_in_t, misc)


def _layer(x, mem, mix_norm_pre, mix_norm_post, w_in, rwkv_shift_mix, rwkv_w0, rwkv_w_up, rwkv_a0, rwkv_a_up,
           rwkv_g_up, rwkv_k_k, rwkv_k_a, rwkv_r_k, rwkv_gn_w, rwkv_gn_b, gdn_conv_w, gdn_a_log, gdn_dt_bias,
           gdn_norm_w, w_branch_rwkv, w_branch_gdn, w_mix_out, xa_norm_pre, xa_norm_mem, xa_norm_post, xa_w_q,
           xa_w_kv, xa_w_o, mlp_norm_pre, mlp_norm_post, mlp_w_up, mlp_w_down):
    batch, seq, d = x.shape
    n = batch * seq
    h = x.reshape(n, d)

    w_in_b = _relayout_w_in_t(jnp.swapaxes(w_in, 0, 1))
    mix = rwkv_shift_mix
    o = 3 * RW
    cp = jnp.stack([mix[0:RW], mix[RW:2 * RW], mix[2 * RW:o], rwkv_w0, rwkv_a0, rwkv_k_k, rwkv_k_a,
                    rwkv_r_k.reshape(RW)], axis=0)
    gp = jnp.concatenate([rwkv_gn_w[None], rwkv_gn_b[None], jnp.zeros((V7X_SUBLANES - 2, RW), F32)], axis=0)
    zeros = lambda k: jnp.zeros((k,), F32)
    lora_pad = MISC_LA - LORA
    mixm = jnp.concatenate([mix[o:o + LORA], zeros(lora_pad), mix[o + LORA:o + 2 * LORA], zeros(lora_pad),
                            mix[o + 2 * LORA:o + 2 * LORA + 256]])[None]
    w_up_p = jnp.concatenate([rwkv_w_up, jnp.zeros((lora_pad, RW), F32)], axis=0).astype(BF16)
    a_up_p = jnp.concatenate([rwkv_a_up, jnp.zeros((lora_pad, RW), F32)], axis=0).astype(BF16)
    g_up_b = rwkv_g_up.astype(BF16)
    alog_l = jnp.concatenate([zeros(ALPHA_LANE), gdn_a_log])[None]
    dtb_l = jnp.concatenate([zeros(ALPHA_LANE), gdn_dt_bias])[None]

    u = _rmsnorm(h, mix_norm_pre, BF16)
    p = _matmul(u, w_in_b, F32, *TILE_IN_PROJ, b_is_nk=True, name="in_proj")
    p3 = p.reshape(batch, seq, NP)
    y_rw = _rwkv_mix(p3, cp, gp, mixm, w_up_p, a_up_p, g_up_b).reshape(n, RW)
    y_gdn = _gdn_mix(p3, gdn_conv_w, alog_l, dtb_l, gdn_norm_w[None]).reshape(n, RW)
    merged = _branch_merge(y_rw, y_gdn, w_branch_rwkv, w_branch_gdn, p)
    y_mix = _matmul_wcast(merged, w_mix_out, BF16, name="mix_out")

    mlen = mem.shape[1]
    m = _rmsnorm(mem.reshape(batch * mlen, d), xa_norm_mem, BF16)
    xw = XA_HEADS * XA_HEAD
    kv = _matmul(m, xa_w_kv.astype(BF16), BF16, batch * mlen, xw, name="xa_kv")
    kmem = kv[:, :xw].reshape(batch, mlen, xw)
    vmem = kv[:, xw:].reshape(batch, mlen, xw)
    h, f = _xattn_block(h, y_mix, mix_norm_post, kmem, vmem, xa_w_q.astype(BF16), xa_w_o.astype(BF16),
                        xa_norm_pre, xa_norm_post, mlp_norm_pre, seq)

    hmid = _matmul_wcast(f, mlp_w_up, BF16, relu2=True, name="mlp_up")
    y_mlp = _matmul_kacc(hmid, mlp_w_down, BF16, name="mlp_down")
    h = _resid_norm(h, y_mlp, mlp_norm_post)
    return h.reshape(batch, seq, d)


def kernel(x, mem, mix_norm_pre, mix_norm_post, w_in, rwkv_shift_mix, rwkv_w0, rwkv_w_up, rwkv_a0, rwkv_a_up,
           rwkv_g_up, rwkv_k_k, rwkv_k_a, rwkv_r_k, rwkv_gn_w, rwkv_gn_b, gdn_conv_w, gdn_a_log, gdn_dt_bias,
           gdn_norm_w, w_branch_rwkv, w_branch_gdn, w_mix_out, xa_norm_pre, xa_norm_mem, xa_norm_post, xa_w_q,
           xa_w_kv, xa_w_o, mlp_norm_pre, mlp_norm_post, mlp_w_up, mlp_w_down):
    args = (mix_norm_pre, mix_norm_post, w_in, rwkv_shift_mix, rwkv_w0, rwkv_w_up, rwkv_a0, rwkv_a_up,
            rwkv_g_up, rwkv_k_k, rwkv_k_a, rwkv_r_k, rwkv_gn_w, rwkv_gn_b, gdn_conv_w, gdn_a_log, gdn_dt_bias,
            gdn_norm_w, w_branch_rwkv, w_branch_gdn, w_mix_out, xa_norm_pre, xa_norm_mem, xa_norm_post, xa_w_q,
            xa_w_kv, xa_w_o, mlp_norm_pre, mlp_norm_post, mlp_w_up, mlp_w_down)
    h = x
    for layer in range(mix_norm_pre.shape[0]):
        h = _layer(h, mem, *(a[layer] for a in args))
    return h
```

```python
import functools
import math

import jax
import jax.numpy as jnp
from jax import lax
from jax.experimental import pallas as pl
from jax.experimental.pallas import tpu as pltpu

F32 = jnp.float32
BF16 = jnp.bfloat16

V7X_LANES = 128
V7X_SUBLANES = 8
V7X_VMEM_BYTES = 64 * 1024 * 1024
V7X_VMEM_LIMIT_BYTES = V7X_VMEM_BYTES - 4 * 1024 * 1024

NORM_EPS = 1e-6
L2_EPS = 1e-6
RWKV_GN_EPS = 64e-5
RWKV_HEAD = 64
GDN_HEAD = 128
CHUNK = 64
XA_HEADS = 4
XA_HEAD = 128

RW = 2048
D_MODEL = 4096
LORA = 96
N_GDN_HEADS = RW // GDN_HEAD
OFF_R, OFF_K, OFF_V = 0, RW, 2 * RW
OFF_GQ, OFF_GK, OFF_GV = 3 * RW, 4 * RW, 5 * RW
OFF_Z = 6 * RW
OFF_MISC = 7 * RW
MISC_W = 512
MISC_LW, MISC_LA, MISC_LG = 0, 128, 256
BETA_LANE = LORA
ALPHA_LANE = LORA + N_GDN_HEADS
OFF_GATE_RW = OFF_MISC + MISC_W
OFF_GATE_GDN = OFF_GATE_RW + D_MODEL
NP = OFF_GATE_GDN + D_MODEL

SRC_LORA = 3 * RW
SRC_GDN = SRC_LORA + 2 * LORA + 256
SRC_BA = SRC_GDN + 4 * RW
SRC_GATES = SRC_BA + 2 * N_GDN_HEADS
RELAYOUT_ROWS = 512

TILE_IN_PROJ = (1024, 1536)
TILE_PANEL = (2048, 512)
TILE_BRANCH = (1024, 512)
TILE_MLP_DOWN = (2048, 1024, 2048)
K_CHUNK = 1024
ROW_TILE = 256


def _cparams(sem):
    return pltpu.CompilerParams(dimension_semantics=sem, vmem_limit_bytes=V7X_VMEM_LIMIT_BYTES)


def _dot(a, b):
    return jnp.dot(a, b, preferred_element_type=F32)


def _dot_nt(a, b):
    return lax.dot_general(a, b, (((1,), (1,)), ((), ())), preferred_element_type=F32)


def _dot_tn(a, b):
    return lax.dot_general(a, b, (((0,), (0,)), ((), ())), preferred_element_type=F32)


def _bf(x):
    return x.astype(BF16)


def _hi_lo(x):
    hi = x.astype(BF16)
    return hi, (x - hi.astype(F32)).astype(BF16)


def _mm(a, b):
    return _dot(_bf(a), _bf(b))


def _mm_nt(a, b):
    return _dot_nt(_bf(a), _bf(b))


def _mm_tn(a, b):
    return _dot_tn(_bf(a), _bf(b))


def _mm_xl(a01x2, b):
    hi, lo = _hi_lo(b)
    return _dot(a01x2, jnp.concatenate([hi, lo], axis=0))


def _sigmoid(x):
    return 0.5 * jnp.tanh(0.5 * x) + 0.5


def _softplus(x):
    return jnp.maximum(x, 0.0) + jnp.log1p(jnp.exp(-jnp.abs(x)))


def _iota(shape, dim):
    return lax.broadcasted_iota(jnp.int32, shape, dim)


def _lane_groups(x):
    return [x[:, g * V7X_LANES:(g + 1) * V7X_LANES] for g in range(x.shape[1] // V7X_LANES)]


def _tril_ones_x2(c):
    return jnp.where(_iota((c, 2 * c), 0) >= _iota((c, 2 * c), 1) % c, 1.0, 0.0).astype(BF16)


def _rmsnorm_kernel(x_ref, g_ref, o_ref):
    x = x_ref[...]
    inv = lax.rsqrt(jnp.mean(x * x, axis=-1, keepdims=True) + NORM_EPS)
    o_ref[...] = (x * inv * g_ref[...]).astype(o_ref.dtype)


def _rmsnorm(x, gain, out_dtype):
    n, d = x.shape
    rows = min(ROW_TILE, n)
    return pl.pallas_call(
        _rmsnorm_kernel,
        grid=(n // rows,),
        in_specs=[pl.BlockSpec((rows, d), lambda i: (i, 0)), pl.BlockSpec((1, d), lambda i: (0, 0))],
        out_specs=pl.BlockSpec((rows, d), lambda i: (i, 0)),
        out_shape=jax.ShapeDtypeStruct((n, d), out_dtype),
        compiler_params=_cparams(("parallel",)),
        name="rmsnorm",
    )(x, gain.reshape(1, d))


def _resid_norm_kernel(h_ref, y_ref, g_ref, o_ref):
    y = y_ref[...].astype(F32)
    inv = lax.rsqrt(jnp.mean(y * y, axis=-1, keepdims=True) + NORM_EPS)
    o_ref[...] = h_ref[...] + y * inv * g_ref[...]


def _resid_norm(h, y, gain):
    n, d = h.shape
    row = pl.BlockSpec((ROW_TILE, d), lambda i: (i, 0))
    return pl.pallas_call(
        _resid_norm_kernel,
        grid=(n // ROW_TILE,),
        in_specs=[row, row, pl.BlockSpec((1, d), lambda i: (0, 0))],
        out_specs=row,
        out_shape=jax.ShapeDtypeStruct((n, d), F32),
        compiler_params=_cparams(("parallel",)),
        name="resid_norm",
    )(h, y, gain.reshape(1, d))


def _mm_kernel(a_ref, b_ref, o_ref, *, b_is_nk):
    acc = _dot_nt(a_ref[...], b_ref[...]) if b_is_nk else _dot(a_ref[...], b_ref[...])
    o_ref[...] = acc.astype(o_ref.dtype)


def _matmul(a, b, out_dtype, tm, tn, b_is_nk=False, name="matmul"):
    m, k = a.shape
    n = b.shape[0] if b_is_nk else b.shape[1]
    b_spec = pl.BlockSpec((tn, k), lambda i, j: (j, 0)) if b_is_nk else pl.BlockSpec((k, tn), lambda i, j: (0, j))
    return pl.pallas_call(
        functools.partial(_mm_kernel, b_is_nk=b_is_nk),
        grid=(m // tm, n // tn),
        in_specs=[pl.BlockSpec((tm, k), lambda i, j: (i, 0)), b_spec],
        out_specs=pl.BlockSpec((tm, tn), lambda i, j: (i, j)),
        out_shape=jax.ShapeDtypeStruct((m, n), out_dtype),
        compiler_params=_cparams(("parallel", "arbitrary")),
        name=name,
    )(a, b)


def _round_and_dot(a_ref, b_ref, b_bf16):
    acc = None
    for k0 in range(0, a_ref.shape[1], K_CHUNK):
        b_bf16[k0:k0 + K_CHUNK, :] = b_ref[k0:k0 + K_CHUNK, :].astype(BF16)
        part = _dot(a_ref[:, k0:k0 + K_CHUNK], b_bf16[k0:k0 + K_CHUNK, :])
        acc = part if acc is None else acc + part
    return acc


def _mm_wcast_kernel(a_ref, b_ref, o_ref, b_bf16, *, relu2):
    def finish(acc):
        if relu2:
            acc = jnp.square(jnp.maximum(acc, 0.0))
        o_ref[...] = acc.astype(o_ref.dtype)

    @pl.when(pl.program_id(1) == 0)
    def _():
        finish(_round_and_dot(a_ref, b_ref, b_bf16))

    @pl.when(pl.program_id(1) != 0)
    def _():
        finish(_dot(a_ref[...], b_bf16[...]))


def _matmul_wcast(a, b, out_dtype, relu2=False, name="matmul_wcast"):
    m, k = a.shape
    _, n = b.shape
    tm, tn = TILE_PANEL
    return pl.pallas_call(
        functools.partial(_mm_wcast_kernel, relu2=relu2),
        grid=(n // tn, m // tm),
        in_specs=[pl.BlockSpec((tm, k), lambda j, i: (i, 0)), pl.BlockSpec((k, tn), lambda j, i: (0, j))],
        out_specs=pl.BlockSpec((tm, tn), lambda j, i: (i, j)),
        out_shape=jax.ShapeDtypeStruct((m, n), out_dtype),
        scratch_shapes=[pltpu.VMEM((k, tn), BF16)],
        compiler_params=_cparams(("parallel", "arbitrary")),
        name=name,
    )(a, b)


def _mm_kacc_kernel(a_ref, b_ref, o_ref, acc_ref):
    l = pl.program_id(2)

    @pl.when(l == 0)
    def _():
        acc_ref[...] = jnp.zeros_like(acc_ref)

    acc = None
    for k0 in range(0, a_ref.shape[1], K_CHUNK):
        part = _dot(a_ref[:, k0:k0 + K_CHUNK], b_ref[k0:k0 + K_CHUNK, :].astype(BF16))
        acc = part if acc is None else acc + part
    acc_ref[...] += acc

    @pl.when(l == pl.num_programs(2) - 1)
    def _():
        o_ref[...] = acc_ref[...].astype(o_ref.dtype)


def _matmul_kacc(a, b, out_dtype, name="matmul_kacc"):
    m, k = a.shape
    _, n = b.shape
    tm, tn, tk = TILE_MLP_DOWN
    return pl.pallas_call(
        _mm_kacc_kernel,
        grid=(m // tm, n // tn, k // tk),
        in_specs=[pl.BlockSpec((tm, tk), lambda i, j, l: (i, l)), pl.BlockSpec((tk, tn), lambda i, j, l: (l, j))],
        out_specs=pl.BlockSpec((tm, tn), lambda i, j, l: (i, j)),
        out_shape=jax.ShapeDtypeStruct((m, n), out_dtype),
        scratch_shapes=[pltpu.VMEM((tm, tn), F32)],
        compiler_params=_cparams(("parallel", "parallel", "arbitrary")),
        name=name,
    )(a, b)


def _branch_kernel(yr_ref, yg_ref, wr_ref, wg_ref, gr_ref, gg_ref, o_ref, wr_bf16, wg_bf16):
    def finish(br, bg):
        o_ref[...] = (_sigmoid(gr_ref[...]) * br + _sigmoid(gg_ref[...]) * bg).astype(o_ref.dtype)

    @pl.when(pl.program_id(1) == 0)
    def _():
        finish(_round_and_dot(yr_ref, wr_ref, wr_bf16), _round_and_dot(yg_ref, wg_ref, wg_bf16))

    @pl.when(pl.program_id(1) != 0)
    def _():
        finish(_dot(yr_ref[...], wr_bf16[...]), _dot(yg_ref[...], wg_bf16[...]))


def _branch_merge(y_rw, y_gdn, w_rw, w_gdn, p):
    m, k = y_rw.shape
    n = w_rw.shape[1]
    tm, tn = TILE_BRANCH
    jr, jg = OFF_GATE_RW // tn, OFF_GATE_GDN // tn
    return pl.pallas_call(
        _branch_kernel,
        grid=(n // tn, m // tm),
        in_specs=[
            pl.BlockSpec((tm, k), lambda j, i: (i, 0)),
            pl.BlockSpec((tm, k), lambda j, i: (i, 0)),
            pl.BlockSpec((k, tn), lambda j, i: (0, j)),
            pl.BlockSpec((k, tn), lambda j, i: (0, j)),
            pl.BlockSpec((tm, tn), lambda j, i: (i, jr + j)),
            pl.BlockSpec((tm, tn), lambda j, i: (i, jg + j)),
        ],
        out_specs=pl.BlockSpec((tm, tn), lambda j, i: (i, j)),
        out_shape=jax.ShapeDtypeStruct((m, n), BF16),
        scratch_shapes=[pltpu.VMEM((k, tn), BF16)] * 2,
        compiler_params=_cparams(("parallel", "arbitrary")),
        name="branch_merge",
    )(y_rw, y_gdn, w_rw, w_gdn, p, p)


def _batch_lanes(ref):
    return jnp.concatenate([ref[b] for b in range(ref.shape[0])], axis=1)


def _tile_lanes(x, n):
    return jnp.concatenate([x] * n, axis=1)


def _token_shift(x_ref, prev_ref, first, mix):
    x = _batch_lanes(x_ref)
    before = jnp.where(first, 0.0, _batch_lanes(prev_ref)[V7X_SUBLANES - 1:V7X_SUBLANES, :])
    prev = jnp.where(_iota(x.shape, 0) == 0, before, pltpu.roll(x, 1, axis=0))
    return x + (prev - x) * mix


def _expand2(x, head0):
    return jnp.concatenate([jnp.where(head0, x, 0.0), jnp.where(head0, 0.0, x)], axis=0)


def _rwkv_kernel(r_ref, k_ref, v_ref, pr_ref, pk_ref, pv_ref, misc_ref, pmisc_ref,
                 cp_ref, gp_ref, mixm_ref, wup_ref, aup_ref, gup_ref, o_ref, s_ref):
    c = CHUNK
    w = V7X_LANES
    batch, _, lw = r_ref.shape
    ci = pl.program_id(0)

    @pl.when(ci == 0)
    def _():
        s_ref[...] = jnp.zeros_like(s_ref)

    first = ci == 0
    mix_r, mix_k, mix_v, w0, a0, k_k, k_a, r_k = (_tile_lanes(cp_ref[i:i + 1, :], batch) for i in range(8))
    gn_w, gn_b = _tile_lanes(gp_ref[0:1, :], batch), _tile_lanes(gp_ref[1:2, :], batch)

    misc_m = _token_shift(misc_ref, pmisc_ref, first, _tile_lanes(mixm_ref[...], batch))
    xr = _token_shift(r_ref, pr_ref, first, mix_r)
    xk = _token_shift(k_ref, pk_ref, first, mix_k)
    xv = _token_shift(v_ref, pv_ref, first, mix_v)

    def lora(fn, lo, hi, w_ref):
        return jnp.concatenate([_dot(_bf(fn(misc_m[:, b * MISC_W + lo:b * MISC_W + hi])), w_ref[...])
                                for b in range(batch)], axis=1)

    wlin = w0 + lora(jnp.tanh, MISC_LW, MISC_LA, wup_ref)
    logw = -math.exp(-0.5) * _sigmoid(wlin)
    a = _sigmoid(a0 + lora(lambda t: t, MISC_LA, MISC_LG, aup_ref))
    gate = lora(_sigmoid, MISC_LG, MISC_W, gup_ref)

    same_head = (_iota((w, w), 0) // RWKV_HEAD) == (_iota((w, w), 1) // RWKV_HEAD)
    bones = jnp.where(same_head, 1.0, 0.0).astype(BF16)
    eye = jnp.where(_iota((w, w), 0) == _iota((w, w), 1), 1.0, 0.0)

    def head_sums(x):
        n = x.shape[1] // w
        s = _dot(_bf(jnp.concatenate(_lane_groups(x), axis=0)), bones)
        return jnp.concatenate([s[g * c:(g + 1) * c] for g in range(n)], axis=1)

    kkr = xk * k_k
    kkn = kkr * lax.rsqrt(head_sums(kkr * kkr) + L2_EPS)
    k2 = xk * (1.0 + (a - 1.0) * k_a)
    am = -kkn
    bm = kkn * a

    cum = _mm_xl(_tril_ones_x2(c), logw)
    cum_ex = cum - logw
    c_last = cum[c - 1:c, :]
    c_mid = cum[c // 2 - 1:c // 2, :]
    e_out = jnp.exp(c_mid - cum)
    e_last = jnp.exp(c_last - cum)
    rt = xr * jnp.exp(cum - c_mid)
    at = am * jnp.exp(cum_ex - c_mid)
    bt = bm * e_out
    kt = k2 * e_out
    a_abs = am * jnp.exp(cum_ex)
    r_abs = xr * jnp.exp(cum)
    bh = bm * e_last
    kh = k2 * e_last
    w_c = jnp.exp(c_last)

    head0 = _iota((c, w), 1) < RWKV_HEAD
    head0_2 = _iota((c, 2 * w), 1) % w < RWKV_HEAD
    wi = _iota((c, 2 * c), 0)
    wj = _iota((c, 2 * c), 1) % c
    strict = wi > wj
    incl = wi >= wj

    gs = range(batch * lw // w)
    G = _lane_groups
    at_g, rt_g, bt_g, kt_g, xv_g = G(at), G(rt), G(bt), G(kt), G(xv)
    aa = [_mm_nt(jnp.concatenate([at_g[g], rt_g[g]], axis=0),
                 jnp.concatenate([_expand2(bt_g[g], head0), _expand2(kt_g[g], head0)], axis=0)) for g in gs]
    a_ab = [jnp.where(strict, aa[g][0:c, 0:2 * c], 0.0) for g in gs]
    a_ak = [jnp.where(strict, aa[g][0:c, 2 * c:4 * c], 0.0) for g in gs]
    a_rb = [jnp.where(incl, aa[g][c:2 * c, 0:2 * c], 0.0) for g in gs]
    a_rk = [jnp.where(incl, aa[g][c:2 * c, 2 * c:4 * c], 0.0) for g in gs]

    xv_e = [_expand2(xv_g[g], head0) for g in gs]
    akv = [_mm(a_ak[g], xv_e[g]) for g in gs]
    eye_w = jnp.where(wi == wj, 1.0, 0.0)
    tw = [eye_w + a_ab[g] for g in gs]
    pw = [_mm(a_ab[g], _expand2(a_ab[g], head0)) for g in gs]
    levels = int(math.log2(c))
    for lvl in range(1, levels):
        if lvl + 1 < levels:
            pt = [_mm(pw[g], _expand2(jnp.concatenate([pw[g], tw[g]], axis=1), head0_2)) for g in gs]
            pw = [pt[g][:, 0:w] for g in gs]
            tw = [tw[g] + pt[g][:, w:2 * w] for g in gs]
        else:
            tw = [tw[g] + _mm(pw[g], _expand2(tw[g], head0)) for g in gs]
    a_abs_g = G(a_abs)
    x = [_mm(tw[g], _expand2(jnp.concatenate([a_abs_g[g], akv[g]], axis=1), head0_2)) for g in gs]
    wm = [x[g][:, 0:w] for g in gs]
    u0 = [x[g][:, w:2 * w] for g in gs]

    bh_g, kh_g, wc_g, rabs_g = G(bh), G(kh), G(w_c), G(r_abs)
    m_mat = [jnp.where(same_head, _mm_tn(bh_g[g], wm[g]), 0.0) + eye * wc_g[g] for g in gs]
    n_mat = [jnp.where(same_head, _mm_tn(jnp.concatenate([bh_g[g], kh_g[g]], axis=0),
                                         jnp.concatenate([u0[g], xv_g[g]], axis=0)), 0.0) for g in gs]
    p_mat = [rabs_g[g] + _mm(a_rb[g], _expand2(wm[g], head0)) for g in gs]
    q_mat = [_mm(jnp.concatenate([a_rb[g], a_rk[g]], axis=1),
                 jnp.concatenate([_expand2(u0[g], head0), xv_e[g]], axis=0)) for g in gs]

    h0 = [_bf(s_ref[g]) for g in gs]
    y = jnp.concatenate([_dot(_bf(p_mat[g]), h0[g]) + q_mat[g] for g in gs], axis=1)
    for g in gs:
        s_ref[g] = _dot(_bf(m_mat[g]), h0[g]) + n_mat[g]

    inv_n = 1.0 / RWKV_HEAD
    d = y - head_sums(y) * inv_n
    var = head_sums(d * d) * inv_n
    yn = d * lax.rsqrt(var + RWKV_GN_EPS) * gn_w + gn_b
    bonus = head_sums(xr * k2 * r_k) * xv
    out = ((yn + bonus) * gate).astype(o_ref.dtype)
    for b in range(batch):
        o_ref[b] = out[:, b * lw:(b + 1) * lw]


def _chunk_specs(batch, c):
    def cur(off, width):
        assert off % width == 0
        return pl.BlockSpec((batch, c, width), lambda t: (0, t, off // width))

    def prev(off, width):
        return pl.BlockSpec((batch, V7X_SUBLANES, width),
                            lambda t: (0, jnp.maximum(t * (c // V7X_SUBLANES) - 1, 0), off // width))

    return cur, prev


def _rwkv_mix(p, cp, gp, mixm, w_up_p, a_up_p, g_up):
    batch, seq, _ = p.shape
    c = CHUNK
    cur, prev = _chunk_specs(batch, c)
    whole = lambda x: pl.BlockSpec(x.shape, lambda t: (0, 0))
    return pl.pallas_call(
        _rwkv_kernel,
        grid=(seq // c,),
        in_specs=[
            cur(OFF_R, RW), cur(OFF_K, RW), cur(OFF_V, RW), prev(OFF_R, RW), prev(OFF_K, RW), prev(OFF_V, RW),
            cur(OFF_MISC, MISC_W), prev(OFF_MISC, MISC_W),
            whole(cp), whole(gp), whole(mixm), whole(w_up_p), whole(a_up_p), whole(g_up),
        ],
        out_specs=pl.BlockSpec((batch, c, RW), lambda t: (0, t, 0)),
        out_shape=jax.ShapeDtypeStruct((batch, seq, RW), BF16),
        scratch_shapes=[pltpu.VMEM((batch * RW // V7X_LANES, V7X_LANES, V7X_LANES), F32)],
        compiler_params=_cparams(("arbitrary",)),
        name="rwkv7_mix",
    )(p, p, p, p, p, p, p, p, cp, gp, mixm, w_up_p, a_up_p, g_up)


def _gdn_kernel(q_ref, k_ref, v_ref, pq_ref, pk_ref, pv_ref, z_ref, misc_ref,
                cwq_ref, cwk_ref, cwv_ref, alog_ref, dtb_ref, nw_ref, o_ref, s_ref):
    c = CHUNK
    w = V7X_LANES
    batch, _, lw = q_ref.shape
    assert lw == N_GDN_HEADS * w
    ci = pl.program_id(0)
    heads = batch * N_GDN_HEADS
    hs = range(heads)
    G = _lane_groups

    @pl.when(ci == 0)
    def _():
        s_ref[...] = jnp.zeros_like(s_ref)

    first = ci == 0

    def conv_silu(x_ref, prev_ref, cw_ref):
        x = _batch_lanes(x_ref)
        prev8 = jnp.where(first, 0.0, _batch_lanes(prev_ref))
        cw = _tile_lanes(cw_ref[...], batch)
        row = _iota(x.shape, 0)
        taps = cw.shape[0]
        acc = x * cw[taps - 1:taps, :]
        for s in range(1, taps):
            tail = jnp.tile(pltpu.roll(prev8, s, axis=0), (c // V7X_SUBLANES, 1))
            acc = acc + jnp.where(row < s, tail, pltpu.roll(x, s, axis=0)) * cw[taps - 1 - s:taps - s, :]
        return acc * _sigmoid(acc)

    def head_sums(x):
        return jnp.concatenate([jnp.broadcast_to(jnp.sum(g, axis=-1, keepdims=True), g.shape) for g in G(x)], axis=1)

    q = conv_silu(q_ref, pq_ref, cwq_ref)
    k = conv_silu(k_ref, pk_ref, cwk_ref)
    v = conv_silu(v_ref, pv_ref, cwv_ref)
    q = q * lax.rsqrt(head_sums(q * q) + L2_EPS) * (GDN_HEAD ** -0.5)
    k = k * lax.rsqrt(head_sums(k * k) + L2_EPS)

    lane2 = _iota((2 * w, w), 0) % w
    lane2r = _iota((c, 2 * w), 1) % w
    bcol, gcol, grow = [], [], []
    for b in range(batch):
        bm = misc_ref[b, :, MISC_LW:MISC_LA]
        beta_hl = jnp.concatenate(_hi_lo(_sigmoid(bm)), axis=1)
        g_full = -jnp.exp(alog_ref[...]) * _softplus(bm + dtb_ref[...])
        gc_hl = jnp.concatenate(_hi_lo(_mm_xl(_tril_ones_x2(c), g_full)), axis=1)
        for h in range(N_GDN_HEADS):
            bcol.append(_dot(beta_hl, jnp.where(lane2 == BETA_LANE + h, 1.0, 0.0).astype(BF16)))
            gcol.append(_dot(gc_hl, jnp.where(lane2 == ALPHA_LANE + h, 1.0, 0.0).astype(BF16)))
            grow.append(_dot_nt(jnp.where(lane2r == ALPHA_LANE + h, 1.0, 0.0).astype(BF16), gc_hl))

    ti = _iota((c, c), 0)
    tj = _iota((c, c), 1)
    causal = ti >= tj
    strict = ti > tj
    decay = [jnp.where(causal, jnp.exp(jnp.where(causal, gcol[h][:, 0:c] - grow[h], 0.0)), 0.0) for h in hs]

    q_g, k_g, v_g = G(q), G(k), G(v)
    kb = [k_g[h] * bcol[h] for h in hs]
    kk = [_mm_nt(jnp.concatenate([kb[h], q_g[h]], axis=0), k_g[h]) for h in hs]
    lneg = [jnp.where(strict, -kk[h][0:c] * decay[h], 0.0) for h in hs]
    a_intra = [kk[h][c:2 * c] * decay[h] for h in hs]

    egc = [jnp.exp(gcol[h]) for h in hs]
    g_last = [gcol[h][c - 1:c, :] for h in hs]
    lane_t = _iota((c, w), 1) >= c
    eye_t = jnp.where(_iota((c, w), 1) == _iota((c, w), 0) + c, 1.0, 0.0)
    zt = [jnp.concatenate([lneg[h], jnp.zeros((c, c), F32)], axis=1) + eye_t for h in hs]
    for _ in range(int(math.log2(c))):
        zb = [_bf(zt[h]) for h in hs]
        zt = [_dot(zb[h][:, 0:c], zb[h]) + jnp.where(lane_t, zt[h], 0.0) for h in hs]
    zeros_x = jnp.zeros((c, 2 * w), F32)
    x = [_mm(zt[h], jnp.concatenate([zeros_x, jnp.concatenate([v_g[h] * bcol[h], kb[h] * egc[h]], axis=1)], axis=0))
         for h in hs]
    uu = [x[h][:, 0:w] for h in hs]
    ww = [x[h][:, w:2 * w] for h in hs]

    kd = [k_g[h] * jnp.exp(g_last[h] - gcol[h]) for h in hs]
    eye = jnp.where(_iota((w, w), 0) == _iota((w, w), 1), 1.0, 0.0)
    m_mat = [eye * jnp.exp(g_last[h]) - _mm_tn(kd[h], ww[h]) for h in hs]
    n_mat = [_mm_tn(kd[h], uu[h]) for h in hs]
    p_mat = [q_g[h] * egc[h] - _mm(a_intra[h], ww[h]) for h in hs]
    q_mat = [_mm(a_intra[h], uu[h]) for h in hs]

    s0 = [_bf(s_ref[h]) for h in hs]
    o = jnp.concatenate([_dot(_bf(p_mat[h]), s0[h]) + q_mat[h] for h in hs], axis=1)
    for h in hs:
        s_ref[h] = _dot(_bf(m_mat[h]), s0[h]) + n_mat[h]

    o = o * lax.rsqrt(head_sums(o * o) * (1.0 / GDN_HEAD) + NORM_EPS) * _tile_lanes(nw_ref[...], heads)
    z = _batch_lanes(z_ref)
    out = (o * (z * _sigmoid(z))).astype(o_ref.dtype)
    for b in range(batch):
        o_ref[b] = out[:, b * lw:(b + 1) * lw]


def _gdn_mix(p, conv_w, alog_l, dtb_l, norm_w):
    batch, seq, _ = p.shape
    c = CHUNK
    cur, prev = _chunk_specs(batch, c)

    def cw(part):
        return pl.BlockSpec((conv_w.shape[0], RW), lambda t: (0, part))

    lane_vec = pl.BlockSpec((1, V7X_LANES), lambda t: (0, 0))
    return pl.pallas_call(
        _gdn_kernel,
        grid=(seq // c,),
        in_specs=[
            cur(OFF_GQ, RW), cur(OFF_GK, RW), cur(OFF_GV, RW), prev(OFF_GQ, RW), prev(OFF_GK, RW), prev(OFF_GV, RW),
            cur(OFF_Z, RW), cur(OFF_MISC, MISC_W),
            cw(0), cw(1), cw(2), lane_vec, lane_vec, lane_vec,
        ],
        out_specs=pl.BlockSpec((batch, c, RW), lambda t: (0, t, 0)),
        out_shape=jax.ShapeDtypeStruct((batch, seq, RW), BF16),
        scratch_shapes=[pltpu.VMEM((batch * N_GDN_HEADS, V7X_LANES, V7X_LANES), F32)],
        compiler_params=_cparams(("arbitrary",)),
        name="gdn_mix",
    )(p, p, p, p, p, p, p, p, conv_w, conv_w, conv_w, alog_l, dtb_l, norm_w)


def _xattn_kernel(x_ref, ymix_ref, gmix_ref, k_ref, v_ref, wq_ref, wo_ref, gpre_ref, gpost_ref, gnext_ref,
                  h_out_ref, f_out_ref):
    ym = ymix_ref[...].astype(F32)
    h = x_ref[...] + ym * lax.rsqrt(jnp.mean(ym * ym, axis=-1, keepdims=True) + NORM_EPS) * gmix_ref[...]
    cn = (h * lax.rsqrt(jnp.mean(h * h, axis=-1, keepdims=True) + NORM_EPS) * gpre_ref[...]).astype(BF16)
    q = _dot(cn, wq_ref[...]).astype(BF16)
    kk = k_ref[0]
    vv = v_ref[0]
    outs = []
    for hh in range(XA_HEADS):
        sl = slice(hh * XA_HEAD, (hh + 1) * XA_HEAD)
        s = _dot_nt(q[:, sl], kk[:, sl]) * (XA_HEAD ** -0.5)
        s = s - jnp.max(s, axis=-1, keepdims=True)
        e = jnp.exp(s)
        pr = e / jnp.sum(e, axis=-1, keepdims=True)
        outs.append(_dot(pr.astype(BF16), vv[:, sl]))
    o = jnp.concatenate(outs, axis=1).astype(BF16)
    y = _dot(o, wo_ref[...])
    h2 = h + y * lax.rsqrt(jnp.mean(y * y, axis=-1, keepdims=True) + NORM_EPS) * gpost_ref[...]
    h_out_ref[...] = h2
    f_out_ref[...] = (h2 * lax.rsqrt(jnp.mean(h2 * h2, axis=-1, keepdims=True) + NORM_EPS)
                      * gnext_ref[...]).astype(f_out_ref.dtype)


def _xattn_block(x, y_mix, g_mix, kmem, vmem, w_q, w_o, g_pre, g_post, g_next, seq):
    n, d = x.shape
    mlen, xw = kmem.shape[1], kmem.shape[2]
    per_b = seq // ROW_TILE
    row = pl.BlockSpec((ROW_TILE, d), lambda i: (i, 0))
    vec = pl.BlockSpec((1, d), lambda i: (0, 0))
    mem = pl.BlockSpec((1, mlen, xw), lambda i: (i // per_b, 0, 0))
    once = pl.Buffered(1)
    return pl.pallas_call(
        _xattn_kernel,
        grid=(n // ROW_TILE,),
        in_specs=[row, row, vec, mem, mem, pl.BlockSpec((d, xw), lambda i: (0, 0), pipeline_mode=once),
                  pl.BlockSpec((xw, d), lambda i: (0, 0), pipeline_mode=once), vec, vec, vec],
        out_specs=[row, row],
        out_shape=[jax.ShapeDtypeStruct((n, d), F32), jax.ShapeDtypeStruct((n, d), BF16)],
        compiler_params=_cparams(("parallel",)),
        name="xattn_block",
    )(x, y_mix, g_mix.reshape(1, d), kmem, vmem, w_q, w_o, g_pre.reshape(1, d), g_post.reshape(1, d),
      g_next.reshape(1, d))


def _relayout_kernel(x_ref, misc_ref, o_ref):
    @pl.when(pl.program_id(0) != OFF_MISC // RELAYOUT_ROWS)
    def _():
        o_ref[...] = x_ref[...].astype(o_ref.dtype)

    @pl.when(pl.program_id(0) == OFF_MISC // RELAYOUT_ROWS)
    def _():
        o_ref[...] = misc_ref[...].astype(o_ref.dtype)


def _relayout_w_in_t(w_in_t):
    n_in, d = w_in_t.shape
    rb = RELAYOUT_ROWS
    assert OFF_GQ % rb == 0 and OFF_MISC % rb == 0 and OFF_GATE_RW % rb == 0 and MISC_W == rb
    zeros = jnp.zeros((MISC_LG - MISC_LA - LORA, d), w_in_t.dtype)
    misc = jnp.concatenate([w_in_t[SRC_LORA:SRC_LORA + LORA], w_in_t[SRC_BA:SRC_BA + 2 * N_GDN_HEADS],
                            w_in_t[SRC_LORA + LORA:SRC_LORA + 2 * LORA], zeros,
                            w_in_t[SRC_LORA + 2 * LORA:SRC_GDN]], axis=0)

    def src_row(j):
        r = j * rb
        src = jnp.where(r < OFF_GQ, r, jnp.where(r < OFF_MISC, r + (SRC_GDN - OFF_GQ),
                                                 jnp.where(r < OFF_GATE_RW, 0, r + (SRC_GATES - OFF_GATE_RW))))
        return pl.multiple_of(src, math.gcd(rb, SRC_GDN, SRC_GATES))

    return pl.pallas_call(
        _relayout_kernel,
        grid=(NP // rb,),
        in_specs=[pl.BlockSpec((pl.Element(rb), pl.Element(d)), lambda j: (src_row(j), 0)),
                  pl.BlockSpec((rb, d), lambda j: (0, 0))],
        out_specs=pl.BlockSpec((rb, d), lambda j: (j, 0)),
        out_shape=jax.ShapeDtypeStruct((NP, d), BF16),
        compiler_params=_cparams(("parallel",)),
        name="relayout_w_in",
    )(w_in_t, misc)


def _layer(x, mem, mix_norm_pre, mix_norm_post, w_in, rwkv_shift_mix, rwkv_w0, rwkv_w_up, rwkv_a0, rwkv_a_up,
           rwkv_g_up, rwkv_k_k, rwkv_k_a, rwkv_r_k, rwkv_gn_w, rwkv_gn_b, gdn_conv_w, gdn_a_log, gdn_dt_bias,
           gdn_norm_w, w_branch_rwkv, w_branch_gdn, w_mix_out, xa_norm_pre, xa_norm_mem, xa_norm_post, xa_w_q,
           xa_w_kv, xa_w_o, mlp_norm_pre, mlp_norm_post, mlp_w_up, mlp_w_down):
    batch, seq, d = x.shape
    n = batch * seq
    h = x.reshape(n, d)

    w_in_b = _relayout_w_in_t(jnp.swapaxes(w_in, 0, 1))
    mix = rwkv_shift_mix
    o = 3 * RW
    cp = jnp.stack([mix[0:RW], mix[RW:2 * RW], mix[2 * RW:o], rwkv_w0, rwkv_a0, rwkv_k_k, rwkv_k_a,
                    rwkv_r_k.reshape(RW)], axis=0)
    gp = jnp.concatenate([rwkv_gn_w[None], rwkv_gn_b[None], jnp.zeros((V7X_SUBLANES - 2, RW), F32)], axis=0)
    zeros = lambda k: jnp.zeros((k,), F32)
    lora_pad = MISC_LA - LORA
    mixm = jnp.concatenate([mix[o:o + LORA], zeros(lora_pad), mix[o + LORA:o + 2 * LORA], zeros(lora_pad),
                            mix[o + 2 * LORA:o + 2 * LORA + 256]])[None]
    w_up_p = jnp.concatenate([rwkv_w_up, jnp.zeros((lora_pad, RW), F32)], axis=0).astype(BF16)
    a_up_p = jnp.concatenate([rwkv_a_up, jnp.zeros((lora_pad, RW), F32)], axis=0).astype(BF16)
    g_up_b = rwkv_g_up.astype(BF16)
    alog_l = jnp.concatenate([zeros(ALPHA_LANE), gdn_a_log])[None]
    dtb_l = jnp.concatenate([zeros(ALPHA_LANE), gdn_dt_bias])[None]

    u = _rmsnorm(h, mix_norm_pre, BF16)
    p = _matmul(u, w_in_b, F32, *TILE_IN_PROJ, b_is_nk=True, name="in_proj")
    p3 = p.reshape(batch, seq, NP)
    y_rw = _rwkv_mix(p3, cp, gp, mixm, w_up_p, a_up_p, g_up_b).reshape(n, RW)
    y_gdn = _gdn_mix(p3, gdn_conv_w, alog_l, dtb_l, gdn_norm_w[None]).reshape(n, RW)
    merged = _branch_merge(y_rw, y_gdn, w_branch_rwkv, w_branch_gdn, p)
    y_mix = _matmul_wcast(merged, w_mix_out, BF16, name="mix_out")

    mlen = mem.shape[1]
    m = _rmsnorm(mem.reshape(batch * mlen, d), xa_norm_mem, BF16)
    xw = XA_HEADS * XA_HEAD
    kv = _matmul(m, xa_w_kv.astype(BF16), BF16, batch * mlen, xw, name="xa_kv")
    kmem = kv[:, :xw].reshape(batch, mlen, xw)
    vmem = kv[:, xw:].reshape(batch, mlen, xw)
    h, f = _xattn_block(h, y_mix, mix_norm_post, kmem, vmem, xa_w_q.astype(BF16), xa_w_o.astype(BF16),
                        xa_norm_pre, xa_norm_post, mlp_norm_pre, seq)

    hmid = _matmul_wcast(f, mlp_w_up, BF16, relu2=True, name="mlp_up")
    y_mlp = _matmul_kacc(hmid, mlp_w_down, BF16, name="mlp_down")
    h = _resid_norm(h, y_mlp, mlp_norm_post)
    return h.reshape(batch, seq, d)


def kernel(x, mem, mix_norm_pre, mix_norm_post, w_in, rwkv_shift_mix, rwkv_w0, rwkv_w_up, rwkv_a0, rwkv_a_up,
           rwkv_g_up, rwkv_k_k, rwkv_k_a, rwkv_r_k, rwkv_gn_w, rwkv_gn_b, gdn_conv_w, gdn_a_log, gdn_dt_bias,
           gdn_norm_w, w_branch_rwkv, w_branch_gdn, w_mix_out, xa_norm_pre, xa_norm_mem, xa_norm_post, xa_w_q,
           xa_w_kv, xa_w_o, mlp_norm_pre, mlp_norm_post, mlp_w_up, mlp_w_down):
    args = (mix_norm_pre, mix_norm_post, w_in, rwkv_shift_mix, rwkv_w0, rwkv_w_up, rwkv_a0, rwkv_a_up,
            rwkv_g_up, rwkv_k_k, rwkv_k_a, rwkv_r_k, rwkv_gn_w, rwkv_gn_b, gdn_conv_w, gdn_a_log, gdn_dt_bias,
            gdn_norm_w, w_branch_rwkv, w_branch_gdn, w_mix_out, xa_norm_pre, xa_norm_mem, xa_norm_post, xa_w_q,
            xa_w_kv, xa_w_o, mlp_norm_pre, mlp_norm_post, mlp_w_up, mlp_w_down)
    h = x
    for layer in range(mix_norm_pre.shape[0]):
        h = _layer(h, mem, *(a[layer] for a in args))
    return h
```

```python
import functools
import math

import jax
import jax.numpy as jnp
from jax import lax
from jax.experimental import pallas as pl
from jax.experimental.pallas import tpu as pltpu

F32 = jnp.float32
BF16 = jnp.bfloat16

V7X_LANES = 128
V7X_SUBLANES = 8
V7X_VMEM_BYTES = 64 * 1024 * 1024
V7X_VMEM_LIMIT_BYTES = V7X_VMEM_BYTES - 4 * 1024 * 1024

NORM_EPS = 1e-6
L2_EPS = 1e-6
RWKV_GN_EPS = 64e-5
RWKV_HEAD = 64
GDN_HEAD = 128
CHUNK = 64
XA_HEADS = 4
XA_HEAD = 128

RW = 2048
D_MODEL = 4096
LORA = 96
N_GDN_HEADS = RW // GDN_HEAD
OFF_R, OFF_K, OFF_V = 0, RW, 2 * RW
OFF_GQ, OFF_GK, OFF_GV = 3 * RW, 4 * RW, 5 * RW
OFF_Z = 6 * RW
OFF_MISC = 7 * RW
MISC_W = 512
MISC_LW, MISC_LA, MISC_LG = 0, 128, 256
BETA_LANE = LORA
ALPHA_LANE = LORA + N_GDN_HEADS
OFF_GATE_RW = OFF_MISC + MISC_W
OFF_GATE_GDN = OFF_GATE_RW + D_MODEL
NP = OFF_GATE_GDN + D_MODEL

SRC_LORA = 3 * RW
SRC_GDN = SRC_LORA + 2 * LORA + 256
SRC_BA = SRC_GDN + 4 * RW
SRC_GATES = SRC_BA + 2 * N_GDN_HEADS
RELAYOUT_ROWS = 512

TILE_IN_PROJ = (1024, 1536)
TILE_PANEL = (2048, 512)
TILE_BRANCH = (2048, 256)
TILE_MLP_DOWN = (2048, 1024, 2048)
K_CHUNK = 1024
ROW_TILE = 256


def _cparams(sem):
    return pltpu.CompilerParams(dimension_semantics=sem, vmem_limit_bytes=V7X_VMEM_LIMIT_BYTES)


def _dot(a, b):
    return jnp.dot(a, b, preferred_element_type=F32)


def _dot_nt(a, b):
    return lax.dot_general(a, b, (((1,), (1,)), ((), ())), preferred_element_type=F32)


def _dot_tn(a, b):
    return lax.dot_general(a, b, (((0,), (0,)), ((), ())), preferred_element_type=F32)


def _bf(x):
    return x.astype(BF16)


def _hi_lo(x):
    hi = x.astype(BF16)
    return hi, (x - hi.astype(F32)).astype(BF16)


def _mm(a, b):
    return _dot(_bf(a), _bf(b))


def _mm_nt(a, b):
    return _dot_nt(_bf(a), _bf(b))


def _mm_tn(a, b):
    return _dot_tn(_bf(a), _bf(b))


def _mm_xl(a01x2, b):
    hi, lo = _hi_lo(b)
    return _dot(a01x2, jnp.concatenate([hi, lo], axis=0))


def _sigmoid(x):
    return 0.5 * jnp.tanh(0.5 * x) + 0.5


def _softplus(x):
    return jnp.maximum(x, 0.0) + jnp.log1p(jnp.exp(-jnp.abs(x)))


def _iota(shape, dim):
    return lax.broadcasted_iota(jnp.int32, shape, dim)


def _lane_groups(x):
    return [x[:, g * V7X_LANES:(g + 1) * V7X_LANES] for g in range(x.shape[1] // V7X_LANES)]


def _tril_ones_x2(c):
    return jnp.where(_iota((c, 2 * c), 0) >= _iota((c, 2 * c), 1) % c, 1.0, 0.0).astype(BF16)


def _rmsnorm_kernel(x_ref, g_ref, o_ref):
    x = x_ref[...]
    inv = lax.rsqrt(jnp.mean(x * x, axis=-1, keepdims=True) + NORM_EPS)
    o_ref[...] = (x * inv * g_ref[...]).astype(o_ref.dtype)


def _rmsnorm(x, gain, out_dtype):
    n, d = x.shape
    rows = min(ROW_TILE, n)
    return pl.pallas_call(
        _rmsnorm_kernel,
        grid=(n // rows,),
        in_specs=[pl.BlockSpec((rows, d), lambda i: (i, 0)), pl.BlockSpec((1, d), lambda i: (0, 0))],
        out_specs=pl.BlockSpec((rows, d), lambda i: (i, 0)),
        out_shape=jax.ShapeDtypeStruct((n, d), out_dtype),
        compiler_params=_cparams(("parallel",)),
        name="rmsnorm",
    )(x, gain.reshape(1, d))


def _resid_norm_kernel(h_ref, y_ref, g_ref, o_ref):
    y = y_ref[...].astype(F32)
    inv = lax.rsqrt(jnp.mean(y * y, axis=-1, keepdims=True) + NORM_EPS)
    o_ref[...] = h_ref[...] + y * inv * g_ref[...]


def _resid_norm(h, y, gain):
    n, d = h.shape
    row = pl.BlockSpec((ROW_TILE, d), lambda i: (i, 0))
    return pl.pallas_call(
        _resid_norm_kernel,
        grid=(n // ROW_TILE,),
        in_specs=[row, row, pl.BlockSpec((1, d), lambda i: (0, 0))],
        out_specs=row,
        out_shape=jax.ShapeDtypeStruct((n, d), F32),
        compiler_params=_cparams(("parallel",)),
        name="resid_norm",
    )(h, y, gain.reshape(1, d))


def _mm_kernel(a_ref, b_ref, o_ref, *, b_is_nk):
    acc = _dot_nt(a_ref[...], b_ref[...]) if b_is_nk else _dot(a_ref[...], b_ref[...])
    o_ref[...] = acc.astype(o_ref.dtype)


def _matmul(a, b, out_dtype, tm, tn, b_is_nk=False, name="matmul"):
    m, k = a.shape
    n = b.shape[0] if b_is_nk else b.shape[1]
    b_spec = pl.BlockSpec((tn, k), lambda i, j: (j, 0)) if b_is_nk else pl.BlockSpec((k, tn), lambda i, j: (0, j))
    return pl.pallas_call(
        functools.partial(_mm_kernel, b_is_nk=b_is_nk),
        grid=(m // tm, n // tn),
        in_specs=[pl.BlockSpec((tm, k), lambda i, j: (i, 0)), b_spec],
        out_specs=pl.BlockSpec((tm, tn), lambda i, j: (i, j)),
        out_shape=jax.ShapeDtypeStruct((m, n), out_dtype),
        compiler_params=_cparams(("parallel", "arbitrary")),
        name=name,
    )(a, b)


def _round_and_dot(a_ref, b_ref, b_bf16):
    acc = None
    for k0 in range(0, a_ref.shape[1], K_CHUNK):
        b_bf16[k0:k0 + K_CHUNK, :] = b_ref[k0:k0 + K_CHUNK, :].astype(BF16)
        part = _dot(a_ref[:, k0:k0 + K_CHUNK], b_bf16[k0:k0 + K_CHUNK, :])
        acc = part if acc is None else acc + part
    return acc


def _mm_wcast_kernel(a_ref, b_ref, o_ref, b_bf16, *, relu2):
    def finish(acc):
        if relu2:
            acc = jnp.square(jnp.maximum(acc, 0.0))
        o_ref[...] = acc.astype(o_ref.dtype)

    @pl.when(pl.program_id(1) == 0)
    def _():
        finish(_round_and_dot(a_ref, b_ref, b_bf16))

    @pl.when(pl.program_id(1) != 0)
    def _():
        finish(_dot(a_ref[...], b_bf16[...]))


def _matmul_wcast(a, b, out_dtype, relu2=False, name="matmul_wcast"):
    m, k = a.shape
    _, n = b.shape
    tm, tn = TILE_PANEL
    return pl.pallas_call(
        functools.partial(_mm_wcast_kernel, relu2=relu2),
        grid=(n // tn, m // tm),
        in_specs=[pl.BlockSpec((tm, k), lambda j, i: (i, 0)), pl.BlockSpec((k, tn), lambda j, i: (0, j))],
        out_specs=pl.BlockSpec((tm, tn), lambda j, i: (i, j)),
        out_shape=jax.ShapeDtypeStruct((m, n), out_dtype),
        scratch_shapes=[pltpu.VMEM((k, tn), BF16)],
        compiler_params=_cparams(("parallel", "arbitrary")),
        name=name,
    )(a, b)


def _mm_kacc_kernel(a_ref, b_ref, o_ref, acc_ref):
    l = pl.program_id(2)

    @pl.when(l == 0)
    def _():
        acc_ref[...] = jnp.zeros_like(acc_ref)

    acc = None
    for k0 in range(0, a_ref.shape[1], K_CHUNK):
        part = _dot(a_ref[:, k0:k0 + K_CHUNK], b_ref[k0:k0 + K_CHUNK, :].astype(BF16))
        acc = part if acc is None else acc + part
    acc_ref[...] += acc

    @pl.when(l == pl.num_programs(2) - 1)
    def _():
        o_ref[...] = acc_ref[...].astype(o_ref.dtype)


def _matmul_kacc(a, b, out_dtype, name="matmul_kacc"):
    m, k = a.shape
    _, n = b.shape
    tm, tn, tk = TILE_MLP_DOWN
    return pl.pallas_call(
        _mm_kacc_kernel,
        grid=(m // tm, n // tn, k // tk),
        in_specs=[pl.BlockSpec((tm, tk), lambda i, j, l: (i, l)), pl.BlockSpec((tk, tn), lambda i, j, l: (l, j))],
        out_specs=pl.BlockSpec((tm, tn), lambda i, j, l: (i, j)),
        out_shape=jax.ShapeDtypeStruct((m, n), out_dtype),
        scratch_shapes=[pltpu.VMEM((tm, tn), F32)],
        compiler_params=_cparams(("parallel", "parallel", "arbitrary")),
        name=name,
    )(a, b)


def _branch_kernel(yr_ref, yg_ref, wr_ref, wg_ref, gr_ref, gg_ref, o_ref, wr_bf16, wg_bf16):
    def finish(br, bg):
        o_ref[...] = (_sigmoid(gr_ref[...]) * br + _sigmoid(gg_ref[...]) * bg).astype(o_ref.dtype)

    @pl.when(pl.program_id(1) == 0)
    def _():
        finish(_round_and_dot(yr_ref, wr_ref, wr_bf16), _round_and_dot(yg_ref, wg_ref, wg_bf16))

    @pl.when(pl.program_id(1) != 0)
    def _():
        finish(_dot(yr_ref[...], wr_bf16[...]), _dot(yg_ref[...], wg_bf16[...]))


def _branch_merge(y_rw, y_gdn, w_rw, w_gdn, p):
    m, k = y_rw.shape
    n = w_rw.shape[1]
    tm, tn = TILE_BRANCH
    jr, jg = OFF_GATE_RW // tn, OFF_GATE_GDN // tn
    return pl.pallas_call(
        _branch_kernel,
        grid=(n // tn, m // tm),
        in_specs=[
            pl.BlockSpec((tm, k), lambda j, i: (i, 0)),
            pl.BlockSpec((tm, k), lambda j, i: (i, 0)),
            pl.BlockSpec((k, tn), lambda j, i: (0, j)),
            pl.BlockSpec((k, tn), lambda j, i: (0, j)),
            pl.BlockSpec((tm, tn), lambda j, i: (i, jr + j)),
            pl.BlockSpec((tm, tn), lambda j, i: (i, jg + j)),
        ],
        out_specs=pl.BlockSpec((tm, tn), lambda j, i: (i, j)),
        out_shape=jax.ShapeDtypeStruct((m, n), BF16),
        scratch_shapes=[pltpu.VMEM((k, tn), BF16)] * 2,
        compiler_params=_cparams(("parallel", "arbitrary")),
        name="branch_merge",
    )(y_rw, y_gdn, w_rw, w_gdn, p, p)


def _batch_lanes(ref):
    return jnp.concatenate([ref[b] for b in range(ref.shape[0])], axis=1)


def _tile_lanes(x, n):
    return jnp.concatenate([x] * n, axis=1)


def _token_shift(x_ref, prev_ref, first, mix):
    x = _batch_lanes(x_ref)
    before = jnp.where(first, 0.0, _batch_lanes(prev_ref)[V7X_SUBLANES - 1:V7X_SUBLANES, :])
    prev = jnp.where(_iota(x.shape, 0) == 0, before, pltpu.roll(x, 1, axis=0))
    return x + (prev - x) * mix


def _expand2(x, head0):
    return jnp.concatenate([jnp.where(head0, x, 0.0), jnp.where(head0, 0.0, x)], axis=0)


def _rwkv_kernel(r_ref, k_ref, v_ref, pr_ref, pk_ref, pv_ref, misc_ref, pmisc_ref,
                 cp_ref, gp_ref, mixm_ref, wup_ref, aup_ref, gup_ref, o_ref, s_ref):
    c = CHUNK
    w = V7X_LANES
    batch, _, lw = r_ref.shape
    ci = pl.program_id(0)

    @pl.when(ci == 0)
    def _():
        s_ref[...] = jnp.zeros_like(s_ref)

    first = ci == 0
    mix_r, mix_k, mix_v, w0, a0, k_k, k_a, r_k = (_tile_lanes(cp_ref[i:i + 1, :], batch) for i in range(8))
    gn_w, gn_b = _tile_lanes(gp_ref[0:1, :], batch), _tile_lanes(gp_ref[1:2, :], batch)

    misc_m = _token_shift(misc_ref, pmisc_ref, first, _tile_lanes(mixm_ref[...], batch))
    xr = _token_shift(r_ref, pr_ref, first, mix_r)
    xk = _token_shift(k_ref, pk_ref, first, mix_k)
    xv = _token_shift(v_ref, pv_ref, first, mix_v)

    def lora(fn, lo, hi, w_ref):
        return jnp.concatenate([_dot(_bf(fn(misc_m[:, b * MISC_W + lo:b * MISC_W + hi])), w_ref[...])
                                for b in range(batch)], axis=1)

    wlin = w0 + lora(jnp.tanh, MISC_LW, MISC_LA, wup_ref)
    logw = -math.exp(-0.5) * _sigmoid(wlin)
    a = _sigmoid(a0 + lora(lambda t: t, MISC_LA, MISC_LG, aup_ref))
    gate = lora(_sigmoid, MISC_LG, MISC_W, gup_ref)

    same_head = (_iota((w, w), 0) // RWKV_HEAD) == (_iota((w, w), 1) // RWKV_HEAD)
    bones = jnp.where(same_head, 1.0, 0.0).astype(BF16)
    eye = jnp.where(_iota((w, w), 0) == _iota((w, w), 1), 1.0, 0.0)

    def head_sums(x):
        n = x.shape[1] // w
        s = _dot(_bf(jnp.concatenate(_lane_groups(x), axis=0)), bones)
        return jnp.concatenate([s[g * c:(g + 1) * c] for g in range(n)], axis=1)

    kkr = xk * k_k
    kkn = kkr * lax.rsqrt(head_sums(kkr * kkr) + L2_EPS)
    k2 = xk * (1.0 + (a - 1.0) * k_a)
    am = -kkn
    bm = kkn * a

    cum = _mm_xl(_tril_ones_x2(c), logw)
    cum_ex = cum - logw
    c_last = cum[c - 1:c, :]
    c_mid = cum[c // 2 - 1:c // 2, :]
    e_out = jnp.exp(c_mid - cum)
    e_last = jnp.exp(c_last - cum)
    rt = xr * jnp.exp(cum - c_mid)
    at = am * jnp.exp(cum_ex - c_mid)
    bt = bm * e_out
    kt = k2 * e_out
    a_abs = am * jnp.exp(cum_ex)
    r_abs = xr * jnp.exp(cum)
    bh = bm * e_last
    kh = k2 * e_last
    w_c = jnp.exp(c_last)

    head0 = _iota((c, w), 1) < RWKV_HEAD
    head0_2 = _iota((c, 2 * w), 1) % w < RWKV_HEAD
    wi = _iota((c, 2 * c), 0)
    wj = _iota((c, 2 * c), 1) % c
    strict = wi > wj
    incl = wi >= wj

    gs = range(batch * lw // w)
    G = _lane_groups
    at_g, rt_g, bt_g, kt_g, xv_g = G(at), G(rt), G(bt), G(kt), G(xv)
    aa = [_mm_nt(jnp.concatenate([at_g[g], rt_g[g]], axis=0),
                 jnp.concatenate([_expand2(bt_g[g], head0), _expand2(kt_g[g], head0)], axis=0)) for g in gs]
    a_ab = [jnp.where(strict, aa[g][0:c, 0:2 * c], 0.0) for g in gs]
    a_ak = [jnp.where(strict, aa[g][0:c, 2 * c:4 * c], 0.0) for g in gs]
    a_rb = [jnp.where(incl, aa[g][c:2 * c, 0:2 * c], 0.0) for g in gs]
    a_rk = [jnp.where(incl, aa[g][c:2 * c, 2 * c:4 * c], 0.0) for g in gs]

    xv_e = [_expand2(xv_g[g], head0) for g in gs]
    akv = [_mm(a_ak[g], xv_e[g]) for g in gs]
    eye_w = jnp.where(wi == wj, 1.0, 0.0)
    tw = [eye_w + a_ab[g] for g in gs]
    pw = [_mm(a_ab[g], _expand2(a_ab[g], head0)) for g in gs]
    levels = int(math.log2(c))
    for lvl in range(1, levels):
        if lvl + 1 < levels:
            pt = [_mm(pw[g], _expand2(jnp.concatenate([pw[g], tw[g]], axis=1), head0_2)) for g in gs]
            pw = [pt[g][:, 0:w] for g in gs]
            tw = [tw[g] + pt[g][:, w:2 * w] for g in gs]
        else:
            tw = [tw[g] + _mm(pw[g], _expand2(tw[g], head0)) for g in gs]
    a_abs_g = G(a_abs)
    x = [_mm(tw[g], _expand2(jnp.concatenate([a_abs_g[g], akv[g]], axis=1), head0_2)) for g in gs]
    wm = [x[g][:, 0:w] for g in gs]
    u0 = [x[g][:, w:2 * w] for g in gs]

    bh_g, kh_g, wc_g, rabs_g = G(bh), G(kh), G(w_c), G(r_abs)
    m_mat = [jnp.where(same_head, _mm_tn(bh_g[g], wm[g]), 0.0) + eye * wc_g[g] for g in gs]
    n_mat = [jnp.where(same_head, _mm_tn(jnp.concatenate([bh_g[g], kh_g[g]], axis=0),
                                         jnp.concatenate([u0[g], xv_g[g]], axis=0)), 0.0) for g in gs]
    p_mat = [rabs_g[g] + _mm(a_rb[g], _expand2(wm[g], head0)) for g in gs]
    q_mat = [_mm(jnp.concatenate([a_rb[g], a_rk[g]], axis=1),
                 jnp.concatenate([_expand2(u0[g], head0), xv_e[g]], axis=0)) for g in gs]

    h0 = [_bf(s_ref[g]) for g in gs]
    y = jnp.concatenate([_dot(_bf(p_mat[g]), h0[g]) + q_mat[g] for g in gs], axis=1)
    for g in gs:
        s_ref[g] = _dot(_bf(m_mat[g]), h0[g]) + n_mat[g]

    inv_n = 1.0 / RWKV_HEAD
    d = y - head_sums(y) * inv_n
    var = head_sums(d * d) * inv_n
    yn = d * lax.rsqrt(var + RWKV_GN_EPS) * gn_w + gn_b
    bonus = head_sums(xr * k2 * r_k) * xv
    out = ((yn + bonus) * gate).astype(o_ref.dtype)
    for b in range(batch):
        o_ref[b] = out[:, b * lw:(b + 1) * lw]


def _chunk_specs(batch, c):
    def cur(off, width):
        assert off % width == 0
        return pl.BlockSpec((batch, c, width), lambda t: (0, t, off // width))

    def prev(off, width):
        return pl.BlockSpec((batch, V7X_SUBLANES, width),
                            lambda t: (0, jnp.maximum(t * (c // V7X_SUBLANES) - 1, 0), off // width))

    return cur, prev


def _rwkv_mix(p, cp, gp, mixm, w_up_p, a_up_p, g_up):
    batch, seq, _ = p.shape
    c = CHUNK
    cur, prev = _chunk_specs(batch, c)
    whole = lambda x: pl.BlockSpec(x.shape, lambda t: (0, 0))
    return pl.pallas_call(
        _rwkv_kernel,
        grid=(seq // c,),
        in_specs=[
            cur(OFF_R, RW), cur(OFF_K, RW), cur(OFF_V, RW), prev(OFF_R, RW), prev(OFF_K, RW), prev(OFF_V, RW),
            cur(OFF_MISC, MISC_W), prev(OFF_MISC, MISC_W),
            whole(cp), whole(gp), whole(mixm), whole(w_up_p), whole(a_up_p), whole(g_up),
        ],
        out_specs=pl.BlockSpec((batch, c, RW), lambda t: (0, t, 0)),
        out_shape=jax.ShapeDtypeStruct((batch, seq, RW), BF16),
        scratch_shapes=[pltpu.VMEM((batch * RW // V7X_LANES, V7X_LANES, V7X_LANES), F32)],
        compiler_params=_cparams(("arbitrary",)),
        name="rwkv7_mix",
    )(p, p, p, p, p, p, p, p, cp, gp, mixm, w_up_p, a_up_p, g_up)


def _gdn_kernel(q_ref, k_ref, v_ref, pq_ref, pk_ref, pv_ref, z_ref, misc_ref,
                cwq_ref, cwk_ref, cwv_ref, alog_ref, dtb_ref, nw_ref, o_ref, s_ref):
    c = CHUNK
    w = V7X_LANES
    batch, _, lw = q_ref.shape
    assert lw == N_GDN_HEADS * w
    ci = pl.program_id(0)
    heads = batch * N_GDN_HEADS
    hs = range(heads)
    G = _lane_groups

    @pl.when(ci == 0)
    def _():
        s_ref[...] = jnp.zeros_like(s_ref)

    first = ci == 0

    def conv_silu(x_ref, prev_ref, cw_ref):
        x = _batch_lanes(x_ref)
        prev8 = jnp.where(first, 0.0, _batch_lanes(prev_ref))
        cw = _tile_lanes(cw_ref[...], batch)
        row = _iota(x.shape, 0)
        taps = cw.shape[0]
        acc = x * cw[taps - 1:taps, :]
        for s in range(1, taps):
            tail = jnp.tile(pltpu.roll(prev8, s, axis=0), (c // V7X_SUBLANES, 1))
            acc = acc + jnp.where(row < s, tail, pltpu.roll(x, s, axis=0)) * cw[taps - 1 - s:taps - s, :]
        return acc * _sigmoid(acc)

    def head_sums(x):
        return jnp.concatenate([jnp.broadcast_to(jnp.sum(g, axis=-1, keepdims=True), g.shape) for g in G(x)], axis=1)

    q = conv_silu(q_ref, pq_ref, cwq_ref)
    k = conv_silu(k_ref, pk_ref, cwk_ref)
    v = conv_silu(v_ref, pv_ref, cwv_ref)
    q = q * lax.rsqrt(head_sums(q * q) + L2_EPS) * (GDN_HEAD ** -0.5)
    k = k * lax.rsqrt(head_sums(k * k) + L2_EPS)

    lane2 = _iota((2 * w, w), 0) % w
    lane2r = _iota((c, 2 * w), 1) % w
    bcol, gcol, grow = [], [], []
    for b in range(batch):
        bm = misc_ref[b, :, MISC_LW:MISC_LA]
        beta_hl = jnp.concatenate(_hi_lo(_sigmoid(bm)), axis=1)
        g_full = -jnp.exp(alog_ref[...]) * _softplus(bm + dtb_ref[...])
        gc_hl = jnp.concatenate(_hi_lo(_mm_xl(_tril_ones_x2(c), g_full)), axis=1)
        for h in range(N_GDN_HEADS):
            bcol.append(_dot(beta_hl, jnp.where(lane2 == BETA_LANE + h, 1.0, 0.0).astype(BF16)))
            gcol.append(_dot(gc_hl, jnp.where(lane2 == ALPHA_LANE + h, 1.0, 0.0).astype(BF16)))
            grow.append(_dot_nt(jnp.where(lane2r == ALPHA_LANE + h, 1.0, 0.0).astype(BF16), gc_hl))

    ti = _iota((c, c), 0)
    tj = _iota((c, c), 1)
    causal = ti >= tj
    strict = ti > tj
    decay = [jnp.where(causal, jnp.exp(jnp.where(causal, gcol[h][:, 0:c] - grow[h], 0.0)), 0.0) for h in hs]

    q_g, k_g, v_g = G(q), G(k), G(v)
    kb = [k_g[h] * bcol[h] for h in hs]
    kk = [_mm_nt(jnp.concatenate([kb[h], q_g[h]], axis=0), k_g[h]) for h in hs]
    lneg = [jnp.where(strict, -kk[h][0:c] * decay[h], 0.0) for h in hs]
    a_intra = [kk[h][c:2 * c] * decay[h] for h in hs]

    egc = [jnp.exp(gcol[h]) for h in hs]
    g_last = [gcol[h][c - 1:c, :] for h in hs]
    lane_t = _iota((c, w), 1) >= c
    eye_t = jnp.where(_iota((c, w), 1) == _iota((c, w), 0) + c, 1.0, 0.0)
    zt = [jnp.concatenate([lneg[h], jnp.zeros((c, c), F32)], axis=1) + eye_t for h in hs]
    for _ in range(int(math.log2(c))):
        zb = [_bf(zt[h]) for h in hs]
        zt = [_dot(zb[h][:, 0:c], zb[h]) + jnp.where(lane_t, zt[h], 0.0) for h in hs]
    zeros_x = jnp.zeros((c, 2 * w), F32)
    x = [_mm(zt[h], jnp.concatenate([zeros_x, jnp.concatenate([v_g[h] * bcol[h], kb[h] * egc[h]], axis=1)], axis=0))
         for h in hs]
    uu = [x[h][:, 0:w] for h in hs]
    ww = [x[h][:, w:2 * w] for h in hs]

    kd = [k_g[h] * jnp.exp(g_last[h] - gcol[h]) for h in hs]
    eye = jnp.where(_iota((w, w), 0) == _iota((w, w), 1), 1.0, 0.0)
    m_mat = [eye * jnp.exp(g_last[h]) - _mm_tn(kd[h], ww[h]) for h in hs]
    n_mat = [_mm_tn(kd[h], uu[h]) for h in hs]
    p_mat = [q_g[h] * egc[h] - _mm(a_intra[h], ww[h]) for h in hs]
    q_mat = [_mm(a_intra[h], uu[h]) for h in hs]

    s0 = [_bf(s_ref[h]) for h in hs]
    o = jnp.concatenate([_dot(_bf(p_mat[h]), s0[h]) + q_mat[h] for h in hs], axis=1)
    for h in hs:
        s_ref[h] = _dot(_bf(m_mat[h]), s0[h]) + n_mat[h]

    o = o * lax.rsqrt(head_sums(o * o) * (1.0 / GDN_HEAD) + NORM_EPS) * _tile_lanes(nw_ref[...], heads)
    z = _batch_lanes(z_ref)
    out = (o * (z * _sigmoid(z))).astype(o_ref.dtype)
    for b in range(batch):
        o_ref[b] = out[:, b * lw:(b + 1) * lw]


def _gdn_mix(p, conv_w, alog_l, dtb_l, norm_w):
    batch, seq, _ = p.shape
    c = CHUNK
    cur, prev = _chunk_specs(batch, c)

    def cw(part):
        return pl.BlockSpec((conv_w.shape[0], RW), lambda t: (0, part))

    lane_vec = pl.BlockSpec((1, V7X_LANES), lambda t: (0, 0))
    return pl.pallas_call(
        _gdn_kernel,
        grid=(seq // c,),
        in_specs=[
            cur(OFF_GQ, RW), cur(OFF_GK, RW), cur(OFF_GV, RW), prev(OFF_GQ, RW), prev(OFF_GK, RW), prev(OFF_GV, RW),
            cur(OFF_Z, RW), cur(OFF_MISC, MISC_W),
            cw(0), cw(1), cw(2), lane_vec, lane_vec, lane_vec,
        ],
        out_specs=pl.BlockSpec((batch, c, RW), lambda t: (0, t, 0)),
        out_shape=jax.ShapeDtypeStruct((batch, seq, RW), BF16),
        scratch_shapes=[pltpu.VMEM((batch * N_GDN_HEADS, V7X_LANES, V7X_LANES), F32)],
        compiler_params=_cparams(("arbitrary",)),
        name="gdn_mix",
    )(p, p, p, p, p, p, p, p, conv_w, conv_w, conv_w, alog_l, dtb_l, norm_w)


def _xattn_kernel(x_ref, ymix_ref, gmix_ref, k_ref, v_ref, wq_ref, wo_ref, gpre_ref, gpost_ref, gnext_ref,
                  h_out_ref, f_out_ref):
    ym = ymix_ref[...].astype(F32)
    h = x_ref[...] + ym * lax.rsqrt(jnp.mean(ym * ym, axis=-1, keepdims=True) + NORM_EPS) * gmix_ref[...]
    cn = (h * lax.rsqrt(jnp.mean(h * h, axis=-1, keepdims=True) + NORM_EPS) * gpre_ref[...]).astype(BF16)
    q = _dot(cn, wq_ref[...]).astype(BF16)
    kk = k_ref[0]
    vv = v_ref[0]
    outs = []
    for hh in range(XA_HEADS):
        sl = slice(hh * XA_HEAD, (hh + 1) * XA_HEAD)
        s = _dot_nt(q[:, sl], kk[:, sl]) * (XA_HEAD ** -0.5)
        s = s - jnp.max(s, axis=-1, keepdims=True)
        e = jnp.exp(s)
        pr = e / jnp.sum(e, axis=-1, keepdims=True)
        outs.append(_dot(pr.astype(BF16), vv[:, sl]))
    o = jnp.concatenate(outs, axis=1).astype(BF16)
    y = _dot(o, wo_ref[...])
    h2 = h + y * lax.rsqrt(jnp.mean(y * y, axis=-1, keepdims=True) + NORM_EPS) * gpost_ref[...]
    h_out_ref[...] = h2
    f_out_ref[...] = (h2 * lax.rsqrt(jnp.mean(h2 * h2, axis=-1, keepdims=True) + NORM_EPS)
                      * gnext_ref[...]).astype(f_out_ref.dtype)


def _xattn_block(x, y_mix, g_mix, kmem, vmem, w_q, w_o, g_pre, g_post, g_next, seq):
    n, d = x.shape
    mlen, xw = kmem.shape[1], kmem.shape[2]
    per_b = seq // ROW_TILE
    row = pl.BlockSpec((ROW_TILE, d), lambda i: (i, 0))
    vec = pl.BlockSpec((1, d), lambda i: (0, 0))
    mem = pl.BlockSpec((1, mlen, xw), lambda i: (i // per_b, 0, 0))
    once = pl.Buffered(1)
    return pl.pallas_call(
        _xattn_kernel,
        grid=(n // ROW_TILE,),
        in_specs=[row, row, vec, mem, mem, pl.BlockSpec((d, xw), lambda i: (0, 0), pipeline_mode=once),
                  pl.BlockSpec((xw, d), lambda i: (0, 0), pipeline_mode=once), vec, vec, vec],
        out_specs=[row, row],
        out_shape=[jax.ShapeDtypeStruct((n, d), F32), jax.ShapeDtypeStruct((n, d), BF16)],
        compiler_params=_cparams(("parallel",)),
        name="xattn_block",
    )(x, y_mix, g_mix.reshape(1, d), kmem, vmem, w_q, w_o, g_pre.reshape(1, d), g_post.reshape(1, d),
      g_next.reshape(1, d))


def _relayout_kernel(x_ref, misc_ref, o_ref):
    @pl.when(pl.program_id(0) != OFF_MISC // RELAYOUT_ROWS)
    def _():
        o_ref[...] = x_ref[...].astype(o_ref.dtype)

    @pl.when(pl.program_id(0) == OFF_MISC // RELAYOUT_ROWS)
    def _():
        o_ref[...] = misc_ref[...].astype(o_ref.dtype)


def _relayout_w_in_t(w_in_t):
    n_in, d = w_in_t.shape
    rb = RELAYOUT_ROWS
    assert OFF_GQ % rb == 0 and OFF_MISC % rb == 0 and OFF_GATE_RW % rb == 0 and MISC_W == rb
    zeros = jnp.zeros((MISC_LG - MISC_LA - LORA, d), w_in_t.dtype)
    misc = jnp.concatenate([w_in_t[SRC_LORA:SRC_LORA + LORA], w_in_t[SRC_BA:SRC_BA + 2 * N_GDN_HEADS],
                            w_in_t[SRC_LORA + LORA:SRC_LORA + 2 * LORA], zeros,
                            w_in_t[SRC_LORA + 2 * LORA:SRC_GDN]], axis=0)

    def src_row(j):
        r = j * rb
        src = jnp.where(r < OFF_GQ, r, jnp.where(r < OFF_MISC, r + (SRC_GDN - OFF_GQ),
                                                 jnp.where(r < OFF_GATE_RW, 0, r + (SRC_GATES - OFF_GATE_RW))))
        return pl.multiple_of(src, math.gcd(rb, SRC_GDN, SRC_GATES))

    return pl.pallas_call(
        _relayout_kernel,
        grid=(NP // rb,),
        in_specs=[pl.BlockSpec((pl.Element(rb), pl.Element(d)), lambda j: (src_row(j), 0)),
                  pl.BlockSpec((rb, d), lambda j: (0, 0))],
        out_specs=pl.BlockSpec((rb, d), lambda j: (j, 0)),
        out_shape=jax.ShapeDtypeStruct((NP, d), BF16),
        compiler_params=_cparams(("parallel",)),
        name="relayout_w_in",
    )(w_in_t, misc)


def _layer(x, mem, mix_norm_pre, mix_norm_post, w_in, rwkv_shift_mix, rwkv_w0, rwkv_w_up, rwkv_a0, rwkv_a_up,
           rwkv_g_up, rwkv_k_k, rwkv_k_a, rwkv_r_k, rwkv_gn_w, rwkv_gn_b, gdn_conv_w, gdn_a_log, gdn_dt_bias,
           gdn_norm_w, w_branch_rwkv, w_branch_gdn, w_mix_out, xa_norm_pre, xa_norm_mem, xa_norm_post, xa_w_q,
           xa_w_kv, xa_w_o, mlp_norm_pre, mlp_norm_post, mlp_w_up, mlp_w_down):
    batch, seq, d = x.shape
    n = batch * seq
    h = x.reshape(n, d)

    w_in_b = _relayout_w_in_t(jnp.swapaxes(w_in, 0, 1))
    mix = rwkv_shift_mix
    o = 3 * RW
    cp = jnp.stack([mix[0:RW], mix[RW:2 * RW], mix[2 * RW:o], rwkv_w0, rwkv_a0, rwkv_k_k, rwkv_k_a,
                    rwkv_r_k.reshape(RW)], axis=0)
    gp = jnp.concatenate([rwkv_gn_w[None], rwkv_gn_b[None], jnp.zeros((V7X_SUBLANES - 2, RW), F32)], axis=0)
    zeros = lambda k: jnp.zeros((k,), F32)
    lora_pad = MISC_LA - LORA
    mixm = jnp.concatenate([mix[o:o + LORA], zeros(lora_pad), mix[o + LORA:o + 2 * LORA], zeros(lora_pad),
                            mix[o + 2 * LORA:o + 2 * LORA + 256]])[None]
    w_up_p = jnp.concatenate([rwkv_w_up, jnp.zeros((lora_pad, RW), F32)], axis=0).astype(BF16)
    a_up_p = jnp.concatenate([rwkv_a_up, jnp.zeros((lora_pad, RW), F32)], axis=0).astype(BF16)
    g_up_b = rwkv_g_up.astype(BF16)
    alog_l = jnp.concatenate([zeros(ALPHA_LANE), gdn_a_log])[None]
    dtb_l = jnp.concatenate([zeros(ALPHA_LANE), gdn_dt_bias])[None]

    u = _rmsnorm(h, mix_norm_pre, BF16)
    p = _matmul(u, w_in_b, F32, *TILE_IN_PROJ, b_is_nk=True, name="in_proj")
    p3 = p.reshape(batch, seq, NP)
    y_rw = _rwkv_mix(p3, cp, gp, mixm, w_up_p, a_up_p, g_up_b).reshape(n, RW)
    y_gdn = _gdn_mix(p3, gdn_conv_w, alog_l, dtb_l, gdn_norm_w[None]).reshape(n, RW)
    merged = _branch_merge(y_rw, y_gdn, w_branch_rwkv, w_branch_gdn, p)
    y_mix = _matmul_wcast(merged, w_mix_out, BF16, name="mix_out")

    mlen = mem.shape[1]
    m = _rmsnorm(mem.reshape(batch * mlen, d), xa_norm_mem, BF16)
    xw = XA_HEADS * XA_HEAD
    kv = _matmul(m, xa_w_kv.astype(BF16), BF16, batch * mlen, xw, name="xa_kv")
    kmem = kv[:, :xw].reshape(batch, mlen, xw)
    vmem = kv[:, xw:].reshape(batch, mlen, xw)
    h, f = _xattn_block(h, y_mix, mix_norm_post, kmem, vmem, xa_w_q.astype(BF16), xa_w_o.astype(BF16),
                        xa_norm_pre, xa_norm_post, mlp_norm_pre, seq)

    hmid = _matmul_wcast(f, mlp_w_up, BF16, relu2=True, name="mlp_up")
    y_mlp = _matmul_kacc(hmid, mlp_w_down, BF16, name="mlp_down")
    h = _resid_norm(h, y_mlp, mlp_norm_post)
    return h.reshape(batch, seq, d)


def kernel(x, mem, mix_norm_pre, mix_norm_post, w_in, rwkv_shift_mix, rwkv_w0, rwkv_w_up, rwkv_a0, rwkv_a_up,
           rwkv_g_up, rwkv_k_k, rwkv_k_a, rwkv_r_k, rwkv_gn_w, rwkv_gn_b, gdn_conv_w, gdn_a_log, gdn_dt_bias,
           gdn_norm_w, w_branch_rwkv, w_branch_gdn, w_mix_out, xa_norm_pre, xa_norm_mem, xa_norm_post, xa_w_q,
           xa_w_kv, xa_w_o, mlp_norm_pre, mlp_norm_post, mlp_w_up, mlp_w_down):
    args = (mix_norm_pre, mix_norm_post, w_in, rwkv_shift_mix, rwkv_w0, rwkv_w_up, rwkv_a0, rwkv_a_up,
            rwkv_g_up, rwkv_k_k, rwkv_k_a, rwkv_r_k, rwkv_gn_w, rwkv_gn_b, gdn_conv_w, gdn_a_log, gdn_dt_bias,
            gdn_norm_w, w_branch_rwkv, w_branch_gdn, w_mix_out, xa_norm_pre, xa_norm_mem, xa_norm_post, xa_w_q,
            xa_w_kv, xa_w_o, mlp_norm_pre, mlp_norm_post, mlp_w_up, mlp_w_down)
    h = x
    for layer in range(mix_norm_pre.shape[0]):
        h = _layer(h, mem, *(a[layer] for a in args))
    return h
```
